```python
import math
import jax
import jax.numpy as jnp
from jax import lax
import numpy as np

D_MODEL = 1024
BATCH = 8
SEQ = 8192
DEPTH = 1

GRID_W = 64
CTX_LEN = 256
EXPAND = 2
D_MIX = EXPAND * D_MODEL
D_A = D_MIX // 2
D_B = D_MIX - D_A
HA_HEADS = 8
HA_DK = 128
HA_DV = D_A // HA_HEADS
A_QK = HA_HEADS * HA_DK
HB_HEADS = 8
HB_DK = 128
HB_DV = D_B // HB_HEADS
B_QK = HB_HEADS * HB_DK
CONV_K = 3
CONV_CH = 2 * B_QK + D_B
CHUNK = 64
NORM_EPS = 1e-6
COL_SIZES = (A_QK, A_QK, A_QK, D_A, D_A, B_QK, B_QK, D_B, D_B, 2 * HB_HEADS, 2 * HB_HEADS)
N_IN = 3 * A_QK + 2 * D_A + 2 * B_QK + 2 * D_B + 4 * HB_HEADS

kernel_name = "hybrid_hgrn2_gdn_flow_block"


def rmsnorm(x, w):
    xf = x.astype(jnp.float32)
    y = xf * lax.rsqrt(jnp.mean(xf * xf, axis=-1, keepdims=True) + NORM_EPS)
    return (y * w.astype(jnp.float32)).astype(x.dtype)


def l2norm(t):
    return t * lax.rsqrt(jnp.sum(t * t, axis=-1, keepdims=True) + NORM_EPS)


def heads(t, n_heads):
    bsz, length, width = t.shape
    return t.reshape(bsz, length, n_heads, width // n_heads).transpose(0, 2, 1, 3)


def merge_heads(t):
    bsz, nh, length, d = t.shape
    return t.transpose(0, 2, 1, 3).reshape(bsz, length, nh * d)


def split_columns(y):
    out, start = [], 0
    for size in COL_SIZES:
        out.append(y[..., start:start + size])
        start += size
    return out


def adaln(cond, ada_w, ada_b):
    m = jax.nn.silu(cond) @ ada_w + ada_b
    return jnp.split(m, 3, axis=-1)


def grid_conv(t, conv_w, rows):
    bsz, length, ch = t.shape
    img = t.reshape(bsz, rows, length // rows, ch)
    out = lax.conv_general_dilated(img, conv_w.astype(t.dtype)[:, :, None, :], (1, 1), "SAME",
                                   dimension_numbers=("NHWC", "HWIO", "NHWC"),
                                   feature_group_count=ch)
    return out.reshape(bsz, length, ch)


def hgrn2_scan(q, k, v, g, s0, readout):
    bsz, nh, length, dk = k.shape
    dv = v.shape[-1]
    n = length // CHUNK
    blk = lambda t: t.reshape(bsz, nh, n, CHUNK, t.shape[-1])
    q, k, v, g = blk(q), blk(k), blk(v), blk(g)
    b = jnp.cumsum(g, axis=-2)
    b_last = b[..., -1:, :]
    xs = [k * jnp.exp(b_last - b), v, jnp.exp(b_last)]
    if readout:
        xs += [q * jnp.exp(b), q, k, b]
    xs = tuple(jnp.moveaxis(t, 2, 0) for t in xs)
    incl = jnp.tril(jnp.ones((CHUNK, CHUNK), dtype=bool))[:, :, None]

    def step(s, inp):
        k_dec, v_i, d_last = inp[:3]
        s_new = d_last[..., 0, :, None] * s + jnp.einsum("bhsk,bhsv->bhkv", k_dec, v_i)
        if not readout:
            return s_new, None
        q_dec, q_i, k_i, b_i = inp[3:]
        diff = b_i[..., :, None, :] - b_i[..., None, :, :]
        dec = jnp.exp(jnp.where(incl, diff, -jnp.inf))
        scores = jnp.einsum("bhtk,bhsk,bhtsk->bhts", q_i, k_i, dec)
        o = jnp.einsum("bhtk,bhkv->bhtv", q_dec, s) + jnp.einsum("bhts,bhsv->bhtv", scores, v_i)
        return s_new, o

    s_fin, o = lax.scan(step, s0, xs)
    if readout:
        o = jnp.moveaxis(o, 0, 2).reshape(bsz, nh, length, dv)
    return o, s_fin


def gdn_scan(q, k, v, g, beta, s0, readout):
    bsz, nh, length, dk = k.shape
    dv = v.shape[-1]
    n = length // CHUNK
    blk = lambda t: t.reshape(bsz, nh, n, CHUNK, t.shape[-1])
    q, k, v = blk(q), blk(k), blk(v)
    g, beta = g.reshape(bsz, nh, n, CHUNK), beta.reshape(bsz, nh, n, CHUNK)
    b = jnp.cumsum(g, axis=-1)
    diff = b[..., :, None] - b[..., None, :]
    strict = jnp.tril(jnp.ones((CHUNK, CHUNK), dtype=bool), -1)
    incl = jnp.tril(jnp.ones((CHUNK, CHUNK), dtype=bool))
    kk = jnp.einsum("bhntk,bhnsk->bhnts", k, k)
    a_mat = jnp.eye(CHUNK, dtype=k.dtype) + beta[..., :, None] * kk * jnp.exp(jnp.where(strict, diff, -jnp.inf))
    rhs = jnp.concatenate([beta[..., None] * v, (beta * jnp.exp(b))[..., None] * k], axis=-1)
    sol = lax.linalg.triangular_solve(a_mat, rhs, left_side=True, lower=True, unit_diagonal=True)
    u0, w = sol[..., :dv], sol[..., dv:]
    b_last = b[..., -1:]
    xs = [u0, w, k * jnp.exp(b_last - b)[..., None], jnp.exp(b_last[..., 0])]
    if readout:
        p = jnp.einsum("bhntk,bhnsk->bhnts", q, k) * jnp.exp(jnp.where(incl, diff, -jnp.inf))
        xs += [q * jnp.exp(b)[..., None], p]
    xs = tuple(jnp.moveaxis(t, 2, 0) for t in xs)

    def step(s, inp):
        u0_i, w_i, k_dec, d_last = inp[:4]
        v_new = u0_i - jnp.einsum("bhck,bhkv->bhcv", w_i, s)
        s_new = d_last[..., None, None] * s + jnp.einsum("bhck,bhcv->bhkv", k_dec, v_new)
        if not readout:
            return s_new, None
        q_dec, p_i = inp[4:]
        o = jnp.einsum("bhtk,bhkv->bhtv", q_dec, s) + jnp.einsum("bhts,bhsv->bhtv", p_i, v_new)
        return s_new, o

    s_fin, o = lax.scan(step, s0, xs)
    if readout:
        o = jnp.moveaxis(o, 0, 2).reshape(bsz, nh, length, dv)
    return o, s_fin


def bidirectional(scan_fn, ctx_args, lat_args, s0, readout_ctx):
    flip = lambda t: jnp.flip(t, axis=2)
    (ctx_f, ctx_b), (lat_f, lat_b) = ctx_args, lat_args
    oc_f, sc_f = scan_fn(*ctx_f, s0, readout_ctx)
    oc_b, sc_b = scan_fn(*(flip(t) for t in ctx_b), s0, readout_ctx)
    ol_f, _ = scan_fn(*lat_f, sc_f, True)
    ol_b, _ = scan_fn(*(flip(t) for t in lat_b), sc_b, True)
    o_ctx = oc_f + flip(oc_b) if readout_ctx else None
    return ol_f + flip(ol_b), o_ctx


def project(h, w_in, conv_w, lb, a_log, dt_bias, rows):
    f32 = jnp.float32
    y = jnp.einsum("bld,dn->bln", h, w_in)
    a_q, a_f_fwd, a_f_bwd, a_i, a_z, b_q, b_k, b_v, b_z, b_a, b_b = split_columns(y)
    bsz, length, _ = y.shape
    q_a = heads(jax.nn.silu(a_q.astype(f32)), HA_HEADS) * HA_DK ** -0.5
    v_a = heads(a_i.astype(f32), HA_HEADS)

    def forget(fz, lb_d):
        fz = fz.astype(f32)
        log_f = jnp.logaddexp(jnp.log(lb_d), jnp.log1p(-lb_d) + jax.nn.log_sigmoid(fz))
        k = (1.0 - lb_d) * jax.nn.sigmoid(-fz)
        return heads(k, HA_HEADS), heads(log_f, HA_HEADS)

    k_af, g_af = forget(a_f_fwd, lb[0])
    k_ab, g_ab = forget(a_f_bwd, lb[1])
    hg_args = ((q_a, k_af, v_a, g_af), (q_a, k_ab, v_a, g_ab))
    qkv = jax.nn.silu(grid_conv(jnp.concatenate([b_q, b_k, b_v], axis=-1), conv_w, rows).astype(f32))
    q_b = l2norm(heads(qkv[..., :B_QK], HB_HEADS)) * HB_DK ** -0.5
    k_b = l2norm(heads(qkv[..., B_QK:2 * B_QK], HB_HEADS))
    v_b = heads(qkv[..., 2 * B_QK:], HB_HEADS)
    alpha_logit = b_a.astype(f32).reshape(bsz, length, 2, HB_HEADS)
    beta = jax.nn.sigmoid(b_b.astype(f32).reshape(bsz, length, 2, HB_HEADS))
    log_alpha = -jnp.exp(a_log.astype(f32)) * jax.nn.softplus(alpha_logit + dt_bias.astype(f32))
    log_alpha = log_alpha.transpose(2, 0, 3, 1)
    beta = beta.transpose(2, 0, 3, 1)
    gdn_args = ((q_b, k_b, v_b, log_alpha[0], beta[0]), (q_b, k_b, v_b, log_alpha[1], beta[1]))
    return hg_args, gdn_args, a_z.astype(f32), b_z.astype(f32)


def group_out(o_a, o_b, z_a, z_b, na_w, nb_w, w_out, dtype):
    def head_norm(o, w):
        o = o * lax.rsqrt(jnp.mean(o * o, axis=-1, keepdims=True) + NORM_EPS)
        return merge_heads(o * w.astype(jnp.float32)[None, :, None, :])

    y_a = jax.nn.silu(z_a) * head_norm(o_a, na_w)
    y_b = jax.nn.silu(z_b) * head_norm(o_b, nb_w)
    y = jnp.concatenate([y_a, y_b], axis=-1).astype(dtype)
    return y @ w_out


def _fwd_setup_inputs(seed: int = 0) -> dict:
    key = jax.random.key(seed)
    ks = jax.random.split(key, 16)
    f32 = jnp.float32
    nrm = lambda k, shape, s: s * jax.random.normal(k, shape, f32)
    x = nrm(ks[0], (BATCH, SEQ, D_MODEL), 1.0)
    c = nrm(ks[1], (BATCH, D_MODEL), 1.0)
    ctx = nrm(ks[2], (BATCH, CTX_LEN, D_MODEL), 1.0)
    c_ctx = nrm(ks[3], (D_MODEL,), 1.0)
    norm_w = 1.0 + nrm(ks[4], (DEPTH, D_MODEL), 0.02)
    ada_w = nrm(ks[5], (DEPTH, D_MODEL, 3 * D_MODEL), 0.5 * D_MODEL ** -0.5)
    ada_b = nrm(ks[6], (DEPTH, 3 * D_MODEL), 0.01)
    w_in = nrm(ks[7], (DEPTH, D_MODEL, N_IN), D_MODEL ** -0.5)
    conv_w = nrm(ks[8], (DEPTH, CONV_K, CONV_K, CONV_CH), 1.0 / CONV_K)
    hg_lb_logits = nrm(ks[9], (DEPTH + 1, 2, A_QK), 0.5)
    gdn_a_log = jnp.log(jax.random.uniform(ks[10], (DEPTH, 2, HB_HEADS), f32, 1.0, 16.0))
    dt = jnp.exp(jax.random.uniform(ks[11], (DEPTH, 2, HB_HEADS), f32, math.log(1e-3), math.log(1e-1)))
    gdn_dt_bias = dt + jnp.log(-jnp.expm1(-dt))
    ha_norm_w = 1.0 + nrm(ks[12], (DEPTH, HA_HEADS, HA_DV), 0.02)
    hb_norm_w = 1.0 + nrm(ks[13], (DEPTH, HB_HEADS, HB_DV), 0.02)
    w_out = nrm(ks[14], (DEPTH, D_MIX, D_MODEL), D_MIX ** -0.5)
    final_norm_w = 1.0 + nrm(ks[15], (D_MODEL,), 0.02)
    return {"x": x, "c": c, "ctx": ctx, "c_ctx": c_ctx, "norm_w": norm_w, "ada_w": ada_w,
            "ada_b": ada_b, "w_in": w_in, "conv_w": conv_w, "hg_lb_logits": hg_lb_logits,
            "gdn_a_log": gdn_a_log, "gdn_dt_bias": gdn_dt_bias, "ha_norm_w": ha_norm_w,
            "hb_norm_w": hb_norm_w, "w_out": w_out, "final_norm_w": final_norm_w}


def _fwd_reference(x, c, ctx, c_ctx, norm_w, ada_w, ada_b, w_in, conv_w, hg_lb_logits, gdn_a_log,
              gdn_dt_bias, ha_norm_w, hb_norm_w, w_out, final_norm_w):
    bsz = x.shape[0]
    rows = x.shape[1] // GRID_W
    lb_all = jnp.cumsum(jax.nn.softmax(hg_lb_logits.astype(jnp.float32), axis=0), axis=0)
    s0_a = jnp.zeros((bsz, HA_HEADS, HA_DK, HA_DV), jnp.float32)
    s0_b = jnp.zeros((bsz, HB_HEADS, HB_DK, HB_DV), jnp.float32)
    for layer in range(DEPTH):
        readout_ctx = layer + 1 < DEPTH
        sh_l, sc_l, gt_l = adaln(c, ada_w[layer], ada_b[layer])
        sh_c, sc_c, gt_c = adaln(c_ctx, ada_w[layer], ada_b[layer])
        h_lat = rmsnorm(x, norm_w[layer]) * (1.0 + sc_l[:, None, :]) + sh_l[:, None, :]
        h_ctx = rmsnorm(ctx, norm_w[layer]) * (1.0 + sc_c) + sh_c
        lat_hg, lat_gdn, lz_a, lz_b = project(h_lat, w_in[layer], conv_w[layer], lb_all[layer],
                                              gdn_a_log[layer], gdn_dt_bias[layer], rows)
        ctx_hg, ctx_gdn, cz_a, cz_b = project(h_ctx, w_in[layer], conv_w[layer], lb_all[layer],
                                              gdn_a_log[layer], gdn_dt_bias[layer], 1)
        oa_lat, oa_ctx = bidirectional(hgrn2_scan, ctx_hg, lat_hg, s0_a, readout_ctx)
        ob_lat, ob_ctx = bidirectional(gdn_scan, ctx_gdn, lat_gdn, s0_b, readout_ctx)
        mix_lat = group_out(oa_lat, ob_lat, lz_a, lz_b, ha_norm_w[layer], hb_norm_w[layer], w_out[layer], x.dtype)
        if readout_ctx:
            mix_ctx = group_out(oa_ctx, ob_ctx, cz_a, cz_b, ha_norm_w[layer], hb_norm_w[layer], w_out[layer], ctx.dtype)
            ctx = ctx + gt_c * mix_ctx
        x = x + gt_l[:, None, :] * mix_lat
    return rmsnorm(x, final_norm_w)


import jax as _jax
import jax.numpy as _jnp

TWIN_FORMAT = 'train_step'
FWD_PARAMS = ['x', 'c', 'ctx', 'c_ctx', 'norm_w', 'ada_w', 'ada_b', 'w_in', 'conv_w', 'hg_lb_logits', 'gdn_a_log', 'gdn_dt_bias', 'ha_norm_w', 'hb_norm_w', 'w_out', 'final_norm_w']
TWIN_WEIGHTS = ['c_ctx', 'norm_w', 'ada_w', 'ada_b', 'w_in', 'conv_w', 'hg_lb_logits', 'gdn_a_log', 'gdn_dt_bias', 'ha_norm_w', 'hb_norm_w', 'w_out', 'final_norm_w']
TWIN_DIFF_INPUT = 'x'
TWIN_INPUTS = ['x', 'c', 'ctx', 'c_ctx', 'norm_w', 'ada_w', 'ada_b', 'w_in', 'conv_w', 'hg_lb_logits', 'gdn_a_log', 'gdn_dt_bias', 'ha_norm_w', 'hb_norm_w', 'w_out', 'final_norm_w', 'loss_target', 'm_c_ctx', 'm_norm_w', 'm_ada_w', 'm_ada_b', 'm_w_in', 'm_conv_w', 'm_hg_lb_logits', 'm_gdn_a_log', 'm_gdn_dt_bias', 'm_ha_norm_w', 'm_hb_norm_w', 'm_w_out', 'm_final_norm_w', 'v_c_ctx', 'v_norm_w', 'v_ada_w', 'v_ada_b', 'v_w_in', 'v_conv_w', 'v_hg_lb_logits', 'v_gdn_a_log', 'v_gdn_dt_bias', 'v_ha_norm_w', 'v_hb_norm_w', 'v_w_out', 'v_final_norm_w']
TWIN_OUTPUTS = ['loss', 'grad_x', 'grad_c_ctx', 'grad_norm_w', 'grad_ada_w', 'grad_ada_b', 'grad_w_in', 'grad_conv_w', 'grad_hg_lb_logits', 'grad_gdn_a_log', 'grad_gdn_dt_bias', 'grad_ha_norm_w', 'grad_hb_norm_w', 'grad_w_out', 'grad_final_norm_w', 'delta_c_ctx', 'delta_norm_w', 'delta_ada_w', 'delta_ada_b', 'delta_w_in', 'delta_conv_w', 'delta_hg_lb_logits', 'delta_gdn_a_log', 'delta_gdn_dt_bias', 'delta_ha_norm_w', 'delta_hb_norm_w', 'delta_w_out', 'delta_final_norm_w', 'new_m_c_ctx', 'new_m_norm_w', 'new_m_ada_w', 'new_m_ada_b', 'new_m_w_in', 'new_m_conv_w', 'new_m_hg_lb_logits', 'new_m_gdn_a_log', 'new_m_gdn_dt_bias', 'new_m_ha_norm_w', 'new_m_hb_norm_w', 'new_m_w_out', 'new_m_final_norm_w', 'new_v_c_ctx', 'new_v_norm_w', 'new_v_ada_w', 'new_v_ada_b', 'new_v_w_in', 'new_v_conv_w', 'new_v_hg_lb_logits', 'new_v_gdn_a_log', 'new_v_gdn_dt_bias', 'new_v_ha_norm_w', 'new_v_hb_norm_w', 'new_v_w_out', 'new_v_final_norm_w']
TWIN_LEAF_KINDS = {'loss': 'loss', 'grad_x': 'grad_x', 'grad_c_ctx': 'grad_w', 'grad_norm_w': 'grad_w', 'grad_ada_w': 'grad_w', 'grad_ada_b': 'grad_w', 'grad_w_in': 'grad_w', 'grad_conv_w': 'grad_w', 'grad_hg_lb_logits': 'grad_w', 'grad_gdn_a_log': 'grad_w', 'grad_gdn_dt_bias': 'grad_w', 'grad_ha_norm_w': 'grad_w', 'grad_hb_norm_w': 'grad_w', 'grad_w_out': 'grad_w', 'grad_final_norm_w': 'grad_w', 'delta_c_ctx': 'delta_w', 'delta_norm_w': 'delta_w', 'delta_ada_w': 'delta_w', 'delta_ada_b': 'delta_w', 'delta_w_in': 'delta_w', 'delta_conv_w': 'delta_w', 'delta_hg_lb_logits': 'delta_w', 'delta_gdn_a_log': 'delta_w', 'delta_gdn_dt_bias': 'delta_w', 'delta_ha_norm_w': 'delta_w', 'delta_hb_norm_w': 'delta_w', 'delta_w_out': 'delta_w', 'delta_final_norm_w': 'delta_w', 'new_m_c_ctx': 'new_m', 'new_m_norm_w': 'new_m', 'new_m_ada_w': 'new_m', 'new_m_ada_b': 'new_m', 'new_m_w_in': 'new_m', 'new_m_conv_w': 'new_m', 'new_m_hg_lb_logits': 'new_m', 'new_m_gdn_a_log': 'new_m', 'new_m_gdn_dt_bias': 'new_m', 'new_m_ha_norm_w': 'new_m', 'new_m_hb_norm_w': 'new_m', 'new_m_w_out': 'new_m', 'new_m_final_norm_w': 'new_m', 'new_v_c_ctx': 'new_v', 'new_v_norm_w': 'new_v', 'new_v_ada_w': 'new_v', 'new_v_ada_b': 'new_v', 'new_v_w_in': 'new_v', 'new_v_conv_w': 'new_v', 'new_v_hg_lb_logits': 'new_v', 'new_v_gdn_a_log': 'new_v', 'new_v_gdn_dt_bias': 'new_v', 'new_v_ha_norm_w': 'new_v', 'new_v_hb_norm_w': 'new_v', 'new_v_w_out': 'new_v', 'new_v_final_norm_w': 'new_v'}


def _forward(args):
    return _fwd_reference(*[args[k] for k in FWD_PARAMS])


def _output_shape():
    def fwd():
        inp = _fwd_setup_inputs(0)
        return _fwd_reference(*[inp[k] for k in FWD_PARAMS])
    out = _jax.eval_shape(fwd)
    return out.shape, out.dtype

N_MICROBATCH = 1
ADAM_LR = 0.001
ADAM_B1 = 0.9
ADAM_B2 = 0.999
ADAM_EPS = 1e-08
ADAM_WD = 0.01
ADAM_STEP = 10
PER_EXAMPLE_BATCH_AXIS = {'x': 0, 'c': 0, 'ctx': 0, 'loss_target': 0}
SHARED_INPUTS = []
_WEIGHT_DTYPES = {'c_ctx': _jnp.float32, 'norm_w': _jnp.float32, 'ada_w': _jnp.float32, 'ada_b': _jnp.float32, 'w_in': _jnp.float32, 'conv_w': _jnp.float32, 'hg_lb_logits': _jnp.float32, 'gdn_a_log': _jnp.float32, 'gdn_dt_bias': _jnp.float32, 'ha_norm_w': _jnp.float32, 'hb_norm_w': _jnp.float32, 'w_out': _jnp.float32, 'final_norm_w': _jnp.float32}
MOMENT_SCALE = {'c_ctx': 2.622549e-03, 'norm_w': 6.666698e-02, 'ada_w': 7.149938e-02, 'ada_b': 1.182140e-01, 'w_in': 2.484882e-02, 'conv_w': 2.458603e-02, 'hg_lb_logits': 1.434199e-03, 'gdn_a_log': 1.209464e-01, 'gdn_dt_bias': 1.185987e-01, 'ha_norm_w': 3.486452e-02, 'hb_norm_w': 3.529313e-02, 'w_out': 4.824125e-02, 'final_norm_w': 6.395180e+01}


def _to_microbatches(a, axis):
    t = _jnp.moveaxis(a, axis, 0)
    t = t.reshape((N_MICROBATCH, t.shape[0] // N_MICROBATCH) + t.shape[1:])
    return _jnp.moveaxis(t, 1, axis + 1)


def setup_inputs(seed: int = 0) -> dict:
    inp = _fwd_setup_inputs(seed)
    key = _jax.random.fold_in(_jax.random.key(seed), 7919)
    shape, _ = _output_shape()
    out = dict(inp)
    out["loss_target"] = _jax.random.normal(_jax.random.fold_in(key, 0), shape, _jnp.float32)
    for i, name in enumerate(TWIN_WEIGHTS):
        w = inp[name].astype(_jnp.float32)
        if MOMENT_SCALE is None:
            s = _jnp.sqrt(_jnp.mean(_jnp.square(w)) + 1e-30)
        else:
            s = MOMENT_SCALE[name]
        km, kv = _jax.random.split(_jax.random.fold_in(key, i + 1))
        out[name] = w
        out["m_" + name] = s * _jax.random.normal(km, w.shape, _jnp.float32)
        out["v_" + name] = (s * s) * _jax.random.uniform(kv, w.shape, _jnp.float32, 0.5, 1.5)
    if N_MICROBATCH > 1:
        for name, axis in PER_EXAMPLE_BATCH_AXIS.items():
            out[name] = _to_microbatches(out[name], axis)
    return {'x': out['x'], 'c': out['c'], 'ctx': out['ctx'], 'c_ctx': out['c_ctx'], 'norm_w': out['norm_w'], 'ada_w': out['ada_w'], 'ada_b': out['ada_b'], 'w_in': out['w_in'], 'conv_w': out['conv_w'], 'hg_lb_logits': out['hg_lb_logits'], 'gdn_a_log': out['gdn_a_log'], 'gdn_dt_bias': out['gdn_dt_bias'], 'ha_norm_w': out['ha_norm_w'], 'hb_norm_w': out['hb_norm_w'], 'w_out': out['w_out'], 'final_norm_w': out['final_norm_w'], 'loss_target': out['loss_target'], 'm_c_ctx': out['m_c_ctx'], 'm_norm_w': out['m_norm_w'], 'm_ada_w': out['m_ada_w'], 'm_ada_b': out['m_ada_b'], 'm_w_in': out['m_w_in'], 'm_conv_w': out['m_conv_w'], 'm_hg_lb_logits': out['m_hg_lb_logits'], 'm_gdn_a_log': out['m_gdn_a_log'], 'm_gdn_dt_bias': out['m_gdn_dt_bias'], 'm_ha_norm_w': out['m_ha_norm_w'], 'm_hb_norm_w': out['m_hb_norm_w'], 'm_w_out': out['m_w_out'], 'm_final_norm_w': out['m_final_norm_w'], 'v_c_ctx': out['v_c_ctx'], 'v_norm_w': out['v_norm_w'], 'v_ada_w': out['v_ada_w'], 'v_ada_b': out['v_ada_b'], 'v_w_in': out['v_w_in'], 'v_conv_w': out['v_conv_w'], 'v_hg_lb_logits': out['v_hg_lb_logits'], 'v_gdn_a_log': out['v_gdn_a_log'], 'v_gdn_dt_bias': out['v_gdn_dt_bias'], 'v_ha_norm_w': out['v_ha_norm_w'], 'v_hb_norm_w': out['v_hb_norm_w'], 'v_w_out': out['v_w_out'], 'v_final_norm_w': out['v_final_norm_w']}


def _loss(weights, diff, rest, loss_target):
    with _jax.named_scope("forward"):
        args = {**rest, TWIN_DIFF_INPUT: diff, **{k: w.astype(_WEIGHT_DTYPES[k]) for k, w in weights.items()}}
        y = _forward(args)
    with _jax.named_scope("loss_head"):
        err = _jnp.square(y.astype(_jnp.float32) - loss_target)
        return 0.5 * _jnp.sum(_jnp.mean(err, axis=-1)) if err.ndim else 0.5 * err


def _adamw(w, g, m, v):
    m = ADAM_B1 * m + (1.0 - ADAM_B1) * g
    v = ADAM_B2 * v + (1.0 - ADAM_B2) * _jnp.square(g)
    m_hat = m / (1.0 - ADAM_B1 ** ADAM_STEP)
    v_hat = v / (1.0 - ADAM_B2 ** ADAM_STEP)
    delta = -ADAM_LR * (m_hat / (_jnp.sqrt(v_hat) + ADAM_EPS) + ADAM_WD * w)
    return delta, m, v


def reference(x, c, ctx, c_ctx, norm_w, ada_w, ada_b, w_in, conv_w, hg_lb_logits, gdn_a_log, gdn_dt_bias, ha_norm_w, hb_norm_w, w_out, final_norm_w, loss_target, m_c_ctx, m_norm_w, m_ada_w, m_ada_b, m_w_in, m_conv_w, m_hg_lb_logits, m_gdn_a_log, m_gdn_dt_bias, m_ha_norm_w, m_hb_norm_w, m_w_out, m_final_norm_w, v_c_ctx, v_norm_w, v_ada_w, v_ada_b, v_w_in, v_conv_w, v_hg_lb_logits, v_gdn_a_log, v_gdn_dt_bias, v_ha_norm_w, v_hb_norm_w, v_w_out, v_final_norm_w):
    given = dict(x=x, c=c, ctx=ctx, c_ctx=c_ctx, norm_w=norm_w, ada_w=ada_w, ada_b=ada_b, w_in=w_in, conv_w=conv_w, hg_lb_logits=hg_lb_logits, gdn_a_log=gdn_a_log, gdn_dt_bias=gdn_dt_bias, ha_norm_w=ha_norm_w, hb_norm_w=hb_norm_w, w_out=w_out, final_norm_w=final_norm_w, loss_target=loss_target, m_c_ctx=m_c_ctx, m_norm_w=m_norm_w, m_ada_w=m_ada_w, m_ada_b=m_ada_b, m_w_in=m_w_in, m_conv_w=m_conv_w, m_hg_lb_logits=m_hg_lb_logits, m_gdn_a_log=m_gdn_a_log, m_gdn_dt_bias=m_gdn_dt_bias, m_ha_norm_w=m_ha_norm_w, m_hb_norm_w=m_hb_norm_w, m_w_out=m_w_out, m_final_norm_w=m_final_norm_w, v_c_ctx=v_c_ctx, v_norm_w=v_norm_w, v_ada_w=v_ada_w, v_ada_b=v_ada_b, v_w_in=v_w_in, v_conv_w=v_conv_w, v_hg_lb_logits=v_hg_lb_logits, v_gdn_a_log=v_gdn_a_log, v_gdn_dt_bias=v_gdn_dt_bias, v_ha_norm_w=v_ha_norm_w, v_hb_norm_w=v_hb_norm_w, v_w_out=v_w_out, v_final_norm_w=v_final_norm_w)
    weights = {n: given[n] for n in TWIN_WEIGHTS}
    shared = {n: given[n] for n in SHARED_INPUTS}
    per_example = {n: given[n] for n in ['x', 'c', 'ctx']}
    grad_fn = _jax.value_and_grad(_loss, argnums=(0, 1))

    def one_microbatch(ex, loss_target):
        ex = dict(ex)
        diff = ex.pop(TWIN_DIFF_INPUT)
        return grad_fn(weights, diff, {**shared, **ex}, loss_target)

    if N_MICROBATCH == 1:
        loss, (grad_w, grad_x) = one_microbatch(per_example, given["loss_target"])
    else:
        def body(carry, xs):
            loss_sum, grad_sum = carry
            l_k, (gw_k, gx_k) = one_microbatch(xs[0], xs[1])
            with _jax.named_scope("update"):
                return (loss_sum + l_k, _jax.tree.map(_jnp.add, grad_sum, gw_k)), gx_k

        init = (_jnp.zeros((), _jnp.float32), _jax.tree.map(_jnp.zeros_like, weights))
        (loss, grad_w), grad_x = _jax.lax.scan(body, init, (per_example, given["loss_target"]))
    with _jax.named_scope("update"):
        delta_w, new_m, new_v = {}, {}, {}
        for n in TWIN_WEIGHTS:
            delta_w[n], new_m[n], new_v[n] = _adamw(weights[n], grad_w[n], given["m_" + n], given["v_" + n])
    return (loss, grad_x, *[grad_w[n] for n in TWIN_WEIGHTS], *[delta_w[n] for n in TWIN_WEIGHTS],
            *[new_m[n] for n in TWIN_WEIGHTS], *[new_v[n] for n in TWIN_WEIGHTS])
```

```python
import functools

import jax
import jax.numpy as jnp
from jax import lax
from jax.experimental import pallas as pl
from jax.experimental.pallas import tpu as pltpu

F32 = jnp.float32
BF16 = jnp.bfloat16
HI = lax.Precision.HIGHEST
MESH = pl.DeviceIdType.MESH
ANY = pl.BlockSpec(memory_space=pl.ANY)

N_DEV = 8
LANES = 128
CHUNK = 64
GRID_W = 64
NORM_EPS = 1e-6
NEG = -1e30
COLS_PER_TILE = 8
VMEM_BIG = 56 * 1024 * 1024
HEAD_UNROLL = 2

ADAM_LR, ADAM_B1, ADAM_B2, ADAM_EPS, ADAM_WD, ADAM_STEP = 0.001, 0.9, 0.999, 1e-08, 0.01, 10


def _params(sem=None, vmem=None):
    kw = {}
    if sem is not None:
        kw["dimension_semantics"] = sem
    if vmem is not None:
        kw["vmem_limit_bytes"] = vmem
    return pltpu.CompilerParams(**kw)


def _pick(n, cands):
    for c in cands:
        if n % c == 0:
            return c
    return n


def _silu(x):
    return x * jax.nn.sigmoid(x)


def _mm(a, b, prec=None):
    return lax.dot_general(a, b, (((1,), (0,)), ((), ())), precision=prec, preferred_element_type=F32)


def _mm_nt(a, b, prec=None):
    return lax.dot_general(a, b, (((1,), (1,)), ((), ())), precision=prec, preferred_element_type=F32)


def _mm_tn(a, b, prec=None):
    return lax.dot_general(a, b, (((0,), (0,)), ((), ())), precision=prec, preferred_element_type=F32)


def _me():
    return 4 * lax.axis_index("x") + 2 * lax.axis_index("y") + lax.axis_index("c")


def _peer(r):
    x, y, c = lax.axis_index("x"), lax.axis_index("y"), lax.axis_index("c")
    px = 1 - x if r & 4 else x
    py = 1 - y if r & 2 else y
    pc = 1 - c if r & 1 else c
    return (px, py, pc), 4 * px + 2 * py + pc


def _exchange(src, name, gather):
    slab = src.shape if gather else src.shape[1:]

    def body(src_ref, out_ref, send_sems, recv_sems, local_sem):
        me = _me()
        mine = src_ref if gather else src_ref.at[me]
        local = pltpu.make_async_copy(mine, out_ref.at[me], local_sem)
        local.start()
        copies = []
        for r in range(1, N_DEV):
            dev, pid = _peer(r)
            cp = pltpu.make_async_remote_copy(
                src_ref=src_ref if gather else src_ref.at[pid], dst_ref=out_ref.at[me],
                send_sem=send_sems.at[r - 1], recv_sem=recv_sems.at[r - 1],
                device_id=dev, device_id_type=MESH)
            cp.start()
            copies.append(cp)
        for cp in copies:
            cp.wait()
        local.wait()

    return pl.pallas_call(
        body, name=name,
        out_shape=jax.ShapeDtypeStruct((N_DEV,) + tuple(slab), src.dtype),
        in_specs=[ANY], out_specs=ANY,
        scratch_shapes=[pltpu.SemaphoreType.DMA((N_DEV - 1,)), pltpu.SemaphoreType.DMA((N_DEV - 1,)),
                        pltpu.SemaphoreType.DMA(())],
    )(src)


def _to_blob(parts, dtype, row_mult):
    flat = [p.reshape(-1).astype(dtype) for p in parts]
    offs, n = [], 0
    for f in flat:
        offs.append(n)
        n += f.shape[0]
    unit = row_mult * LANES
    total = -(-n // unit) * unit
    if total > n:
        flat.append(jnp.zeros((total - n,), dtype))
    return jnp.concatenate(flat).reshape(total // LANES, LANES), offs


def _sum_slots(buf, name):
    n, rows, _ = buf.shape
    tr = _pick(rows, (2048, 1024, 512, 256, 128, 64, 32, 16, 8))

    def body(b_ref, o_ref):
        acc = b_ref[0]
        for s in range(1, n):
            acc = acc + b_ref[s]
        o_ref[...] = acc

    return pl.pallas_call(
        body, name=name, grid=(rows // tr,),
        in_specs=[pl.BlockSpec((n, tr, LANES), lambda i: (0, i, 0))],
        out_specs=pl.BlockSpec((tr, LANES), lambda i: (i, 0)),
        out_shape=jax.ShapeDtypeStruct((rows, LANES), F32),
        compiler_params=_params(("arbitrary",)),
    )(buf)


def _matmul(a, b, name, a_cbm=False, b_cbm=False, out_cbm=False, out_dtype=F32, tm=None, tn=None, tk=None):
    if a_cbm:
        m, k = a.shape[1], a.shape[0] * LANES
    else:
        m, k = a.shape
    n = b.shape[0] * LANES if b_cbm else b.shape[1]
    tm = tm or _pick(m, (1024, 768, 512, 384, 256, 128, 64))
    tn = tn or _pick(n, (1024, 512, 256, 128))
    tk = tk or _pick(k, (2048, 1024, 768, 512, 256, 128))
    nk = k // tk

    def load(ref, cbm):
        if not cbm:
            return ref[...].astype(BF16)
        return jnp.concatenate([ref[j].astype(BF16) for j in range(ref.shape[0])], axis=1)

    def body(a_ref, b_ref, o_ref, acc_ref):
        kk = pl.program_id(2)
        part = _mm(load(a_ref, a_cbm), load(b_ref, b_cbm))

        @pl.when(kk == 0)
        def _():
            acc_ref[...] = part

        @pl.when(kk > 0)
        def _():
            acc_ref[...] += part

        @pl.when(kk == nk - 1)
        def _():
            r = acc_ref[...]
            if out_cbm:
                for j in range(tn // LANES):
                    o_ref[j] = r[:, j * LANES:(j + 1) * LANES].astype(out_dtype)
            else:
                o_ref[...] = r.astype(out_dtype)

    if a_cbm:
        a_spec = pl.BlockSpec((tk // LANES, tm, LANES), lambda j, i, kk: (kk, i, 0))
    else:
        a_spec = pl.BlockSpec((tm, tk), lambda j, i, kk: (i, kk))
    if b_cbm:
        b_spec = pl.BlockSpec((tn // LANES, tk, LANES), lambda j, i, kk: (j, kk, 0))
    else:
        b_spec = pl.BlockSpec((tk, tn), lambda j, i, kk: (kk, j))
    if out_cbm:
        o_spec = pl.BlockSpec((tn // LANES, tm, LANES), lambda j, i, kk: (j, i, 0))
        o_shape = jax.ShapeDtypeStruct((n // LANES, m, LANES), out_dtype)
    else:
        o_spec = pl.BlockSpec((tm, tn), lambda j, i, kk: (i, j))
        o_shape = jax.ShapeDtypeStruct((m, n), out_dtype)
    return pl.pallas_call(
        body, name=name, grid=(n // tn, m // tm, nk),
        in_specs=[a_spec, b_spec], out_specs=o_spec, out_shape=o_shape,
        scratch_shapes=[pltpu.VMEM((tm, tn), F32)],
        compiler_params=_params(("arbitrary", "arbitrary", "arbitrary"), VMEM_BIG),
    )(a, b)


def _ada_fwd(cvec, ada_w, ada_b):
    def body(c_ref, w_ref, b_ref, o_ref):
        o_ref[...] = _mm(_silu(c_ref[...]), w_ref[...], HI) + b_ref[...]

    return pl.pallas_call(
        body, name="ada_fwd", out_shape=jax.ShapeDtypeStruct((8, ada_w.shape[1]), F32),
        compiler_params=_params(None, VMEM_BIG),
    )(cvec, ada_w, ada_b)


def _ada_bwd(cvec, ada_w, dmods):
    def body(c_ref, w_ref, d_ref, o_ref):
        (_, vj) = jax.vjp(_silu, c_ref[...])
        o_ref[...] = vj(_mm_nt(d_ref[...], w_ref[...], HI))[0]

    return pl.pallas_call(
        body, name="ada_bwd", out_shape=jax.ShapeDtypeStruct(cvec.shape, F32),
        compiler_params=_params(None, VMEM_BIG),
    )(cvec, ada_w, dmods)


def _ada_wgrad(conds, dm_lat, dm_ctx):
    d = conds.shape[1]
    cols = dm_lat.shape[1]

    def body(c_ref, dl_ref, dc_ref, o_ref):
        dctx = dc_ref[0:1]
        for s in range(1, N_DEV):
            dctx = dctx + dc_ref[s:s + 1]
        rhs = jnp.concatenate([dl_ref[...], dctx, jnp.zeros((7, cols), F32)], axis=0)
        o_ref[...] = _mm_tn(_silu(c_ref[...]), rhs, HI)

    return pl.pallas_call(
        body, name="ada_wgrad", out_shape=jax.ShapeDtypeStruct((d, cols), F32),
    )(conds, dm_lat, dm_ctx)


def _prenorm_tile(xt, nw, sc, sh):
    r = lax.rsqrt(jnp.mean(xt * xt, axis=-1, keepdims=True) + NORM_EPS)
    return (xt * r * nw) * (1.0 + sc) + sh


def _tok_specs(l_lat, l_ctx, d, tm):
    nl, nc = l_lat // tm, l_ctx // tm
    lat = pl.BlockSpec((tm, d), lambda i: (jnp.minimum(i, nl - 1), 0))
    ctx = pl.BlockSpec((tm, d), lambda i: (jnp.maximum(i - nl, 0), 0))
    return nl, nc, lat, ctx


def _prenorm_fwd(x, ctx, norm_w, mods):
    l_lat, d = x.shape
    l_ctx = ctx.shape[0]
    tm = _pick(l_ctx, (256, 128, 64))
    nl, nc, lat_spec, ctx_spec = _tok_specs(l_lat, l_ctx, d, tm)

    def body(x_ref, c_ref, nw_ref, m_ref, h_ref):
        is_lat = pl.program_id(0) < nl
        xt = jnp.where(is_lat, x_ref[...], c_ref[...])
        row = jnp.where(is_lat, m_ref[0:1, :], m_ref[1:2, :])
        h_ref[...] = _prenorm_tile(xt, nw_ref[...], row[:, d:2 * d], row[:, 0:d]).astype(BF16)

    return pl.pallas_call(
        body, name="prenorm_fwd", grid=(nl + nc,),
        in_specs=[lat_spec, ctx_spec, pl.BlockSpec((1, d), lambda i: (0, 0)), pl.BlockSpec((8, 3 * d), lambda i: (0, 0))],
        out_specs=pl.BlockSpec((tm, d), lambda i: (i, 0)),
        out_shape=jax.ShapeDtypeStruct((l_lat + l_ctx, d), BF16),
        compiler_params=_params(("arbitrary",)),
    )(x, ctx, norm_w, mods)


def _prenorm_bwd(x, ctx, norm_w, mods, dh, dx_res):
    l_lat, d = x.shape
    l_ctx = ctx.shape[0]
    tm = _pick(l_ctx, (256, 128, 64))
    nl, nc, lat_spec, ctx_spec = _tok_specs(l_lat, l_ctx, d, tm)

    def body(x_ref, c_ref, nw_ref, m_ref, dh_ref, dr_ref, gx_ref, dnw_ref, dm_ref):
        i = pl.program_id(0)
        is_lat = i < nl

        @pl.when(i == 0)
        def _():
            dnw_ref[...] = jnp.zeros_like(dnw_ref)
            dm_ref[...] = jnp.zeros_like(dm_ref)

        xt = jnp.where(is_lat, x_ref[...], c_ref[...])
        row = jnp.where(is_lat, m_ref[0:1, :], m_ref[1:2, :])
        _, vj = jax.vjp(_prenorm_tile, xt, nw_ref[...], row[:, d:2 * d], row[:, 0:d])
        dxt, dnw, dsc, dsh = vj(dh_ref[...])
        dnw_ref[...] += dnw
        upd = jnp.concatenate([dsh, dsc, jnp.zeros_like(dsh)], axis=1)

        @pl.when(is_lat)
        def _():
            gx_ref[...] = dr_ref[...] + dxt
            dm_ref[0:1, :] += upd

        @pl.when(jnp.logical_not(is_lat))
        def _():
            dm_ref[1:2, :] += upd

    return pl.pallas_call(
        body, name="prenorm_bwd", grid=(nl + nc,),
        in_specs=[lat_spec, ctx_spec, pl.BlockSpec((1, d), lambda i: (0, 0)), pl.BlockSpec((8, 3 * d), lambda i: (0, 0)),
                  pl.BlockSpec((tm, d), lambda i: (i, 0)), lat_spec],
        out_specs=[lat_spec, pl.BlockSpec((1, d), lambda i: (0, 0)), pl.BlockSpec((8, 3 * d), lambda i: (0, 0))],
        out_shape=[jax.ShapeDtypeStruct((l_lat, d), F32), jax.ShapeDtypeStruct((1, d), F32),
                   jax.ShapeDtypeStruct((8, 3 * d), F32)],
        compiler_params=_params(("arbitrary",)),
    )(x, ctx, norm_w, mods, dh, dx_res)


def _conv_call(src, wts, l_lat, first_block, n_blocks, name, dout=None):
    t = src.shape[1]
    l_ctx = t - l_lat
    rc = _pick(l_lat, (512, 256, 128, 64))
    n_rc = l_lat // rc
    bwd = dout is not None

    def shifted(xc, n, period):
        pos = lax.broadcasted_iota(jnp.int32, (n, LANES), 0) % period
        xm = jnp.where(pos == 0, 0.0, pltpu.roll(xc, 1, 0))
        xp = jnp.where(pos == period - 1, 0.0, pltpu.roll(xc, n - 1, 0))
        return xm, xc, xp

    def fill(ref, sm, s0, sp):
        zero = jnp.zeros((GRID_W, LANES), F32)
        for s in (sm, s0, sp):
            s[pl.ds(0, GRID_W), :] = zero
            s[pl.ds(l_lat + GRID_W, GRID_W), :] = zero

        def step(i, carry):
            st = pl.multiple_of(i * rc, rc)
            xm, x0, xp = shifted(ref[0, pl.ds(st, rc), :], rc, GRID_W)
            sm[pl.ds(st + GRID_W, rc), :] = xm
            s0[pl.ds(st + GRID_W, rc), :] = x0
            sp[pl.ds(st + GRID_W, rc), :] = xp
            return carry

        lax.fori_loop(0, n_rc, step, 0)

    def apply(w, sm, s0, sp, out_ref, flip):
        def step(i, carry):
            st = pl.multiple_of(i * rc, rc)
            acc = jnp.zeros((rc, LANES), F32)
            for di in range(3):
                for dj, s in enumerate((sm, s0, sp)):
                    kidx = (2 - di) * 3 + (2 - dj) if flip else di * 3 + dj
                    acc = acc + w[kidx:kidx + 1, :] * s[pl.ds(st + di * GRID_W, rc), :]
            out_ref[0, pl.ds(st, rc), :] = acc
            return carry

        lax.fori_loop(0, n_rc, step, 0)

    def ctx_apply(w, xc, flip):
        xm, x0, xp = shifted(xc, l_ctx, l_ctx)
        ks = (5, 4, 3) if flip else (3, 4, 5)
        return w[ks[0]:ks[0] + 1, :] * xm + w[4:5, :] * x0 + w[ks[2]:ks[2] + 1, :] * xp

    def fwd_body(x_ref, w_ref, o_ref, sm, s0, sp):
        w = w_ref[0]
        fill(x_ref, sm, s0, sp)
        apply(w, sm, s0, sp, o_ref, False)
        o_ref[0, pl.ds(l_lat, l_ctx), :] = ctx_apply(w, x_ref[0, pl.ds(l_lat, l_ctx), :], False)

    def bwd_body(x_ref, w_ref, d_ref, dx_ref, dw_ref, sm, s0, sp):
        w = w_ref[0]
        fill(x_ref, sm, s0, sp)

        def step(i, acc):
            st = pl.multiple_of(i * rc, rc)
            dc = d_ref[0, pl.ds(st, rc), :]
            rows = []
            for di in range(3):
                for s in (sm, s0, sp):
                    rows.append(jnp.sum(dc * s[pl.ds(st + di * GRID_W, rc), :], axis=0, keepdims=True))
            return acc + jnp.concatenate(rows + [jnp.zeros((7, LANES), F32)], axis=0)

        acc = lax.fori_loop(0, n_rc, step, jnp.zeros((16, LANES), F32))
        dctx = d_ref[0, pl.ds(l_lat, l_ctx), :]
        xm, x0, xp = shifted(x_ref[0, pl.ds(l_lat, l_ctx), :], l_ctx, l_ctx)
        crow = [jnp.sum(dctx * s, axis=0, keepdims=True) for s in (xm, x0, xp)]
        acc = acc + jnp.concatenate([jnp.zeros((3, LANES), F32)] + crow + [jnp.zeros((10, LANES), F32)], axis=0)
        dw_ref[0] = acc
        fill(d_ref, sm, s0, sp)
        apply(w, sm, s0, sp, dx_ref, True)
        dx_ref[0, pl.ds(l_lat, l_ctx), :] = ctx_apply(w, dctx, True)

    blk = lambda off: pl.BlockSpec((1, t, LANES), lambda j: (j + off, 0, 0))
    w_spec = pl.BlockSpec((1, 16, LANES), lambda j: (j, 0, 0))
    scratch = [pltpu.VMEM((l_lat + 2 * GRID_W, LANES), F32)] * 3
    out_t = jax.ShapeDtypeStruct((n_blocks, t, LANES), F32)
    if not bwd:
        return pl.pallas_call(
            fwd_body, name=name, grid=(n_blocks,), in_specs=[blk(first_block), w_spec], out_specs=blk(0),
            out_shape=out_t, scratch_shapes=scratch, compiler_params=_params(("arbitrary",), VMEM_BIG),
        )(src, wts)
    return pl.pallas_call(
        bwd_body, name=name, grid=(n_blocks,), in_specs=[blk(first_block), w_spec, blk(0)],
        out_specs=[blk(0), w_spec], out_shape=[out_t, jax.ShapeDtypeStruct((n_blocks, 16, LANES), F32)],
        scratch_shapes=scratch, compiler_params=_params(("arbitrary",), VMEM_BIG),
    )(src, wts, dout)


def _positions(shape, axis, rev):
    p = lax.broadcasted_iota(jnp.int32, shape, axis)
    return (CHUNK - 1 - p) if rev else p


def _hg_chunk(raw_q, raw_f, raw_i, l0, l1, st, rev):
    pr, pc = _positions((CHUNK, CHUNK), 0, rev), _positions((CHUNK, CHUNK), 1, rev)
    prow = _positions((CHUNK, LANES), 0, rev)
    lb = jax.nn.sigmoid(l0 - l1)
    q = _silu(raw_q) * (LANES ** -0.5)
    g = jnp.log(lb + (1.0 - lb) * jax.nn.sigmoid(raw_f))
    k = (1.0 - lb) * jax.nn.sigmoid(-raw_f)
    v = raw_i
    b = _mm((pc <= pr).astype(F32), g, HI)
    b_tot = jnp.sum(g, axis=0, keepdims=True)
    q_dec = q * jnp.exp(b)
    k_dec = k * jnp.exp(b_tot - b)
    a = jnp.where(pr == pc, jnp.sum(q * k, axis=1, keepdims=True), 0.0)
    half = CHUNK // 2
    while half >= 1:
        width = 2 * half
        shift = width.bit_length() - 1
        ref_pos = ((pr >> shift) << shift) + (half - 1)
        r = _mm((pc == ref_pos).astype(F32), b, HI)
        second = (prow & (width - 1)) >= half
        first = jnp.logical_not(second)
        qt = jnp.where(second, q * jnp.exp(jnp.where(second, b - r, 0.0)), 0.0)
        kt = jnp.where(first, k * jnp.exp(jnp.where(first, r - b, 0.0)), 0.0)
        same = (pr >> shift) == (pc >> shift)
        a = a + jnp.where(same, _mm_nt(qt, kt), 0.0)
        half //= 2
    o = _mm_nt(q_dec, st) + _mm(a, v)
    st_new = st * jnp.exp(b_tot) + _mm_tn(v, k_dec)
    return o, st_new


def _unit_lower_inverse(a, pr, pc):
    xinv = (pr == pc).astype(F32)
    half = 1
    while half < CHUNK:
        width = 2 * half
        shift = width.bit_length() - 1
        low_left = jnp.logical_and((pr >> shift) == (pc >> shift),
                                   jnp.logical_and((pr & (width - 1)) >= half, (pc & (width - 1)) < half))
        xinv = xinv - _mm(_mm(xinv, jnp.where(low_left, a, 0.0), HI), xinv, HI)
        half = width
    return xinv


def _gdn_chunk(u_q, u_k, u_v, tail, oh_a, oh_b, avec, dtvec, s, rev):
    pr, pc = _positions((CHUNK, CHUNK), 0, rev), _positions((CHUNK, CHUNK), 1, rev)

    def l2n(t):
        return t * lax.rsqrt(jnp.sum(t * t, axis=-1, keepdims=True) + NORM_EPS)

    q = l2n(_silu(u_q)) * (LANES ** -0.5)
    k = l2n(_silu(u_k))
    v = _silu(u_v)
    za = jnp.sum(tail * oh_a, axis=1, keepdims=True) + dtvec
    beta = jax.nn.sigmoid(jnp.sum(tail * oh_b, axis=1, keepdims=True))
    g = -jnp.exp(avec) * (jnp.maximum(za, 0.0) + jnp.log1p(jnp.exp(-jnp.abs(za))))
    b = _mm((pc <= pr).astype(F32), g, HI)
    b_tot = jnp.sum(g, axis=0, keepdims=True)
    eye = (pr == pc).astype(F32)
    b_row = _mm(jnp.ones((CHUNK, CHUNK), F32), eye * b[:, :CHUNK], HI)
    diff = b[:, :CHUNK] - b_row
    kk = _mm_nt(k, k)
    a_mat = beta * kk * jnp.exp(jnp.where(pc < pr, diff, NEG))
    xinv = _unit_lower_inverse(a_mat, pr, pc)
    rhs = jnp.concatenate([beta * v, beta * jnp.exp(b) * k], axis=1)
    sol = _mm(xinv, rhs, HI)
    u0, w = sol[:, :LANES], sol[:, LANES:]
    k_dec = k * jnp.exp(b_tot - b)
    p = _mm_nt(q, k) * jnp.exp(jnp.where(pc <= pr, diff, NEG))
    q_dec = q * jnp.exp(b)
    v_new = u0 - _mm(w, s)
    s_new = jnp.exp(b_tot) * s + _mm_tn(k_dec, v_new)
    o = _mm(q_dec, s) + _mm(p, v_new)
    return o, s_new


def _chunk_index(l_lat, l_ctx, rev):
    nl, nc = l_lat // CHUNK, l_ctx // CHUNK

    def idx(i):
        if rev:
            return jnp.where(i < nc, nl + nc - 1 - i, nl - 1 - (i - nc))
        return jnp.where(i < nc, nl + i, i - nc)

    return nl + nc, idx


def _scan_fwd(kind, rev, cbm, segs, vecs, o_prev, l_lat, l_ctx, h, name, tail=None, d_index=0):
    n_steps, cidx = _chunk_index(l_lat, l_ctx, rev)
    t = l_lat + l_ctx
    n_in = len(segs)
    gdn = kind == "gdn"

    def body(*refs):
        in_refs = refs[:n_in]
        pos = n_in
        tail_ref = None
        if gdn:
            tail_ref = refs[pos]
            pos += 1
        v0_ref, v1_ref = refs[pos], refs[pos + 1]
        pos += 2
        prev_ref = None
        if o_prev is not None:
            prev_ref = refs[pos]
            pos += 1
        o_ref, st_ref, s_scr = refs[pos], refs[pos + 1], refs[pos + 2]

        @pl.when(pl.program_id(0) == 0)
        def _():
            s_scr[...] = jnp.zeros_like(s_scr)

        def head(hh, carry):
            s = s_scr[hh]
            st_ref[0, hh] = s
            if gdn:
                lane = lax.broadcasted_iota(jnp.int32, (1, LANES), 1)
                oh_a = (lane == d_index * h + hh).astype(F32)
                oh_b = (lane == 2 * h + d_index * h + hh).astype(F32)
                o, s_new = _gdn_chunk(in_refs[0][hh], in_refs[1][hh], in_refs[2][hh], tail_ref[0], oh_a, oh_b,
                                      v0_ref[hh], v1_ref[hh], s, rev)
            else:
                o, s_new = _hg_chunk(in_refs[0][hh], in_refs[1][hh], in_refs[2][hh], v0_ref[hh], v1_ref[hh], s, rev)
            if prev_ref is not None:
                o = o + prev_ref[hh]
            o_ref[hh] = o
            s_scr[hh] = s_new
            return carry

        lax.fori_loop(0, h, head, 0, unroll=HEAD_UNROLL)

    seg_spec = lambda sg: pl.BlockSpec((h, CHUNK, LANES), lambda i: (sg, cidx(i), 0))
    vec_spec = pl.BlockSpec((h, 1, LANES), lambda i: (0, 0, 0))
    in_specs = [seg_spec(sg) for sg in segs]
    args = [cbm] * n_in
    if gdn:
        in_specs.append(pl.BlockSpec((1, CHUNK, LANES), lambda i: (0, cidx(i), 0)))
        args.append(tail)
    in_specs += [vec_spec, vec_spec]
    args += list(vecs)
    if o_prev is not None:
        in_specs.append(seg_spec(0))
        args.append(o_prev)
    return pl.pallas_call(
        body, name=name, grid=(n_steps,), in_specs=in_specs,
        out_specs=[seg_spec(0), pl.BlockSpec((1, h, LANES, LANES), lambda i: (i, 0, 0, 0))],
        out_shape=[jax.ShapeDtypeStruct((h, t, LANES), F32), jax.ShapeDtypeStruct((n_steps, h, LANES, LANES), F32)],
        scratch_shapes=[pltpu.VMEM((h, LANES, LANES), F32)],
        compiler_params=_params(("arbitrary",)),
    )(*args)


def _scan_bwd(kind, rev, cbm, segs, vecs, states, d_o, acc, l_lat, l_ctx, h, name, tail=None, d_index=0):
    n_steps, cidx = _chunk_index(l_lat, l_ctx, rev)
    t = l_lat + l_ctx
    n_in = len(segs)
    gdn = kind == "gdn"
    n_grad = n_in + (1 if gdn else 0)
    assert len(acc) == n_grad
    step_of = lambda j: n_steps - 1 - j

    def body(*refs):
        in_refs = refs[:n_in]
        pos = n_in
        tail_ref = None
        if gdn:
            tail_ref = refs[pos]
            pos += 1
        v0_ref, v1_ref, st_ref, do_ref = refs[pos:pos + 4]
        pos += 4
        acc_refs = []
        for a in acc:
            if a is None:
                acc_refs.append(None)
            else:
                acc_refs.append(refs[pos])
                pos += 1
        g_refs = refs[pos:pos + n_grad]
        dv0_ref, dv1_ref, ds_scr = refs[pos + n_grad:pos + n_grad + 3]

        @pl.when(pl.program_id(0) == 0)
        def _():
            ds_scr[...] = jnp.zeros_like(ds_scr)
            dv0_ref[...] = jnp.zeros_like(dv0_ref)
            dv1_ref[...] = jnp.zeros_like(dv1_ref)

        if gdn:
            g_refs[n_in][0] = jnp.zeros((CHUNK, LANES), F32) if acc_refs[n_in] is None else acc_refs[n_in][0]

        def head(hh, carry):
            s = st_ref[0, hh]
            ins = [r[hh] for r in in_refs]
            if gdn:
                lane = lax.broadcasted_iota(jnp.int32, (1, LANES), 1)
                oh_a = (lane == d_index * h + hh).astype(F32)
                oh_b = (lane == 2 * h + d_index * h + hh).astype(F32)
                fn = lambda uq, uk, uv, tl, av, dt, ss: _gdn_chunk(uq, uk, uv, tl, oh_a, oh_b, av, dt, ss, rev)
                prim = ins + [tail_ref[0], v0_ref[hh], v1_ref[hh], s]
            else:
                fn = lambda rq, rf, ri, l0, l1, ss: _hg_chunk(rq, rf, ri, l0, l1, ss, rev)
                prim = ins + [v0_ref[hh], v1_ref[hh], s]
            _, vj = jax.vjp(fn, *prim)
            grads = vj((do_ref[hh], ds_scr[hh]))
            for n in range(n_in):
                gn = grads[n]
                if acc_refs[n] is not None:
                    gn = gn + acc_refs[n][hh]
                g_refs[n][hh] = gn
            if gdn:
                g_refs[n_in][0] += grads[n_in]
            dv0 = grads[n_grad]
            dv1 = grads[n_grad + 1]
            if gdn:
                dv0 = jnp.broadcast_to(jnp.sum(dv0, axis=1, keepdims=True), dv0.shape)
                dv1 = jnp.broadcast_to(jnp.sum(dv1, axis=1, keepdims=True), dv1.shape)
            dv0_ref[hh] += dv0
            dv1_ref[hh] += dv1
            ds_scr[hh] = grads[n_grad + 2]
            return carry

        lax.fori_loop(0, h, head, 0, unroll=HEAD_UNROLL)

    seg_spec = lambda sg: pl.BlockSpec((h, CHUNK, LANES), lambda j: (sg, cidx(step_of(j)), 0))
    tail_spec = pl.BlockSpec((1, CHUNK, LANES), lambda j: (0, cidx(step_of(j)), 0))
    vec_spec = pl.BlockSpec((h, 1, LANES), lambda j: (0, 0, 0))
    in_specs = [seg_spec(sg) for sg in segs]
    args = [cbm] * n_in
    if gdn:
        in_specs.append(tail_spec)
        args.append(tail)
    in_specs += [vec_spec, vec_spec, pl.BlockSpec((1, h, LANES, LANES), lambda j: (step_of(j), 0, 0, 0)), seg_spec(0)]
    args += list(vecs) + [states, d_o]
    for n, a in enumerate(acc):
        if a is not None:
            in_specs.append(tail_spec if (gdn and n == n_in) else seg_spec(0))
            args.append(a)
    out_specs = [seg_spec(0)] * n_in + ([tail_spec] if gdn else []) + [vec_spec, vec_spec]
    out_shape = ([jax.ShapeDtypeStruct((h, t, LANES), F32)] * n_in
                 + ([jax.ShapeDtypeStruct((1, t, LANES), F32)] if gdn else [])
                 + [jax.ShapeDtypeStruct((h, 1, LANES), F32)] * 2)
    return pl.pallas_call(
        body, name=name, grid=(n_steps,), in_specs=in_specs, out_specs=out_specs, out_shape=out_shape,
        scratch_shapes=[pltpu.VMEM((h, LANES, LANES), F32)],
        compiler_params=_params(("arbitrary",)),
    )(*args)


def _gout_tile(oa, ob, za, zb, naw, nbw):
    def hn(o, w):
        return o * lax.rsqrt(jnp.mean(o * o, axis=-1, keepdims=True) + NORM_EPS) * w

    ya = _silu(za) * hn(oa, naw)
    yb = _silu(zb) * hn(ob, nbw)
    nh = oa.shape[0]
    return jnp.concatenate([ya[i] for i in range(nh)] + [yb[i] for i in range(nh)], axis=1)


def _gout_fwd(oa, ob, y_cbm, naw, nbw, l_lat, h):
    tm = _pick(l_lat, (256, 128, 64))
    blk = lambda sg: pl.BlockSpec((h, tm, LANES), lambda i: (sg, i, 0))
    vec = pl.BlockSpec((h, 1, LANES), lambda i: (0, 0, 0))

    def body(oa_ref, ob_ref, za_ref, zb_ref, na_ref, nb_ref, y_ref):
        y_ref[...] = _gout_tile(oa_ref[...], ob_ref[...], za_ref[...], zb_ref[...], na_ref[...], nb_ref[...]).astype(BF16)

    return pl.pallas_call(
        body, name="gout_fwd", grid=(l_lat // tm,),
        in_specs=[blk(0), blk(0), blk(4), blk(8), vec, vec],
        out_specs=pl.BlockSpec((tm, 2 * h * LANES), lambda i: (i, 0)),
        out_shape=jax.ShapeDtypeStruct((l_lat, 2 * h * LANES), BF16),
        compiler_params=_params(("arbitrary",)),
    )(oa, ob, y_cbm, y_cbm, naw, nbw)


def _gout_bwd(oa, ob, y_cbm, naw, nbw, dymix, l_lat, l_ctx, h):
    tm = _pick(l_ctx, (256, 128, 64))
    nl, nc = l_lat // tm, l_ctx // tm
    t = l_lat + l_ctx
    blk_in = lambda sg: pl.BlockSpec((h, tm, LANES), lambda i: (sg, jnp.minimum(i, nl - 1), 0))
    blk_out = pl.BlockSpec((h, tm, LANES), lambda i: (0, i, 0))
    vec = pl.BlockSpec((h, 1, LANES), lambda i: (0, 0, 0))

    def body(oa_ref, ob_ref, za_ref, zb_ref, na_ref, nb_ref, dy_ref, doa_ref, dob_ref, dza_ref, dzb_ref, dna_ref, dnb_ref):
        i = pl.program_id(0)

        @pl.when(i == 0)
        def _():
            dna_ref[...] = jnp.zeros_like(dna_ref)
            dnb_ref[...] = jnp.zeros_like(dnb_ref)

        @pl.when(i < nl)
        def _():
            _, vj = jax.vjp(_gout_tile, oa_ref[...], ob_ref[...], za_ref[...], zb_ref[...], na_ref[...], nb_ref[...])
            doa, dob, dza, dzb, dna, dnb = vj(dy_ref[...])
            doa_ref[...] = doa
            dob_ref[...] = dob
            dza_ref[...] = dza
            dzb_ref[...] = dzb
            dna_ref[...] += dna
            dnb_ref[...] += dnb

        @pl.when(i >= nl)
        def _():
            for r in (doa_ref, dob_ref, dza_ref, dzb_ref):
                r[...] = jnp.zeros_like(r)

    big = jax.ShapeDtypeStruct((h, t, LANES), F32)
    small = jax.ShapeDtypeStruct((h, 1, LANES), F32)
    return pl.pallas_call(
        body, name="gout_bwd", grid=(nl + nc,),
        in_specs=[blk_in(0), blk_in(0), blk_in(4), blk_in(8), vec, vec,
                  pl.BlockSpec((tm, 2 * h * LANES), lambda i: (jnp.minimum(i, nl - 1), 0))],
        out_specs=[blk_out] * 4 + [vec, vec], out_shape=[big] * 4 + [small, small],
        compiler_params=_params(("arbitrary",)),
    )(oa, ob, y_cbm, y_cbm, naw, nbw, dymix)


def _head_tile(xt, mixt, tgt, gt, fw):
    xo = xt + gt * mixt
    y = xo * lax.rsqrt(jnp.mean(xo * xo, axis=-1, keepdims=True) + NORM_EPS) * fw
    err = y - tgt
    return 0.5 * jnp.sum(jnp.mean(err * err, axis=-1, keepdims=True), axis=0, keepdims=True)


def _loss_head(x, mix, target, mods, final_w):
    l_lat, d = x.shape
    tm = _pick(l_lat, (256, 128, 64))
    tok = pl.BlockSpec((tm, d), lambda i: (i, 0))
    row = pl.BlockSpec((1, d), lambda i: (0, 0))

    def body(x_ref, m_ref, t_ref, mod_ref, fw_ref, loss_ref, dmix_ref, dx_ref, dgt_ref, dfw_ref):
        @pl.when(pl.program_id(0) == 0)
        def _():
            loss_ref[...] = jnp.zeros_like(loss_ref)
            dgt_ref[...] = jnp.zeros_like(dgt_ref)
            dfw_ref[...] = jnp.zeros_like(dfw_ref)

        gt = mod_ref[0:1, 2 * d:3 * d]
        fn = lambda xt, mt, g, fw: _head_tile(xt, mt, t_ref[...], g, fw)
        val, vj = jax.vjp(fn, x_ref[...], m_ref[...], gt, fw_ref[...])
        dxt, dmt, dgt, dfw = vj(jnp.ones((1, 1), F32))
        loss_ref[...] += jnp.broadcast_to(val, loss_ref.shape)
        dmix_ref[...] = dmt.astype(BF16)
        dx_ref[...] = dxt
        dgt_ref[...] += dgt
        dfw_ref[...] += dfw

    return pl.pallas_call(
        body, name="loss_head", grid=(l_lat // tm,),
        in_specs=[tok, tok, tok, pl.BlockSpec((8, 3 * d), lambda i: (0, 0)), row],
        out_specs=[pl.BlockSpec((8, LANES), lambda i: (0, 0)), tok, tok, row, row],
        out_shape=[jax.ShapeDtypeStruct((8, LANES), F32), jax.ShapeDtypeStruct((l_lat, d), BF16),
                   jax.ShapeDtypeStruct((l_lat, d), F32), jax.ShapeDtypeStruct((1, d), F32),
                   jax.ShapeDtypeStruct((1, d), F32)],
        compiler_params=_params(("arbitrary",)),
    )(x, mix, target, mods, final_w)


def _adamw(w, g, m, v, name):
    rows, cols = w.shape
    tr = _pick(rows, (256, 128, 64, 32, 16, 8))

    def body(w_ref, g_ref, m_ref, v_ref, d_ref, nm_ref, nv_ref):
        gg = g_ref[...]
        m2 = ADAM_B1 * m_ref[...] + (1.0 - ADAM_B1) * gg
        v2 = ADAM_B2 * v_ref[...] + (1.0 - ADAM_B2) * (gg * gg)
        m_hat = m2 / (1.0 - ADAM_B1 ** ADAM_STEP)
        v_hat = v2 / (1.0 - ADAM_B2 ** ADAM_STEP)
        d_ref[...] = -ADAM_LR * (m_hat / (jnp.sqrt(v_hat) + ADAM_EPS) + ADAM_WD * w_ref[...])
        nm_ref[...] = m2
        nv_ref[...] = v2

    spec = pl.BlockSpec((tr, cols), lambda i: (i, 0))
    shp = jax.ShapeDtypeStruct((rows, cols), F32)
    return pl.pallas_call(
        body, name=name, grid=(rows // tr,), in_specs=[spec] * 4, out_specs=[spec] * 3, out_shape=[shp] * 3,
        compiler_params=_params(("arbitrary",)),
    )(w, g, m, v)


def _adamw_nd(w, g, m, v, name):
    shape = w.shape
    two_d = (-1, shape[-1])
    outs = _adamw(w.reshape(two_d), g.reshape(two_d), m.reshape(two_d), v.reshape(two_d), name)
    return [o.reshape(shape) for o in outs]


def kernel(x, c, ctx, c_ctx, norm_w, ada_w, ada_b, w_in, conv_w, hg_lb_logits, gdn_a_log, gdn_dt_bias, ha_norm_w, hb_norm_w, w_out, final_norm_w, loss_target, m_c_ctx, m_norm_w, m_ada_w, m_ada_b, m_w_in, m_conv_w, m_hg_lb_logits, m_gdn_a_log, m_gdn_dt_bias, m_ha_norm_w, m_hb_norm_w, m_w_out, m_final_norm_w, v_c_ctx, v_norm_w, v_ada_w, v_ada_b, v_w_in, v_conv_w, v_hg_lb_logits, v_gdn_a_log, v_gdn_dt_bias, v_ha_norm_w, v_hb_norm_w, v_w_out, v_final_norm_w):
    l_lat, d = x.shape[1], x.shape[2]
    l_ctx = ctx.shape[1]
    t = l_lat + l_ctx
    h = d // LANES
    n_in = 9 * d + 4 * h
    nb = -(-(9 * h + 1) // COLS_PER_TILE) * COLS_PER_TILE
    n_pad = nb * LANES
    win_c, ada_c, conv_c, lb_c, wout_r = w_in.shape[2], ada_w.shape[2], conv_w.shape[3], hg_lb_logits.shape[2], w_out.shape[1]
    me = _me()

    blob16, off16 = _to_blob([w_in[0], w_out[0]], BF16, 16)
    blob32, off32 = _to_blob([ada_w[0], conv_w[0], hg_lb_logits], F32, 8)
    g16 = _exchange(blob16, "gather_bf16", True).reshape(N_DEV, -1)
    g32 = _exchange(blob32, "gather_f32", True).reshape(N_DEV, -1)

    def whole(g, off, shape, axis):
        n = 1
        for s in shape:
            n *= s
        parts = g[:, off:off + n].reshape((N_DEV,) + tuple(shape))
        full = list(shape)
        full[axis] *= N_DEV
        return jnp.moveaxis(parts, 0, axis).reshape(tuple(full))

    w_in_full = whole(g16, off16[0], (d, win_c), 1)
    w_out_full = whole(g16, off16[1], (wout_r, d), 0)
    ada_full = whole(g32, off32[0], (d, ada_c), 1)
    conv_full = whole(g32, off32[1], (3, 3, conv_c), 2)
    lb_full = whole(g32, off32[2], (2, 2, lb_c), 2)
    w_pad = jnp.pad(w_in_full, ((0, 0), (0, n_pad - n_in)))
    w_pad_t = w_pad.T
    w_out_t = w_out_full.T

    lbl = lb_full.reshape(2, 2, h, 1, LANES)
    conv_rows = jnp.pad(conv_full.reshape(9, 3 * h, LANES).transpose(1, 0, 2), ((0, 0), (0, 7), (0, 0)))
    lane_bc = lambda a: jnp.broadcast_to(a.reshape(2, h, 1, 1), (2, h, 1, LANES))
    avec, dtvec = lane_bc(gdn_a_log[0]), lane_bc(gdn_dt_bias[0])
    naw, nbw = ha_norm_w[0].reshape(h, 1, LANES), hb_norm_w[0].reshape(h, 1, LANES)

    cvec = jnp.concatenate([c, c_ctx[None, :], jnp.zeros((6, d), F32)], axis=0)
    mods = _ada_fwd(cvec, ada_full, ada_b)
    x2, ctx2, tgt2 = x[0], ctx[0], loss_target[0]
    h_all = _prenorm_fwd(x2, ctx2, norm_w, mods)
    y = _matmul(h_all, w_pad, "proj", out_cbm=True, tn=COLS_PER_TILE * LANES)
    tail = lax.slice_in_dim(y, 9 * h, 9 * h + 1, axis=0)
    u = _conv_call(y, conv_rows, l_lat, 5 * h, 3 * h, "conv_fwd")

    oa, st_af = _scan_fwd("hg", False, y, (0, 1, 3), (lbl[0, 0], lbl[1, 0]), None, l_lat, l_ctx, h, "hg_fwd_f")
    oa, st_ab = _scan_fwd("hg", True, y, (0, 2, 3), (lbl[0, 1], lbl[1, 1]), oa, l_lat, l_ctx, h, "hg_fwd_b")
    ob, st_bf = _scan_fwd("gdn", False, u, (0, 1, 2), (avec[0], dtvec[0]), None, l_lat, l_ctx, h, "gdn_fwd_f", tail, 0)
    ob, st_bb = _scan_fwd("gdn", True, u, (0, 1, 2), (avec[1], dtvec[1]), ob, l_lat, l_ctx, h, "gdn_fwd_b", tail, 1)

    ymix = _gout_fwd(oa, ob, y, naw, nbw, l_lat, h)
    mix = _matmul(ymix, w_out_full, "out_proj")
    loss_blk, dmix, dx_res, dgt, dfw = _loss_head(x2, mix, tgt2, mods, final_norm_w.reshape(1, d))

    dymix = _matmul(dmix, w_out_t, "d_ymix")
    dw_out = _matmul(ymix.T, dmix, "d_w_out")
    doa, dob, dza, dzb, dnaw, dnbw = _gout_bwd(oa, ob, y, naw, nbw, dymix, l_lat, l_ctx, h)

    gq, gff, gi, dl0f, dl1f = _scan_bwd("hg", False, y, (0, 1, 3), (lbl[0, 0], lbl[1, 0]), st_af, doa,
                                        [None, None, None], l_lat, l_ctx, h, "hg_bwd_f")
    gq, gfb, gi, dl0b, dl1b = _scan_bwd("hg", True, y, (0, 2, 3), (lbl[0, 1], lbl[1, 1]), st_ab, doa,
                                        [gq, None, gi], l_lat, l_ctx, h, "hg_bwd_b")
    guq, guk, guv, gtail, da_f, ddt_f = _scan_bwd("gdn", False, u, (0, 1, 2), (avec[0], dtvec[0]), st_bf, dob,
                                                  [None] * 4, l_lat, l_ctx, h, "gdn_bwd_f", tail, 0)
    guq, guk, guv, gtail, da_b, ddt_b = _scan_bwd("gdn", True, u, (0, 1, 2), (avec[1], dtvec[1]), st_bb, dob,
                                                  [guq, guk, guv, gtail], l_lat, l_ctx, h, "gdn_bwd_b", tail, 1)
    du = jnp.concatenate([guq, guk, guv], axis=0)
    dconv_in, dconv_rows = _conv_call(y, conv_rows, l_lat, 5 * h, 3 * h, "conv_bwd", dout=du)

    pad_blocks = nb - 9 * h - 1
    dy = jnp.concatenate([gq, gff, gfb, gi, dza, dconv_in, dzb, gtail]
                         + ([jnp.zeros((pad_blocks, t, LANES), F32)] if pad_blocks else []), axis=0)
    dh = _matmul(dy, w_pad_t, "d_h", a_cbm=True)
    dw_in = _matmul(h_all.T, dy, "d_w_in", b_cbm=True)
    grad_x, dnorm_w, dmods_pre = _prenorm_bwd(x2, ctx2, norm_w, mods, dh, dx_res)

    dmods = jnp.concatenate([dmods_pre[:, :2 * d],
                             jnp.concatenate([dgt, jnp.zeros((7, d), F32)], axis=0)], axis=1)
    dcond = _ada_bwd(cvec, ada_full, dmods)

    dw_in_s = dw_in[:, :n_in].reshape(d, N_DEV, win_c).transpose(1, 0, 2)
    dw_out_s = dw_out.reshape(N_DEV, wout_r, d)
    dconv_s = dconv_rows[:, :9, :].transpose(1, 0, 2).reshape(9, N_DEV, conv_c).transpose(1, 0, 2)
    dlb = jnp.stack([jnp.stack([dl0f, dl0b]), jnp.stack([dl1f, dl1b])]).reshape(2, 2, N_DEV, lb_c).transpose(2, 0, 1, 3)
    per_dest = jnp.concatenate([dw_in_s.reshape(N_DEV, -1), dw_out_s.reshape(N_DEV, -1),
                                dconv_s.reshape(N_DEV, -1), dlb.reshape(N_DEV, -1)], axis=1)
    n_rs = per_dest.shape[1]
    rs_rows = -(-n_rs // (8 * LANES)) * 8
    per_dest = jnp.pad(per_dest, ((0, 0), (0, rs_rows * LANES - n_rs))).reshape(N_DEV, rs_rows, LANES)
    g_sh = _sum_slots(_exchange(per_dest, "scatter_grads", False), "sum_grads").reshape(-1)
    o = 0
    g_w_in = g_sh[o:o + d * win_c].reshape(1, d, win_c)
    o += d * win_c
    g_w_out = g_sh[o:o + wout_r * d].reshape(1, wout_r, d)
    o += wout_r * d
    g_conv = g_sh[o:o + 9 * conv_c].reshape(1, 3, 3, conv_c)
    o += 9 * conv_c
    g_lb = g_sh[o:o + 4 * lb_c].reshape(2, 2, lb_c)

    da = jnp.stack([da_f[:, 0, 0], da_b[:, 0, 0]])
    ddt = jnp.stack([ddt_f[:, 0, 0], ddt_b[:, 0, 0]])
    small_parts = [dcond[1], dnorm_w, dmods[0] + dmods[1], dnaw, dnbw, dfw, da, ddt, c[0], dmods[0], dmods[1]]
    small_blob, soff = _to_blob(small_parts, F32, 8)
    gathered = _exchange(small_blob, "gather_small", True)
    summed = _sum_slots(gathered, "sum_small").reshape(-1)
    gflat = gathered.reshape(N_DEV, -1)
    take = lambda k, n: summed[soff[k]:soff[k] + n]
    g_c_ctx = take(0, d)
    g_norm_w = take(1, d).reshape(1, d)
    g_ada_b = take(2, 3 * d).reshape(1, 3 * d)
    g_ha = take(3, d).reshape(1, h, LANES)
    g_hb = take(4, d).reshape(1, h, LANES)
    g_final = take(5, d)
    g_a_log = take(6, 2 * h).reshape(1, 2, h)
    g_dt = take(7, 2 * h).reshape(1, 2, h)
    conds = jnp.concatenate([gflat[:, soff[8]:soff[8] + d], c_ctx[None, :], jnp.zeros((7, d), F32)], axis=0)
    col0 = me * ada_c
    dm_lat = lax.dynamic_slice_in_dim(gflat[:, soff[9]:soff[9] + 3 * d], col0, ada_c, axis=1)
    dm_ctx = lax.dynamic_slice_in_dim(gflat[:, soff[10]:soff[10] + 3 * d], col0, ada_c, axis=1)
    g_ada_w = _ada_wgrad(conds, dm_lat, dm_ctx).reshape(1, d, ada_c)

    loss = lax.psum(loss_blk[0, 0], ("x", "y", "c"))

    weights = [("c_ctx", c_ctx, g_c_ctx, m_c_ctx, v_c_ctx), ("norm_w", norm_w, g_norm_w, m_norm_w, v_norm_w),
               ("ada_w", ada_w, g_ada_w, m_ada_w, v_ada_w), ("ada_b", ada_b, g_ada_b, m_ada_b, v_ada_b),
               ("w_in", w_in, g_w_in, m_w_in, v_w_in), ("conv_w", conv_w, g_conv, m_conv_w, v_conv_w),
               ("hg_lb_logits", hg_lb_logits, g_lb, m_hg_lb_logits, v_hg_lb_logits),
               ("gdn_a_log", gdn_a_log, g_a_log, m_gdn_a_log, v_gdn_a_log),
               ("gdn_dt_bias", gdn_dt_bias, g_dt, m_gdn_dt_bias, v_gdn_dt_bias),
               ("ha_norm_w", ha_norm_w, g_ha, m_ha_norm_w, v_ha_norm_w), ("hb_norm_w", hb_norm_w, g_hb, m_hb_norm_w, v_hb_norm_w),
               ("w_out", w_out, g_w_out, m_w_out, v_w_out), ("final_norm_w", final_norm_w, g_final, m_final_norm_w, v_final_norm_w)]
    grads, deltas, new_ms, new_vs = [], [], [], []
    for nm, w, g, m, v in weights:
        dl, m2, v2 = _adamw_nd(w, g, m, v, "adamw_" + nm)
        grads.append(g)
        deltas.append(dl)
        new_ms.append(m2)
        new_vs.append(v2)
    return (loss, grad_x[None], *grads, *deltas, *new_ms, *new_vs)
```

```python
import functools

import jax
import jax.numpy as jnp
from jax import lax
from jax.experimental import pallas as pl
from jax.experimental.pallas import tpu as pltpu

F32 = jnp.float32
BF16 = jnp.bfloat16
HI = lax.Precision.HIGHEST
HIGH = lax.Precision.HIGH
MESH = pl.DeviceIdType.MESH
ANY = pl.BlockSpec(memory_space=pl.ANY)

N_DEV = 8
LANES = 128
CHUNK = 64
GRID_W = 64
NORM_EPS = 1e-6
NEG = -1e30
COLS_PER_TILE = 8
VMEM_BIG = 56 * 1024 * 1024
HEADS_PER_GROUP = 4
GROUP_UNROLL = 2
HALVES = (32, 16, 8, 4, 2, 1)

ADAM_LR, ADAM_B1, ADAM_B2, ADAM_EPS, ADAM_WD, ADAM_STEP = 0.001, 0.9, 0.999, 1e-08, 0.01, 10


def _params(sem=None, vmem=None):
    kw = {}
    if sem is not None:
        kw["dimension_semantics"] = sem
    if vmem is not None:
        kw["vmem_limit_bytes"] = vmem
    return pltpu.CompilerParams(**kw)


def _pick(n, cands):
    for c in cands:
        if n % c == 0:
            return c
    return n


def _silu(x):
    return x * jax.nn.sigmoid(x)


def _mm(a, b, prec=None):
    return lax.dot_general(a, b, (((1,), (0,)), ((), ())), precision=prec, preferred_element_type=F32)


def _mm_nt(a, b, prec=None):
    return lax.dot_general(a, b, (((1,), (1,)), ((), ())), precision=prec, preferred_element_type=F32)


def _mm_tn(a, b, prec=None):
    return lax.dot_general(a, b, (((0,), (0,)), ((), ())), precision=prec, preferred_element_type=F32)


def _me():
    return 4 * lax.axis_index("x") + 2 * lax.axis_index("y") + lax.axis_index("c")


def _peer(r):
    x, y, c = lax.axis_index("x"), lax.axis_index("y"), lax.axis_index("c")
    px = 1 - x if r & 4 else x
    py = 1 - y if r & 2 else y
    pc = 1 - c if r & 1 else c
    return (px, py, pc), 4 * px + 2 * py + pc


def _exchange(arrs, name, gather):
    n_a = len(arrs)

    def body(*refs):
        srcs, outs = refs[:n_a], refs[n_a:2 * n_a]
        send_sems, recv_sems, local_sems = refs[2 * n_a:]
        me = _me()
        copies = []
        for k in range(n_a):
            local = pltpu.make_async_copy(srcs[k] if gather else srcs[k].at[me], outs[k].at[me], local_sems.at[k])
            local.start()
            copies.append(local)
        for r in range(N_DEV - 1, 0, -1):
            dev, pid = _peer(r)
            for k in range(n_a):
                cp = pltpu.make_async_remote_copy(
                    src_ref=srcs[k] if gather else srcs[k].at[pid], dst_ref=outs[k].at[me],
                    send_sem=send_sems.at[(r - 1) * n_a + k], recv_sem=recv_sems.at[(r - 1) * n_a + k],
                    device_id=dev, device_id_type=MESH)
                cp.start()
                copies.append(cp)
        for cp in copies:
            cp.wait()

    n_sem = (N_DEV - 1) * n_a
    return pl.pallas_call(
        body, name=name,
        out_shape=[jax.ShapeDtypeStruct(((N_DEV,) + a.shape) if gather else a.shape, a.dtype) for a in arrs],
        in_specs=[ANY] * n_a, out_specs=[ANY] * n_a,
        scratch_shapes=[pltpu.SemaphoreType.DMA((n_sem,)), pltpu.SemaphoreType.DMA((n_sem,)),
                        pltpu.SemaphoreType.DMA((n_a,))],
    )(*arrs)


def _to_blob(parts):
    flat = [p.reshape(-1).astype(F32) for p in parts]
    offs, n = [], 0
    for f in flat:
        offs.append(n)
        n += f.shape[0]
    unit = 8 * LANES
    total = -(-n // unit) * unit
    if total > n:
        flat.append(jnp.zeros((total - n,), F32))
    return jnp.concatenate(flat).reshape(total // LANES, LANES), offs


def _sum_slots(buf, name):
    n, rows, cols = buf.shape
    tr = _pick(rows, (512, 256, 128, 64, 32, 16, 8))

    def body(b_ref, o_ref):
        acc = b_ref[0]
        for s in range(1, n):
            acc = acc + b_ref[s]
        o_ref[...] = acc

    return pl.pallas_call(
        body, name=name, grid=(rows // tr,),
        in_specs=[pl.BlockSpec((n, tr, cols), lambda i: (0, i, 0))],
        out_specs=pl.BlockSpec((tr, cols), lambda i: (i, 0)),
        out_shape=jax.ShapeDtypeStruct((rows, cols), F32),
        compiler_params=_params(("arbitrary",)),
    )(buf)


def _matmul(a, b, name, a_cbm=False, b_cbm=False, out_cbm=False, out_dtype=F32, tm=None, tn=None, tk=None):
    if a_cbm:
        m, k = a.shape[1], a.shape[0] * LANES
    else:
        m, k = a.shape
    n = b.shape[0] * LANES if b_cbm else b.shape[1]
    tm = tm or _pick(m, (1024, 768, 512, 384, 256, 128, 64))
    tn = tn or _pick(n, (1024, 512, 256, 128))
    tk = tk or _pick(k, (2048, 1024, 768, 512, 256, 128))
    nk = k // tk

    def load(ref, cbm):
        if not cbm:
            return ref[...].astype(BF16)
        return jnp.concatenate([ref[j].astype(BF16) for j in range(ref.shape[0])], axis=1)

    def body(a_ref, b_ref, o_ref, acc_ref):
        kk = pl.program_id(2)
        part = _mm(load(a_ref, a_cbm), load(b_ref, b_cbm))

        @pl.when(kk == 0)
        def _():
            acc_ref[...] = part

        @pl.when(kk > 0)
        def _():
            acc_ref[...] += part

        @pl.when(kk == nk - 1)
        def _():
            r = acc_ref[...]
            if out_cbm:
                for j in range(tn // LANES):
                    o_ref[j] = r[:, j * LANES:(j + 1) * LANES].astype(out_dtype)
            else:
                o_ref[...] = r.astype(out_dtype)

    if a_cbm:
        a_spec = pl.BlockSpec((tk // LANES, tm, LANES), lambda j, i, kk: (kk, i, 0))
    else:
        a_spec = pl.BlockSpec((tm, tk), lambda j, i, kk: (i, kk))
    if b_cbm:
        b_spec = pl.BlockSpec((tn // LANES, tk, LANES), lambda j, i, kk: (j, kk, 0))
    else:
        b_spec = pl.BlockSpec((tk, tn), lambda j, i, kk: (kk, j))
    if out_cbm:
        o_spec = pl.BlockSpec((tn // LANES, tm, LANES), lambda j, i, kk: (j, i, 0))
        o_shape = jax.ShapeDtypeStruct((n // LANES, m, LANES), out_dtype)
    else:
        o_spec = pl.BlockSpec((tm, tn), lambda j, i, kk: (i, j))
        o_shape = jax.ShapeDtypeStruct((m, n), out_dtype)
    return pl.pallas_call(
        body, name=name, grid=(n // tn, m // tm, nk),
        in_specs=[a_spec, b_spec], out_specs=o_spec, out_shape=o_shape,
        scratch_shapes=[pltpu.VMEM((tm, tn), F32)],
        compiler_params=_params(("arbitrary", "arbitrary", "arbitrary"), VMEM_BIG),
    )(a, b)


def _ada_fwd(cvec, ada_w, ada_b):
    def body(c_ref, w_ref, b_ref, o_ref):
        o_ref[...] = _mm(_silu(c_ref[...]), w_ref[...], HI) + b_ref[...]

    return pl.pallas_call(
        body, name="ada_fwd", out_shape=jax.ShapeDtypeStruct((8, ada_w.shape[1]), F32),
        compiler_params=_params(None, VMEM_BIG),
    )(cvec, ada_w, ada_b)


def _ada_bwd(cvec, ada_w, dmods):
    def body(c_ref, w_ref, d_ref, o_ref):
        (_, vj) = jax.vjp(_silu, c_ref[...])
        o_ref[...] = vj(_mm_nt(d_ref[...], w_ref[...], HI))[0]

    return pl.pallas_call(
        body, name="ada_bwd", out_shape=jax.ShapeDtypeStruct(cvec.shape, F32),
        compiler_params=_params(None, VMEM_BIG),
    )(cvec, ada_w, dmods)


def _ada_wgrad(conds, dm_lat, dm_ctx):
    d = conds.shape[1]
    cols = dm_lat.shape[1]

    def body(c_ref, dl_ref, dc_ref, o_ref):
        dctx = dc_ref[0:1]
        for s in range(1, N_DEV):
            dctx = dctx + dc_ref[s:s + 1]
        rhs = jnp.concatenate([dl_ref[...], dctx, jnp.zeros((7, cols), F32)], axis=0)
        o_ref[...] = _mm_tn(_silu(c_ref[...]), rhs, HI)

    return pl.pallas_call(
        body, name="ada_wgrad", out_shape=jax.ShapeDtypeStruct((d, cols), F32),
    )(conds, dm_lat, dm_ctx)


def _prenorm_tile(xt, nw, sc, sh):
    r = lax.rsqrt(jnp.mean(xt * xt, axis=-1, keepdims=True) + NORM_EPS)
    return (xt * r * nw) * (1.0 + sc) + sh


def _tok_specs(l_lat, l_ctx, d, tm):
    nl, nc = l_lat // tm, l_ctx // tm
    lat = pl.BlockSpec((tm, d), lambda i: (jnp.minimum(i, nl - 1), 0))
    ctx = pl.BlockSpec((tm, d), lambda i: (jnp.maximum(i - nl, 0), 0))
    return nl, nc, lat, ctx


def _prenorm_fwd(x, ctx, norm_w, mods):
    l_lat, d = x.shape
    l_ctx = ctx.shape[0]
    tm = _pick(l_ctx, (256, 128, 64))
    nl, nc, lat_spec, ctx_spec = _tok_specs(l_lat, l_ctx, d, tm)

    def body(x_ref, c_ref, nw_ref, m_ref, h_ref):
        is_lat = pl.program_id(0) < nl
        xt = jnp.where(is_lat, x_ref[...], c_ref[...])
        row = jnp.where(is_lat, m_ref[0:1, :], m_ref[1:2, :])
        h_ref[...] = _prenorm_tile(xt, nw_ref[...], row[:, d:2 * d], row[:, 0:d]).astype(BF16)

    return pl.pallas_call(
        body, name="prenorm_fwd", grid=(nl + nc,),
        in_specs=[lat_spec, ctx_spec, pl.BlockSpec((1, d), lambda i: (0, 0)), pl.BlockSpec((8, 3 * d), lambda i: (0, 0))],
        out_specs=pl.BlockSpec((tm, d), lambda i: (i, 0)),
        out_shape=jax.ShapeDtypeStruct((l_lat + l_ctx, d), BF16),
        compiler_params=_params(("arbitrary",)),
    )(x, ctx, norm_w, mods)


def _prenorm_bwd(x, ctx, norm_w, mods, dh, dx_res):
    l_lat, d = x.shape
    l_ctx = ctx.shape[0]
    tm = _pick(l_ctx, (256, 128, 64))
    nl, nc, lat_spec, ctx_spec = _tok_specs(l_lat, l_ctx, d, tm)

    def body(x_ref, c_ref, nw_ref, m_ref, dh_ref, dr_ref, gx_ref, dnw_ref, dm_ref):
        i = pl.program_id(0)
        is_lat = i < nl

        @pl.when(i == 0)
        def _():
            dnw_ref[...] = jnp.zeros_like(dnw_ref)
            dm_ref[...] = jnp.zeros_like(dm_ref)

        xt = jnp.where(is_lat, x_ref[...], c_ref[...])
        row = jnp.where(is_lat, m_ref[0:1, :], m_ref[1:2, :])
        _, vj = jax.vjp(_prenorm_tile, xt, nw_ref[...], row[:, d:2 * d], row[:, 0:d])
        dxt, dnw, dsc, dsh = vj(dh_ref[...])
        dnw_ref[...] += dnw
        upd = jnp.concatenate([dsh, dsc, jnp.zeros_like(dsh)], axis=1)

        @pl.when(is_lat)
        def _():
            gx_ref[...] = dr_ref[...] + dxt
            dm_ref[0:1, :] += upd

        @pl.when(jnp.logical_not(is_lat))
        def _():
            dm_ref[1:2, :] += upd

    return pl.pallas_call(
        body, name="prenorm_bwd", grid=(nl + nc,),
        in_specs=[lat_spec, ctx_spec, pl.BlockSpec((1, d), lambda i: (0, 0)), pl.BlockSpec((8, 3 * d), lambda i: (0, 0)),
                  pl.BlockSpec((tm, d), lambda i: (i, 0)), lat_spec],
        out_specs=[lat_spec, pl.BlockSpec((1, d), lambda i: (0, 0)), pl.BlockSpec((8, 3 * d), lambda i: (0, 0))],
        out_shape=[jax.ShapeDtypeStruct((l_lat, d), F32), jax.ShapeDtypeStruct((1, d), F32),
                   jax.ShapeDtypeStruct((8, 3 * d), F32)],
        compiler_params=_params(("arbitrary",)),
    )(x, ctx, norm_w, mods, dh, dx_res)


def _conv_call(src, wts, l_lat, first_block, n_blocks, name, dout=None):
    t = src.shape[1]
    l_ctx = t - l_lat
    rc = _pick(l_lat, (512, 256, 128, 64))
    n_rc = l_lat // rc
    bwd = dout is not None

    def shifted(xc, n, period):
        pos = lax.broadcasted_iota(jnp.int32, (n, LANES), 0) % period
        xm = jnp.where(pos == 0, 0.0, pltpu.roll(xc, 1, 0))
        xp = jnp.where(pos == period - 1, 0.0, pltpu.roll(xc, n - 1, 0))
        return xm, xc, xp

    def fill(ref, sm, s0, sp):
        zero = jnp.zeros((GRID_W, LANES), F32)
        for s in (sm, s0, sp):
            s[pl.ds(0, GRID_W), :] = zero
            s[pl.ds(l_lat + GRID_W, GRID_W), :] = zero

        def step(i, carry):
            st = pl.multiple_of(i * rc, rc)
            xm, x0, xp = shifted(ref[0, pl.ds(st, rc), :], rc, GRID_W)
            sm[pl.ds(st + GRID_W, rc), :] = xm
            s0[pl.ds(st + GRID_W, rc), :] = x0
            sp[pl.ds(st + GRID_W, rc), :] = xp
            return carry

        lax.fori_loop(0, n_rc, step, 0)

    def apply(w, sm, s0, sp, out_ref, flip):
        def step(i, carry):
            st = pl.multiple_of(i * rc, rc)
            acc = jnp.zeros((rc, LANES), F32)
            for di in range(3):
                for dj, s in enumerate((sm, s0, sp)):
                    kidx = (2 - di) * 3 + (2 - dj) if flip else di * 3 + dj
                    acc = acc + w[kidx:kidx + 1, :] * s[pl.ds(st + di * GRID_W, rc), :]
            out_ref[0, pl.ds(st, rc), :] = acc
            return carry

        lax.fori_loop(0, n_rc, step, 0)

    def ctx_apply(w, xc, flip):
        xm, x0, xp = shifted(xc, l_ctx, l_ctx)
        ks = (5, 4, 3) if flip else (3, 4, 5)
        return w[ks[0]:ks[0] + 1, :] * xm + w[4:5, :] * x0 + w[ks[2]:ks[2] + 1, :] * xp

    def fwd_body(x_ref, w_ref, o_ref, sm, s0, sp):
        w = w_ref[0]
        fill(x_ref, sm, s0, sp)
        apply(w, sm, s0, sp, o_ref, False)
        o_ref[0, pl.ds(l_lat, l_ctx), :] = ctx_apply(w, x_ref[0, pl.ds(l_lat, l_ctx), :], False)

    def bwd_body(x_ref, w_ref, d_ref, dx_ref, dw_ref, sm, s0, sp):
        w = w_ref[0]
        fill(x_ref, sm, s0, sp)

        def step(i, acc):
            st = pl.multiple_of(i * rc, rc)
            dc = d_ref[0, pl.ds(st, rc), :]
            rows = []
            for di in range(3):
                for s in (sm, s0, sp):
                    rows.append(jnp.sum(dc * s[pl.ds(st + di * GRID_W, rc), :], axis=0, keepdims=True))
            return acc + jnp.concatenate(rows + [jnp.zeros((7, LANES), F32)], axis=0)

        acc = lax.fori_loop(0, n_rc, step, jnp.zeros((16, LANES), F32))
        dctx = d_ref[0, pl.ds(l_lat, l_ctx), :]
        xm, x0, xp = shifted(x_ref[0, pl.ds(l_lat, l_ctx), :], l_ctx, l_ctx)
        crow = [jnp.sum(dctx * s, axis=0, keepdims=True) for s in (xm, x0, xp)]
        acc = acc + jnp.concatenate([jnp.zeros((3, LANES), F32)] + crow + [jnp.zeros((10, LANES), F32)], axis=0)
        dw_ref[0] = acc
        fill(d_ref, sm, s0, sp)
        apply(w, sm, s0, sp, dx_ref, True)
        dx_ref[0, pl.ds(l_lat, l_ctx), :] = ctx_apply(w, dctx, True)

    blk = lambda off: pl.BlockSpec((1, t, LANES), lambda j: (j + off, 0, 0))
    w_spec = pl.BlockSpec((1, 16, LANES), lambda j: (j, 0, 0))
    scratch = [pltpu.VMEM((l_lat + 2 * GRID_W, LANES), F32)] * 3
    out_t = jax.ShapeDtypeStruct((n_blocks, t, LANES), F32)
    if not bwd:
        return pl.pallas_call(
            fwd_body, name=name, grid=(n_blocks,), in_specs=[blk(first_block), w_spec], out_specs=blk(0),
            out_shape=out_t, scratch_shapes=scratch, compiler_params=_params(("arbitrary",), VMEM_BIG),
        )(src, wts)
    return pl.pallas_call(
        bwd_body, name=name, grid=(n_blocks,), in_specs=[blk(first_block), w_spec, blk(0)],
        out_specs=[blk(0), w_spec], out_shape=[out_t, jax.ShapeDtypeStruct((n_blocks, 16, LANES), F32)],
        scratch_shapes=scratch, compiler_params=_params(("arbitrary",), VMEM_BIG),
    )(src, wts, dout)


def _scan_consts(kind, g, h, rev, d_index):
    n = g * CHUNK
    i = jnp.arange(n, dtype=jnp.int32)
    head, pos = i // CHUNK, i % CHUNK
    p = (CHUNK - 1 - pos) if rev else pos
    same_head = head[:, None] == head[None, :]
    pr, pc = p[:, None], p[None, :]
    f = lambda m: m.astype(F32)
    incl = same_head & (pc <= pr)
    out = {"eye": f(i[:, None] == i[None, :]), "incl": f(incl), "incl_t": f(incl).T}
    if kind == "hg":
        mid, same, sec = [], [], []
        for half in HALVES:
            width = 2 * half
            blk = p // width
            second = (p % width) >= half
            ref_pos = blk * width + half - 1
            mid.append(same_head & (pc <= ref_pos[:, None]))
            same.append(same_head & (blk[:, None] == blk[None, :]))
            sec.append(jnp.broadcast_to(second[:, None], (n, LANES)))
        mid = f(jnp.concatenate(mid, axis=0))
        out.update(mid=mid, mid_t=mid.T, same=f(jnp.stack(same)), sec=f(jnp.stack(sec)))
    else:
        low = []
        for half in reversed(HALVES):
            width = 2 * half
            blk = p // width
            second = (p % width) >= half
            low.append(same_head & (blk[:, None] == blk[None, :]) & second[:, None] & jnp.logical_not(second[None, :]))
        lane = jnp.arange(LANES, dtype=jnp.int32)[None, :]
        hh = jnp.arange(h * CHUNK, dtype=jnp.int32) // CHUNK
        out.update(strict=f(same_head & (pc < pr)), ones=jnp.ones((n, n), F32), low=f(jnp.stack(low)),
                   oh_a=f(lane == (d_index * h + hh)[:, None]), oh_b=f(lane == (2 * h + d_index * h + hh)[:, None]))
    return out


def _const_mm(c, c_t, diff, prec=HIGH):
    if not diff:
        return lambda x: _mm(c, x, prec)

    @jax.custom_vjp
    def f(x):
        return _mm(c, x, prec)

    f.defvjp(lambda x: (_mm(c, x, prec), None), lambda _, ct: (_mm(c_t, ct, HIGH),))
    return f


def _kept_inverse(x_kept):
    @jax.custom_vjp
    def f(a):
        return x_kept

    f.defvjp(lambda a: (x_kept, None), lambda _, ct: (-_mm_nt(_mm_tn(x_kept, ct, HIGH), x_kept, HIGH),))
    return f


def _stack_helpers(g):
    n = g * CHUNK
    rows = lambda vec: jnp.broadcast_to(vec, (g, CHUNK, LANES)).reshape(n, LANES)
    per_head = lambda t: [t[i * CHUNK:(i + 1) * CHUNK] for i in range(g)]
    head_sum = lambda t: rows(jnp.sum(t.reshape(g, CHUNK, LANES), axis=1, keepdims=True))
    return rows, per_head, head_sum


def _hg_chunk(raw_q, raw_f, raw_i, l0, l1, sts, cst, g, diff=False):
    n = g * CHUNK
    rows, per_head, head_sum = _stack_helpers(g)
    lb = rows(jax.nn.sigmoid(l0 - l1))
    q = _silu(raw_q) * (LANES ** -0.5)
    gl = jnp.log(lb + (1.0 - lb) * jax.nn.sigmoid(raw_f))
    k = (1.0 - lb) * jax.nn.sigmoid(-raw_f)
    v = raw_i
    b = _const_mm(cst["incl"], cst["incl_t"], diff)(gl)
    b_tot = head_sum(gl)
    q_dec = q * jnp.exp(b)
    k_dec = k * jnp.exp(b_tot - b)
    mids = _const_mm(cst["mid"], cst["mid_t"], diff, None)(gl)
    a = cst["eye"] * jnp.sum(q * k, axis=1, keepdims=True)
    for lv in range(len(HALVES)):
        r = mids[lv * n:(lv + 1) * n]
        sec = cst["sec"][lv]
        fst = 1.0 - sec
        qt = q * jnp.exp((b - r) * sec) * sec
        kt = k * jnp.exp((r - b) * fst) * fst
        a = a + _mm_nt(qt, kt) * cst["same"][lv]
    o_intra = per_head(_mm(a, v))
    decay = per_head(jnp.exp(b_tot))
    qd, kd, vs = per_head(q_dec), per_head(k_dec), per_head(v)
    outs = [o_intra[i] + _mm_nt(qd[i], sts[i]) for i in range(g)]
    new = [sts[i] * decay[i][0:1] + _mm_tn(vs[i], kd[i]) for i in range(g)]
    return jnp.concatenate(outs, axis=0), new


def _gdn_chunk(u_q, u_k, u_v, tail, avec, dtvec, ss, cst, g, x_kept=None):
    n = g * CHUNK
    diff = x_kept is not None
    rows, per_head, head_sum = _stack_helpers(g)

    def l2n(t):
        return t * lax.rsqrt(jnp.sum(t * t, axis=-1, keepdims=True) + NORM_EPS)

    q = l2n(_silu(u_q)) * (LANES ** -0.5)
    k = l2n(_silu(u_k))
    v = _silu(u_v)
    tail_n = jnp.concatenate([tail] * g, axis=0)
    za = jnp.sum(tail_n * cst["oh_a"], axis=1, keepdims=True) + rows(dtvec)
    beta = jax.nn.sigmoid(jnp.sum(tail_n * cst["oh_b"], axis=1, keepdims=True))
    gl = -jnp.exp(rows(avec)) * (jnp.maximum(za, 0.0) + jnp.log1p(jnp.exp(-jnp.abs(za))))
    b = _const_mm(cst["incl"], cst["incl_t"], diff)(gl)
    b_tot = head_sum(gl)
    bb = jnp.concatenate([b] * (n // LANES), axis=1)
    b_row = _const_mm(cst["ones"], cst["ones"], diff)(cst["eye"] * bb)
    bdiff = bb - b_row
    a_mat = beta * _mm_nt(k, k) * jnp.exp(jnp.where(cst["strict"] > 0.5, bdiff, NEG))
    if diff:
        xinv = _kept_inverse(x_kept)(a_mat)
    else:
        xinv = cst["eye"]
        for lv in range(len(HALVES)):
            xinv = xinv - _mm(_mm(xinv, a_mat * cst["low"][lv], HIGH), xinv, HIGH)
    rhs = jnp.concatenate([beta * v, beta * jnp.exp(b) * k], axis=1)
    sol = _mm(xinv, rhs, HIGH)
    u0, w = per_head(sol[:, :LANES]), per_head(sol[:, LANES:])
    k_dec = per_head(k * jnp.exp(b_tot - b))
    q_dec = per_head(q * jnp.exp(b))
    decay = per_head(jnp.exp(b_tot))
    p = _mm_nt(q, k) * jnp.exp(jnp.where(cst["incl"] > 0.5, bdiff, NEG))
    v_new = [u0[i] - _mm(w[i], ss[i]) for i in range(g)]
    o_state = [_mm(q_dec[i], ss[i]) for i in range(g)]
    o = _mm(p, jnp.concatenate(v_new, axis=0)) + jnp.concatenate(o_state, axis=0)
    new = [decay[i][0:1] * ss[i] + _mm_tn(k_dec[i], v_new[i]) for i in range(g)]
    return o, new, xinv


def _chunk_index(l_lat, l_ctx, rev):
    nl, nc = l_lat // CHUNK, l_ctx // CHUNK

    def idx(i):
        if rev:
            return jnp.where(i < nc, nl + nc - 1 - i, nl - 1 - (i - nc))
        return jnp.where(i < nc, nl + i, i - nc)

    return nl + nc, idx


def _const_args(kind, g, h, rev, d_index):
    consts = _scan_consts(kind, g, h, rev, d_index)
    names = sorted(consts)
    arrs = [consts[k] for k in names]
    specs = [pl.BlockSpec(a.shape, functools.partial(lambda nd, i: (0,) * nd, a.ndim)) for a in arrs]
    return names, arrs, specs


def _load_consts(names, refs, g0, g):
    cst = {}
    for k, r in zip(names, refs):
        if k in ("oh_a", "oh_b"):
            cst[k] = r[pl.ds(pl.multiple_of(g0 * CHUNK, CHUNK), g * CHUNK), :]
        else:
            cst[k] = r[...]
    return cst


def _scan_fwd(kind, rev, cbm, segs, vecs, o_prev, l_lat, l_ctx, h, name, tail=None, d_index=0):
    n_steps, cidx = _chunk_index(l_lat, l_ctx, rev)
    t = l_lat + l_ctx
    n_in = len(segs)
    gdn = kind == "gdn"
    g = min(HEADS_PER_GROUP, h)
    n = g * CHUNK
    c_names, c_arrs, c_specs = _const_args(kind, g, h, rev, d_index)

    def body(*refs):
        in_refs = refs[:n_in]
        pos = n_in
        tail_ref = None
        if gdn:
            tail_ref = refs[pos]
            pos += 1
        v0_ref, v1_ref = refs[pos], refs[pos + 1]
        pos += 2
        prev_ref = None
        if o_prev is not None:
            prev_ref = refs[pos]
            pos += 1
        c_refs = refs[pos:pos + len(c_names)]
        pos += len(c_names)
        o_ref, st_ref = refs[pos], refs[pos + 1]
        pos += 2
        x_ref = None
        if gdn:
            x_ref = refs[pos]
            pos += 1
        s_scr = refs[pos]

        @pl.when(pl.program_id(0) == 0)
        def _():
            s_scr[...] = jnp.zeros_like(s_scr)

        def group(gi, carry):
            g0 = pl.multiple_of(gi * g, g)
            hs = pl.ds(g0, g)
            cst = _load_consts(c_names, c_refs, g0, g)
            states = [s_scr[g0 + i] for i in range(g)]
            for i in range(g):
                st_ref[0, g0 + i] = states[i]
            ins = [r[hs].reshape(n, LANES) for r in in_refs]
            if gdn:
                o, new, xinv = _gdn_chunk(*ins, tail_ref[0], v0_ref[hs], v1_ref[hs], states, cst, g)
                x_ref[0, gi] = xinv
            else:
                o, new = _hg_chunk(*ins, v0_ref[hs], v1_ref[hs], states, cst, g)
            o = o.reshape(g, CHUNK, LANES)
            if prev_ref is not None:
                o = o + prev_ref[hs]
            o_ref[hs] = o
            for i in range(g):
                s_scr[g0 + i] = new[i]
            return carry

        lax.fori_loop(0, h // g, group, 0, unroll=GROUP_UNROLL)

    seg_spec = lambda sg: pl.BlockSpec((h, CHUNK, LANES), lambda i: (sg, cidx(i), 0))
    vec_spec = pl.BlockSpec((h, 1, LANES), lambda i: (0, 0, 0))
    in_specs = [seg_spec(sg) for sg in segs]
    args = [cbm] * n_in
    if gdn:
        in_specs.append(pl.BlockSpec((1, CHUNK, LANES), lambda i: (0, cidx(i), 0)))
        args.append(tail)
    in_specs += [vec_spec, vec_spec]
    args += list(vecs)
    if o_prev is not None:
        in_specs.append(seg_spec(0))
        args.append(o_prev)
    in_specs += c_specs
    args += c_arrs
    return pl.pallas_call(
        body, name=name, grid=(n_steps,), in_specs=in_specs,
        out_specs=[seg_spec(0), pl.BlockSpec((1, h, LANES, LANES), lambda i: (i, 0, 0, 0))]
        + ([pl.BlockSpec((1, h // g, n, n), lambda i: (i, 0, 0, 0))] if gdn else []),
        out_shape=[jax.ShapeDtypeStruct((h, t, LANES), F32), jax.ShapeDtypeStruct((n_steps, h, LANES, LANES), F32)]
        + ([jax.ShapeDtypeStruct((n_steps, h // g, n, n), F32)] if gdn else []),
        scratch_shapes=[pltpu.VMEM((h, LANES, LANES), F32)],
        compiler_params=_params(("arbitrary",), VMEM_BIG),
    )(*args)


def _scan_bwd(kind, rev, cbm, segs, vecs, states, d_o, acc, l_lat, l_ctx, h, name, tail=None, d_index=0, x_kept=None):
    n_steps, cidx = _chunk_index(l_lat, l_ctx, rev)
    t = l_lat + l_ctx
    n_in = len(segs)
    gdn = kind == "gdn"
    n_grad = n_in + (1 if gdn else 0)
    assert len(acc) == n_grad
    step_of = lambda j: n_steps - 1 - j
    g = min(HEADS_PER_GROUP, h)
    n = g * CHUNK
    c_names, c_arrs, c_specs = _const_args(kind, g, h, rev, d_index)

    def body(*refs):
        in_refs = refs[:n_in]
        pos = n_in
        tail_ref = None
        if gdn:
            tail_ref = refs[pos]
            pos += 1
        v0_ref, v1_ref, st_ref, do_ref = refs[pos:pos + 4]
        pos += 4
        xk_ref = None
        if gdn:
            xk_ref = refs[pos]
            pos += 1
        acc_refs = []
        for a in acc:
            if a is None:
                acc_refs.append(None)
            else:
                acc_refs.append(refs[pos])
                pos += 1
        c_refs = refs[pos:pos + len(c_names)]
        pos += len(c_names)
        g_refs = refs[pos:pos + n_grad]
        dv0_ref, dv1_ref, ds_scr = refs[pos + n_grad:pos + n_grad + 3]

        @pl.when(pl.program_id(0) == 0)
        def _():
            ds_scr[...] = jnp.zeros_like(ds_scr)
            dv0_ref[...] = jnp.zeros_like(dv0_ref)
            dv1_ref[...] = jnp.zeros_like(dv1_ref)

        if gdn:
            g_refs[n_in][0] = jnp.zeros((CHUNK, LANES), F32) if acc_refs[n_in] is None else acc_refs[n_in][0]

        def group(gi, carry):
            g0 = pl.multiple_of(gi * g, g)
            hs = pl.ds(g0, g)
            cst = _load_consts(c_names, c_refs, g0, g)
            sts = [st_ref[0, g0 + i] for i in range(g)]
            ins = [r[hs].reshape(n, LANES) for r in in_refs]
            if gdn:
                xk = xk_ref[0, gi]
                fn = lambda uq, uk, uv, tl, av, dt, ss: _gdn_chunk(uq, uk, uv, tl, av, dt, ss, cst, g, xk)[:2]
                prim = ins + [tail_ref[0], v0_ref[hs], v1_ref[hs], sts]
            else:
                fn = lambda rq, rf, ri, l0, l1, ss: _hg_chunk(rq, rf, ri, l0, l1, ss, cst, g, True)
                prim = ins + [v0_ref[hs], v1_ref[hs], sts]
            _, vj = jax.vjp(fn, *prim)
            grads = vj((do_ref[hs].reshape(n, LANES), [ds_scr[g0 + i] for i in range(g)]))
            for k in range(n_in):
                gk = grads[k].reshape(g, CHUNK, LANES)
                if acc_refs[k] is not None:
                    gk = gk + acc_refs[k][hs]
                g_refs[k][hs] = gk
            if gdn:
                g_refs[n_in][0] += grads[n_in]
            dv0 = grads[n_grad]
            dv1 = grads[n_grad + 1]
            if gdn:
                dv0 = jnp.broadcast_to(jnp.sum(dv0, axis=2, keepdims=True), dv0.shape)
                dv1 = jnp.broadcast_to(jnp.sum(dv1, axis=2, keepdims=True), dv1.shape)
            dv0_ref[hs] += dv0
            dv1_ref[hs] += dv1
            for i in range(g):
                ds_scr[g0 + i] = grads[n_grad + 2][i]
            return carry

        lax.fori_loop(0, h // g, group, 0, unroll=GROUP_UNROLL)

    seg_spec = lambda sg: pl.BlockSpec((h, CHUNK, LANES), lambda j: (sg, cidx(step_of(j)), 0))
    tail_spec = pl.BlockSpec((1, CHUNK, LANES), lambda j: (0, cidx(step_of(j)), 0))
    vec_spec = pl.BlockSpec((h, 1, LANES), lambda j: (0, 0, 0))
    in_specs = [seg_spec(sg) for sg in segs]
    args = [cbm] * n_in
    if gdn:
        in_specs.append(tail_spec)
        args.append(tail)
    in_specs += [vec_spec, vec_spec, pl.BlockSpec((1, h, LANES, LANES), lambda j: (step_of(j), 0, 0, 0)), seg_spec(0)]
    args += list(vecs) + [states, d_o]
    if gdn:
        in_specs.append(pl.BlockSpec((1, h // g, n, n), lambda j: (step_of(j), 0, 0, 0)))
        args.append(x_kept)
    for k, a in enumerate(acc):
        if a is not None:
            in_specs.append(tail_spec if (gdn and k == n_in) else seg_spec(0))
            args.append(a)
    in_specs += c_specs
    args += c_arrs
    out_specs = [seg_spec(0)] * n_in + ([tail_spec] if gdn else []) + [vec_spec, vec_spec]
    out_shape = ([jax.ShapeDtypeStruct((h, t, LANES), F32)] * n_in
                 + ([jax.ShapeDtypeStruct((1, t, LANES), F32)] if gdn else [])
                 + [jax.ShapeDtypeStruct((h, 1, LANES), F32)] * 2)
    return pl.pallas_call(
        body, name=name, grid=(n_steps,), in_specs=in_specs, out_specs=out_specs, out_shape=out_shape,
        scratch_shapes=[pltpu.VMEM((h, LANES, LANES), F32)],
        compiler_params=_params(("arbitrary",), VMEM_BIG),
    )(*args)


def _gout_tile(oa, ob, za, zb, naw, nbw):
    def hn(o, w):
        return o * lax.rsqrt(jnp.mean(o * o, axis=-1, keepdims=True) + NORM_EPS) * w

    ya = _silu(za) * hn(oa, naw)
    yb = _silu(zb) * hn(ob, nbw)
    nh = oa.shape[0]
    return jnp.concatenate([ya[i] for i in range(nh)] + [yb[i] for i in range(nh)], axis=1)


def _gout_fwd(oa, ob, y_cbm, naw, nbw, l_lat, h):
    tm = _pick(l_lat, (256, 128, 64))
    blk = lambda sg: pl.BlockSpec((h, tm, LANES), lambda i: (sg, i, 0))
    vec = pl.BlockSpec((h, 1, LANES), lambda i: (0, 0, 0))

    def body(oa_ref, ob_ref, za_ref, zb_ref, na_ref, nb_ref, y_ref):
        y_ref[...] = _gout_tile(oa_ref[...], ob_ref[...], za_ref[...], zb_ref[...], na_ref[...], nb_ref[...]).astype(BF16)

    return pl.pallas_call(
        body, name="gout_fwd", grid=(l_lat // tm,),
        in_specs=[blk(0), blk(0), blk(4), blk(8), vec, vec],
        out_specs=pl.BlockSpec((tm, 2 * h * LANES), lambda i: (i, 0)),
        out_shape=jax.ShapeDtypeStruct((l_lat, 2 * h * LANES), BF16),
        compiler_params=_params(("arbitrary",)),
    )(oa, ob, y_cbm, y_cbm, naw, nbw)


def _gout_bwd(oa, ob, y_cbm, naw, nbw, dymix, l_lat, l_ctx, h):
    tm = _pick(l_ctx, (256, 128, 64))
    nl, nc = l_lat // tm, l_ctx // tm
    t = l_lat + l_ctx
    blk_in = lambda sg: pl.BlockSpec((h, tm, LANES), lambda i: (sg, jnp.minimum(i, nl - 1), 0))
    blk_out = pl.BlockSpec((h, tm, LANES), lambda i: (0, i, 0))
    vec = pl.BlockSpec((h, 1, LANES), lambda i: (0, 0, 0))

    def body(oa_ref, ob_ref, za_ref, zb_ref, na_ref, nb_ref, dy_ref, doa_ref, dob_ref, dza_ref, dzb_ref, dna_ref, dnb_ref):
        i = pl.program_id(0)

        @pl.when(i == 0)
        def _():
            dna_ref[...] = jnp.zeros_like(dna_ref)
            dnb_ref[...] = jnp.zeros_like(dnb_ref)

        @pl.when(i < nl)
        def _():
            _, vj = jax.vjp(_gout_tile, oa_ref[...], ob_ref[...], za_ref[...], zb_ref[...], na_ref[...], nb_ref[...])
            doa, dob, dza, dzb, dna, dnb = vj(dy_ref[...])
            doa_ref[...] = doa
            dob_ref[...] = dob
            dza_ref[...] = dza
            dzb_ref[...] = dzb
            dna_ref[...] += dna
            dnb_ref[...] += dnb

        @pl.when(i >= nl)
        def _():
            for r in (doa_ref, dob_ref, dza_ref, dzb_ref):
                r[...] = jnp.zeros_like(r)

    big = jax.ShapeDtypeStruct((h, t, LANES), F32)
    small = jax.ShapeDtypeStruct((h, 1, LANES), F32)
    return pl.pallas_call(
        body, name="gout_bwd", grid=(nl + nc,),
        in_specs=[blk_in(0), blk_in(0), blk_in(4), blk_in(8), vec, vec,
                  pl.BlockSpec((tm, 2 * h * LANES), lambda i: (jnp.minimum(i, nl - 1), 0))],
        out_specs=[blk_out] * 4 + [vec, vec], out_shape=[big] * 4 + [small, small],
        compiler_params=_params(("arbitrary",)),
    )(oa, ob, y_cbm, y_cbm, naw, nbw, dymix)


def _head_tile(xt, mixt, tgt, gt, fw):
    xo = xt + gt * mixt
    y = xo * lax.rsqrt(jnp.mean(xo * xo, axis=-1, keepdims=True) + NORM_EPS) * fw
    err = y - tgt
    return 0.5 * jnp.sum(jnp.mean(err * err, axis=-1, keepdims=True), axis=0, keepdims=True)


def _loss_head(x, mix, target, mods, final_w):
    l_lat, d = x.shape
    tm = _pick(l_lat, (256, 128, 64))
    tok = pl.BlockSpec((tm, d), lambda i: (i, 0))
    row = pl.BlockSpec((1, d), lambda i: (0, 0))

    def body(x_ref, m_ref, t_ref, mod_ref, fw_ref, loss_ref, dmix_ref, dx_ref, dgt_ref, dfw_ref):
        @pl.when(pl.program_id(0) == 0)
        def _():
            loss_ref[...] = jnp.zeros_like(loss_ref)
            dgt_ref[...] = jnp.zeros_like(dgt_ref)
            dfw_ref[...] = jnp.zeros_like(dfw_ref)

        gt = mod_ref[0:1, 2 * d:3 * d]
        fn = lambda xt, mt, g, fw: _head_tile(xt, mt, t_ref[...], g, fw)
        val, vj = jax.vjp(fn, x_ref[...], m_ref[...], gt, fw_ref[...])
        dxt, dmt, dgt, dfw = vj(jnp.ones((1, 1), F32))
        loss_ref[...] += jnp.broadcast_to(val, loss_ref.shape)
        dmix_ref[...] = dmt.astype(BF16)
        dx_ref[...] = dxt
        dgt_ref[...] += dgt
        dfw_ref[...] += dfw

    return pl.pallas_call(
        body, name="loss_head", grid=(l_lat // tm,),
        in_specs=[tok, tok, tok, pl.BlockSpec((8, 3 * d), lambda i: (0, 0)), row],
        out_specs=[pl.BlockSpec((8, LANES), lambda i: (0, 0)), tok, tok, row, row],
        out_shape=[jax.ShapeDtypeStruct((8, LANES), F32), jax.ShapeDtypeStruct((l_lat, d), BF16),
                   jax.ShapeDtypeStruct((l_lat, d), F32), jax.ShapeDtypeStruct((1, d), F32),
                   jax.ShapeDtypeStruct((1, d), F32)],
        compiler_params=_params(("arbitrary",)),
    )(x, mix, target, mods, final_w)


def _adamw(w, g_slots, m, v, name):
    rows, cols = w.shape
    n = g_slots.shape[0]
    tr = _pick(rows, (64, 32, 16, 8))

    def body(w_ref, g_ref, m_ref, v_ref, go_ref, d_ref, nm_ref, nv_ref):
        gg = g_ref[0]
        for s in range(1, n):
            gg = gg + g_ref[s]
        m2 = ADAM_B1 * m_ref[...] + (1.0 - ADAM_B1) * gg
        v2 = ADAM_B2 * v_ref[...] + (1.0 - ADAM_B2) * (gg * gg)
        m_hat = m2 / (1.0 - ADAM_B1 ** ADAM_STEP)
        v_hat = v2 / (1.0 - ADAM_B2 ** ADAM_STEP)
        go_ref[...] = gg
        d_ref[...] = -ADAM_LR * (m_hat / (jnp.sqrt(v_hat) + ADAM_EPS) + ADAM_WD * w_ref[...])
        nm_ref[...] = m2
        nv_ref[...] = v2

    spec = pl.BlockSpec((tr, cols), lambda i: (i, 0))
    shp = jax.ShapeDtypeStruct((rows, cols), F32)
    return pl.pallas_call(
        body, name=name, grid=(rows // tr,),
        in_specs=[spec, pl.BlockSpec((n, tr, cols), lambda i: (0, i, 0)), spec, spec],
        out_specs=[spec] * 4, out_shape=[shp] * 4,
        compiler_params=_params(("arbitrary",)),
    )(w, g_slots, m, v)


def _adamw_nd(w, g_slots, m, v, name):
    shape = w.shape
    two_d = (-1, shape[-1])
    w2 = w.reshape(two_d)
    outs = _adamw(w2, g_slots.reshape((g_slots.shape[0],) + w2.shape), m.reshape(two_d), v.reshape(two_d), name)
    return [o.reshape(shape) for o in outs]


def kernel(x, c, ctx, c_ctx, norm_w, ada_w, ada_b, w_in, conv_w, hg_lb_logits, gdn_a_log, gdn_dt_bias, ha_norm_w, hb_norm_w, w_out, final_norm_w, loss_target, m_c_ctx, m_norm_w, m_ada_w, m_ada_b, m_w_in, m_conv_w, m_hg_lb_logits, m_gdn_a_log, m_gdn_dt_bias, m_ha_norm_w, m_hb_norm_w, m_w_out, m_final_norm_w, v_c_ctx, v_norm_w, v_ada_w, v_ada_b, v_w_in, v_conv_w, v_hg_lb_logits, v_gdn_a_log, v_gdn_dt_bias, v_ha_norm_w, v_hb_norm_w, v_w_out, v_final_norm_w):
    l_lat, d = x.shape[1], x.shape[2]
    l_ctx = ctx.shape[1]
    t = l_lat + l_ctx
    h = d // LANES
    n_in = 9 * d + 4 * h
    nb = -(-(9 * h + 1) // COLS_PER_TILE) * COLS_PER_TILE
    n_pad = nb * LANES
    win_c, ada_c, conv_c, lb_c, wout_r = w_in.shape[2], ada_w.shape[2], conv_w.shape[3], hg_lb_logits.shape[2], w_out.shape[1]
    me = _me()

    gw_in, gw_out, gw_ada, gw_conv, gw_lb = _exchange(
        [w_in[0].astype(BF16), w_out[0].astype(BF16), ada_w[0], conv_w[0], hg_lb_logits], "gather_weights", True)
    join = lambda g, axis: jnp.concatenate([g[k] for k in range(N_DEV)], axis=axis)
    w_in_full = join(gw_in, 1)
    w_out_full = gw_out.reshape(N_DEV * wout_r, d)
    ada_full = join(gw_ada, 1)
    conv_full = join(gw_conv, 2)
    lb_full = join(gw_lb, 2)
    w_pad = jnp.pad(w_in_full, ((0, 0), (0, n_pad - n_in)))
    w_pad_t = w_pad.T
    w_out_t = w_out_full.T

    lbl = lb_full.reshape(2, 2, h, 1, LANES)
    conv_rows = jnp.pad(conv_full.reshape(9, 3 * h, LANES).transpose(1, 0, 2), ((0, 0), (0, 7), (0, 0)))
    lane_bc = lambda a: jnp.broadcast_to(a.reshape(2, h, 1, 1), (2, h, 1, LANES))
    avec, dtvec = lane_bc(gdn_a_log[0]), lane_bc(gdn_dt_bias[0])
    naw, nbw = ha_norm_w[0].reshape(h, 1, LANES), hb_norm_w[0].reshape(h, 1, LANES)

    cvec = jnp.concatenate([c, c_ctx[None, :], jnp.zeros((6, d), F32)], axis=0)
    mods = _ada_fwd(cvec, ada_full, ada_b)
    x2, ctx2, tgt2 = x[0], ctx[0], loss_target[0]
    h_all = _prenorm_fwd(x2, ctx2, norm_w, mods)
    y = _matmul(h_all, w_pad, "proj", out_cbm=True, tn=COLS_PER_TILE * LANES)
    tail = lax.slice_in_dim(y, 9 * h, 9 * h + 1, axis=0)
    u = _conv_call(y, conv_rows, l_lat, 5 * h, 3 * h, "conv_fwd")

    oa, st_af = _scan_fwd("hg", False, y, (0, 1, 3), (lbl[0, 0], lbl[1, 0]), None, l_lat, l_ctx, h, "hg_fwd_f")
    oa, st_ab = _scan_fwd("hg", True, y, (0, 2, 3), (lbl[0, 1], lbl[1, 1]), oa, l_lat, l_ctx, h, "hg_fwd_b")
    ob, st_bf, xk_f = _scan_fwd("gdn", False, u, (0, 1, 2), (avec[0], dtvec[0]), None, l_lat, l_ctx, h, "gdn_fwd_f", tail, 0)
    ob, st_bb, xk_b = _scan_fwd("gdn", True, u, (0, 1, 2), (avec[1], dtvec[1]), ob, l_lat, l_ctx, h, "gdn_fwd_b", tail, 1)

    ymix = _gout_fwd(oa, ob, y, naw, nbw, l_lat, h)
    mix = _matmul(ymix, w_out_full, "out_proj")
    loss_blk, dmix, dx_res, dgt, dfw = _loss_head(x2, mix, tgt2, mods, final_norm_w.reshape(1, d))

    dymix = _matmul(dmix, w_out_t, "d_ymix")
    dw_out = _matmul(ymix.T, dmix, "d_w_out")
    doa, dob, dza, dzb, dnaw, dnbw = _gout_bwd(oa, ob, y, naw, nbw, dymix, l_lat, l_ctx, h)

    gq, gff, gi, dl0f, dl1f = _scan_bwd("hg", False, y, (0, 1, 3), (lbl[0, 0], lbl[1, 0]), st_af, doa,
                                        [None, None, None], l_lat, l_ctx, h, "hg_bwd_f")
    gq, gfb, gi, dl0b, dl1b = _scan_bwd("hg", True, y, (0, 2, 3), (lbl[0, 1], lbl[1, 1]), st_ab, doa,
                                        [gq, None, gi], l_lat, l_ctx, h, "hg_bwd_b")
    guq, guk, guv, gtail, da_f, ddt_f = _scan_bwd("gdn", False, u, (0, 1, 2), (avec[0], dtvec[0]), st_bf, dob,
                                                  [None] * 4, l_lat, l_ctx, h, "gdn_bwd_f", tail, 0, xk_f)
    guq, guk, guv, gtail, da_b, ddt_b = _scan_bwd("gdn", True, u, (0, 1, 2), (avec[1], dtvec[1]), st_bb, dob,
                                                  [guq, guk, guv, gtail], l_lat, l_ctx, h, "gdn_bwd_b", tail, 1, xk_b)
    du = jnp.concatenate([guq, guk, guv], axis=0)
    dconv_in, dconv_rows = _conv_call(y, conv_rows, l_lat, 5 * h, 3 * h, "conv_bwd", dout=du)

    pad_blocks = nb - 9 * h - 1
    dy = jnp.concatenate([gq, gff, gfb, gi, dza, dconv_in, dzb, gtail]
                         + ([jnp.zeros((pad_blocks, t, LANES), F32)] if pad_blocks else []), axis=0)
    dh = _matmul(dy, w_pad_t, "d_h", a_cbm=True)
    dw_in = _matmul(h_all.T, dy, "d_w_in", b_cbm=True)
    grad_x, dnorm_w, dmods_pre = _prenorm_bwd(x2, ctx2, norm_w, mods, dh, dx_res)

    dmods = jnp.concatenate([dmods_pre[:, :2 * d],
                             jnp.concatenate([dgt, jnp.zeros((7, d), F32)], axis=0)], axis=1)
    dcond = _ada_bwd(cvec, ada_full, dmods)

    dw_in_s = jnp.stack([dw_in[:, k * win_c:(k + 1) * win_c] for k in range(N_DEV)])
    dw_out_s = dw_out.reshape(N_DEV, wout_r, d)
    dconv_s = dconv_rows[:, :9, :].transpose(1, 0, 2).reshape(3, 3, N_DEV, conv_c).transpose(2, 0, 1, 3)
    dlb_s = jnp.stack([jnp.stack([dl0f, dl0b]), jnp.stack([dl1f, dl1b])]).reshape(2, 2, N_DEV, lb_c).transpose(2, 0, 1, 3)
    s_w_in, s_w_out, s_conv, s_lb = _exchange([dw_in_s, dw_out_s, dconv_s, dlb_s], "scatter_grads", False)

    da = jnp.stack([da_f[:, 0, 0], da_b[:, 0, 0]])
    ddt = jnp.stack([ddt_f[:, 0, 0], ddt_b[:, 0, 0]])
    small_parts = [dcond[1], dnorm_w, dmods[0] + dmods[1], dnaw, dnbw, dfw, da, ddt, c[0], dmods[0], dmods[1]]
    small_blob, soff = _to_blob(small_parts)
    gathered, = _exchange([small_blob], "gather_small", True)
    summed = _sum_slots(gathered, "sum_small").reshape(-1)
    gflat = gathered.reshape(N_DEV, -1)
    take = lambda k, n: summed[soff[k]:soff[k] + n]
    one = lambda k, n, shape: take(k, n).reshape((1,) + shape)
    s_c_ctx = one(0, d, (d,))
    s_norm_w = one(1, d, (1, d))
    s_ada_b = one(2, 3 * d, (1, 3 * d))
    s_ha = one(3, d, (1, h, LANES))
    s_hb = one(4, d, (1, h, LANES))
    s_final = one(5, d, (d,))
    s_a_log = one(6, 2 * h, (1, 2, h))
    s_dt = one(7, 2 * h, (1, 2, h))
    conds = jnp.concatenate([gflat[:, soff[8]:soff[8] + d], c_ctx[None, :], jnp.zeros((7, d), F32)], axis=0)
    col0 = me * ada_c
    dm_lat = lax.dynamic_slice_in_dim(gflat[:, soff[9]:soff[9] + 3 * d], col0, ada_c, axis=1)
    dm_ctx = lax.dynamic_slice_in_dim(gflat[:, soff[10]:soff[10] + 3 * d], col0, ada_c, axis=1)
    s_ada_w = _ada_wgrad(conds, dm_lat, dm_ctx).reshape(1, 1, d, ada_c)

    loss = lax.psum(loss_blk[0, 0], ("x", "y", "c"))

    weights = [("c_ctx", c_ctx, s_c_ctx, m_c_ctx, v_c_ctx), ("norm_w", norm_w, s_norm_w, m_norm_w, v_norm_w),
               ("ada_w", ada_w, s_ada_w, m_ada_w, v_ada_w), ("ada_b", ada_b, s_ada_b, m_ada_b, v_ada_b),
               ("w_in", w_in, s_w_in, m_w_in, v_w_in), ("conv_w", conv_w, s_conv, m_conv_w, v_conv_w),
               ("hg_lb_logits", hg_lb_logits, s_lb, m_hg_lb_logits, v_hg_lb_logits),
               ("gdn_a_log", gdn_a_log, s_a_log, m_gdn_a_log, v_gdn_a_log),
               ("gdn_dt_bias", gdn_dt_bias, s_dt, m_gdn_dt_bias, v_gdn_dt_bias),
               ("ha_norm_w", ha_norm_w, s_ha, m_ha_norm_w, v_ha_norm_w), ("hb_norm_w", hb_norm_w, s_hb, m_hb_norm_w, v_hb_norm_w),
               ("w_out", w_out, s_w_out, m_w_out, v_w_out), ("final_norm_w", final_norm_w, s_final, m_final_norm_w, v_final_norm_w)]
    grads, deltas, new_ms, new_vs = [], [], [], []
    for nm, w, gs, m, v in weights:
        g, dl, m2, v2 = _adamw_nd(w, gs, m, v, "adamw_" + nm)
        grads.append(g)
        deltas.append(dl)
        new_ms.append(m2)
        new_vs.append(v2)
    return (loss, grad_x[None], *grads, *deltas, *new_ms, *new_vs)
```

```python
import functools

import jax
import jax.numpy as jnp
from jax import lax
from jax.experimental import pallas as pl
from jax.experimental.pallas import tpu as pltpu

F32 = jnp.float32
BF16 = jnp.bfloat16
HI = lax.Precision.HIGHEST
HIGH = lax.Precision.HIGH
MESH = pl.DeviceIdType.MESH
ANY = pl.BlockSpec(memory_space=pl.ANY)

N_DEV = 8
LANES = 128
CHUNK = 64
GRID_W = 64
NORM_EPS = 1e-6
NEG = -1e30
COLS_PER_TILE = 8
VMEM_BIG = 56 * 1024 * 1024
HEADS_PER_GROUP = 4
GROUP_UNROLL = 2
STEP_CHUNKS = 2
HALVES = (32, 16, 8, 4, 2, 1)

ADAM_LR, ADAM_B1, ADAM_B2, ADAM_EPS, ADAM_WD, ADAM_STEP = 0.001, 0.9, 0.999, 1e-08, 0.01, 10


def _params(sem=None, vmem=None):
    kw = {}
    if sem is not None:
        kw["dimension_semantics"] = sem
    if vmem is not None:
        kw["vmem_limit_bytes"] = vmem
    return pltpu.CompilerParams(**kw)


def _pick(n, cands):
    for c in cands:
        if n % c == 0:
            return c
    return n


def _silu(x):
    return x * jax.nn.sigmoid(x)


def _mm(a, b, prec=None):
    return lax.dot_general(a, b, (((1,), (0,)), ((), ())), precision=prec, preferred_element_type=F32)


def _mm_nt(a, b, prec=None):
    return lax.dot_general(a, b, (((1,), (1,)), ((), ())), precision=prec, preferred_element_type=F32)


def _mm_tn(a, b, prec=None):
    return lax.dot_general(a, b, (((0,), (0,)), ((), ())), precision=prec, preferred_element_type=F32)


def _me():
    return 4 * lax.axis_index("x") + 2 * lax.axis_index("y") + lax.axis_index("c")


def _peer(r):
    x, y, c = lax.axis_index("x"), lax.axis_index("y"), lax.axis_index("c")
    px = 1 - x if r & 4 else x
    py = 1 - y if r & 2 else y
    pc = 1 - c if r & 1 else c
    return (px, py, pc), 4 * px + 2 * py + pc


def _exchange(arrs, name, gather):
    n_a = len(arrs)

    def body(*refs):
        srcs, outs = refs[:n_a], refs[n_a:2 * n_a]
        send_sems, recv_sems, local_sems = refs[2 * n_a:]
        me = _me()
        copies = []
        for k in range(n_a):
            local = pltpu.make_async_copy(srcs[k] if gather else srcs[k].at[me], outs[k].at[me], local_sems.at[k])
            local.start()
            copies.append(local)
        for r in range(N_DEV - 1, 0, -1):
            dev, pid = _peer(r)
            for k in range(n_a):
                cp = pltpu.make_async_remote_copy(
                    src_ref=srcs[k] if gather else srcs[k].at[pid], dst_ref=outs[k].at[me],
                    send_sem=send_sems.at[(r - 1) * n_a + k], recv_sem=recv_sems.at[(r - 1) * n_a + k],
                    device_id=dev, device_id_type=MESH)
                cp.start()
                copies.append(cp)
        for cp in copies:
            cp.wait()

    n_sem = (N_DEV - 1) * n_a
    return pl.pallas_call(
        body, name=name,
        out_shape=[jax.ShapeDtypeStruct(((N_DEV,) + a.shape) if gather else a.shape, a.dtype) for a in arrs],
        in_specs=[ANY] * n_a, out_specs=[ANY] * n_a,
        scratch_shapes=[pltpu.SemaphoreType.DMA((n_sem,)), pltpu.SemaphoreType.DMA((n_sem,)),
                        pltpu.SemaphoreType.DMA((n_a,))],
    )(*arrs)


def _gather_two_level(arrs, name):
    n_a = len(arrs)

    def body(*refs):
        srcs, outs = refs[:n_a], refs[n_a:2 * n_a]
        send_sems, recv_sems, local_sems = refs[2 * n_a:]
        x, y, c = lax.axis_index("x"), lax.axis_index("y"), lax.axis_index("c")
        me, sibling = (x, y, c), (x, y, 1 - c)
        chips = [(1 - x, y), (x, 1 - y), (1 - x, 1 - y)]

        def copy(k, j, block, to, src=None):
            slot = outs[k].at[4 * block[0] + 2 * block[1] + block[2]]
            return pltpu.make_async_remote_copy(
                src_ref=slot if src is None else src, dst_ref=slot,
                send_sem=send_sems.at[j * n_a + k], recv_sem=recv_sems.at[j * n_a + k],
                device_id=to, device_id_type=MESH)

        mine = [pltpu.make_async_copy(srcs[k], outs[k].at[4 * x + 2 * y + c], local_sems.at[k]) for k in range(n_a)]
        for cp in mine:
            cp.start()
        first = [copy(k, 1 + j, me, (*chip, c), srcs[k]) for j, chip in enumerate(chips) for k in range(n_a)]
        first += [copy(k, 0, me, sibling, srcs[k]) for k in range(n_a)]
        for cp in first:
            cp.start()
        passed = []
        for j, chip in enumerate(chips):
            for k in range(n_a):
                copy(k, 1 + j, (*chip, c), me).wait_recv()
                cp = copy(k, 4 + j, (*chip, c), sibling)
                cp.start()
                passed.append(cp)
        for k in range(n_a):
            copy(k, 0, sibling, me).wait_recv()
        for j, chip in enumerate(chips):
            for k in range(n_a):
                copy(k, 4 + j, (*chip, 1 - c), me).wait_recv()
        for cp in first + passed:
            cp.wait_send()
        for cp in mine:
            cp.wait()

    n_sem = (N_DEV - 1) * n_a
    return pl.pallas_call(
        body, name=name,
        out_shape=[jax.ShapeDtypeStruct((N_DEV,) + a.shape, a.dtype) for a in arrs],
        in_specs=[ANY] * n_a, out_specs=[ANY] * n_a,
        scratch_shapes=[pltpu.SemaphoreType.DMA((n_sem,)), pltpu.SemaphoreType.DMA((n_sem,)),
                        pltpu.SemaphoreType.DMA((n_a,))],
    )(*arrs)


def _swap_with_sibling(arrs, name):
    n_a = len(arrs)

    def body(*refs):
        srcs, outs = refs[:n_a], refs[n_a:2 * n_a]
        send_sems, recv_sems = refs[2 * n_a:]
        sibling = (lax.axis_index("x"), lax.axis_index("y"), 1 - lax.axis_index("c"))
        copies = [pltpu.make_async_remote_copy(src_ref=srcs[k], dst_ref=outs[k], send_sem=send_sems.at[k],
                                               recv_sem=recv_sems.at[k], device_id=sibling, device_id_type=MESH)
                  for k in range(n_a)]
        for cp in copies:
            cp.start()
        for cp in copies:
            cp.wait()

    return pl.pallas_call(
        body, name=name, out_shape=[jax.ShapeDtypeStruct(a.shape, a.dtype) for a in arrs],
        in_specs=[ANY] * n_a, out_specs=[ANY] * n_a,
        scratch_shapes=[pltpu.SemaphoreType.DMA((n_a,)), pltpu.SemaphoreType.DMA((n_a,))],
    )(*arrs)


def _scatter_over_chips(arrs, name):
    n_a = len(arrs)
    n_chip = N_DEV // 2

    def body(*refs):
        srcs, outs = refs[:n_a], refs[n_a:2 * n_a]
        send_sems, recv_sems, local_sems = refs[2 * n_a:]
        x, y, c = lax.axis_index("x"), lax.axis_index("y"), lax.axis_index("c")
        my_chip = 2 * x + y
        copies = []
        for k in range(n_a):
            local = pltpu.make_async_copy(srcs[k].at[4 * x + 2 * y + c], outs[k].at[my_chip], local_sems.at[k])
            local.start()
            copies.append(local)
        for r in range(n_chip - 1, 0, -1):
            px = 1 - x if r & 2 else x
            py = 1 - y if r & 1 else y
            for k in range(n_a):
                cp = pltpu.make_async_remote_copy(
                    src_ref=srcs[k].at[4 * px + 2 * py + c], dst_ref=outs[k].at[my_chip],
                    send_sem=send_sems.at[(r - 1) * n_a + k], recv_sem=recv_sems.at[(r - 1) * n_a + k],
                    device_id=(px, py, c), device_id_type=MESH)
                cp.start()
                copies.append(cp)
        for cp in copies:
            cp.wait()

    n_sem = (n_chip - 1) * n_a
    return pl.pallas_call(
        body, name=name,
        out_shape=[jax.ShapeDtypeStruct((n_chip,) + a.shape[1:], a.dtype) for a in arrs],
        in_specs=[ANY] * n_a, out_specs=[ANY] * n_a,
        scratch_shapes=[pltpu.SemaphoreType.DMA((n_sem,)), pltpu.SemaphoreType.DMA((n_sem,)),
                        pltpu.SemaphoreType.DMA((n_a,))],
    )(*arrs)


def _pair_sum(a, b, name):
    shape = a.shape
    a2, b2 = a.reshape(-1, shape[-1]), b.reshape(-1, shape[-1])
    rows, cols = a2.shape
    tr = _pick(rows, (256, 128, 64, 32, 16))

    def body(a_ref, b_ref, o_ref):
        o_ref[...] = (a_ref[...] + b_ref[...].astype(F32)).astype(BF16)

    spec = pl.BlockSpec((tr, cols), lambda i: (i, 0))
    return pl.pallas_call(
        body, name=name, grid=(rows // tr,), in_specs=[spec, spec], out_specs=spec,
        out_shape=jax.ShapeDtypeStruct((rows, cols), BF16), compiler_params=_params(("arbitrary",)),
    )(a2, b2).reshape(shape)


def _to_blob(parts):
    flat = [p.reshape(-1).astype(F32) for p in parts]
    offs, n = [], 0
    for f in flat:
        offs.append(n)
        n += f.shape[0]
    unit = 8 * LANES
    total = -(-n // unit) * unit
    if total > n:
        flat.append(jnp.zeros((total - n,), F32))
    return jnp.concatenate(flat).reshape(total // LANES, LANES), offs


def _sum_slots(buf, name):
    n, rows, cols = buf.shape
    tr = _pick(rows, (512, 256, 128, 64, 32, 16, 8))

    def body(b_ref, o_ref):
        acc = b_ref[0]
        for s in range(1, n):
            acc = acc + b_ref[s]
        o_ref[...] = acc

    return pl.pallas_call(
        body, name=name, grid=(rows // tr,),
        in_specs=[pl.BlockSpec((n, tr, cols), lambda i: (0, i, 0))],
        out_specs=pl.BlockSpec((tr, cols), lambda i: (i, 0)),
        out_shape=jax.ShapeDtypeStruct((rows, cols), F32),
        compiler_params=_params(("arbitrary",)),
    )(buf)


def _matmul(a, b, name, a_cbm=False, b_cbm=False, out_cbm=False, out_dtype=F32, tm=None, tn=None, tk=None):
    if a_cbm:
        m, k = a.shape[1], a.shape[0] * LANES
    else:
        m, k = a.shape
    n = b.shape[0] * LANES if b_cbm else b.shape[1]
    tm = tm or _pick(m, (1024, 768, 512, 384, 256, 128, 64))
    tn = tn or _pick(n, (1024, 512, 256, 128))
    tk = tk or _pick(k, (2048, 1024, 768, 512, 256, 128))
    nk = k // tk

    def load(ref, cbm):
        if not cbm:
            return ref[...].astype(BF16)
        return jnp.concatenate([ref[j].astype(BF16) for j in range(ref.shape[0])], axis=1)

    def body(a_ref, b_ref, o_ref, acc_ref):
        kk = pl.program_id(2)
        part = _mm(load(a_ref, a_cbm), load(b_ref, b_cbm))

        @pl.when(kk == 0)
        def _():
            acc_ref[...] = part

        @pl.when(kk > 0)
        def _():
            acc_ref[...] += part

        @pl.when(kk == nk - 1)
        def _():
            r = acc_ref[...]
            if out_cbm:
                for j in range(tn // LANES):
                    o_ref[j] = r[:, j * LANES:(j + 1) * LANES].astype(out_dtype)
            else:
                o_ref[...] = r.astype(out_dtype)

    if a_cbm:
        a_spec = pl.BlockSpec((tk // LANES, tm, LANES), lambda j, i, kk: (kk, i, 0))
    else:
        a_spec = pl.BlockSpec((tm, tk), lambda j, i, kk: (i, kk))
    if b_cbm:
        b_spec = pl.BlockSpec((tn // LANES, tk, LANES), lambda j, i, kk: (j, kk, 0))
    else:
        b_spec = pl.BlockSpec((tk, tn), lambda j, i, kk: (kk, j))
    if out_cbm:
        o_spec = pl.BlockSpec((tn // LANES, tm, LANES), lambda j, i, kk: (j, i, 0))
        o_shape = jax.ShapeDtypeStruct((n // LANES, m, LANES), out_dtype)
    else:
        o_spec = pl.BlockSpec((tm, tn), lambda j, i, kk: (i, j))
        o_shape = jax.ShapeDtypeStruct((m, n), out_dtype)
    return pl.pallas_call(
        body, name=name, grid=(n // tn, m // tm, nk),
        in_specs=[a_spec, b_spec], out_specs=o_spec, out_shape=o_shape,
        scratch_shapes=[pltpu.VMEM((tm, tn), F32)],
        compiler_params=_params(("arbitrary", "arbitrary", "arbitrary"), VMEM_BIG),
    )(a, b)


def _ada_fwd(cvec, ada_w, ada_b):
    def body(c_ref, w_ref, b_ref, o_ref):
        o_ref[...] = _mm(_silu(c_ref[...]), w_ref[...], HI) + b_ref[...]

    return pl.pallas_call(
        body, name="ada_fwd", out_shape=jax.ShapeDtypeStruct((8, ada_w.shape[1]), F32),
        compiler_params=_params(None, VMEM_BIG),
    )(cvec, ada_w, ada_b)


def _ada_bwd(cvec, ada_w, dmods):
    def body(c_ref, w_ref, d_ref, o_ref):
        (_, vj) = jax.vjp(_silu, c_ref[...])
        o_ref[...] = vj(_mm_nt(d_ref[...], w_ref[...], HI))[0]

    return pl.pallas_call(
        body, name="ada_bwd", out_shape=jax.ShapeDtypeStruct(cvec.shape, F32),
        compiler_params=_params(None, VMEM_BIG),
    )(cvec, ada_w, dmods)


def _ada_wgrad(conds, dm_lat, dm_ctx):
    d = conds.shape[1]
    cols = dm_lat.shape[1]

    def body(c_ref, dl_ref, dc_ref, o_ref):
        dctx = dc_ref[0:1]
        for s in range(1, N_DEV):
            dctx = dctx + dc_ref[s:s + 1]
        rhs = jnp.concatenate([dl_ref[...], dctx, jnp.zeros((7, cols), F32)], axis=0)
        o_ref[...] = _mm_tn(_silu(c_ref[...]), rhs, HI)

    return pl.pallas_call(
        body, name="ada_wgrad", out_shape=jax.ShapeDtypeStruct((d, cols), F32),
    )(conds, dm_lat, dm_ctx)


def _prenorm_tile(xt, nw, sc, sh):
    r = lax.rsqrt(jnp.mean(xt * xt, axis=-1, keepdims=True) + NORM_EPS)
    return (xt * r * nw) * (1.0 + sc) + sh


def _tok_specs(l_lat, l_ctx, d, tm):
    nl, nc = l_lat // tm, l_ctx // tm
    lat = pl.BlockSpec((tm, d), lambda i: (jnp.minimum(i, nl - 1), 0))
    ctx = pl.BlockSpec((tm, d), lambda i: (jnp.maximum(i - nl, 0), 0))
    return nl, nc, lat, ctx


def _prenorm_fwd(x, ctx, norm_w, mods):
    l_lat, d = x.shape
    l_ctx = ctx.shape[0]
    tm = _pick(l_ctx, (256, 128, 64))
    nl, nc, lat_spec, ctx_spec = _tok_specs(l_lat, l_ctx, d, tm)

    def body(x_ref, c_ref, nw_ref, m_ref, h_ref):
        is_lat = pl.program_id(0) < nl
        xt = jnp.where(is_lat, x_ref[...], c_ref[...])
        row = jnp.where(is_lat, m_ref[0:1, :], m_ref[1:2, :])
        h_ref[...] = _prenorm_tile(xt, nw_ref[...], row[:, d:2 * d], row[:, 0:d]).astype(BF16)

    return pl.pallas_call(
        body, name="prenorm_fwd", grid=(nl + nc,),
        in_specs=[lat_spec, ctx_spec, pl.BlockSpec((1, d), lambda i: (0, 0)), pl.BlockSpec((8, 3 * d), lambda i: (0, 0))],
        out_specs=pl.BlockSpec((tm, d), lambda i: (i, 0)),
        out_shape=jax.ShapeDtypeStruct((l_lat + l_ctx, d), BF16),
        compiler_params=_params(("arbitrary",)),
    )(x, ctx, norm_w, mods)


def _prenorm_bwd(x, ctx, norm_w, mods, dh, dx_res):
    l_lat, d = x.shape
    l_ctx = ctx.shape[0]
    tm = _pick(l_ctx, (256, 128, 64))
    nl, nc, lat_spec, ctx_spec = _tok_specs(l_lat, l_ctx, d, tm)

    def body(x_ref, c_ref, nw_ref, m_ref, dh_ref, dr_ref, gx_ref, dnw_ref, dm_ref):
        i = pl.program_id(0)
        is_lat = i < nl

        @pl.when(i == 0)
        def _():
            dnw_ref[...] = jnp.zeros_like(dnw_ref)
            dm_ref[...] = jnp.zeros_like(dm_ref)

        xt = jnp.where(is_lat, x_ref[...], c_ref[...])
        row = jnp.where(is_lat, m_ref[0:1, :], m_ref[1:2, :])
        _, vj = jax.vjp(_prenorm_tile, xt, nw_ref[...], row[:, d:2 * d], row[:, 0:d])
        dxt, dnw, dsc, dsh = vj(dh_ref[...])
        dnw_ref[...] += dnw
        upd = jnp.concatenate([dsh, dsc, jnp.zeros_like(dsh)], axis=1)

        @pl.when(is_lat)
        def _():
            gx_ref[...] = dr_ref[...] + dxt
            dm_ref[0:1, :] += upd

        @pl.when(jnp.logical_not(is_lat))
        def _():
            dm_ref[1:2, :] += upd

    return pl.pallas_call(
        body, name="prenorm_bwd", grid=(nl + nc,),
        in_specs=[lat_spec, ctx_spec, pl.BlockSpec((1, d), lambda i: (0, 0)), pl.BlockSpec((8, 3 * d), lambda i: (0, 0)),
                  pl.BlockSpec((tm, d), lambda i: (i, 0)), lat_spec],
        out_specs=[lat_spec, pl.BlockSpec((1, d), lambda i: (0, 0)), pl.BlockSpec((8, 3 * d), lambda i: (0, 0))],
        out_shape=[jax.ShapeDtypeStruct((l_lat, d), F32), jax.ShapeDtypeStruct((1, d), F32),
                   jax.ShapeDtypeStruct((8, 3 * d), F32)],
        compiler_params=_params(("arbitrary",)),
    )(x, ctx, norm_w, mods, dh, dx_res)


def _conv_call(src, wts, l_lat, first_block, n_blocks, name, dout=None):
    t = src.shape[1]
    l_ctx = t - l_lat
    rc = _pick(l_lat, (512, 256, 128, 64))
    n_rc = l_lat // rc
    bwd = dout is not None

    def shifted(xc, n, period):
        pos = lax.broadcasted_iota(jnp.int32, (n, LANES), 0) % period
        xm = jnp.where(pos == 0, 0.0, pltpu.roll(xc, 1, 0))
        xp = jnp.where(pos == period - 1, 0.0, pltpu.roll(xc, n - 1, 0))
        return xm, xc, xp

    def fill(ref, sm, s0, sp):
        zero = jnp.zeros((GRID_W, LANES), F32)
        for s in (sm, s0, sp):
            s[pl.ds(0, GRID_W), :] = zero
            s[pl.ds(l_lat + GRID_W, GRID_W), :] = zero

        def step(i, carry):
            st = pl.multiple_of(i * rc, rc)
            xm, x0, xp = shifted(ref[0, pl.ds(st, rc), :], rc, GRID_W)
            sm[pl.ds(st + GRID_W, rc), :] = xm
            s0[pl.ds(st + GRID_W, rc), :] = x0
            sp[pl.ds(st + GRID_W, rc), :] = xp
            return carry

        lax.fori_loop(0, n_rc, step, 0)

    def apply(w, sm, s0, sp, out_ref, flip):
        def step(i, carry):
            st = pl.multiple_of(i * rc, rc)
            acc = jnp.zeros((rc, LANES), F32)
            for di in range(3):
                for dj, s in enumerate((sm, s0, sp)):
                    kidx = (2 - di) * 3 + (2 - dj) if flip else di * 3 + dj
                    acc = acc + w[kidx:kidx + 1, :] * s[pl.ds(st + di * GRID_W, rc), :]
            out_ref[0, pl.ds(st, rc), :] = acc
            return carry

        lax.fori_loop(0, n_rc, step, 0)

    def ctx_apply(w, xc, flip):
        xm, x0, xp = shifted(xc, l_ctx, l_ctx)
        ks = (5, 4, 3) if flip else (3, 4, 5)
        return w[ks[0]:ks[0] + 1, :] * xm + w[4:5, :] * x0 + w[ks[2]:ks[2] + 1, :] * xp

    def fwd_body(x_ref, w_ref, o_ref, sm, s0, sp):
        w = w_ref[0]
        fill(x_ref, sm, s0, sp)
        apply(w, sm, s0, sp, o_ref, False)
        o_ref[0, pl.ds(l_lat, l_ctx), :] = ctx_apply(w, x_ref[0, pl.ds(l_lat, l_ctx), :], False)

    def bwd_body(x_ref, w_ref, d_ref, dx_ref, dw_ref, sm, s0, sp):
        w = w_ref[0]
        fill(x_ref, sm, s0, sp)

        def step(i, acc):
            st = pl.multiple_of(i * rc, rc)
            dc = d_ref[0, pl.ds(st, rc), :]
            rows = []
            for di in range(3):
                for s in (sm, s0, sp):
                    rows.append(jnp.sum(dc * s[pl.ds(st + di * GRID_W, rc), :], axis=0, keepdims=True))
            return acc + jnp.concatenate(rows + [jnp.zeros((7, LANES), F32)], axis=0)

        acc = lax.fori_loop(0, n_rc, step, jnp.zeros((16, LANES), F32))
        dctx = d_ref[0, pl.ds(l_lat, l_ctx), :]
        xm, x0, xp = shifted(x_ref[0, pl.ds(l_lat, l_ctx), :], l_ctx, l_ctx)
        crow = [jnp.sum(dctx * s, axis=0, keepdims=True) for s in (xm, x0, xp)]
        acc = acc + jnp.concatenate([jnp.zeros((3, LANES), F32)] + crow + [jnp.zeros((10, LANES), F32)], axis=0)
        dw_ref[0] = acc
        fill(d_ref, sm, s0, sp)
        apply(w, sm, s0, sp, dx_ref, True)
        dx_ref[0, pl.ds(l_lat, l_ctx), :] = ctx_apply(w, dctx, True)

    blk = lambda off: pl.BlockSpec((1, t, LANES), lambda j: (j + off, 0, 0))
    w_spec = pl.BlockSpec((1, 16, LANES), lambda j: (j, 0, 0))
    scratch = [pltpu.VMEM((l_lat + 2 * GRID_W, LANES), F32)] * 3
    out_t = jax.ShapeDtypeStruct((n_blocks, t, LANES), F32)
    if not bwd:
        return pl.pallas_call(
            fwd_body, name=name, grid=(n_blocks,), in_specs=[blk(first_block), w_spec], out_specs=blk(0),
            out_shape=out_t, scratch_shapes=scratch, compiler_params=_params(("arbitrary",), VMEM_BIG),
        )(src, wts)
    return pl.pallas_call(
        bwd_body, name=name, grid=(n_blocks,), in_specs=[blk(first_block), w_spec, blk(0)],
        out_specs=[blk(0), w_spec], out_shape=[out_t, jax.ShapeDtypeStruct((n_blocks, 16, LANES), F32)],
        scratch_shapes=scratch, compiler_params=_params(("arbitrary",), VMEM_BIG),
    )(src, wts, dout)


def _scan_consts(kind, g, h, rev, d_index):
    n = g * CHUNK
    i = jnp.arange(n, dtype=jnp.int32)
    head, pos = i // CHUNK, i % CHUNK
    p = (CHUNK - 1 - pos) if rev else pos
    same_head = head[:, None] == head[None, :]
    pr, pc = p[:, None], p[None, :]
    f = lambda m: m.astype(F32)
    incl = same_head & (pc <= pr)
    out = {"eye": f(i[:, None] == i[None, :]), "incl": f(incl), "incl_t": f(incl).T}
    if kind == "hg":
        mid, same, sec = [], [], []
        for half in HALVES:
            width = 2 * half
            blk = p // width
            second = (p % width) >= half
            ref_pos = blk * width + half - 1
            mid.append(same_head & (pc <= ref_pos[:, None]))
            same.append(same_head & (blk[:, None] == blk[None, :]))
            sec.append(jnp.broadcast_to(second[:, None], (n, LANES)))
        mid = f(jnp.concatenate(mid, axis=0))
        out.update(mid=mid, mid_t=mid.T, same=f(jnp.stack(same)), sec=f(jnp.stack(sec)))
    else:
        low = []
        for half in reversed(HALVES):
            width = 2 * half
            blk = p // width
            second = (p % width) >= half
            low.append(same_head & (blk[:, None] == blk[None, :]) & second[:, None] & jnp.logical_not(second[None, :]))
        lane = jnp.arange(LANES, dtype=jnp.int32)[None, :]
        hh = jnp.arange(h * CHUNK, dtype=jnp.int32) // CHUNK
        out.update(strict=f(same_head & (pc < pr)), ones=jnp.ones((n, n), F32), low=f(jnp.stack(low)),
                   oh_a=f(lane == (d_index * h + hh)[:, None]), oh_b=f(lane == (2 * h + d_index * h + hh)[:, None]))
    return out


def _const_mm(c, c_t, diff, prec=HIGH):
    if not diff:
        return lambda x: _mm(c, x, prec)

    @jax.custom_vjp
    def f(x):
        return _mm(c, x, prec)

    f.defvjp(lambda x: (_mm(c, x, prec), None), lambda _, ct: (_mm(c_t, ct, HIGH),))
    return f


def _kept_inverse(x_kept):
    @jax.custom_vjp
    def f(a):
        return x_kept

    f.defvjp(lambda a: (x_kept, None), lambda _, ct: (-_mm_nt(_mm_tn(x_kept, ct, HIGH), x_kept, HIGH),))
    return f


def _stack_helpers(g):
    n = g * CHUNK
    rows = lambda vec: jnp.broadcast_to(vec, (g, CHUNK, LANES)).reshape(n, LANES)
    per_head = lambda t: [t[i * CHUNK:(i + 1) * CHUNK] for i in range(g)]
    head_sum = lambda t: rows(jnp.sum(t.reshape(g, CHUNK, LANES), axis=1, keepdims=True))
    return rows, per_head, head_sum


def _hg_chunk(raw_q, raw_f, raw_i, l0, l1, sts, cst, g, diff=False):
    n = g * CHUNK
    rows, per_head, head_sum = _stack_helpers(g)
    lb = rows(jax.nn.sigmoid(l0 - l1))
    q = _silu(raw_q) * (LANES ** -0.5)
    gl = jnp.log(lb + (1.0 - lb) * jax.nn.sigmoid(raw_f))
    k = (1.0 - lb) * jax.nn.sigmoid(-raw_f)
    v = raw_i
    b = _const_mm(cst["incl"], cst["incl_t"], diff)(gl)
    b_tot = head_sum(gl)
    q_dec = q * jnp.exp(b)
    k_dec = k * jnp.exp(b_tot - b)
    mids = _const_mm(cst["mid"], cst["mid_t"], diff, None)(gl)
    a = cst["eye"] * jnp.sum(q * k, axis=1, keepdims=True)
    for lv in range(len(HALVES)):
        r = mids[lv * n:(lv + 1) * n]
        sec = cst["sec"][lv]
        fst = 1.0 - sec
        qt = q * jnp.exp((b - r) * sec) * sec
        kt = k * jnp.exp((r - b) * fst) * fst
        a = a + _mm_nt(qt, kt) * cst["same"][lv]
    o_intra = per_head(_mm(a, v))
    decay = per_head(jnp.exp(b_tot))
    qd, kd, vs = per_head(q_dec), per_head(k_dec), per_head(v)
    outs = [o_intra[i] + _mm_nt(qd[i], sts[i]) for i in range(g)]
    new = [sts[i] * decay[i][0:1] + _mm_tn(vs[i], kd[i]) for i in range(g)]
    return jnp.concatenate(outs, axis=0), new


def _gdn_chunk(u_q, u_k, u_v, tail, avec, dtvec, ss, cst, g, x_kept=None):
    n = g * CHUNK
    diff = x_kept is not None
    rows, per_head, head_sum = _stack_helpers(g)

    def l2n(t):
        return t * lax.rsqrt(jnp.sum(t * t, axis=-1, keepdims=True) + NORM_EPS)

    q = l2n(_silu(u_q)) * (LANES ** -0.5)
    k = l2n(_silu(u_k))
    v = _silu(u_v)
    tail_n = jnp.concatenate([tail] * g, axis=0)
    za = jnp.sum(tail_n * cst["oh_a"], axis=1, keepdims=True) + rows(dtvec)
    beta = jax.nn.sigmoid(jnp.sum(tail_n * cst["oh_b"], axis=1, keepdims=True))
    gl = -jnp.exp(rows(avec)) * (jnp.maximum(za, 0.0) + jnp.log1p(jnp.exp(-jnp.abs(za))))
    b = _const_mm(cst["incl"], cst["incl_t"], diff)(gl)
    b_tot = head_sum(gl)
    bb = jnp.concatenate([b] * (n // LANES), axis=1)
    b_row = _const_mm(cst["ones"], cst["ones"], diff)(cst["eye"] * bb)
    bdiff = bb - b_row
    a_mat = beta * _mm_nt(k, k) * jnp.exp(jnp.where(cst["strict"] > 0.5, bdiff, NEG))
    if diff:
        xinv = _kept_inverse(x_kept)(a_mat)
    else:
        xinv = cst["eye"] - a_mat * cst["low"][0]
        for lv in range(1, len(HALVES)):
            xinv = xinv - _mm(_mm(xinv, a_mat * cst["low"][lv]), xinv)
        xinv = xinv + _mm(xinv, cst["eye"] - _mm(cst["eye"] + a_mat, xinv, HIGH), HIGH)
    rhs = jnp.concatenate([beta * v, beta * jnp.exp(b) * k], axis=1)
    sol = _mm(xinv, rhs, HIGH)
    u0, w = per_head(sol[:, :LANES]), per_head(sol[:, LANES:])
    k_dec = per_head(k * jnp.exp(b_tot - b))
    q_dec = per_head(q * jnp.exp(b))
    decay = per_head(jnp.exp(b_tot))
    p = _mm_nt(q, k) * jnp.exp(jnp.where(cst["incl"] > 0.5, bdiff, NEG))
    v_new = [u0[i] - _mm(w[i], ss[i]) for i in range(g)]
    o_state = [_mm(q_dec[i], ss[i]) for i in range(g)]
    o = _mm(p, jnp.concatenate(v_new, axis=0)) + jnp.concatenate(o_state, axis=0)
    new = [decay[i][0:1] * ss[i] + _mm_tn(k_dec[i], v_new[i]) for i in range(g)]
    return o, new, xinv


def _chunk_index(l_lat, l_ctx, rev):
    rows = STEP_CHUNKS * CHUNK
    nl, nc = l_lat // rows, l_ctx // rows

    def idx(i):
        if rev:
            return jnp.where(i < nc, nl + nc - 1 - i, nl - 1 - (i - nc))
        return jnp.where(i < nc, nl + i, i - nc)

    subs = list(range(STEP_CHUNKS))
    return nl + nc, idx, (subs[::-1] if rev else subs)


def _const_args(kind, g, h, rev, d_index):
    consts = _scan_consts(kind, g, h, rev, d_index)
    names = sorted(consts)
    arrs = [consts[k] for k in names]
    specs = [pl.BlockSpec(a.shape, functools.partial(lambda nd, i: (0,) * nd, a.ndim)) for a in arrs]
    return names, arrs, specs


def _load_consts(names, refs, g0, g):
    cst = {}
    for k, r in zip(names, refs):
        if k in ("oh_a", "oh_b"):
            cst[k] = r[pl.ds(pl.multiple_of(g0 * CHUNK, CHUNK), g * CHUNK), :]
        else:
            cst[k] = r[...]
    return cst


def _scan_fwd(kind, rev, cbm, segs, vecs, o_prev, l_lat, l_ctx, h, name, tail=None, d_index=0):
    n_steps, cidx, subs = _chunk_index(l_lat, l_ctx, rev)
    rows = STEP_CHUNKS * CHUNK
    t = l_lat + l_ctx
    n_in = len(segs)
    gdn = kind == "gdn"
    g = min(HEADS_PER_GROUP, h)
    n = g * CHUNK
    c_names, c_arrs, c_specs = _const_args(kind, g, h, rev, d_index)

    def body(*refs):
        in_refs = refs[:n_in]
        pos = n_in
        tail_ref = None
        if gdn:
            tail_ref = refs[pos]
            pos += 1
        v0_ref, v1_ref = refs[pos], refs[pos + 1]
        pos += 2
        prev_ref = None
        if o_prev is not None:
            prev_ref = refs[pos]
            pos += 1
        c_refs = refs[pos:pos + len(c_names)]
        pos += len(c_names)
        o_ref, st_ref = refs[pos], refs[pos + 1]
        pos += 2
        x_ref = None
        if gdn:
            x_ref = refs[pos]
            pos += 1
        s_scr = refs[pos]

        @pl.when(pl.program_id(0) == 0)
        def _():
            s_scr[...] = jnp.zeros_like(s_scr)

        def chunk(k, sub):
            rs = pl.ds(sub * CHUNK, CHUNK)

            def group(gi, carry):
                g0 = pl.multiple_of(gi * g, g)
                hs = pl.ds(g0, g)
                cst = _load_consts(c_names, c_refs, g0, g)
                states = [s_scr[g0 + i] for i in range(g)]
                for i in range(g):
                    st_ref[k, g0 + i] = states[i]
                ins = [r[hs, rs, :].reshape(n, LANES) for r in in_refs]
                if gdn:
                    o, new, xinv = _gdn_chunk(*ins, tail_ref[0, rs, :], v0_ref[hs], v1_ref[hs], states, cst, g)
                    x_ref[k, gi] = xinv
                else:
                    o, new = _hg_chunk(*ins, v0_ref[hs], v1_ref[hs], states, cst, g)
                o = o.reshape(g, CHUNK, LANES)
                if prev_ref is not None:
                    o = o + prev_ref[hs, rs, :]
                o_ref[hs, rs, :] = o
                for i in range(g):
                    s_scr[g0 + i] = new[i]
                return carry

            lax.fori_loop(0, h // g, group, 0, unroll=GROUP_UNROLL)

        for k, sub in enumerate(subs):
            chunk(k, sub)

    seg_spec = lambda sg: pl.BlockSpec((h, rows, LANES), lambda i: (sg, cidx(i), 0))
    vec_spec = pl.BlockSpec((h, 1, LANES), lambda i: (0, 0, 0))
    in_specs = [seg_spec(sg) for sg in segs]
    args = [cbm] * n_in
    if gdn:
        in_specs.append(pl.BlockSpec((1, rows, LANES), lambda i: (0, cidx(i), 0)))
        args.append(tail)
    in_specs += [vec_spec, vec_spec]
    args += list(vecs)
    if o_prev is not None:
        in_specs.append(seg_spec(0))
        args.append(o_prev)
    in_specs += c_specs
    args += c_arrs
    return pl.pallas_call(
        body, name=name, grid=(n_steps,), in_specs=in_specs,
        out_specs=[seg_spec(0), pl.BlockSpec((STEP_CHUNKS, h, LANES, LANES), lambda i: (i, 0, 0, 0))]
        + ([pl.BlockSpec((STEP_CHUNKS, h // g, n, n), lambda i: (i, 0, 0, 0))] if gdn else []),
        out_shape=[jax.ShapeDtypeStruct((h, t, LANES), F32),
                   jax.ShapeDtypeStruct((n_steps * STEP_CHUNKS, h, LANES, LANES), F32)]
        + ([jax.ShapeDtypeStruct((n_steps * STEP_CHUNKS, h // g, n, n), F32)] if gdn else []),
        scratch_shapes=[pltpu.VMEM((h, LANES, LANES), F32)],
        compiler_params=_params(("arbitrary",), VMEM_BIG),
    )(*args)


def _scan_bwd(kind, rev, cbm, segs, vecs, states, d_o, acc, l_lat, l_ctx, h, name, tail=None, d_index=0, x_kept=None):
    n_steps, cidx, subs = _chunk_index(l_lat, l_ctx, rev)
    rows = STEP_CHUNKS * CHUNK
    t = l_lat + l_ctx
    n_in = len(segs)
    gdn = kind == "gdn"
    n_grad = n_in + (1 if gdn else 0)
    assert len(acc) == n_grad
    step_of = lambda j: n_steps - 1 - j
    g = min(HEADS_PER_GROUP, h)
    n = g * CHUNK
    c_names, c_arrs, c_specs = _const_args(kind, g, h, rev, d_index)

    def body(*refs):
        in_refs = refs[:n_in]
        pos = n_in
        tail_ref = None
        if gdn:
            tail_ref = refs[pos]
            pos += 1
        v0_ref, v1_ref, st_ref, do_ref = refs[pos:pos + 4]
        pos += 4
        xk_ref = None
        if gdn:
            xk_ref = refs[pos]
            pos += 1
        acc_refs = []
        for a in acc:
            if a is None:
                acc_refs.append(None)
            else:
                acc_refs.append(refs[pos])
                pos += 1
        c_refs = refs[pos:pos + len(c_names)]
        pos += len(c_names)
        g_refs = refs[pos:pos + n_grad]
        dv0_ref, dv1_ref, ds_scr = refs[pos + n_grad:pos + n_grad + 3]

        @pl.when(pl.program_id(0) == 0)
        def _():
            ds_scr[...] = jnp.zeros_like(ds_scr)
            dv0_ref[...] = jnp.zeros_like(dv0_ref)
            dv1_ref[...] = jnp.zeros_like(dv1_ref)

        if gdn:
            g_refs[n_in][0] = jnp.zeros((rows, LANES), F32) if acc_refs[n_in] is None else acc_refs[n_in][0]

        def chunk(k, sub):
            rs = pl.ds(sub * CHUNK, CHUNK)

            def group(gi, carry):
                g0 = pl.multiple_of(gi * g, g)
                hs = pl.ds(g0, g)
                cst = _load_consts(c_names, c_refs, g0, g)
                sts = [st_ref[k, g0 + i] for i in range(g)]
                ins = [r[hs, rs, :].reshape(n, LANES) for r in in_refs]
                if gdn:
                    xk = xk_ref[k, gi]
                    fn = lambda uq, uk, uv, tl, av, dt, ss: _gdn_chunk(uq, uk, uv, tl, av, dt, ss, cst, g, xk)[:2]
                    prim = ins + [tail_ref[0, rs, :], v0_ref[hs], v1_ref[hs], sts]
                else:
                    fn = lambda rq, rf, ri, l0, l1, ss: _hg_chunk(rq, rf, ri, l0, l1, ss, cst, g, True)
                    prim = ins + [v0_ref[hs], v1_ref[hs], sts]
                _, vj = jax.vjp(fn, *prim)
                grads = vj((do_ref[hs, rs, :].reshape(n, LANES), [ds_scr[g0 + i] for i in range(g)]))
                for m in range(n_in):
                    gm = grads[m].reshape(g, CHUNK, LANES)
                    if acc_refs[m] is not None:
                        gm = gm + acc_refs[m][hs, rs, :]
                    g_refs[m][hs, rs, :] = gm
                if gdn:
                    g_refs[n_in][0, rs, :] += grads[n_in]
                dv0 = grads[n_grad]
                dv1 = grads[n_grad + 1]
                if gdn:
                    dv0 = jnp.broadcast_to(jnp.sum(dv0, axis=2, keepdims=True), dv0.shape)
                    dv1 = jnp.broadcast_to(jnp.sum(dv1, axis=2, keepdims=True), dv1.shape)
                dv0_ref[hs] += dv0
                dv1_ref[hs] += dv1
                for i in range(g):
                    ds_scr[g0 + i] = grads[n_grad + 2][i]
                return carry

            lax.fori_loop(0, h // g, group, 0, unroll=GROUP_UNROLL)

        for k in reversed(range(STEP_CHUNKS)):
            chunk(k, subs[k])

    seg_spec = lambda sg: pl.BlockSpec((h, rows, LANES), lambda j: (sg, cidx(step_of(j)), 0))
    tail_spec = pl.BlockSpec((1, rows, LANES), lambda j: (0, cidx(step_of(j)), 0))
    vec_spec = pl.BlockSpec((h, 1, LANES), lambda j: (0, 0, 0))
    in_specs = [seg_spec(sg) for sg in segs]
    args = [cbm] * n_in
    if gdn:
        in_specs.append(tail_spec)
        args.append(tail)
    in_specs += [vec_spec, vec_spec, pl.BlockSpec((STEP_CHUNKS, h, LANES, LANES), lambda j: (step_of(j), 0, 0, 0)),
                 seg_spec(0)]
    args += list(vecs) + [states, d_o]
    if gdn:
        in_specs.append(pl.BlockSpec((STEP_CHUNKS, h // g, n, n), lambda j: (step_of(j), 0, 0, 0)))
        args.append(x_kept)
    for k, a in enumerate(acc):
        if a is not None:
            in_specs.append(tail_spec if (gdn and k == n_in) else seg_spec(0))
            args.append(a)
    in_specs += c_specs
    args += c_arrs
    out_specs = [seg_spec(0)] * n_in + ([tail_spec] if gdn else []) + [vec_spec, vec_spec]
    out_shape = ([jax.ShapeDtypeStruct((h, t, LANES), F32)] * n_in
                 + ([jax.ShapeDtypeStruct((1, t, LANES), F32)] if gdn else [])
                 + [jax.ShapeDtypeStruct((h, 1, LANES), F32)] * 2)
    return pl.pallas_call(
        body, name=name, grid=(n_steps,), in_specs=in_specs, out_specs=out_specs, out_shape=out_shape,
        scratch_shapes=[pltpu.VMEM((h, LANES, LANES), F32)],
        compiler_params=_params(("arbitrary",), VMEM_BIG),
    )(*args)


def _gout_tile(oa, ob, za, zb, naw, nbw):
    def hn(o, w):
        return o * lax.rsqrt(jnp.mean(o * o, axis=-1, keepdims=True) + NORM_EPS) * w

    ya = _silu(za) * hn(oa, naw)
    yb = _silu(zb) * hn(ob, nbw)
    nh = oa.shape[0]
    return jnp.concatenate([ya[i] for i in range(nh)] + [yb[i] for i in range(nh)], axis=1)


def _gout_fwd(oa, ob, y_cbm, naw, nbw, l_lat, h):
    tm = _pick(l_lat, (256, 128, 64))
    blk = lambda sg: pl.BlockSpec((h, tm, LANES), lambda i: (sg, i, 0))
    vec = pl.BlockSpec((h, 1, LANES), lambda i: (0, 0, 0))

    def body(oa_ref, ob_ref, za_ref, zb_ref, na_ref, nb_ref, y_ref):
        y_ref[...] = _gout_tile(oa_ref[...], ob_ref[...], za_ref[...], zb_ref[...], na_ref[...], nb_ref[...]).astype(BF16)

    return pl.pallas_call(
        body, name="gout_fwd", grid=(l_lat // tm,),
        in_specs=[blk(0), blk(0), blk(4), blk(8), vec, vec],
        out_specs=pl.BlockSpec((tm, 2 * h * LANES), lambda i: (i, 0)),
        out_shape=jax.ShapeDtypeStruct((l_lat, 2 * h * LANES), BF16),
        compiler_params=_params(("arbitrary",)),
    )(oa, ob, y_cbm, y_cbm, naw, nbw)


def _gout_bwd(oa, ob, y_cbm, naw, nbw, dymix, l_lat, l_ctx, h):
    tm = _pick(l_ctx, (256, 128, 64))
    nl, nc = l_lat // tm, l_ctx // tm
    t = l_lat + l_ctx
    blk_in = lambda sg: pl.BlockSpec((h, tm, LANES), lambda i: (sg, jnp.minimum(i, nl - 1), 0))
    blk_out = pl.BlockSpec((h, tm, LANES), lambda i: (0, i, 0))
    vec = pl.BlockSpec((h, 1, LANES), lambda i: (0, 0, 0))

    def body(oa_ref, ob_ref, za_ref, zb_ref, na_ref, nb_ref, dy_ref, doa_ref, dob_ref, dza_ref, dzb_ref, dna_ref, dnb_ref):
        i = pl.program_id(0)

        @pl.when(i == 0)
        def _():
            dna_ref[...] = jnp.zeros_like(dna_ref)
            dnb_ref[...] = jnp.zeros_like(dnb_ref)

        @pl.when(i < nl)
        def _():
            _, vj = jax.vjp(_gout_tile, oa_ref[...], ob_ref[...], za_ref[...], zb_ref[...], na_ref[...], nb_ref[...])
            doa, dob, dza, dzb, dna, dnb = vj(dy_ref[...])
            doa_ref[...] = doa
            dob_ref[...] = dob
            dza_ref[...] = dza
            dzb_ref[...] = dzb
            dna_ref[...] += dna
            dnb_ref[...] += dnb

        @pl.when(i >= nl)
        def _():
            for r in (doa_ref, dob_ref, dza_ref, dzb_ref):
                r[...] = jnp.zeros_like(r)

    big = jax.ShapeDtypeStruct((h, t, LANES), F32)
    small = jax.ShapeDtypeStruct((h, 1, LANES), F32)
    return pl.pallas_call(
        body, name="gout_bwd", grid=(nl + nc,),
        in_specs=[blk_in(0), blk_in(0), blk_in(4), blk_in(8), vec, vec,
                  pl.BlockSpec((tm, 2 * h * LANES), lambda i: (jnp.minimum(i, nl - 1), 0))],
        out_specs=[blk_out] * 4 + [vec, vec], out_shape=[big] * 4 + [small, small],
        compiler_params=_params(("arbitrary",)),
    )(oa, ob, y_cbm, y_cbm, naw, nbw, dymix)


def _head_tile(xt, mixt, tgt, gt, fw):
    xo = xt + gt * mixt
    y = xo * lax.rsqrt(jnp.mean(xo * xo, axis=-1, keepdims=True) + NORM_EPS) * fw
    err = y - tgt
    return 0.5 * jnp.sum(jnp.mean(err * err, axis=-1, keepdims=True), axis=0, keepdims=True)


def _loss_head(x, mix, target, mods, final_w):
    l_lat, d = x.shape
    tm = _pick(l_lat, (256, 128, 64))
    tok = pl.BlockSpec((tm, d), lambda i: (i, 0))
    row = pl.BlockSpec((1, d), lambda i: (0, 0))

    def body(x_ref, m_ref, t_ref, mod_ref, fw_ref, loss_ref, dmix_ref, dx_ref, dgt_ref, dfw_ref):
        @pl.when(pl.program_id(0) == 0)
        def _():
            loss_ref[...] = jnp.zeros_like(loss_ref)
            dgt_ref[...] = jnp.zeros_like(dgt_ref)
            dfw_ref[...] = jnp.zeros_like(dfw_ref)

        gt = mod_ref[0:1, 2 * d:3 * d]
        fn = lambda xt, mt, g, fw: _head_tile(xt, mt, t_ref[...], g, fw)
        val, vj = jax.vjp(fn, x_ref[...], m_ref[...], gt, fw_ref[...])
        dxt, dmt, dgt, dfw = vj(jnp.ones((1, 1), F32))
        loss_ref[...] += jnp.broadcast_to(val, loss_ref.shape)
        dmix_ref[...] = dmt.astype(BF16)
        dx_ref[...] = dxt
        dgt_ref[...] += dgt
        dfw_ref[...] += dfw

    return pl.pallas_call(
        body, name="loss_head", grid=(l_lat // tm,),
        in_specs=[tok, tok, tok, pl.BlockSpec((8, 3 * d), lambda i: (0, 0)), row],
        out_specs=[pl.BlockSpec((8, LANES), lambda i: (0, 0)), tok, tok, row, row],
        out_shape=[jax.ShapeDtypeStruct((8, LANES), F32), jax.ShapeDtypeStruct((l_lat, d), BF16),
                   jax.ShapeDtypeStruct((l_lat, d), F32), jax.ShapeDtypeStruct((1, d), F32),
                   jax.ShapeDtypeStruct((1, d), F32)],
        compiler_params=_params(("arbitrary",)),
    )(x, mix, target, mods, final_w)


def _adamw(w, g_slots, m, v, name):
    rows, cols = w.shape
    n = g_slots.shape[0]
    tr = _pick(rows, (64, 32, 16, 8))

    def body(w_ref, g_ref, m_ref, v_ref, go_ref, d_ref, nm_ref, nv_ref):
        gg = g_ref[0].astype(F32)
        for s in range(1, n):
            gg = gg + g_ref[s].astype(F32)
        m2 = ADAM_B1 * m_ref[...] + (1.0 - ADAM_B1) * gg
        v2 = ADAM_B2 * v_ref[...] + (1.0 - ADAM_B2) * (gg * gg)
        m_hat = m2 / (1.0 - ADAM_B1 ** ADAM_STEP)
        v_hat = v2 / (1.0 - ADAM_B2 ** ADAM_STEP)
        go_ref[...] = gg
        d_ref[...] = -ADAM_LR * (m_hat / (jnp.sqrt(v_hat) + ADAM_EPS) + ADAM_WD * w_ref[...])
        nm_ref[...] = m2
        nv_ref[...] = v2

    spec = pl.BlockSpec((tr, cols), lambda i: (i, 0))
    shp = jax.ShapeDtypeStruct((rows, cols), F32)
    return pl.pallas_call(
        body, name=name, grid=(rows // tr,),
        in_specs=[spec, pl.BlockSpec((n, tr, cols), lambda i: (0, i, 0)), spec, spec],
        out_specs=[spec] * 4, out_shape=[shp] * 4,
        compiler_params=_params(("arbitrary",)),
    )(w, g_slots, m, v)


def _adamw_nd(w, g_slots, m, v, name):
    shape = w.shape
    two_d = (-1, shape[-1])
    w2 = w.reshape(two_d)
    outs = _adamw(w2, g_slots.reshape((g_slots.shape[0],) + w2.shape), m.reshape(two_d), v.reshape(two_d), name)
    return [o.reshape(shape) for o in outs]


def kernel(x, c, ctx, c_ctx, norm_w, ada_w, ada_b, w_in, conv_w, hg_lb_logits, gdn_a_log, gdn_dt_bias, ha_norm_w, hb_norm_w, w_out, final_norm_w, loss_target, m_c_ctx, m_norm_w, m_ada_w, m_ada_b, m_w_in, m_conv_w, m_hg_lb_logits, m_gdn_a_log, m_gdn_dt_bias, m_ha_norm_w, m_hb_norm_w, m_w_out, m_final_norm_w, v_c_ctx, v_norm_w, v_ada_w, v_ada_b, v_w_in, v_conv_w, v_hg_lb_logits, v_gdn_a_log, v_gdn_dt_bias, v_ha_norm_w, v_hb_norm_w, v_w_out, v_final_norm_w):
    l_lat, d = x.shape[1], x.shape[2]
    l_ctx = ctx.shape[1]
    t = l_lat + l_ctx
    h = d // LANES
    n_in = 9 * d + 4 * h
    nb = -(-(9 * h + 1) // COLS_PER_TILE) * COLS_PER_TILE
    n_pad = nb * LANES
    win_c, ada_c, conv_c, lb_c, wout_r = w_in.shape[2], ada_w.shape[2], conv_w.shape[3], hg_lb_logits.shape[2], w_out.shape[1]
    me = _me()

    gw_in, gw_out, gw_ada, gw_conv, gw_lb = _gather_two_level(
        [w_in[0].astype(BF16), w_out[0].astype(BF16), ada_w[0], conv_w[0], hg_lb_logits], "gather_weights")
    join = lambda g, axis: jnp.concatenate([g[k] for k in range(N_DEV)], axis=axis)
    w_in_full = join(gw_in, 1)
    w_out_full = gw_out.reshape(N_DEV * wout_r, d)
    ada_full = join(gw_ada, 1)
    conv_full = join(gw_conv, 2)
    lb_full = join(gw_lb, 2)
    w_pad = jnp.pad(w_in_full, ((0, 0), (0, n_pad - n_in)))
    w_pad_t = w_pad.T
    w_out_t = w_out_full.T

    lbl = lb_full.reshape(2, 2, h, 1, LANES)
    conv_rows = jnp.pad(conv_full.reshape(9, 3 * h, LANES).transpose(1, 0, 2), ((0, 0), (0, 7), (0, 0)))
    lane_bc = lambda a: jnp.broadcast_to(a.reshape(2, h, 1, 1), (2, h, 1, LANES))
    avec, dtvec = lane_bc(gdn_a_log[0]), lane_bc(gdn_dt_bias[0])
    naw, nbw = ha_norm_w[0].reshape(h, 1, LANES), hb_norm_w[0].reshape(h, 1, LANES)

    cvec = jnp.concatenate([c, c_ctx[None, :], jnp.zeros((6, d), F32)], axis=0)
    mods = _ada_fwd(cvec, ada_full, ada_b)
    x2, ctx2, tgt2 = x[0], ctx[0], loss_target[0]
    h_all = _prenorm_fwd(x2, ctx2, norm_w, mods)
    y = _matmul(h_all, w_pad, "proj", out_cbm=True, tn=COLS_PER_TILE * LANES)
    tail = lax.slice_in_dim(y, 9 * h, 9 * h + 1, axis=0)
    u = _conv_call(y, conv_rows, l_lat, 5 * h, 3 * h, "conv_fwd")

    oa, st_af = _scan_fwd("hg", False, y, (0, 1, 3), (lbl[0, 0], lbl[1, 0]), None, l_lat, l_ctx, h, "hg_fwd_f")
    oa, st_ab = _scan_fwd("hg", True, y, (0, 2, 3), (lbl[0, 1], lbl[1, 1]), oa, l_lat, l_ctx, h, "hg_fwd_b")
    ob, st_bf, xk_f = _scan_fwd("gdn", False, u, (0, 1, 2), (avec[0], dtvec[0]), None, l_lat, l_ctx, h, "gdn_fwd_f", tail, 0)
    ob, st_bb, xk_b = _scan_fwd("gdn", True, u, (0, 1, 2), (avec[1], dtvec[1]), ob, l_lat, l_ctx, h, "gdn_fwd_b", tail, 1)

    ymix = _gout_fwd(oa, ob, y, naw, nbw, l_lat, h)
    mix = _matmul(ymix, w_out_full, "out_proj")
    loss_blk, dmix, dx_res, dgt, dfw = _loss_head(x2, mix, tgt2, mods, final_norm_w.reshape(1, d))

    dymix = _matmul(dmix, w_out_t, "d_ymix")
    dw_out = _matmul(ymix.T, dmix, "d_w_out")
    doa, dob, dza, dzb, dnaw, dnbw = _gout_bwd(oa, ob, y, naw, nbw, dymix, l_lat, l_ctx, h)

    gq, gff, gi, dl0f, dl1f = _scan_bwd("hg", False, y, (0, 1, 3), (lbl[0, 0], lbl[1, 0]), st_af, doa,
                                        [None, None, None], l_lat, l_ctx, h, "hg_bwd_f")
    gq, gfb, gi, dl0b, dl1b = _scan_bwd("hg", True, y, (0, 2, 3), (lbl[0, 1], lbl[1, 1]), st_ab, doa,
                                        [gq, None, gi], l_lat, l_ctx, h, "hg_bwd_b")
    guq, guk, guv, gtail, da_f, ddt_f = _scan_bwd("gdn", False, u, (0, 1, 2), (avec[0], dtvec[0]), st_bf, dob,
                                                  [None] * 4, l_lat, l_ctx, h, "gdn_bwd_f", tail, 0, xk_f)
    guq, guk, guv, gtail, da_b, ddt_b = _scan_bwd("gdn", True, u, (0, 1, 2), (avec[1], dtvec[1]), st_bb, dob,
                                                  [guq, guk, guv, gtail], l_lat, l_ctx, h, "gdn_bwd_b", tail, 1, xk_b)
    du = jnp.concatenate([guq, guk, guv], axis=0)
    dconv_in, dconv_rows = _conv_call(y, conv_rows, l_lat, 5 * h, 3 * h, "conv_bwd", dout=du)

    pad_blocks = nb - 9 * h - 1
    dy = jnp.concatenate([gq, gff, gfb, gi, dza, dconv_in, dzb, gtail]
                         + ([jnp.zeros((pad_blocks, t, LANES), F32)] if pad_blocks else []), axis=0)
    dh = _matmul(dy, w_pad_t, "d_h", a_cbm=True)
    dw_in = _matmul(h_all.T, dy, "d_w_in", b_cbm=True)
    grad_x, dnorm_w, dmods_pre = _prenorm_bwd(x2, ctx2, norm_w, mods, dh, dx_res)

    dmods = jnp.concatenate([dmods_pre[:, :2 * d],
                             jnp.concatenate([dgt, jnp.zeros((7, d), F32)], axis=0)], axis=1)
    dcond = _ada_bwd(cvec, ada_full, dmods)

    dw_in_s = jnp.stack([dw_in[:, k * win_c:(k + 1) * win_c] for k in range(N_DEV)])
    dw_out_s = dw_out.reshape(N_DEV, wout_r, d)
    dconv_s = dconv_rows[:, :9, :].transpose(1, 0, 2).reshape(3, 3, N_DEV, conv_c).transpose(2, 0, 1, 3)
    dlb_s = jnp.stack([jnp.stack([dl0f, dl0b]), jnp.stack([dl1f, dl1b])]).reshape(2, 2, N_DEV, lb_c).transpose(2, 0, 1, 3)
    slabs = [dw_in_s, dw_out_s, dconv_s, dlb_s]
    theirs = _swap_with_sibling([a.astype(BF16) for a in slabs], "swap_grads")
    pairs = [_pair_sum(a, b, "pair_sum_%d" % k) for k, (a, b) in enumerate(zip(slabs, theirs))]
    s_w_in, s_w_out, s_conv, s_lb = _scatter_over_chips(pairs, "scatter_grads")

    da = jnp.stack([da_f[:, 0, 0], da_b[:, 0, 0]])
    ddt = jnp.stack([ddt_f[:, 0, 0], ddt_b[:, 0, 0]])
    small_parts = [dcond[1], dnorm_w, dmods[0] + dmods[1], dnaw, dnbw, dfw, da, ddt, c[0], dmods[0], dmods[1]]
    small_blob, soff = _to_blob(small_parts)
    gathered, = _exchange([small_blob], "gather_small", True)
    summed = _sum_slots(gathered, "sum_small").reshape(-1)
    gflat = gathered.reshape(N_DEV, -1)
    take = lambda k, n: summed[soff[k]:soff[k] + n]
    one = lambda k, n, shape: take(k, n).reshape((1,) + shape)
    s_c_ctx = one(0, d, (d,))
    s_norm_w = one(1, d, (1, d))
    s_ada_b = one(2, 3 * d, (1, 3 * d))
    s_ha = one(3, d, (1, h, LANES))
    s_hb = one(4, d, (1, h, LANES))
    s_final = one(5, d, (d,))
    s_a_log = one(6, 2 * h, (1, 2, h))
    s_dt = one(7, 2 * h, (1, 2, h))
    conds = jnp.concatenate([gflat[:, soff[8]:soff[8] + d], c_ctx[None, :], jnp.zeros((7, d), F32)], axis=0)
    col0 = me * ada_c
    dm_lat = lax.dynamic_slice_in_dim(gflat[:, soff[9]:soff[9] + 3 * d], col0, ada_c, axis=1)
    dm_ctx = lax.dynamic_slice_in_dim(gflat[:, soff[10]:soff[10] + 3 * d], col0, ada_c, axis=1)
    s_ada_w = _ada_wgrad(conds, dm_lat, dm_ctx).reshape(1, 1, d, ada_c)

    loss = lax.psum(loss_blk[0, 0], ("x", "y", "c"))

    weights = [("c_ctx", c_ctx, s_c_ctx, m_c_ctx, v_c_ctx), ("norm_w", norm_w, s_norm_w, m_norm_w, v_norm_w),
               ("ada_w", ada_w, s_ada_w, m_ada_w, v_ada_w), ("ada_b", ada_b, s_ada_b, m_ada_b, v_ada_b),
               ("w_in", w_in, s_w_in, m_w_in, v_w_in), ("conv_w", conv_w, s_conv, m_conv_w, v_conv_w),
               ("hg_lb_logits", hg_lb_logits, s_lb, m_hg_lb_logits, v_hg_lb_logits),
               ("gdn_a_log", gdn_a_log, s_a_log, m_gdn_a_log, v_gdn_a_log),
               ("gdn_dt_bias", gdn_dt_bias, s_dt, m_gdn_dt_bias, v_gdn_dt_bias),
               ("ha_norm_w", ha_norm_w, s_ha, m_ha_norm_w, v_ha_norm_w), ("hb_norm_w", hb_norm_w, s_hb, m_hb_norm_w, v_hb_norm_w),
               ("w_out", w_out, s_w_out, m_w_out, v_w_out), ("final_norm_w", final_norm_w, s_final, m_final_norm_w, v_final_norm_w)]
    grads, deltas, new_ms, new_vs = [], [], [], []
    for nm, w, gs, m, v in weights:
        g, dl, m2, v2 = _adamw_nd(w, gs, m, v, "adamw_" + nm)
        grads.append(g)
        deltas.append(dl)
        new_ms.append(m2)
        new_vs.append(v2)
    return (loss, grad_x[None], *grads, *deltas, *new_ms, *new_vs)
```

```python
import functools

import jax
import jax.numpy as jnp
from jax import lax
from jax.experimental import pallas as pl
from jax.experimental.pallas import tpu as pltpu

F32 = jnp.float32
BF16 = jnp.bfloat16
HI = lax.Precision.HIGHEST
HIGH = lax.Precision.HIGH
MESH = pl.DeviceIdType.MESH
ANY = pl.BlockSpec(memory_space=pl.ANY)

N_DEV = 8
LANES = 128
CHUNK = 64
GRID_W = 64
NORM_EPS = 1e-6
NEG = -1e30
COLS_PER_TILE = 8
VMEM_BIG = 56 * 1024 * 1024
HEADS_PER_GROUP = 4
GROUP_UNROLL = 2
STEP_CHUNKS = 4
HALVES = (32, 16, 8, 4, 2, 1)

ADAM_LR, ADAM_B1, ADAM_B2, ADAM_EPS, ADAM_WD, ADAM_STEP = 0.001, 0.9, 0.999, 1e-08, 0.01, 10


def _params(sem=None, vmem=None):
    kw = {}
    if sem is not None:
        kw["dimension_semantics"] = sem
    if vmem is not None:
        kw["vmem_limit_bytes"] = vmem
    return pltpu.CompilerParams(**kw)


def _pick(n, cands):
    for c in cands:
        if n % c == 0:
            return c
    return n


def _silu(x):
    return x * jax.nn.sigmoid(x)


def _mm(a, b, prec=None):
    return lax.dot_general(a, b, (((1,), (0,)), ((), ())), precision=prec, preferred_element_type=F32)


def _mm_nt(a, b, prec=None):
    return lax.dot_general(a, b, (((1,), (1,)), ((), ())), precision=prec, preferred_element_type=F32)


def _mm_tn(a, b, prec=None):
    return lax.dot_general(a, b, (((0,), (0,)), ((), ())), precision=prec, preferred_element_type=F32)


def _me():
    return 4 * lax.axis_index("x") + 2 * lax.axis_index("y") + lax.axis_index("c")


def _peer(r):
    x, y, c = lax.axis_index("x"), lax.axis_index("y"), lax.axis_index("c")
    px = 1 - x if r & 4 else x
    py = 1 - y if r & 2 else y
    pc = 1 - c if r & 1 else c
    return (px, py, pc), 4 * px + 2 * py + pc


def _exchange(arrs, name, gather):
    n_a = len(arrs)

    def body(*refs):
        srcs, outs = refs[:n_a], refs[n_a:2 * n_a]
        send_sems, recv_sems, local_sems = refs[2 * n_a:]
        me = _me()
        copies = []
        for k in range(n_a):
            local = pltpu.make_async_copy(srcs[k] if gather else srcs[k].at[me], outs[k].at[me], local_sems.at[k])
            local.start()
            copies.append(local)
        for r in range(N_DEV - 1, 0, -1):
            dev, pid = _peer(r)
            for k in range(n_a):
                cp = pltpu.make_async_remote_copy(
                    src_ref=srcs[k] if gather else srcs[k].at[pid], dst_ref=outs[k].at[me],
                    send_sem=send_sems.at[(r - 1) * n_a + k], recv_sem=recv_sems.at[(r - 1) * n_a + k],
                    device_id=dev, device_id_type=MESH)
                cp.start()
                copies.append(cp)
        for cp in copies:
            cp.wait()

    n_sem = (N_DEV - 1) * n_a
    return pl.pallas_call(
        body, name=name,
        out_shape=[jax.ShapeDtypeStruct(((N_DEV,) + a.shape) if gather else a.shape, a.dtype) for a in arrs],
        in_specs=[ANY] * n_a, out_specs=[ANY] * n_a,
        scratch_shapes=[pltpu.SemaphoreType.DMA((n_sem,)), pltpu.SemaphoreType.DMA((n_sem,)),
                        pltpu.SemaphoreType.DMA((n_a,))],
    )(*arrs)


def _gather_two_level(arrs, name):
    n_a = len(arrs)

    def body(*refs):
        srcs, outs = refs[:n_a], refs[n_a:2 * n_a]
        send_sems, recv_sems, local_sems = refs[2 * n_a:]
        x, y, c = lax.axis_index("x"), lax.axis_index("y"), lax.axis_index("c")
        me, sibling = (x, y, c), (x, y, 1 - c)
        chips = [(1 - x, y), (x, 1 - y), (1 - x, 1 - y)]

        def copy(k, j, block, to, src=None):
            slot = outs[k].at[4 * block[0] + 2 * block[1] + block[2]]
            return pltpu.make_async_remote_copy(
                src_ref=slot if src is None else src, dst_ref=slot,
                send_sem=send_sems.at[j * n_a + k], recv_sem=recv_sems.at[j * n_a + k],
                device_id=to, device_id_type=MESH)

        mine = [pltpu.make_async_copy(srcs[k], outs[k].at[4 * x + 2 * y + c], local_sems.at[k]) for k in range(n_a)]
        for cp in mine:
            cp.start()
        first = [copy(k, 1 + j, me, (*chip, c), srcs[k]) for j, chip in enumerate(chips) for k in range(n_a)]
        first += [copy(k, 0, me, sibling, srcs[k]) for k in range(n_a)]
        for cp in first:
            cp.start()
        passed = []
        for j, chip in enumerate(chips):
            for k in range(n_a):
                copy(k, 1 + j, (*chip, c), me).wait_recv()
                cp = copy(k, 4 + j, (*chip, c), sibling)
                cp.start()
                passed.append(cp)
        for k in range(n_a):
            copy(k, 0, sibling, me).wait_recv()
        for j, chip in enumerate(chips):
            for k in range(n_a):
                copy(k, 4 + j, (*chip, 1 - c), me).wait_recv()
        for cp in first + passed:
            cp.wait_send()
        for cp in mine:
            cp.wait()

    n_sem = (N_DEV - 1) * n_a
    return pl.pallas_call(
        body, name=name,
        out_shape=[jax.ShapeDtypeStruct((N_DEV,) + a.shape, a.dtype) for a in arrs],
        in_specs=[ANY] * n_a, out_specs=[ANY] * n_a,
        scratch_shapes=[pltpu.SemaphoreType.DMA((n_sem,)), pltpu.SemaphoreType.DMA((n_sem,)),
                        pltpu.SemaphoreType.DMA((n_a,))],
    )(*arrs)


def _swap_with_sibling(arrs, name):
    n_a = len(arrs)

    def body(*refs):
        srcs, outs = refs[:n_a], refs[n_a:2 * n_a]
        send_sems, recv_sems = refs[2 * n_a:]
        sibling = (lax.axis_index("x"), lax.axis_index("y"), 1 - lax.axis_index("c"))
        copies = [pltpu.make_async_remote_copy(src_ref=srcs[k], dst_ref=outs[k], send_sem=send_sems.at[k],
                                               recv_sem=recv_sems.at[k], device_id=sibling, device_id_type=MESH)
                  for k in range(n_a)]
        for cp in copies:
            cp.start()
        for cp in copies:
            cp.wait()

    return pl.pallas_call(
        body, name=name, out_shape=[jax.ShapeDtypeStruct(a.shape, a.dtype) for a in arrs],
        in_specs=[ANY] * n_a, out_specs=[ANY] * n_a,
        scratch_shapes=[pltpu.SemaphoreType.DMA((n_a,)), pltpu.SemaphoreType.DMA((n_a,))],
    )(*arrs)


def _scatter_over_chips(arrs, name):
    n_a = len(arrs)
    n_chip = N_DEV // 2

    def body(*refs):
        srcs, outs = refs[:n_a], refs[n_a:2 * n_a]
        send_sems, recv_sems, local_sems = refs[2 * n_a:]
        x, y, c = lax.axis_index("x"), lax.axis_index("y"), lax.axis_index("c")
        my_chip = 2 * x + y
        copies = []
        for k in range(n_a):
            local = pltpu.make_async_copy(srcs[k].at[4 * x + 2 * y + c], outs[k].at[my_chip], local_sems.at[k])
            local.start()
            copies.append(local)
        for r in range(n_chip - 1, 0, -1):
            px = 1 - x if r & 2 else x
            py = 1 - y if r & 1 else y
            for k in range(n_a):
                cp = pltpu.make_async_remote_copy(
                    src_ref=srcs[k].at[4 * px + 2 * py + c], dst_ref=outs[k].at[my_chip],
                    send_sem=send_sems.at[(r - 1) * n_a + k], recv_sem=recv_sems.at[(r - 1) * n_a + k],
                    device_id=(px, py, c), device_id_type=MESH)
                cp.start()
                copies.append(cp)
        for cp in copies:
            cp.wait()

    n_sem = (n_chip - 1) * n_a
    return pl.pallas_call(
        body, name=name,
        out_shape=[jax.ShapeDtypeStruct((n_chip,) + a.shape[1:], a.dtype) for a in arrs],
        in_specs=[ANY] * n_a, out_specs=[ANY] * n_a,
        scratch_shapes=[pltpu.SemaphoreType.DMA((n_sem,)), pltpu.SemaphoreType.DMA((n_sem,)),
                        pltpu.SemaphoreType.DMA((n_a,))],
    )(*arrs)


def _pair_sum(a, b, name):
    shape = a.shape
    a2, b2 = a.reshape(-1, shape[-1]), b.reshape(-1, shape[-1])
    rows, cols = a2.shape
    tr = _pick(rows, (256, 128, 64, 32, 16))

    def body(a_ref, b_ref, o_ref):
        o_ref[...] = (a_ref[...] + b_ref[...].astype(F32)).astype(BF16)

    spec = pl.BlockSpec((tr, cols), lambda i: (i, 0))
    return pl.pallas_call(
        body, name=name, grid=(rows // tr,), in_specs=[spec, spec], out_specs=spec,
        out_shape=jax.ShapeDtypeStruct((rows, cols), BF16), compiler_params=_params(("arbitrary",)),
    )(a2, b2).reshape(shape)


def _to_blob(parts):
    flat = [p.reshape(-1).astype(F32) for p in parts]
    offs, n = [], 0
    for f in flat:
        offs.append(n)
        n += f.shape[0]
    unit = 8 * LANES
    total = -(-n // unit) * unit
    if total > n:
        flat.append(jnp.zeros((total - n,), F32))
    return jnp.concatenate(flat).reshape(total // LANES, LANES), offs


def _sum_slots(buf, name):
    n, rows, cols = buf.shape
    tr = _pick(rows, (512, 256, 128, 64, 32, 16, 8))

    def body(b_ref, o_ref):
        acc = b_ref[0]
        for s in range(1, n):
            acc = acc + b_ref[s]
        o_ref[...] = acc

    return pl.pallas_call(
        body, name=name, grid=(rows // tr,),
        in_specs=[pl.BlockSpec((n, tr, cols), lambda i: (0, i, 0))],
        out_specs=pl.BlockSpec((tr, cols), lambda i: (i, 0)),
        out_shape=jax.ShapeDtypeStruct((rows, cols), F32),
        compiler_params=_params(("arbitrary",)),
    )(buf)


def _matmul(a, b, name, a_cbm=False, b_cbm=False, out_cbm=False, a_t=False, out_dtype=F32, tm=None, tn=None, tk=None):
    if a_cbm:
        m, k = a.shape[1], a.shape[0] * LANES
    elif a_t:
        k, m = a.shape
    else:
        m, k = a.shape
    n = b.shape[0] * LANES if b_cbm else b.shape[1]
    tm = tm or _pick(m, (1024, 768, 512, 384, 256, 128, 64))
    tn = tn or _pick(n, (1024, 512, 256, 128))
    tk = tk or _pick(k, (2048, 1024, 768, 512, 256, 128))
    nk = k // tk

    def load(ref, cbm):
        if not cbm:
            return ref[...].astype(BF16)
        return jnp.concatenate([ref[j].astype(BF16) for j in range(ref.shape[0])], axis=1)

    def body(a_ref, b_ref, o_ref, acc_ref):
        kk = pl.program_id(2)
        part = (_mm_tn if a_t else _mm)(load(a_ref, a_cbm), load(b_ref, b_cbm))

        @pl.when(kk == 0)
        def _():
            acc_ref[...] = part

        @pl.when(kk > 0)
        def _():
            acc_ref[...] += part

        @pl.when(kk == nk - 1)
        def _():
            r = acc_ref[...]
            if out_cbm:
                for j in range(tn // LANES):
                    o_ref[j] = r[:, j * LANES:(j + 1) * LANES].astype(out_dtype)
            else:
                o_ref[...] = r.astype(out_dtype)

    if a_cbm:
        a_spec = pl.BlockSpec((tk // LANES, tm, LANES), lambda j, i, kk: (kk, i, 0))
    elif a_t:
        a_spec = pl.BlockSpec((tk, tm), lambda j, i, kk: (kk, i))
    else:
        a_spec = pl.BlockSpec((tm, tk), lambda j, i, kk: (i, kk))
    if b_cbm:
        b_spec = pl.BlockSpec((tn // LANES, tk, LANES), lambda j, i, kk: (j, kk, 0))
    else:
        b_spec = pl.BlockSpec((tk, tn), lambda j, i, kk: (kk, j))
    if out_cbm:
        o_spec = pl.BlockSpec((tn // LANES, tm, LANES), lambda j, i, kk: (j, i, 0))
        o_shape = jax.ShapeDtypeStruct((n // LANES, m, LANES), out_dtype)
    else:
        o_spec = pl.BlockSpec((tm, tn), lambda j, i, kk: (i, j))
        o_shape = jax.ShapeDtypeStruct((m, n), out_dtype)
    return pl.pallas_call(
        body, name=name, grid=(n // tn, m // tm, nk),
        in_specs=[a_spec, b_spec], out_specs=o_spec, out_shape=o_shape,
        scratch_shapes=[pltpu.VMEM((tm, tn), F32)],
        compiler_params=_params(("arbitrary", "arbitrary", "arbitrary"), VMEM_BIG),
    )(a, b)


def _ada_fwd(cvec, ada_w, ada_b):
    def body(c_ref, w_ref, b_ref, o_ref):
        o_ref[...] = _mm(_silu(c_ref[...]), w_ref[...], HI) + b_ref[...]

    return pl.pallas_call(
        body, name="ada_fwd", out_shape=jax.ShapeDtypeStruct((8, ada_w.shape[1]), F32),
        compiler_params=_params(None, VMEM_BIG),
    )(cvec, ada_w, ada_b)


def _ada_bwd(cvec, ada_w, dmods):
    def body(c_ref, w_ref, d_ref, o_ref):
        (_, vj) = jax.vjp(_silu, c_ref[...])
        o_ref[...] = vj(_mm_nt(d_ref[...], w_ref[...], HI))[0]

    return pl.pallas_call(
        body, name="ada_bwd", out_shape=jax.ShapeDtypeStruct(cvec.shape, F32),
        compiler_params=_params(None, VMEM_BIG),
    )(cvec, ada_w, dmods)


def _ada_wgrad(conds, dm_lat, dm_ctx):
    d = conds.shape[1]
    cols = dm_lat.shape[1]

    def body(c_ref, dl_ref, dc_ref, o_ref):
        dctx = dc_ref[0:1]
        for s in range(1, N_DEV):
            dctx = dctx + dc_ref[s:s + 1]
        rhs = jnp.concatenate([dl_ref[...], dctx, jnp.zeros((7, cols), F32)], axis=0)
        o_ref[...] = _mm_tn(_silu(c_ref[...]), rhs, HI)

    return pl.pallas_call(
        body, name="ada_wgrad", out_shape=jax.ShapeDtypeStruct((d, cols), F32),
    )(conds, dm_lat, dm_ctx)


def _prenorm_tile(xt, nw, sc, sh):
    r = lax.rsqrt(jnp.mean(xt * xt, axis=-1, keepdims=True) + NORM_EPS)
    return (xt * r * nw) * (1.0 + sc) + sh


def _tok_specs(l_lat, l_ctx, d, tm):
    nl, nc = l_lat // tm, l_ctx // tm
    lat = pl.BlockSpec((tm, d), lambda i: (jnp.minimum(i, nl - 1), 0))
    ctx = pl.BlockSpec((tm, d), lambda i: (jnp.maximum(i - nl, 0), 0))
    return nl, nc, lat, ctx


def _prenorm_fwd(x, ctx, norm_w, mods):
    l_lat, d = x.shape
    l_ctx = ctx.shape[0]
    tm = _pick(l_ctx, (256, 128, 64))
    nl, nc, lat_spec, ctx_spec = _tok_specs(l_lat, l_ctx, d, tm)

    def body(x_ref, c_ref, nw_ref, m_ref, h_ref):
        is_lat = pl.program_id(0) < nl
        xt = jnp.where(is_lat, x_ref[...], c_ref[...])
        row = jnp.where(is_lat, m_ref[0:1, :], m_ref[1:2, :])
        h_ref[...] = _prenorm_tile(xt, nw_ref[...], row[:, d:2 * d], row[:, 0:d]).astype(BF16)

    return pl.pallas_call(
        body, name="prenorm_fwd", grid=(nl + nc,),
        in_specs=[lat_spec, ctx_spec, pl.BlockSpec((1, d), lambda i: (0, 0)), pl.BlockSpec((8, 3 * d), lambda i: (0, 0))],
        out_specs=pl.BlockSpec((tm, d), lambda i: (i, 0)),
        out_shape=jax.ShapeDtypeStruct((l_lat + l_ctx, d), BF16),
        compiler_params=_params(("arbitrary",)),
    )(x, ctx, norm_w, mods)


def _prenorm_bwd(x, ctx, norm_w, mods, dh, dx_res):
    l_lat, d = x.shape
    l_ctx = ctx.shape[0]
    tm = _pick(l_ctx, (256, 128, 64))
    nl, nc, lat_spec, ctx_spec = _tok_specs(l_lat, l_ctx, d, tm)

    def body(x_ref, c_ref, nw_ref, m_ref, dh_ref, dr_ref, gx_ref, dnw_ref, dm_ref):
        i = pl.program_id(0)
        is_lat = i < nl

        @pl.when(i == 0)
        def _():
            dnw_ref[...] = jnp.zeros_like(dnw_ref)
            dm_ref[...] = jnp.zeros_like(dm_ref)

        xt = jnp.where(is_lat, x_ref[...], c_ref[...])
        row = jnp.where(is_lat, m_ref[0:1, :], m_ref[1:2, :])
        _, vj = jax.vjp(_prenorm_tile, xt, nw_ref[...], row[:, d:2 * d], row[:, 0:d])
        dxt, dnw, dsc, dsh = vj(dh_ref[...])
        dnw_ref[...] += dnw
        upd = jnp.concatenate([dsh, dsc, jnp.zeros_like(dsh)], axis=1)

        @pl.when(is_lat)
        def _():
            gx_ref[...] = dr_ref[...] + dxt
            dm_ref[0:1, :] += upd

        @pl.when(jnp.logical_not(is_lat))
        def _():
            dm_ref[1:2, :] += upd

    return pl.pallas_call(
        body, name="prenorm_bwd", grid=(nl + nc,),
        in_specs=[lat_spec, ctx_spec, pl.BlockSpec((1, d), lambda i: (0, 0)), pl.BlockSpec((8, 3 * d), lambda i: (0, 0)),
                  pl.BlockSpec((tm, d), lambda i: (i, 0)), lat_spec],
        out_specs=[lat_spec, pl.BlockSpec((1, d), lambda i: (0, 0)), pl.BlockSpec((8, 3 * d), lambda i: (0, 0))],
        out_shape=[jax.ShapeDtypeStruct((l_lat, d), F32), jax.ShapeDtypeStruct((1, d), F32),
                   jax.ShapeDtypeStruct((8, 3 * d), F32)],
        compiler_params=_params(("arbitrary",)),
    )(x, ctx, norm_w, mods, dh, dx_res)


def _conv_call(src, wts, l_lat, first_block, n_blocks, name, dout=None, passthru=None, into=None):
    t = src.shape[1]
    l_ctx = t - l_lat
    rc = _pick(l_lat, (512, 256, 128, 64))
    n_rc = l_lat // rc
    bwd = dout is not None

    def shifted(xc, n, period):
        pos = lax.broadcasted_iota(jnp.int32, (n, LANES), 0) % period
        xm = jnp.where(pos == 0, 0.0, pltpu.roll(xc, 1, 0))
        xp = jnp.where(pos == period - 1, 0.0, pltpu.roll(xc, n - 1, 0))
        return xm, xc, xp

    def fill(ref, sm, s0, sp):
        zero = jnp.zeros((GRID_W, LANES), F32)
        for s in (sm, s0, sp):
            s[pl.ds(0, GRID_W), :] = zero
            s[pl.ds(l_lat + GRID_W, GRID_W), :] = zero

        def step(i, carry):
            st = pl.multiple_of(i * rc, rc)
            xm, x0, xp = shifted(ref[0, pl.ds(st, rc), :], rc, GRID_W)
            sm[pl.ds(st + GRID_W, rc), :] = xm
            s0[pl.ds(st + GRID_W, rc), :] = x0
            sp[pl.ds(st + GRID_W, rc), :] = xp
            return carry

        lax.fori_loop(0, n_rc, step, 0)

    def apply(w, sm, s0, sp, out_ref, flip):
        def step(i, carry):
            st = pl.multiple_of(i * rc, rc)
            acc = jnp.zeros((rc, LANES), F32)
            for di in range(3):
                for dj, s in enumerate((sm, s0, sp)):
                    kidx = (2 - di) * 3 + (2 - dj) if flip else di * 3 + dj
                    acc = acc + w[kidx:kidx + 1, :] * s[pl.ds(st + di * GRID_W, rc), :]
            out_ref[0, pl.ds(st, rc), :] = acc
            return carry

        lax.fori_loop(0, n_rc, step, 0)

    def ctx_apply(w, xc, flip):
        xm, x0, xp = shifted(xc, l_ctx, l_ctx)
        ks = (5, 4, 3) if flip else (3, 4, 5)
        return w[ks[0]:ks[0] + 1, :] * xm + w[4:5, :] * x0 + w[ks[2]:ks[2] + 1, :] * xp

    def fwd_body(x_ref, w_ref, o_ref, sm, s0, sp):
        w = w_ref[0]
        fill(x_ref, sm, s0, sp)
        apply(w, sm, s0, sp, o_ref, False)
        o_ref[0, pl.ds(l_lat, l_ctx), :] = ctx_apply(w, x_ref[0, pl.ds(l_lat, l_ctx), :], False)

    n_pass = 0 if passthru is None else passthru.shape[0]

    def bwd_body(x_ref, w_ref, d_ref, *rest):
        p_ref = rest[0] if n_pass else None
        dx_ref, dw_ref, sm, s0, sp = rest[-5:]

        @pl.when(pl.program_id(0) < n_blocks)
        def _():
            bwd_block(x_ref, w_ref, d_ref, dx_ref, dw_ref, sm, s0, sp)

        if n_pass:
            @pl.when(pl.program_id(0) >= n_blocks)
            def _():
                dx_ref[0] = p_ref[0]

    def bwd_block(x_ref, w_ref, d_ref, dx_ref, dw_ref, sm, s0, sp):
        w = w_ref[0]
        fill(x_ref, sm, s0, sp)

        def step(i, acc):
            st = pl.multiple_of(i * rc, rc)
            dc = d_ref[0, pl.ds(st, rc), :]
            rows = []
            for di in range(3):
                for s in (sm, s0, sp):
                    rows.append(jnp.sum(dc * s[pl.ds(st + di * GRID_W, rc), :], axis=0, keepdims=True))
            return acc + jnp.concatenate(rows + [jnp.zeros((7, LANES), F32)], axis=0)

        acc = lax.fori_loop(0, n_rc, step, jnp.zeros((16, LANES), F32))
        dctx = d_ref[0, pl.ds(l_lat, l_ctx), :]
        xm, x0, xp = shifted(x_ref[0, pl.ds(l_lat, l_ctx), :], l_ctx, l_ctx)
        crow = [jnp.sum(dctx * s, axis=0, keepdims=True) for s in (xm, x0, xp)]
        acc = acc + jnp.concatenate([jnp.zeros((3, LANES), F32)] + crow + [jnp.zeros((10, LANES), F32)], axis=0)
        dw_ref[0] = acc
        fill(d_ref, sm, s0, sp)
        apply(w, sm, s0, sp, dx_ref, True)
        dx_ref[0, pl.ds(l_lat, l_ctx), :] = ctx_apply(w, dctx, True)

    blk = lambda off: pl.BlockSpec((1, t, LANES), lambda j: (j + off, 0, 0))
    w_spec = pl.BlockSpec((1, 16, LANES), lambda j: (j, 0, 0))
    scratch = [pltpu.VMEM((l_lat + 2 * GRID_W, LANES), F32)] * 3
    out_t = jax.ShapeDtypeStruct((n_blocks, t, LANES), F32)
    if not bwd:
        return pl.pallas_call(
            fwd_body, name=name, grid=(n_blocks,), in_specs=[blk(first_block), w_spec], out_specs=blk(0),
            out_shape=out_t, scratch_shapes=scratch, compiler_params=_params(("arbitrary",), VMEM_BIG),
        )(src, wts)
    last = n_blocks - 1
    clamped = lambda off: pl.BlockSpec((1, t, LANES), lambda j: (jnp.minimum(j, last) + off, 0, 0))
    w_clamped = pl.BlockSpec((1, 16, LANES), lambda j: (jnp.minimum(j, last), 0, 0))
    in_specs, args, aliases = [clamped(first_block), w_clamped, clamped(0)], [src, wts, dout], {}
    if n_pass:
        in_specs.append(pl.BlockSpec((1, t, LANES), lambda j: (jnp.maximum(j - n_blocks, 0), 0, 0)))
        args.append(passthru)
    first_out = 0
    if into is not None:
        aliases[len(args)] = 0
        in_specs.append(ANY)
        args.append(into[0])
        first_out = into[1]
        out_t = jax.ShapeDtypeStruct(into[0].shape, F32)
    else:
        out_t = jax.ShapeDtypeStruct((n_blocks + n_pass, t, LANES), F32)
    return pl.pallas_call(
        bwd_body, name=name, grid=(n_blocks + n_pass,), in_specs=in_specs,
        out_specs=[pl.BlockSpec((1, t, LANES), lambda j: (j + first_out, 0, 0)), w_clamped],
        out_shape=[out_t, jax.ShapeDtypeStruct((n_blocks, 16, LANES), F32)],
        scratch_shapes=scratch, input_output_aliases=aliases, compiler_params=_params(("arbitrary",), VMEM_BIG),
    )(*args)


def _scan_consts(kind, g, h, rev, d_index):
    n = g * CHUNK
    i = jnp.arange(n, dtype=jnp.int32)
    head, pos = i // CHUNK, i % CHUNK
    p = (CHUNK - 1 - pos) if rev else pos
    same_head = head[:, None] == head[None, :]
    pr, pc = p[:, None], p[None, :]
    f = lambda m: m.astype(F32)
    incl = same_head & (pc <= pr)
    out = {"eye": f(i[:, None] == i[None, :]), "incl": f(incl), "incl_t": f(incl).T}
    if kind == "hg":
        mid, same, sec = [], [], []
        for half in HALVES:
            width = 2 * half
            blk = p // width
            second = (p % width) >= half
            ref_pos = blk * width + half - 1
            mid.append(same_head & (pc <= ref_pos[:, None]))
            same.append(same_head & (blk[:, None] == blk[None, :]))
            sec.append(jnp.broadcast_to(second[:, None], (n, LANES)))
        mid = f(jnp.concatenate(mid, axis=0))
        out.update(mid=mid, mid_t=mid.T, same=f(jnp.stack(same)), sec=f(jnp.stack(sec)))
    else:
        low = []
        for half in reversed(HALVES):
            width = 2 * half
            blk = p // width
            second = (p % width) >= half
            low.append(same_head & (blk[:, None] == blk[None, :]) & second[:, None] & jnp.logical_not(second[None, :]))
        lane = jnp.arange(LANES, dtype=jnp.int32)[None, :]
        hh = jnp.arange(h * CHUNK, dtype=jnp.int32) // CHUNK
        out.update(strict=f(same_head & (pc < pr)), low=f(jnp.stack(low)),
                   oh_a=f(lane == (d_index * h + hh)[:, None]), oh_b=f(lane == (2 * h + d_index * h + hh)[:, None]))
    return out


def _const_mm(c, c_t, diff, prec=HIGH):
    if not diff:
        return lambda x: _mm(c, x, prec)

    @jax.custom_vjp
    def f(x):
        return _mm(c, x, prec)

    f.defvjp(lambda x: (_mm(c, x, prec), None), lambda _, ct: (_mm(c_t, ct, HIGH),))
    return f


def _kept_inverse(x_kept):
    @jax.custom_vjp
    def f(a):
        return x_kept

    f.defvjp(lambda a: (x_kept, None), lambda _, ct: (-_mm_nt(_mm_tn(x_kept, ct, HIGH), x_kept, HIGH),))
    return f


def _stack_helpers(g):
    n = g * CHUNK
    rows = lambda vec: jnp.broadcast_to(vec, (g, CHUNK, LANES)).reshape(n, LANES)
    per_head = lambda t: [t[i * CHUNK:(i + 1) * CHUNK] for i in range(g)]
    head_sum = lambda t: rows(jnp.sum(t.reshape(g, CHUNK, LANES), axis=1, keepdims=True))
    return rows, per_head, head_sum


def _hg_chunk(raw_q, raw_f, raw_i, l0, l1, sts, cst, g, diff=False):
    n = g * CHUNK
    rows, per_head, head_sum = _stack_helpers(g)
    lb = rows(jax.nn.sigmoid(l0 - l1))
    q = _silu(raw_q) * (LANES ** -0.5)
    gl = jnp.log(lb + (1.0 - lb) * jax.nn.sigmoid(raw_f))
    k = (1.0 - lb) * jax.nn.sigmoid(-raw_f)
    v = raw_i
    b = _const_mm(cst["incl"], cst["incl_t"], diff)(gl)
    b_tot = head_sum(gl)
    q_dec = q * jnp.exp(b)
    k_dec = k * jnp.exp(b_tot - b)
    mids = _const_mm(cst["mid"], cst["mid_t"], diff, None)(gl)
    a = cst["eye"] * jnp.sum(q * k, axis=1, keepdims=True)
    for lv in range(len(HALVES)):
        r = mids[lv * n:(lv + 1) * n]
        sec = cst["sec"][lv]
        fst = 1.0 - sec
        qt = q * jnp.exp((b - r) * sec) * sec
        kt = k * jnp.exp((r - b) * fst) * fst
        a = a + _mm_nt(qt, kt) * cst["same"][lv]
    o_intra = per_head(_mm(a, v))
    decay = per_head(jnp.exp(b_tot))
    qd, kd, vs = per_head(q_dec), per_head(k_dec), per_head(v)
    outs = [o_intra[i] + _mm_nt(qd[i], sts[i]) for i in range(g)]
    new = [sts[i] * decay[i][0:1] + _mm_tn(vs[i], kd[i]) for i in range(g)]
    return jnp.concatenate(outs, axis=0), new


def _gdn_chunk(u_q, u_k, u_v, tail, avec, dtvec, ss, cst, g, x_kept=None):
    n = g * CHUNK
    diff = x_kept is not None
    rows, per_head, head_sum = _stack_helpers(g)

    def l2n(t):
        return t * lax.rsqrt(jnp.sum(t * t, axis=-1, keepdims=True) + NORM_EPS)

    q = l2n(_silu(u_q)) * (LANES ** -0.5)
    k = l2n(_silu(u_k))
    v = _silu(u_v)
    tail_n = jnp.concatenate([tail] * g, axis=0)
    za = jnp.sum(tail_n * cst["oh_a"], axis=1, keepdims=True) + rows(dtvec)
    beta = jax.nn.sigmoid(jnp.sum(tail_n * cst["oh_b"], axis=1, keepdims=True))
    gl = -jnp.exp(rows(avec)) * (jnp.maximum(za, 0.0) + jnp.log1p(jnp.exp(-jnp.abs(za))))
    b = _const_mm(cst["incl"], cst["incl_t"], diff)(gl)
    b_tot = head_sum(gl)
    bb = jnp.concatenate([b] * (n // LANES), axis=1)
    bdiff = bb - bb.T
    a_mat = beta * _mm_nt(k, k) * jnp.exp(jnp.where(cst["strict"] > 0.5, bdiff, NEG))
    if diff:
        xinv = _kept_inverse(x_kept)(a_mat)
    else:
        xinv = cst["eye"] - a_mat * cst["low"][0]
        for lv in range(1, len(HALVES)):
            xinv = xinv - _mm(_mm(xinv, a_mat * cst["low"][lv]), xinv)
        xinv = xinv + _mm(xinv, cst["eye"] - _mm(cst["eye"] + a_mat, xinv, HIGH), HIGH)
    rhs = jnp.concatenate([beta * v, beta * jnp.exp(b) * k], axis=1)
    sol = _mm(xinv, rhs, HIGH)
    u0, w = per_head(sol[:, :LANES]), per_head(sol[:, LANES:])
    k_dec = per_head(k * jnp.exp(b_tot - b))
    q_dec = per_head(q * jnp.exp(b))
    decay = per_head(jnp.exp(b_tot))
    p = _mm_nt(q, k) * jnp.exp(jnp.where(cst["incl"] > 0.5, bdiff, NEG))
    v_new = [u0[i] - _mm(w[i], ss[i]) for i in range(g)]
    o_state = [_mm(q_dec[i], ss[i]) for i in range(g)]
    o = _mm(p, jnp.concatenate(v_new, axis=0)) + jnp.concatenate(o_state, axis=0)
    new = [decay[i][0:1] * ss[i] + _mm_tn(k_dec[i], v_new[i]) for i in range(g)]
    return o, new, xinv


def _chunk_index(l_lat, l_ctx, rev):
    rows = STEP_CHUNKS * CHUNK
    nl, nc = l_lat // rows, l_ctx // rows

    def idx(i):
        if rev:
            return jnp.where(i < nc, nl + nc - 1 - i, nl - 1 - (i - nc))
        return jnp.where(i < nc, nl + i, i - nc)

    subs = list(range(STEP_CHUNKS))
    return nl + nc, idx, (subs[::-1] if rev else subs)


def _const_args(kind, g, h, rev, d_index):
    consts = _scan_consts(kind, g, h, rev, d_index)
    names = sorted(consts)
    arrs = [consts[k] for k in names]
    specs = [pl.BlockSpec(a.shape, functools.partial(lambda nd, i: (0,) * nd, a.ndim)) for a in arrs]
    return names, arrs, specs


def _load_consts(names, refs, g0, g):
    cst = {}
    for k, r in zip(names, refs):
        if k in ("oh_a", "oh_b"):
            cst[k] = r[pl.ds(pl.multiple_of(g0 * CHUNK, CHUNK), g * CHUNK), :]
        else:
            cst[k] = r[...]
    return cst


def _scan_fwd(kind, rev, cbm, segs, vecs, o_prev, l_lat, l_ctx, h, name, tail=None, d_index=0):
    n_steps, cidx, subs = _chunk_index(l_lat, l_ctx, rev)
    rows = STEP_CHUNKS * CHUNK
    t = l_lat + l_ctx
    n_in = len(segs)
    gdn = kind == "gdn"
    g = min(HEADS_PER_GROUP, h)
    n = g * CHUNK
    c_names, c_arrs, c_specs = _const_args(kind, g, h, rev, d_index)

    def body(*refs):
        in_refs = refs[:n_in]
        pos = n_in
        tail_ref = None
        if gdn:
            tail_ref = refs[pos]
            pos += 1
        v0_ref, v1_ref = refs[pos], refs[pos + 1]
        pos += 2
        prev_ref = None
        if o_prev is not None:
            prev_ref = refs[pos]
            pos += 1
        c_refs = refs[pos:pos + len(c_names)]
        pos += len(c_names)
        o_ref, st_ref = refs[pos], refs[pos + 1]
        pos += 2
        x_ref = None
        if gdn:
            x_ref = refs[pos]
            pos += 1
        s_scr = refs[pos]

        @pl.when(pl.program_id(0) == 0)
        def _():
            s_scr[...] = jnp.zeros_like(s_scr)

        def chunk(k, sub):
            rs = pl.ds(sub * CHUNK, CHUNK)

            def group(gi, carry):
                g0 = pl.multiple_of(gi * g, g)
                hs = pl.ds(g0, g)
                cst = _load_consts(c_names, c_refs, g0, g)
                states = [s_scr[g0 + i] for i in range(g)]
                for i in range(g):
                    st_ref[k, g0 + i] = states[i]
                ins = [r[hs, rs, :].reshape(n, LANES) for r in in_refs]
                if gdn:
                    o, new, xinv = _gdn_chunk(*ins, tail_ref[0, rs, :], v0_ref[hs], v1_ref[hs], states, cst, g)
                    x_ref[k, gi] = xinv
                else:
                    o, new = _hg_chunk(*ins, v0_ref[hs], v1_ref[hs], states, cst, g)
                o = o.reshape(g, CHUNK, LANES)
                if prev_ref is not None:
                    o = o + prev_ref[hs, rs, :]
                o_ref[hs, rs, :] = o
                for i in range(g):
                    s_scr[g0 + i] = new[i]
                return carry

            lax.fori_loop(0, h // g, group, 0, unroll=GROUP_UNROLL)

        for k, sub in enumerate(subs):
            chunk(k, sub)

    seg_spec = lambda sg: pl.BlockSpec((h, rows, LANES), lambda i: (sg, cidx(i), 0))
    vec_spec = pl.BlockSpec((h, 1, LANES), lambda i: (0, 0, 0))
    in_specs = [seg_spec(sg) for sg in segs]
    args = [cbm] * n_in
    if gdn:
        in_specs.append(pl.BlockSpec((1, rows, LANES), lambda i: (0, cidx(i), 0)))
        args.append(tail)
    in_specs += [vec_spec, vec_spec]
    args += list(vecs)
    if o_prev is not None:
        in_specs.append(seg_spec(0))
        args.append(o_prev)
    in_specs += c_specs
    args += c_arrs
    return pl.pallas_call(
        body, name=name, grid=(n_steps,), in_specs=in_specs,
        out_specs=[seg_spec(0), pl.BlockSpec((STEP_CHUNKS, h, LANES, LANES), lambda i: (i, 0, 0, 0))]
        + ([pl.BlockSpec((STEP_CHUNKS, h // g, n, n), lambda i: (i, 0, 0, 0))] if gdn else []),
        out_shape=[jax.ShapeDtypeStruct((h, t, LANES), F32),
                   jax.ShapeDtypeStruct((n_steps * STEP_CHUNKS, h, LANES, LANES), F32)]
        + ([jax.ShapeDtypeStruct((n_steps * STEP_CHUNKS, h // g, n, n), F32)] if gdn else []),
        scratch_shapes=[pltpu.VMEM((h, LANES, LANES), F32)],
        compiler_params=_params(("arbitrary",), VMEM_BIG),
    )(*args)


def _scan_bwd(kind, rev, cbm, segs, vecs, states, d_o, acc, l_lat, l_ctx, h, name, tail=None, d_index=0, x_kept=None,
              pack=None, pack_into=None, tail_into=None):
    n_steps, cidx, subs = _chunk_index(l_lat, l_ctx, rev)
    rows = STEP_CHUNKS * CHUNK
    t = l_lat + l_ctx
    n_in = len(segs)
    gdn = kind == "gdn"
    n_grad = n_in + (1 if gdn else 0)
    assert len(acc) == n_grad
    step_of = lambda j: n_steps - 1 - j
    g = min(HEADS_PER_GROUP, h)
    n = g * CHUNK
    c_names, c_arrs, c_specs = _const_args(kind, g, h, rev, d_index)
    items = pack or []
    packed_at = {it[1]: p for p, it in enumerate(items) if it[0] == "grad"}
    copies = [(p, it[1]) for p, it in enumerate(items) if it[0] == "copy"]
    separate = [m for m in range(n_in) if m not in packed_at]
    in_place = pack_into if pack_into is not None else tail_into
    assert pack_into is None or tail_into is None
    tail_blocks = tail_into[2] if tail_into is not None else 1

    def body(*refs):
        in_refs = refs[:n_in]
        pos = n_in
        tail_ref = None
        if gdn:
            tail_ref = refs[pos]
            pos += 1
        v0_ref, v1_ref, st_ref, do_ref = refs[pos:pos + 4]
        pos += 4
        xk_ref = None
        if gdn:
            xk_ref = refs[pos]
            pos += 1
        acc_refs = []
        for a in acc:
            if a is None:
                acc_refs.append(None)
            else:
                acc_refs.append(refs[pos])
                pos += 1
        copy_refs = refs[pos:pos + len(copies)]
        pos += len(copies) + (1 if in_place is not None else 0)
        c_refs = refs[pos:pos + len(c_names)]
        pos += len(c_names)
        sep_refs = refs[pos:pos + len(separate)]
        pos += len(separate)
        pack_ref = None
        if items:
            pack_ref = refs[pos]
            pos += 1
        tail_out = None
        if gdn:
            tail_out = refs[pos]
            pos += 1
        dv0_ref, dv1_ref, ds_scr = refs[pos:pos + 3]

        @pl.when(pl.program_id(0) == 0)
        def _():
            ds_scr[...] = jnp.zeros_like(ds_scr)
            dv0_ref[...] = jnp.zeros_like(dv0_ref)
            dv1_ref[...] = jnp.zeros_like(dv1_ref)

        if gdn:
            tail_out[0] = jnp.zeros((rows, LANES), F32) if acc_refs[n_in] is None else acc_refs[n_in][0]
            for q in range(1, tail_blocks):
                tail_out[q] = jnp.zeros((rows, LANES), F32)

        def chunk(k, sub):
            rs = pl.ds(sub * CHUNK, CHUNK)

            def group(gi, carry):
                g0 = pl.multiple_of(gi * g, g)
                hs = pl.ds(g0, g)
                cst = _load_consts(c_names, c_refs, g0, g)
                sts = [st_ref[k, g0 + i] for i in range(g)]
                ins = [r[hs, rs, :].reshape(n, LANES) for r in in_refs]
                if gdn:
                    xk = xk_ref[k, gi]
                    fn = lambda uq, uk, uv, tl, av, dt, ss: _gdn_chunk(uq, uk, uv, tl, av, dt, ss, cst, g, xk)[:2]
                    prim = ins + [tail_ref[0, rs, :], v0_ref[hs], v1_ref[hs], sts]
                else:
                    fn = lambda rq, rf, ri, l0, l1, ss: _hg_chunk(rq, rf, ri, l0, l1, ss, cst, g, True)
                    prim = ins + [v0_ref[hs], v1_ref[hs], sts]
                _, vj = jax.vjp(fn, *prim)
                grads = vj((do_ref[hs, rs, :].reshape(n, LANES), [ds_scr[g0 + i] for i in range(g)]))
                for m in range(n_in):
                    gm = grads[m].reshape(g, CHUNK, LANES)
                    if acc_refs[m] is not None:
                        gm = gm + acc_refs[m][hs, rs, :]
                    if m in packed_at:
                        pack_ref[pl.ds(packed_at[m] * h + g0, g), rs, :] = gm
                    else:
                        sep_refs[separate.index(m)][hs, rs, :] = gm
                for (p, _), c_ref in zip(copies, copy_refs):
                    pack_ref[pl.ds(p * h + g0, g), rs, :] = c_ref[hs, rs, :]
                if gdn:
                    tail_out[0, rs, :] += grads[n_in]
                dv0 = grads[n_grad]
                dv1 = grads[n_grad + 1]
                if gdn:
                    dv0 = jnp.broadcast_to(jnp.sum(dv0, axis=2, keepdims=True), dv0.shape)
                    dv1 = jnp.broadcast_to(jnp.sum(dv1, axis=2, keepdims=True), dv1.shape)
                dv0_ref[hs] += dv0
                dv1_ref[hs] += dv1
                for i in range(g):
                    ds_scr[g0 + i] = grads[n_grad + 2][i]
                return carry

            lax.fori_loop(0, h // g, group, 0, unroll=GROUP_UNROLL)

        for k in reversed(range(STEP_CHUNKS)):
            chunk(k, subs[k])

    seg_spec = lambda sg: pl.BlockSpec((h, rows, LANES), lambda j: (sg, cidx(step_of(j)), 0))
    tail_spec = pl.BlockSpec((1, rows, LANES), lambda j: (0, cidx(step_of(j)), 0))
    vec_spec = pl.BlockSpec((h, 1, LANES), lambda j: (0, 0, 0))
    in_specs = [seg_spec(sg) for sg in segs]
    args = [cbm] * n_in
    if gdn:
        in_specs.append(tail_spec)
        args.append(tail)
    in_specs += [vec_spec, vec_spec, pl.BlockSpec((STEP_CHUNKS, h, LANES, LANES), lambda j: (step_of(j), 0, 0, 0)),
                 seg_spec(0)]
    args += list(vecs) + [states, d_o]
    if gdn:
        in_specs.append(pl.BlockSpec((STEP_CHUNKS, h // g, n, n), lambda j: (step_of(j), 0, 0, 0)))
        args.append(x_kept)
    for k, a in enumerate(acc):
        if a is not None:
            in_specs.append(tail_spec if (gdn and k == n_in) else seg_spec(0))
            args.append(a)
    for _, arr in copies:
        in_specs.append(seg_spec(0))
        args.append(arr)
    aliases = {}
    if in_place is not None:
        aliases[len(args)] = len(separate) + (1 if (items and tail_into is not None) else 0)
        in_specs.append(ANY)
        args.append(in_place[0])
    in_specs += c_specs
    args += c_arrs
    out_specs = [seg_spec(0)] * len(separate)
    out_shape = [jax.ShapeDtypeStruct((h, t, LANES), F32)] * len(separate)
    if items:
        pb = len(items) * h
        first = pack_into[1] if pack_into is not None else 0
        assert first % pb == 0
        out_specs.append(pl.BlockSpec((pb, rows, LANES), lambda j: (first // pb, cidx(step_of(j)), 0)))
        out_shape.append(jax.ShapeDtypeStruct(pack_into[0].shape if pack_into is not None else (pb, t, LANES), F32))
    if gdn and tail_into is not None:
        first_t = tail_into[1]
        assert first_t % tail_blocks == 0
        out_specs.append(pl.BlockSpec((tail_blocks, rows, LANES), lambda j: (first_t // tail_blocks, cidx(step_of(j)), 0)))
        out_shape.append(jax.ShapeDtypeStruct(tail_into[0].shape, F32))
    elif gdn:
        out_specs.append(tail_spec)
        out_shape.append(jax.ShapeDtypeStruct((1, t, LANES), F32))
    out_specs += [vec_spec, vec_spec]
    out_shape += [jax.ShapeDtypeStruct((h, 1, LANES), F32)] * 2
    return pl.pallas_call(
        body, name=name, grid=(n_steps,), in_specs=in_specs, out_specs=out_specs, out_shape=out_shape,
        scratch_shapes=[pltpu.VMEM((h, LANES, LANES), F32)], input_output_aliases=aliases,
        compiler_params=_params(("arbitrary",), VMEM_BIG),
    )(*args)


def _gout_tile(oa, ob, za, zb, naw, nbw):
    def hn(o, w):
        return o * lax.rsqrt(jnp.mean(o * o, axis=-1, keepdims=True) + NORM_EPS) * w

    ya = _silu(za) * hn(oa, naw)
    yb = _silu(zb) * hn(ob, nbw)
    nh = oa.shape[0]
    return jnp.concatenate([ya[i] for i in range(nh)] + [yb[i] for i in range(nh)], axis=1)


def _gout_fwd(oa, ob, y_cbm, naw, nbw, l_lat, h):
    tm = _pick(l_lat, (256, 128, 64))
    blk = lambda sg: pl.BlockSpec((h, tm, LANES), lambda i: (sg, i, 0))
    vec = pl.BlockSpec((h, 1, LANES), lambda i: (0, 0, 0))

    def body(oa_ref, ob_ref, za_ref, zb_ref, na_ref, nb_ref, y_ref):
        y_ref[...] = _gout_tile(oa_ref[...], ob_ref[...], za_ref[...], zb_ref[...], na_ref[...], nb_ref[...]).astype(BF16)

    return pl.pallas_call(
        body, name="gout_fwd", grid=(l_lat // tm,),
        in_specs=[blk(0), blk(0), blk(4), blk(8), vec, vec],
        out_specs=pl.BlockSpec((tm, 2 * h * LANES), lambda i: (i, 0)),
        out_shape=jax.ShapeDtypeStruct((l_lat, 2 * h * LANES), BF16),
        compiler_params=_params(("arbitrary",)),
    )(oa, ob, y_cbm, y_cbm, naw, nbw)


def _gout_bwd(oa, ob, y_cbm, naw, nbw, dymix, l_lat, l_ctx, h):
    tm = _pick(l_ctx, (256, 128, 64))
    nl, nc = l_lat // tm, l_ctx // tm
    t = l_lat + l_ctx
    blk_in = lambda sg: pl.BlockSpec((h, tm, LANES), lambda i: (sg, jnp.minimum(i, nl - 1), 0))
    blk_out = pl.BlockSpec((h, tm, LANES), lambda i: (0, i, 0))
    vec = pl.BlockSpec((h, 1, LANES), lambda i: (0, 0, 0))

    def body(oa_ref, ob_ref, za_ref, zb_ref, na_ref, nb_ref, dy_ref, doa_ref, dob_ref, dza_ref, dzb_ref, dna_ref, dnb_ref):
        i = pl.program_id(0)

        @pl.when(i == 0)
        def _():
            dna_ref[...] = jnp.zeros_like(dna_ref)
            dnb_ref[...] = jnp.zeros_like(dnb_ref)

        @pl.when(i < nl)
        def _():
            _, vj = jax.vjp(_gout_tile, oa_ref[...], ob_ref[...], za_ref[...], zb_ref[...], na_ref[...], nb_ref[...])
            doa, dob, dza, dzb, dna, dnb = vj(dy_ref[...])
            doa_ref[...] = doa
            dob_ref[...] = dob
            dza_ref[...] = dza
            dzb_ref[...] = dzb
            dna_ref[...] += dna
            dnb_ref[...] += dnb

        @pl.when(i >= nl)
        def _():
            for r in (doa_ref, dob_ref, dza_ref, dzb_ref):
                r[...] = jnp.zeros_like(r)

    big = jax.ShapeDtypeStruct((h, t, LANES), F32)
    small = jax.ShapeDtypeStruct((h, 1, LANES), F32)
    dza_out = pl.BlockSpec((h, tm, LANES), lambda i: (4, i, 0))
    return pl.pallas_call(
        body, name="gout_bwd", grid=(nl + nc,),
        in_specs=[blk_in(0), blk_in(0), blk_in(4), blk_in(8), vec, vec,
                  pl.BlockSpec((tm, 2 * h * LANES), lambda i: (jnp.minimum(i, nl - 1), 0))],
        out_specs=[blk_out, blk_out, dza_out, blk_out, vec, vec],
        out_shape=[big, big, jax.ShapeDtypeStruct(y_cbm.shape, F32), big, small, small],
        compiler_params=_params(("arbitrary",)),
    )(oa, ob, y_cbm, y_cbm, naw, nbw, dymix)


def _head_tile(xt, mixt, tgt, gt, fw):
    xo = xt + gt * mixt
    y = xo * lax.rsqrt(jnp.mean(xo * xo, axis=-1, keepdims=True) + NORM_EPS) * fw
    err = y - tgt
    return 0.5 * jnp.sum(jnp.mean(err * err, axis=-1, keepdims=True), axis=0, keepdims=True)


def _loss_head(x, mix, target, mods, final_w):
    l_lat, d = x.shape
    tm = _pick(l_lat, (256, 128, 64))
    tok = pl.BlockSpec((tm, d), lambda i: (i, 0))
    row = pl.BlockSpec((1, d), lambda i: (0, 0))

    def body(x_ref, m_ref, t_ref, mod_ref, fw_ref, loss_ref, dmix_ref, dx_ref, dgt_ref, dfw_ref):
        @pl.when(pl.program_id(0) == 0)
        def _():
            loss_ref[...] = jnp.zeros_like(loss_ref)
            dgt_ref[...] = jnp.zeros_like(dgt_ref)
            dfw_ref[...] = jnp.zeros_like(dfw_ref)

        gt = mod_ref[0:1, 2 * d:3 * d]
        fn = lambda xt, mt, g, fw: _head_tile(xt, mt, t_ref[...], g, fw)
        val, vj = jax.vjp(fn, x_ref[...], m_ref[...], gt, fw_ref[...])
        dxt, dmt, dgt, dfw = vj(jnp.ones((1, 1), F32))
        loss_ref[...] += jnp.broadcast_to(val, loss_ref.shape)
        dmix_ref[...] = dmt.astype(BF16)
        dx_ref[...] = dxt
        dgt_ref[...] += dgt
        dfw_ref[...] += dfw

    return pl.pallas_call(
        body, name="loss_head", grid=(l_lat // tm,),
        in_specs=[tok, tok, tok, pl.BlockSpec((8, 3 * d), lambda i: (0, 0)), row],
        out_specs=[pl.BlockSpec((8, LANES), lambda i: (0, 0)), tok, tok, row, row],
        out_shape=[jax.ShapeDtypeStruct((8, LANES), F32), jax.ShapeDtypeStruct((l_lat, d), BF16),
                   jax.ShapeDtypeStruct((l_lat, d), F32), jax.ShapeDtypeStruct((1, d), F32),
                   jax.ShapeDtypeStruct((1, d), F32)],
        compiler_params=_params(("arbitrary",)),
    )(x, mix, target, mods, final_w)


def _adamw(w, g_slots, m, v, name):
    rows, cols = w.shape
    n = g_slots.shape[0]
    tr = _pick(rows, (64, 32, 16, 8))

    def body(w_ref, g_ref, m_ref, v_ref, go_ref, d_ref, nm_ref, nv_ref):
        gg = g_ref[0].astype(F32)
        for s in range(1, n):
            gg = gg + g_ref[s].astype(F32)
        m2 = ADAM_B1 * m_ref[...] + (1.0 - ADAM_B1) * gg
        v2 = ADAM_B2 * v_ref[...] + (1.0 - ADAM_B2) * (gg * gg)
        m_hat = m2 / (1.0 - ADAM_B1 ** ADAM_STEP)
        v_hat = v2 / (1.0 - ADAM_B2 ** ADAM_STEP)
        go_ref[...] = gg
        d_ref[...] = -ADAM_LR * (m_hat / (jnp.sqrt(v_hat) + ADAM_EPS) + ADAM_WD * w_ref[...])
        nm_ref[...] = m2
        nv_ref[...] = v2

    spec = pl.BlockSpec((tr, cols), lambda i: (i, 0))
    shp = jax.ShapeDtypeStruct((rows, cols), F32)
    return pl.pallas_call(
        body, name=name, grid=(rows // tr,),
        in_specs=[spec, pl.BlockSpec((n, tr, cols), lambda i: (0, i, 0)), spec, spec],
        out_specs=[spec] * 4, out_shape=[shp] * 4,
        compiler_params=_params(("arbitrary",)),
    )(w, g_slots, m, v)


def _adamw_nd(w, g_slots, m, v, name):
    shape = w.shape
    two_d = (-1, shape[-1])
    w2 = w.reshape(two_d)
    outs = _adamw(w2, g_slots.reshape((g_slots.shape[0],) + w2.shape), m.reshape(two_d), v.reshape(two_d), name)
    return [o.reshape(shape) for o in outs]


def kernel(x, c, ctx, c_ctx, norm_w, ada_w, ada_b, w_in, conv_w, hg_lb_logits, gdn_a_log, gdn_dt_bias, ha_norm_w, hb_norm_w, w_out, final_norm_w, loss_target, m_c_ctx, m_norm_w, m_ada_w, m_ada_b, m_w_in, m_conv_w, m_hg_lb_logits, m_gdn_a_log, m_gdn_dt_bias, m_ha_norm_w, m_hb_norm_w, m_w_out, m_final_norm_w, v_c_ctx, v_norm_w, v_ada_w, v_ada_b, v_w_in, v_conv_w, v_hg_lb_logits, v_gdn_a_log, v_gdn_dt_bias, v_ha_norm_w, v_hb_norm_w, v_w_out, v_final_norm_w):
    l_lat, d = x.shape[1], x.shape[2]
    l_ctx = ctx.shape[1]
    t = l_lat + l_ctx
    h = d // LANES
    n_in = 9 * d + 4 * h
    nb = -(-(9 * h + 1) // COLS_PER_TILE) * COLS_PER_TILE
    n_pad = nb * LANES
    win_c, ada_c, conv_c, lb_c, wout_r = w_in.shape[2], ada_w.shape[2], conv_w.shape[3], hg_lb_logits.shape[2], w_out.shape[1]
    me = _me()

    gw_in, gw_out, gw_ada, gw_conv, gw_lb = _gather_two_level(
        [w_in[0].astype(BF16), w_out[0].astype(BF16), ada_w[0], conv_w[0], hg_lb_logits], "gather_weights")
    join = lambda g, axis: jnp.concatenate([g[k] for k in range(N_DEV)], axis=axis)
    w_in_full = join(gw_in, 1)
    w_out_full = gw_out.reshape(N_DEV * wout_r, d)
    ada_full = join(gw_ada, 1)
    conv_full = join(gw_conv, 2)
    lb_full = join(gw_lb, 2)
    w_pad = jnp.pad(w_in_full, ((0, 0), (0, n_pad - n_in)))
    w_pad_t = w_pad.T
    w_out_t = w_out_full.T

    lbl = lb_full.reshape(2, 2, h, 1, LANES)
    conv_rows = jnp.pad(conv_full.reshape(9, 3 * h, LANES).transpose(1, 0, 2), ((0, 0), (0, 7), (0, 0)))
    lane_bc = lambda a: jnp.broadcast_to(a.reshape(2, h, 1, 1), (2, h, 1, LANES))
    avec, dtvec = lane_bc(gdn_a_log[0]), lane_bc(gdn_dt_bias[0])
    naw, nbw = ha_norm_w[0].reshape(h, 1, LANES), hb_norm_w[0].reshape(h, 1, LANES)

    cvec = jnp.concatenate([c, c_ctx[None, :], jnp.zeros((6, d), F32)], axis=0)
    mods = _ada_fwd(cvec, ada_full, ada_b)
    x2, ctx2, tgt2 = x[0], ctx[0], loss_target[0]
    h_all = _prenorm_fwd(x2, ctx2, norm_w, mods)
    y = _matmul(h_all, w_pad, "proj", out_cbm=True, tn=COLS_PER_TILE * LANES)
    tail = lax.slice_in_dim(y, 9 * h, 9 * h + 1, axis=0)
    u = _conv_call(y, conv_rows, l_lat, 5 * h, 3 * h, "conv_fwd")

    oa, st_af = _scan_fwd("hg", False, y, (0, 1, 3), (lbl[0, 0], lbl[1, 0]), None, l_lat, l_ctx, h, "hg_fwd_f")
    oa, st_ab = _scan_fwd("hg", True, y, (0, 2, 3), (lbl[0, 1], lbl[1, 1]), oa, l_lat, l_ctx, h, "hg_fwd_b")
    ob, st_bf, xk_f = _scan_fwd("gdn", False, u, (0, 1, 2), (avec[0], dtvec[0]), None, l_lat, l_ctx, h, "gdn_fwd_f", tail, 0)
    ob, st_bb, xk_b = _scan_fwd("gdn", True, u, (0, 1, 2), (avec[1], dtvec[1]), ob, l_lat, l_ctx, h, "gdn_fwd_b", tail, 1)

    ymix = _gout_fwd(oa, ob, y, naw, nbw, l_lat, h)
    mix = _matmul(ymix, w_out_full, "out_proj")
    loss_blk, dmix, dx_res, dgt, dfw = _loss_head(x2, mix, tgt2, mods, final_norm_w.reshape(1, d))

    dymix = _matmul(dmix, w_out_t, "d_ymix")
    dw_out = _matmul(ymix, dmix, "d_w_out", a_t=True)
    doa, dob, dy, dzb, dnaw, dnbw = _gout_bwd(oa, ob, y, naw, nbw, dymix, l_lat, l_ctx, h)

    gq, gff, gi, dl0f, dl1f = _scan_bwd("hg", False, y, (0, 1, 3), (lbl[0, 0], lbl[1, 0]), st_af, doa,
                                        [None, None, None], l_lat, l_ctx, h, "hg_bwd_f")
    dy, dl0b, dl1b = _scan_bwd("hg", True, y, (0, 2, 3), (lbl[0, 1], lbl[1, 1]), st_ab, doa, [gq, None, gi],
                               l_lat, l_ctx, h, "hg_bwd_b",
                               pack=[("grad", 0), ("copy", gff), ("grad", 1), ("grad", 2)], pack_into=(dy, 0))
    guq, guk, guv, gtail, da_f, ddt_f = _scan_bwd("gdn", False, u, (0, 1, 2), (avec[0], dtvec[0]), st_bf, dob,
                                                  [None] * 4, l_lat, l_ctx, h, "gdn_bwd_f", tail, 0, xk_f)
    du, dy, da_b, ddt_b = _scan_bwd("gdn", True, u, (0, 1, 2), (avec[1], dtvec[1]), st_bb, dob,
                                    [guq, guk, guv, gtail], l_lat, l_ctx, h, "gdn_bwd_b", tail, 1, xk_b,
                                    pack=[("grad", 0), ("grad", 1), ("grad", 2)], tail_into=(dy, 9 * h, nb - 9 * h))
    dy, dconv_rows = _conv_call(y, conv_rows, l_lat, 5 * h, 3 * h, "conv_bwd", dout=du, passthru=dzb, into=(dy, 5 * h))
    dh = _matmul(dy, w_pad_t, "d_h", a_cbm=True)
    dw_in = _matmul(h_all, dy, "d_w_in", a_t=True, b_cbm=True)
    grad_x, dnorm_w, dmods_pre = _prenorm_bwd(x2, ctx2, norm_w, mods, dh, dx_res)

    dmods = jnp.concatenate([dmods_pre[:, :2 * d],
                             jnp.concatenate([dgt, jnp.zeros((7, d), F32)], axis=0)], axis=1)
    dcond = _ada_bwd(cvec, ada_full, dmods)

    dw_in_s = jnp.stack([dw_in[:, k * win_c:(k + 1) * win_c] for k in range(N_DEV)])
    dw_out_s = dw_out.reshape(N_DEV, wout_r, d)
    dconv_s = dconv_rows[:, :9, :].transpose(1, 0, 2).reshape(3, 3, N_DEV, conv_c).transpose(2, 0, 1, 3)
    dlb_s = jnp.stack([jnp.stack([dl0f, dl0b]), jnp.stack([dl1f, dl1b])]).reshape(2, 2, N_DEV, lb_c).transpose(2, 0, 1, 3)
    slabs = [dw_in_s, dw_out_s, dconv_s, dlb_s]
    theirs = _swap_with_sibling([a.astype(BF16) for a in slabs], "swap_grads")
    pairs = [_pair_sum(a, b, "pair_sum_%d" % k) for k, (a, b) in enumerate(zip(slabs, theirs))]
    s_w_in, s_w_out, s_conv, s_lb = _scatter_over_chips(pairs, "scatter_grads")

    da = jnp.stack([da_f[:, 0, 0], da_b[:, 0, 0]])
    ddt = jnp.stack([ddt_f[:, 0, 0], ddt_b[:, 0, 0]])
    small_parts = [dcond[1], dnorm_w, dmods[0] + dmods[1], dnaw, dnbw, dfw, da, ddt, c[0], dmods[0], dmods[1]]
    small_blob, soff = _to_blob(small_parts)
    gathered, = _exchange([small_blob], "gather_small", True)
    summed = _sum_slots(gathered, "sum_small").reshape(-1)
    gflat = gathered.reshape(N_DEV, -1)
    take = lambda k, n: summed[soff[k]:soff[k] + n]
    one = lambda k, n, shape: take(k, n).reshape((1,) + shape)
    s_c_ctx = one(0, d, (d,))
    s_norm_w = one(1, d, (1, d))
    s_ada_b = one(2, 3 * d, (1, 3 * d))
    s_ha = one(3, d, (1, h, LANES))
    s_hb = one(4, d, (1, h, LANES))
    s_final = one(5, d, (d,))
    s_a_log = one(6, 2 * h, (1, 2, h))
    s_dt = one(7, 2 * h, (1, 2, h))
    conds = jnp.concatenate([gflat[:, soff[8]:soff[8] + d], c_ctx[None, :], jnp.zeros((7, d), F32)], axis=0)
    col0 = me * ada_c
    dm_lat = lax.dynamic_slice_in_dim(gflat[:, soff[9]:soff[9] + 3 * d], col0, ada_c, axis=1)
    dm_ctx = lax.dynamic_slice_in_dim(gflat[:, soff[10]:soff[10] + 3 * d], col0, ada_c, axis=1)
    s_ada_w = _ada_wgrad(conds, dm_lat, dm_ctx).reshape(1, 1, d, ada_c)

    loss = lax.psum(loss_blk[0, 0], ("x", "y", "c"))

    weights = [("c_ctx", c_ctx, s_c_ctx, m_c_ctx, v_c_ctx), ("norm_w", norm_w, s_norm_w, m_norm_w, v_norm_w),
               ("ada_w", ada_w, s_ada_w, m_ada_w, v_ada_w), ("ada_b", ada_b, s_ada_b, m_ada_b, v_ada_b),
               ("w_in", w_in, s_w_in, m_w_in, v_w_in), ("conv_w", conv_w, s_conv, m_conv_w, v_conv_w),
               ("hg_lb_logits", hg_lb_logits, s_lb, m_hg_lb_logits, v_hg_lb_logits),
               ("gdn_a_log", gdn_a_log, s_a_log, m_gdn_a_log, v_gdn_a_log),
               ("gdn_dt_bias", gdn_dt_bias, s_dt, m_gdn_dt_bias, v_gdn_dt_bias),
               ("ha_norm_w", ha_norm_w, s_ha, m_ha_norm_w, v_ha_norm_w), ("hb_norm_w", hb_norm_w, s_hb, m_hb_norm_w, v_hb_norm_w),
               ("w_out", w_out, s_w_out, m_w_out, v_w_out), ("final_norm_w", final_norm_w, s_final, m_final_norm_w, v_final_norm_w)]
    grads, deltas, new_ms, new_vs = [], [], [], []
    for nm, w, gs, m, v in weights:
        g, dl, m2, v2 = _adamw_nd(w, gs, m, v, "adamw_" + nm)
        grads.append(g)
        deltas.append(dl)
        new_ms.append(m2)
        new_vs.append(v2)
    return (loss, grad_x[None], *grads, *deltas, *new_ms, *new_vs)
```

```python
import functools

import jax
import jax.numpy as jnp
from jax import lax
from jax.experimental import pallas as pl
from jax.experimental.pallas import tpu as pltpu

F32 = jnp.float32
BF16 = jnp.bfloat16
HI = lax.Precision.HIGHEST
HIGH = lax.Precision.HIGH
MESH = pl.DeviceIdType.MESH
ANY = pl.BlockSpec(memory_space=pl.ANY)

N_DEV = 8
LANES = 128
CHUNK = 64
GRID_W = 64
NORM_EPS = 1e-6
NEG = -1e30
COLS_PER_TILE = 8
VMEM_BIG = 56 * 1024 * 1024
HEADS_PER_GROUP = 4
FWD_STEP = 4
BWD_STEP = 2
HALVES = (32, 16, 8, 4, 2, 1)

ADAM_LR, ADAM_B1, ADAM_B2, ADAM_EPS, ADAM_WD, ADAM_STEP = 0.001, 0.9, 0.999, 1e-08, 0.01, 10


def _params(sem=None, vmem=None):
    kw = {}
    if sem is not None:
        kw["dimension_semantics"] = sem
    if vmem is not None:
        kw["vmem_limit_bytes"] = vmem
    return pltpu.CompilerParams(**kw)


def _pick(n, cands):
    for c in cands:
        if n % c == 0:
            return c
    return n


def _silu(x):
    return x * jax.nn.sigmoid(x)


def _mm(a, b, prec=None):
    return lax.dot_general(a, b, (((1,), (0,)), ((), ())), precision=prec, preferred_element_type=F32)


def _mm_nt(a, b, prec=None):
    return lax.dot_general(a, b, (((1,), (1,)), ((), ())), precision=prec, preferred_element_type=F32)


def _mm_tn(a, b, prec=None):
    return lax.dot_general(a, b, (((0,), (0,)), ((), ())), precision=prec, preferred_element_type=F32)


def _me():
    return 4 * lax.axis_index("x") + 2 * lax.axis_index("y") + lax.axis_index("c")


def _peer(r):
    x, y, c = lax.axis_index("x"), lax.axis_index("y"), lax.axis_index("c")
    px = 1 - x if r & 4 else x
    py = 1 - y if r & 2 else y
    pc = 1 - c if r & 1 else c
    return (px, py, pc), 4 * px + 2 * py + pc


def _exchange(arrs, name, gather):
    n_a = len(arrs)

    def body(*refs):
        srcs, outs = refs[:n_a], refs[n_a:2 * n_a]
        send_sems, recv_sems, local_sems = refs[2 * n_a:]
        me = _me()
        copies = []
        for k in range(n_a):
            local = pltpu.make_async_copy(srcs[k] if gather else srcs[k].at[me], outs[k].at[me], local_sems.at[k])
            local.start()
            copies.append(local)
        for r in range(N_DEV - 1, 0, -1):
            dev, pid = _peer(r)
            for k in range(n_a):
                cp = pltpu.make_async_remote_copy(
                    src_ref=srcs[k] if gather else srcs[k].at[pid], dst_ref=outs[k].at[me],
                    send_sem=send_sems.at[(r - 1) * n_a + k], recv_sem=recv_sems.at[(r - 1) * n_a + k],
                    device_id=dev, device_id_type=MESH)
                cp.start()
                copies.append(cp)
        for cp in copies:
            cp.wait()

    n_sem = (N_DEV - 1) * n_a
    return pl.pallas_call(
        body, name=name,
        out_shape=[jax.ShapeDtypeStruct(((N_DEV,) + a.shape) if gather else a.shape, a.dtype) for a in arrs],
        in_specs=[ANY] * n_a, out_specs=[ANY] * n_a,
        scratch_shapes=[pltpu.SemaphoreType.DMA((n_sem,)), pltpu.SemaphoreType.DMA((n_sem,)),
                        pltpu.SemaphoreType.DMA((n_a,))],
    )(*arrs)


def _gather_two_level(arrs, name):
    n_a = len(arrs)

    def body(*refs):
        srcs, outs = refs[:n_a], refs[n_a:2 * n_a]
        send_sems, recv_sems, local_sems = refs[2 * n_a:]
        x, y, c = lax.axis_index("x"), lax.axis_index("y"), lax.axis_index("c")
        me, sibling = (x, y, c), (x, y, 1 - c)
        chips = [(1 - x, y), (x, 1 - y), (1 - x, 1 - y)]

        def copy(k, j, block, to, src=None):
            slot = outs[k].at[4 * block[0] + 2 * block[1] + block[2]]
            return pltpu.make_async_remote_copy(
                src_ref=slot if src is None else src, dst_ref=slot,
                send_sem=send_sems.at[j * n_a + k], recv_sem=recv_sems.at[j * n_a + k],
                device_id=to, device_id_type=MESH)

        mine = [pltpu.make_async_copy(srcs[k], outs[k].at[4 * x + 2 * y + c], local_sems.at[k]) for k in range(n_a)]
        for cp in mine:
            cp.start()
        first = [copy(k, 1 + j, me, (*chip, c), srcs[k]) for j, chip in enumerate(chips) for k in range(n_a)]
        first += [copy(k, 0, me, sibling, srcs[k]) for k in range(n_a)]
        for cp in first:
            cp.start()
        passed = []
        for j, chip in enumerate(chips):
            for k in range(n_a):
                copy(k, 1 + j, (*chip, c), me).wait_recv()
                cp = copy(k, 4 + j, (*chip, c), sibling)
                cp.start()
                passed.append(cp)
        for k in range(n_a):
            copy(k, 0, sibling, me).wait_recv()
        for j, chip in enumerate(chips):
            for k in range(n_a):
                copy(k, 4 + j, (*chip, 1 - c), me).wait_recv()
        for cp in first + passed:
            cp.wait_send()
        for cp in mine:
            cp.wait()

    n_sem = (N_DEV - 1) * n_a
    return pl.pallas_call(
        body, name=name,
        out_shape=[jax.ShapeDtypeStruct((N_DEV,) + a.shape, a.dtype) for a in arrs],
        in_specs=[ANY] * n_a, out_specs=[ANY] * n_a,
        scratch_shapes=[pltpu.SemaphoreType.DMA((n_sem,)), pltpu.SemaphoreType.DMA((n_sem,)),
                        pltpu.SemaphoreType.DMA((n_a,))],
    )(*arrs)


def _swap_with_sibling(arrs, name):
    n_a = len(arrs)

    def body(*refs):
        srcs, outs = refs[:n_a], refs[n_a:2 * n_a]
        send_sems, recv_sems = refs[2 * n_a:]
        sibling = (lax.axis_index("x"), lax.axis_index("y"), 1 - lax.axis_index("c"))
        copies = [pltpu.make_async_remote_copy(src_ref=srcs[k], dst_ref=outs[k], send_sem=send_sems.at[k],
                                               recv_sem=recv_sems.at[k], device_id=sibling, device_id_type=MESH)
                  for k in range(n_a)]
        for cp in copies:
            cp.start()
        for cp in copies:
            cp.wait()

    return pl.pallas_call(
        body, name=name, out_shape=[jax.ShapeDtypeStruct(a.shape, a.dtype) for a in arrs],
        in_specs=[ANY] * n_a, out_specs=[ANY] * n_a,
        scratch_shapes=[pltpu.SemaphoreType.DMA((n_a,)), pltpu.SemaphoreType.DMA((n_a,))],
    )(*arrs)


def _scatter_over_chips(arrs, name):
    n_a = len(arrs)
    n_chip = N_DEV // 2

    def body(*refs):
        srcs, outs = refs[:n_a], refs[n_a:2 * n_a]
        send_sems, recv_sems, local_sems = refs[2 * n_a:]
        x, y, c = lax.axis_index("x"), lax.axis_index("y"), lax.axis_index("c")
        my_chip = 2 * x + y
        copies = []
        for k in range(n_a):
            local = pltpu.make_async_copy(srcs[k].at[4 * x + 2 * y + c], outs[k].at[my_chip], local_sems.at[k])
            local.start()
            copies.append(local)
        for r in range(n_chip - 1, 0, -1):
            px = 1 - x if r & 2 else x
            py = 1 - y if r & 1 else y
            for k in range(n_a):
                cp = pltpu.make_async_remote_copy(
                    src_ref=srcs[k].at[4 * px + 2 * py + c], dst_ref=outs[k].at[my_chip],
                    send_sem=send_sems.at[(r - 1) * n_a + k], recv_sem=recv_sems.at[(r - 1) * n_a + k],
                    device_id=(px, py, c), device_id_type=MESH)
                cp.start()
                copies.append(cp)
        for cp in copies:
            cp.wait()

    n_sem = (n_chip - 1) * n_a
    return pl.pallas_call(
        body, name=name,
        out_shape=[jax.ShapeDtypeStruct((n_chip,) + a.shape[1:], a.dtype) for a in arrs],
        in_specs=[ANY] * n_a, out_specs=[ANY] * n_a,
        scratch_shapes=[pltpu.SemaphoreType.DMA((n_sem,)), pltpu.SemaphoreType.DMA((n_sem,)),
                        pltpu.SemaphoreType.DMA((n_a,))],
    )(*arrs)


def _pair_sum(a, b, name):
    shape = a.shape
    a2, b2 = a.reshape(-1, shape[-1]), b.reshape(-1, shape[-1])
    rows, cols = a2.shape
    tr = _pick(rows, (256, 128, 64, 32, 16))

    def body(a_ref, b_ref, o_ref):
        o_ref[...] = (a_ref[...] + b_ref[...].astype(F32)).astype(BF16)

    spec = pl.BlockSpec((tr, cols), lambda i: (i, 0))
    return pl.pallas_call(
        body, name=name, grid=(rows // tr,), in_specs=[spec, spec], out_specs=spec,
        out_shape=jax.ShapeDtypeStruct((rows, cols), BF16), compiler_params=_params(("arbitrary",)),
    )(a2, b2).reshape(shape)


def _to_blob(parts):
    flat = [p.reshape(-1).astype(F32) for p in parts]
    offs, n = [], 0
    for f in flat:
        offs.append(n)
        n += f.shape[0]
    unit = 8 * LANES
    total = -(-n // unit) * unit
    if total > n:
        flat.append(jnp.zeros((total - n,), F32))
    return jnp.concatenate(flat).reshape(total // LANES, LANES), offs


def _sum_slots(buf, name):
    n, rows, cols = buf.shape
    tr = _pick(rows, (512, 256, 128, 64, 32, 16, 8))

    def body(b_ref, o_ref):
        acc = b_ref[0]
        for s in range(1, n):
            acc = acc + b_ref[s]
        o_ref[...] = acc

    return pl.pallas_call(
        body, name=name, grid=(rows // tr,),
        in_specs=[pl.BlockSpec((n, tr, cols), lambda i: (0, i, 0))],
        out_specs=pl.BlockSpec((tr, cols), lambda i: (i, 0)),
        out_shape=jax.ShapeDtypeStruct((rows, cols), F32),
        compiler_params=_params(("arbitrary",)),
    )(buf)


def _matmul(a, b, name, a_cbm=False, b_cbm=False, out_cbm=False, a_t=False, out_dtype=F32, tm=None, tn=None, tk=None):
    if a_cbm:
        m, k = a.shape[1], a.shape[0] * LANES
    elif a_t:
        k, m = a.shape
    else:
        m, k = a.shape
    n = b.shape[0] * LANES if b_cbm else b.shape[1]
    tm = tm or _pick(m, (1024, 768, 512, 384, 256, 128, 64))
    tn = tn or _pick(n, (1024, 512, 256, 128))
    tk = tk or _pick(k, (2048, 1024, 768, 512, 256, 128))
    nk = k // tk

    def load(ref, cbm):
        if not cbm:
            return ref[...].astype(BF16)
        return jnp.concatenate([ref[j].astype(BF16) for j in range(ref.shape[0])], axis=1)

    def body(a_ref, b_ref, o_ref, acc_ref):
        kk = pl.program_id(2)
        part = (_mm_tn if a_t else _mm)(load(a_ref, a_cbm), load(b_ref, b_cbm))

        @pl.when(kk == 0)
        def _():
            acc_ref[...] = part

        @pl.when(kk > 0)
        def _():
            acc_ref[...] += part

        @pl.when(kk == nk - 1)
        def _():
            r = acc_ref[...]
            if out_cbm:
                for j in range(tn // LANES):
                    o_ref[j] = r[:, j * LANES:(j + 1) * LANES].astype(out_dtype)
            else:
                o_ref[...] = r.astype(out_dtype)

    if a_cbm:
        a_spec = pl.BlockSpec((tk // LANES, tm, LANES), lambda j, i, kk: (kk, i, 0))
    elif a_t:
        a_spec = pl.BlockSpec((tk, tm), lambda j, i, kk: (kk, i))
    else:
        a_spec = pl.BlockSpec((tm, tk), lambda j, i, kk: (i, kk))
    if b_cbm:
        b_spec = pl.BlockSpec((tn // LANES, tk, LANES), lambda j, i, kk: (j, kk, 0))
    else:
        b_spec = pl.BlockSpec((tk, tn), lambda j, i, kk: (kk, j))
    if out_cbm:
        o_spec = pl.BlockSpec((tn // LANES, tm, LANES), lambda j, i, kk: (j, i, 0))
        o_shape = jax.ShapeDtypeStruct((n // LANES, m, LANES), out_dtype)
    else:
        o_spec = pl.BlockSpec((tm, tn), lambda j, i, kk: (i, j))
        o_shape = jax.ShapeDtypeStruct((m, n), out_dtype)
    return pl.pallas_call(
        body, name=name, grid=(n // tn, m // tm, nk),
        in_specs=[a_spec, b_spec], out_specs=o_spec, out_shape=o_shape,
        scratch_shapes=[pltpu.VMEM((tm, tn), F32)],
        compiler_params=_params(("arbitrary", "arbitrary", "arbitrary"), VMEM_BIG),
    )(a, b)


def _ada_fwd(cvec, ada_w, ada_b):
    def body(c_ref, w_ref, b_ref, o_ref):
        o_ref[...] = _mm(_silu(c_ref[...]), w_ref[...], HI) + b_ref[...]

    return pl.pallas_call(
        body, name="ada_fwd", out_shape=jax.ShapeDtypeStruct((8, ada_w.shape[1]), F32),
        compiler_params=_params(None, VMEM_BIG),
    )(cvec, ada_w, ada_b)


def _ada_bwd(cvec, ada_w, dmods):
    def body(c_ref, w_ref, d_ref, o_ref):
        (_, vj) = jax.vjp(_silu, c_ref[...])
        o_ref[...] = vj(_mm_nt(d_ref[...], w_ref[...], HI))[0]

    return pl.pallas_call(
        body, name="ada_bwd", out_shape=jax.ShapeDtypeStruct(cvec.shape, F32),
        compiler_params=_params(None, VMEM_BIG),
    )(cvec, ada_w, dmods)


def _ada_wgrad(conds, dm_lat, dm_ctx):
    d = conds.shape[1]
    cols = dm_lat.shape[1]

    def body(c_ref, dl_ref, dc_ref, o_ref):
        dctx = dc_ref[0:1]
        for s in range(1, N_DEV):
            dctx = dctx + dc_ref[s:s + 1]
        rhs = jnp.concatenate([dl_ref[...], dctx, jnp.zeros((7, cols), F32)], axis=0)
        o_ref[...] = _mm_tn(_silu(c_ref[...]), rhs, HI)

    return pl.pallas_call(
        body, name="ada_wgrad", out_shape=jax.ShapeDtypeStruct((d, cols), F32),
    )(conds, dm_lat, dm_ctx)


def _prenorm_tile(xt, nw, sc, sh):
    r = lax.rsqrt(jnp.mean(xt * xt, axis=-1, keepdims=True) + NORM_EPS)
    return (xt * r * nw) * (1.0 + sc) + sh


def _tok_specs(l_lat, l_ctx, d, tm):
    nl, nc = l_lat // tm, l_ctx // tm
    lat = pl.BlockSpec((tm, d), lambda i: (jnp.minimum(i, nl - 1), 0))
    ctx = pl.BlockSpec((tm, d), lambda i: (jnp.maximum(i - nl, 0), 0))
    return nl, nc, lat, ctx


def _prenorm_fwd(x, ctx, norm_w, mods):
    l_lat, d = x.shape
    l_ctx = ctx.shape[0]
    tm = _pick(l_ctx, (256, 128, 64))
    nl, nc, lat_spec, ctx_spec = _tok_specs(l_lat, l_ctx, d, tm)

    def body(x_ref, c_ref, nw_ref, m_ref, h_ref):
        is_lat = pl.program_id(0) < nl
        xt = jnp.where(is_lat, x_ref[...], c_ref[...])
        row = jnp.where(is_lat, m_ref[0:1, :], m_ref[1:2, :])
        h_ref[...] = _prenorm_tile(xt, nw_ref[...], row[:, d:2 * d], row[:, 0:d]).astype(BF16)

    return pl.pallas_call(
        body, name="prenorm_fwd", grid=(nl + nc,),
        in_specs=[lat_spec, ctx_spec, pl.BlockSpec((1, d), lambda i: (0, 0)), pl.BlockSpec((8, 3 * d), lambda i: (0, 0))],
        out_specs=pl.BlockSpec((tm, d), lambda i: (i, 0)),
        out_shape=jax.ShapeDtypeStruct((l_lat + l_ctx, d), BF16),
        compiler_params=_params(("arbitrary",)),
    )(x, ctx, norm_w, mods)


def _prenorm_bwd(x, ctx, norm_w, mods, dh, dx_res):
    l_lat, d = x.shape
    l_ctx = ctx.shape[0]
    tm = _pick(l_ctx, (256, 128, 64))
    nl, nc, lat_spec, ctx_spec = _tok_specs(l_lat, l_ctx, d, tm)

    def body(x_ref, c_ref, nw_ref, m_ref, dh_ref, dr_ref, gx_ref, dnw_ref, dm_ref):
        i = pl.program_id(0)
        is_lat = i < nl

        @pl.when(i == 0)
        def _():
            dnw_ref[...] = jnp.zeros_like(dnw_ref)
            dm_ref[...] = jnp.zeros_like(dm_ref)

        xt = jnp.where(is_lat, x_ref[...], c_ref[...])
        row = jnp.where(is_lat, m_ref[0:1, :], m_ref[1:2, :])
        _, vj = jax.vjp(_prenorm_tile, xt, nw_ref[...], row[:, d:2 * d], row[:, 0:d])
        dxt, dnw, dsc, dsh = vj(dh_ref[...])
        dnw_ref[...] += dnw
        upd = jnp.concatenate([dsh, dsc, jnp.zeros_like(dsh)], axis=1)

        @pl.when(is_lat)
        def _():
            gx_ref[...] = dr_ref[...] + dxt
            dm_ref[0:1, :] += upd

        @pl.when(jnp.logical_not(is_lat))
        def _():
            dm_ref[1:2, :] += upd

    return pl.pallas_call(
        body, name="prenorm_bwd", grid=(nl + nc,),
        in_specs=[lat_spec, ctx_spec, pl.BlockSpec((1, d), lambda i: (0, 0)), pl.BlockSpec((8, 3 * d), lambda i: (0, 0)),
                  pl.BlockSpec((tm, d), lambda i: (i, 0)), lat_spec],
        out_specs=[lat_spec, pl.BlockSpec((1, d), lambda i: (0, 0)), pl.BlockSpec((8, 3 * d), lambda i: (0, 0))],
        out_shape=[jax.ShapeDtypeStruct((l_lat, d), F32), jax.ShapeDtypeStruct((1, d), F32),
                   jax.ShapeDtypeStruct((8, 3 * d), F32)],
        compiler_params=_params(("arbitrary",)),
    )(x, ctx, norm_w, mods, dh, dx_res)


def _conv_call(src, wts, l_lat, first_block, n_blocks, name, dout=None, passthru=None, into=None):
    t = src.shape[1]
    l_ctx = t - l_lat
    rc = _pick(l_lat, (512, 256, 128, 64))
    n_rc = l_lat // rc
    bwd = dout is not None

    def shifted(xc, n, period):
        pos = lax.broadcasted_iota(jnp.int32, (n, LANES), 0) % period
        xm = jnp.where(pos == 0, 0.0, pltpu.roll(xc, 1, 0))
        xp = jnp.where(pos == period - 1, 0.0, pltpu.roll(xc, n - 1, 0))
        return xm, xc, xp

    def fill(ref, sm, s0, sp):
        zero = jnp.zeros((GRID_W, LANES), F32)
        for s in (sm, s0, sp):
            s[pl.ds(0, GRID_W), :] = zero
            s[pl.ds(l_lat + GRID_W, GRID_W), :] = zero

        def step(i, carry):
            st = pl.multiple_of(i * rc, rc)
            xm, x0, xp = shifted(ref[0, pl.ds(st, rc), :], rc, GRID_W)
            sm[pl.ds(st + GRID_W, rc), :] = xm
            s0[pl.ds(st + GRID_W, rc), :] = x0
            sp[pl.ds(st + GRID_W, rc), :] = xp
            return carry

        lax.fori_loop(0, n_rc, step, 0)

    def apply(w, sm, s0, sp, out_ref, flip):
        def step(i, carry):
            st = pl.multiple_of(i * rc, rc)
            acc = jnp.zeros((rc, LANES), F32)
            for di in range(3):
                for dj, s in enumerate((sm, s0, sp)):
                    kidx = (2 - di) * 3 + (2 - dj) if flip else di * 3 + dj
                    acc = acc + w[kidx:kidx + 1, :] * s[pl.ds(st + di * GRID_W, rc), :]
            out_ref[0, pl.ds(st, rc), :] = acc
            return carry

        lax.fori_loop(0, n_rc, step, 0)

    def ctx_apply(w, xc, flip):
        xm, x0, xp = shifted(xc, l_ctx, l_ctx)
        ks = (5, 4, 3) if flip else (3, 4, 5)
        return w[ks[0]:ks[0] + 1, :] * xm + w[4:5, :] * x0 + w[ks[2]:ks[2] + 1, :] * xp

    def fwd_body(x_ref, w_ref, o_ref, sm, s0, sp):
        w = w_ref[0]
        fill(x_ref, sm, s0, sp)
        apply(w, sm, s0, sp, o_ref, False)
        o_ref[0, pl.ds(l_lat, l_ctx), :] = ctx_apply(w, x_ref[0, pl.ds(l_lat, l_ctx), :], False)

    n_pass = 0 if passthru is None else passthru.shape[0]

    def bwd_body(x_ref, w_ref, d_ref, *rest):
        p_ref = rest[0] if n_pass else None
        dx_ref, dw_ref, sm, s0, sp = rest[-5:]

        @pl.when(pl.program_id(0) < n_blocks)
        def _():
            bwd_block(x_ref, w_ref, d_ref, dx_ref, dw_ref, sm, s0, sp)

        if n_pass:
            @pl.when(pl.program_id(0) >= n_blocks)
            def _():
                dx_ref[0] = p_ref[0]

    def bwd_block(x_ref, w_ref, d_ref, dx_ref, dw_ref, sm, s0, sp):
        w = w_ref[0]
        fill(x_ref, sm, s0, sp)

        def step(i, acc):
            st = pl.multiple_of(i * rc, rc)
            dc = d_ref[0, pl.ds(st, rc), :]
            rows = []
            for di in range(3):
                for s in (sm, s0, sp):
                    rows.append(jnp.sum(dc * s[pl.ds(st + di * GRID_W, rc), :], axis=0, keepdims=True))
            return acc + jnp.concatenate(rows + [jnp.zeros((7, LANES), F32)], axis=0)

        acc = lax.fori_loop(0, n_rc, step, jnp.zeros((16, LANES), F32))
        dctx = d_ref[0, pl.ds(l_lat, l_ctx), :]
        xm, x0, xp = shifted(x_ref[0, pl.ds(l_lat, l_ctx), :], l_ctx, l_ctx)
        crow = [jnp.sum(dctx * s, axis=0, keepdims=True) for s in (xm, x0, xp)]
        acc = acc + jnp.concatenate([jnp.zeros((3, LANES), F32)] + crow + [jnp.zeros((10, LANES), F32)], axis=0)
        dw_ref[0] = acc
        fill(d_ref, sm, s0, sp)
        apply(w, sm, s0, sp, dx_ref, True)
        dx_ref[0, pl.ds(l_lat, l_ctx), :] = ctx_apply(w, dctx, True)

    blk = lambda off: pl.BlockSpec((1, t, LANES), lambda j: (j + off, 0, 0))
    w_spec = pl.BlockSpec((1, 16, LANES), lambda j: (j, 0, 0))
    scratch = [pltpu.VMEM((l_lat + 2 * GRID_W, LANES), F32)] * 3
    out_t = jax.ShapeDtypeStruct((n_blocks, t, LANES), F32)
    if not bwd:
        return pl.pallas_call(
            fwd_body, name=name, grid=(n_blocks,), in_specs=[blk(first_block), w_spec], out_specs=blk(0),
            out_shape=out_t, scratch_shapes=scratch, compiler_params=_params(("arbitrary",), VMEM_BIG),
        )(src, wts)
    last = n_blocks - 1
    clamped = lambda off: pl.BlockSpec((1, t, LANES), lambda j: (jnp.minimum(j, last) + off, 0, 0))
    w_clamped = pl.BlockSpec((1, 16, LANES), lambda j: (jnp.minimum(j, last), 0, 0))
    in_specs, args, aliases = [clamped(first_block), w_clamped, clamped(0)], [src, wts, dout], {}
    if n_pass:
        in_specs.append(pl.BlockSpec((1, t, LANES), lambda j: (jnp.maximum(j - n_blocks, 0), 0, 0)))
        args.append(passthru)
    first_out = 0
    if into is not None:
        aliases[len(args)] = 0
        in_specs.append(ANY)
        args.append(into[0])
        first_out = into[1]
        out_t = jax.ShapeDtypeStruct(into[0].shape, F32)
    else:
        out_t = jax.ShapeDtypeStruct((n_blocks + n_pass, t, LANES), F32)
    return pl.pallas_call(
        bwd_body, name=name, grid=(n_blocks + n_pass,), in_specs=in_specs,
        out_specs=[pl.BlockSpec((1, t, LANES), lambda j: (j + first_out, 0, 0)), w_clamped],
        out_shape=[out_t, jax.ShapeDtypeStruct((n_blocks, 16, LANES), F32)],
        scratch_shapes=scratch, input_output_aliases=aliases, compiler_params=_params(("arbitrary",), VMEM_BIG),
    )(*args)


def _scan_consts(kind, g, h, rev, d_index):
    n = g * CHUNK
    i = jnp.arange(n, dtype=jnp.int32)
    head, pos = i // CHUNK, i % CHUNK
    p = (CHUNK - 1 - pos) if rev else pos
    same_head = head[:, None] == head[None, :]
    pr, pc = p[:, None], p[None, :]
    f = lambda m: m.astype(F32)
    incl = same_head & (pc <= pr)
    out = {"eye": f(i[:, None] == i[None, :]), "incl": f(incl), "incl_t": f(incl).T}
    if kind == "hg":
        mid, same, sec = [], [], []
        for half in HALVES:
            width = 2 * half
            blk = p // width
            second = (p % width) >= half
            ref_pos = blk * width + half - 1
            mid.append(same_head & (pc <= ref_pos[:, None]))
            same.append(same_head & (blk[:, None] == blk[None, :]))
            sec.append(jnp.broadcast_to(second[:, None], (n, LANES)))
        mid = f(jnp.concatenate(mid, axis=0))
        out.update(mid=mid, mid_t=mid.T, same=f(jnp.stack(same)), sec=f(jnp.stack(sec)))
    else:
        low = []
        for half in reversed(HALVES):
            width = 2 * half
            blk = p // width
            second = (p % width) >= half
            low.append(same_head & (blk[:, None] == blk[None, :]) & second[:, None] & jnp.logical_not(second[None, :]))
        lane = jnp.arange(LANES, dtype=jnp.int32)[None, :]
        hh = jnp.arange(h * CHUNK, dtype=jnp.int32) // CHUNK
        out.update(strict=f(same_head & (pc < pr)), low=f(jnp.stack(low)),
                   oh_a=f(lane == (d_index * h + hh)[:, None]), oh_b=f(lane == (2 * h + d_index * h + hh)[:, None]))
    return out


def _const_mm(c, c_t, diff, prec=HIGH):
    if not diff:
        return lambda x: _mm(c, x, prec)

    @jax.custom_vjp
    def f(x):
        return _mm(c, x, prec)

    f.defvjp(lambda x: (_mm(c, x, prec), None), lambda _, ct: (_mm(c_t, ct, HIGH),))
    return f


def _kept_inverse(x_kept):
    @jax.custom_vjp
    def f(a):
        return x_kept

    f.defvjp(lambda a: (x_kept, None), lambda _, ct: (-_mm_nt(_mm_tn(x_kept, ct, HIGH), x_kept, HIGH),))
    return f


def _stack_helpers(g):
    n = g * CHUNK
    rows = lambda vec: jnp.broadcast_to(vec, (g, CHUNK, LANES)).reshape(n, LANES)
    per_head = lambda t: [t[i * CHUNK:(i + 1) * CHUNK] for i in range(g)]
    head_sum = lambda t: rows(jnp.sum(t.reshape(g, CHUNK, LANES), axis=1, keepdims=True))
    return rows, per_head, head_sum


def _lockstep(gens):
    results = [None] * len(gens)
    live = list(range(len(gens)))
    while live:
        for i in list(live):
            try:
                next(gens[i])
            except StopIteration as done:
                results[i] = done.value
                live.remove(i)
    return results


def _hg_chunk(raw_q, raw_f, raw_i, l0, l1, state, cst, g, diff=False):
    n = g * CHUNK
    rows, per_head, head_sum = _stack_helpers(g)
    lb = rows(jax.nn.sigmoid(l0 - l1))
    q = _silu(raw_q) * (LANES ** -0.5)
    gl = jnp.log(lb + (1.0 - lb) * jax.nn.sigmoid(raw_f))
    k = (1.0 - lb) * jax.nn.sigmoid(-raw_f)
    v = raw_i
    b = _const_mm(cst["incl"], cst["incl_t"], diff)(gl)
    b_tot = head_sum(gl)
    q_dec = q * jnp.exp(b)
    k_dec = k * jnp.exp(b_tot - b)
    mids = _const_mm(cst["mid"], cst["mid_t"], diff, None)(gl)
    yield
    a = cst["eye"] * jnp.sum(q * k, axis=1, keepdims=True)
    for lv in range(len(HALVES)):
        r = mids[lv * n:(lv + 1) * n]
        sec = cst["sec"][lv]
        fst = 1.0 - sec
        qt = q * jnp.exp((b - r) * sec) * sec
        kt = k * jnp.exp((r - b) * fst) * fst
        a = a + _mm_nt(qt, kt) * cst["same"][lv]
        yield
    o_intra = per_head(_mm(a, v))
    decay = per_head(jnp.exp(b_tot))
    qd, kd, vs = per_head(q_dec), per_head(k_dec), per_head(v)
    sts = state()
    while sts is None:
        yield
        sts = state()
    outs = [o_intra[i] + _mm_nt(qd[i], sts[i]) for i in range(g)]
    new = [sts[i] * decay[i][0:1] + _mm_tn(vs[i], kd[i]) for i in range(g)]
    return jnp.concatenate(outs, axis=0), new, sts


def _gdn_chunk(u_q, u_k, u_v, tail, avec, dtvec, state, cst, g, x_kept=None):
    n = g * CHUNK
    diff = x_kept is not None
    rows, per_head, head_sum = _stack_helpers(g)

    def l2n(t):
        return t * lax.rsqrt(jnp.sum(t * t, axis=-1, keepdims=True) + NORM_EPS)

    q = l2n(_silu(u_q)) * (LANES ** -0.5)
    k = l2n(_silu(u_k))
    v = _silu(u_v)
    tail_n = jnp.concatenate([tail] * g, axis=0)
    za = jnp.sum(tail_n * cst["oh_a"], axis=1, keepdims=True) + rows(dtvec)
    beta = jax.nn.sigmoid(jnp.sum(tail_n * cst["oh_b"], axis=1, keepdims=True))
    gl = -jnp.exp(rows(avec)) * (jnp.maximum(za, 0.0) + jnp.log1p(jnp.exp(-jnp.abs(za))))
    b = _const_mm(cst["incl"], cst["incl_t"], diff)(gl)
    kk = _mm_nt(k, k)
    qk = _mm_nt(q, k)
    yield
    b_tot = head_sum(gl)
    bb = jnp.concatenate([b] * (n // LANES), axis=1)
    bdiff = bb - bb.T
    a_mat = beta * kk * jnp.exp(jnp.where(cst["strict"] > 0.5, bdiff, NEG))
    if diff:
        xinv = _kept_inverse(x_kept)(a_mat)
    else:
        xinv = cst["eye"] - a_mat * cst["low"][0]
        for lv in range(1, len(HALVES)):
            left = _mm(xinv, a_mat * cst["low"][lv])
            yield
            xinv = xinv - _mm(left, xinv)
            yield
        resid = cst["eye"] - _mm(cst["eye"] + a_mat, xinv, HIGH)
        yield
        xinv = xinv + _mm(xinv, resid, HIGH)
        yield
    rhs = jnp.concatenate([beta * v, beta * jnp.exp(b) * k], axis=1)
    sol = _mm(xinv, rhs, HIGH)
    yield
    u0, w = per_head(sol[:, :LANES]), per_head(sol[:, LANES:])
    k_dec = per_head(k * jnp.exp(b_tot - b))
    q_dec = per_head(q * jnp.exp(b))
    decay = per_head(jnp.exp(b_tot))
    p = qk * jnp.exp(jnp.where(cst["incl"] > 0.5, bdiff, NEG))
    ss = state()
    while ss is None:
        yield
        ss = state()
    v_new = [u0[i] - _mm(w[i], ss[i]) for i in range(g)]
    o_state = [_mm(q_dec[i], ss[i]) for i in range(g)]
    yield
    o = _mm(p, jnp.concatenate(v_new, axis=0)) + jnp.concatenate(o_state, axis=0)
    new = [decay[i][0:1] * ss[i] + _mm_tn(k_dec[i], v_new[i]) for i in range(g)]
    return o, new, xinv, ss


def _chunk_index(l_lat, l_ctx, rev, sc):
    rows = sc * CHUNK
    nl, nc = l_lat // rows, l_ctx // rows

    def idx(i):
        if rev:
            return jnp.where(i < nc, nl + nc - 1 - i, nl - 1 - (i - nc))
        return jnp.where(i < nc, nl + i, i - nc)

    subs = list(range(sc))
    return nl + nc, idx, (subs[::-1] if rev else subs)


def _const_args(kind, g, h, rev, d_index):
    consts = _scan_consts(kind, g, h, rev, d_index)
    names = sorted(consts)
    arrs = [consts[k] for k in names]
    specs = [pl.BlockSpec(a.shape, functools.partial(lambda nd, i: (0,) * nd, a.ndim)) for a in arrs]
    return names, arrs, specs


def _load_consts(names, refs, g0, g):
    cst = {}
    for k, r in zip(names, refs):
        if k in ("oh_a", "oh_b"):
            cst[k] = r[pl.ds(pl.multiple_of(g0 * CHUNK, CHUNK), g * CHUNK), :]
        else:
            cst[k] = r[...]
    return cst


def _scan_fwd(kind, rev, cbm, segs, vecs, o_prev, l_lat, l_ctx, h, name, tail=None, d_index=0):
    sc = FWD_STEP
    n_steps, cidx, subs = _chunk_index(l_lat, l_ctx, rev, sc)
    rows = sc * CHUNK
    t = l_lat + l_ctx
    n_in = len(segs)
    gdn = kind == "gdn"
    g = min(HEADS_PER_GROUP, h)
    n = g * CHUNK
    c_names, c_arrs, c_specs = _const_args(kind, g, h, rev, d_index)

    def body(*refs):
        in_refs = refs[:n_in]
        pos = n_in
        tail_ref = None
        if gdn:
            tail_ref = refs[pos]
            pos += 1
        v0_ref, v1_ref = refs[pos], refs[pos + 1]
        pos += 2
        prev_ref = None
        if o_prev is not None:
            prev_ref = refs[pos]
            pos += 1
        c_refs = refs[pos:pos + len(c_names)]
        pos += len(c_names)
        o_ref, st_ref = refs[pos], refs[pos + 1]
        pos += 2
        x_ref = None
        if gdn:
            x_ref = refs[pos]
            pos += 1
        s_scr = refs[pos]

        @pl.when(pl.program_id(0) == 0)
        def _():
            s_scr[...] = jnp.zeros_like(s_scr)

        def instance(k, sub, gi, before, after):
            rs = pl.ds(sub * CHUNK, CHUNK)
            g0 = gi * g
            hs = pl.ds(g0, g)
            cst = _load_consts(c_names, c_refs, g0, g)
            state = (lambda: [s_scr[g0 + i] for i in range(g)]) if before is None else (lambda: before[0])
            ins = [r[hs, rs, :].reshape(n, LANES) for r in in_refs]
            if gdn:
                o, new, xinv, old = yield from _gdn_chunk(*ins, tail_ref[0, rs, :], v0_ref[hs], v1_ref[hs], state, cst, g)
                x_ref[k, gi] = xinv
            else:
                o, new, old = yield from _hg_chunk(*ins, v0_ref[hs], v1_ref[hs], state, cst, g)
            after[0] = new
            for i in range(g):
                st_ref[k, g0 + i] = old[i]
            o = o.reshape(g, CHUNK, LANES)
            if prev_ref is not None:
                o = o + prev_ref[hs, rs, :]
            o_ref[hs, rs, :] = o
            if k == sc - 1:
                for i in range(g):
                    s_scr[g0 + i] = new[i]

        gens = []
        handed = [None] * (h // g)
        for k, sub in enumerate(subs):
            for gi in range(h // g):
                after = [None]
                gens.append(instance(k, sub, gi, handed[gi], after))
                handed[gi] = after
        _lockstep(gens)

    seg_spec = lambda sg: pl.BlockSpec((h, rows, LANES), lambda i: (sg, cidx(i), 0))
    vec_spec = pl.BlockSpec((h, 1, LANES), lambda i: (0, 0, 0))
    in_specs = [seg_spec(sg) for sg in segs]
    args = [cbm] * n_in
    if gdn:
        in_specs.append(pl.BlockSpec((1, rows, LANES), lambda i: (0, cidx(i), 0)))
        args.append(tail)
    in_specs += [vec_spec, vec_spec]
    args += list(vecs)
    if o_prev is not None:
        in_specs.append(seg_spec(0))
        args.append(o_prev)
    in_specs += c_specs
    args += c_arrs
    return pl.pallas_call(
        body, name=name, grid=(n_steps,), in_specs=in_specs,
        out_specs=[seg_spec(0), pl.BlockSpec((sc, h, LANES, LANES), lambda i: (i, 0, 0, 0))]
        + ([pl.BlockSpec((sc, h // g, n, n), lambda i: (i, 0, 0, 0))] if gdn else []),
        out_shape=[jax.ShapeDtypeStruct((h, t, LANES), F32),
                   jax.ShapeDtypeStruct((n_steps * sc, h, LANES, LANES), F32)]
        + ([jax.ShapeDtypeStruct((n_steps * sc, h // g, n, n), F32)] if gdn else []),
        scratch_shapes=[pltpu.VMEM((h, LANES, LANES), F32)],
        compiler_params=_params(("arbitrary",), VMEM_BIG),
    )(*args)


def _scan_bwd(kind, rev, cbm, segs, vecs, states, d_o, acc, l_lat, l_ctx, h, name, tail=None, d_index=0, x_kept=None,
              pack=None, pack_into=None, tail_into=None):
    sc = BWD_STEP
    n_steps, cidx, subs = _chunk_index(l_lat, l_ctx, rev, sc)
    rows = sc * CHUNK
    t = l_lat + l_ctx
    n_in = len(segs)
    gdn = kind == "gdn"
    n_grad = n_in + (1 if gdn else 0)
    assert len(acc) == n_grad
    step_of = lambda j: n_steps - 1 - j
    g = min(HEADS_PER_GROUP, h)
    n = g * CHUNK
    c_names, c_arrs, c_specs = _const_args(kind, g, h, rev, d_index)
    items = pack or []
    packed_at = {it[1]: p for p, it in enumerate(items) if it[0] == "grad"}
    copies = [(p, it[1]) for p, it in enumerate(items) if it[0] == "copy"]
    separate = [m for m in range(n_in) if m not in packed_at]
    in_place = pack_into if pack_into is not None else tail_into
    assert pack_into is None or tail_into is None
    tail_blocks = tail_into[2] if tail_into is not None else 1

    def body(*refs):
        in_refs = refs[:n_in]
        pos = n_in
        tail_ref = None
        if gdn:
            tail_ref = refs[pos]
            pos += 1
        v0_ref, v1_ref, st_ref, do_ref = refs[pos:pos + 4]
        pos += 4
        xk_ref = None
        if gdn:
            xk_ref = refs[pos]
            pos += 1
        acc_refs = []
        for a in acc:
            if a is None:
                acc_refs.append(None)
            else:
                acc_refs.append(refs[pos])
                pos += 1
        copy_refs = refs[pos:pos + len(copies)]
        pos += len(copies) + (1 if in_place is not None else 0)
        c_refs = refs[pos:pos + len(c_names)]
        pos += len(c_names)
        sep_refs = refs[pos:pos + len(separate)]
        pos += len(separate)
        pack_ref = None
        if items:
            pack_ref = refs[pos]
            pos += 1
        tail_out = None
        if gdn:
            tail_out = refs[pos]
            pos += 1
        dv0_ref, dv1_ref, ds_scr = refs[pos:pos + 3]

        @pl.when(pl.program_id(0) == 0)
        def _():
            ds_scr[...] = jnp.zeros_like(ds_scr)
            dv0_ref[...] = jnp.zeros_like(dv0_ref)
            dv1_ref[...] = jnp.zeros_like(dv1_ref)

        if gdn:
            tail_out[0] = jnp.zeros((rows, LANES), F32) if acc_refs[n_in] is None else acc_refs[n_in][0]
            for q in range(1, tail_blocks):
                tail_out[q] = jnp.zeros((rows, LANES), F32)

        n_groups = h // g

        def chunk(k, sub):
            rs = pl.ds(sub * CHUNK, CHUNK)
            hss = [pl.ds(gi * g, g) for gi in range(n_groups)]
            csts = [_load_consts(c_names, c_refs, gi * g, g) for gi in range(n_groups)]
            xks = [xk_ref[k, gi] for gi in range(n_groups)] if gdn else None

            def fn(ins, tl, v0s, v1s, sts):
                gens = []
                for gi in range(n_groups):
                    state = functools.partial(lambda s: s, sts[gi])
                    if gdn:
                        gens.append(_gdn_chunk(*ins[gi], tl, v0s[gi], v1s[gi], state, csts[gi], g, xks[gi]))
                    else:
                        gens.append(_hg_chunk(*ins[gi], v0s[gi], v1s[gi], state, csts[gi], g, True))
                return [res[:2] for res in _lockstep(gens)]

            ins = [[r[hs, rs, :].reshape(n, LANES) for r in in_refs] for hs in hss]
            tl = tail_ref[0, rs, :] if gdn else jnp.zeros((1, 1), F32)
            sts = [[st_ref[k, gi * g + i] for i in range(g)] for gi in range(n_groups)]
            _, vj = jax.vjp(fn, ins, tl, [v0_ref[hs] for hs in hss], [v1_ref[hs] for hs in hss], sts)
            d_out = [(do_ref[hs, rs, :].reshape(n, LANES), [ds_scr[gi * g + i] for i in range(g)])
                     for gi, hs in enumerate(hss)]
            d_ins, d_tl, d_v0s, d_v1s, d_sts = vj(d_out)
            for gi, hs in enumerate(hss):
                g0 = gi * g
                for m in range(n_in):
                    gm = d_ins[gi][m].reshape(g, CHUNK, LANES)
                    if acc_refs[m] is not None:
                        gm = gm + acc_refs[m][hs, rs, :]
                    if m in packed_at:
                        pack_ref[pl.ds(packed_at[m] * h + g0, g), rs, :] = gm
                    else:
                        sep_refs[separate.index(m)][hs, rs, :] = gm
                for (p, _), c_ref in zip(copies, copy_refs):
                    pack_ref[pl.ds(p * h + g0, g), rs, :] = c_ref[hs, rs, :]
                dv0, dv1 = d_v0s[gi], d_v1s[gi]
                if gdn:
                    dv0 = jnp.broadcast_to(jnp.sum(dv0, axis=2, keepdims=True), dv0.shape)
                    dv1 = jnp.broadcast_to(jnp.sum(dv1, axis=2, keepdims=True), dv1.shape)
                dv0_ref[hs] += dv0
                dv1_ref[hs] += dv1
                for i in range(g):
                    ds_scr[g0 + i] = d_sts[gi][i]
            if gdn:
                tail_out[0, rs, :] += d_tl

        for k in reversed(range(sc)):
            chunk(k, subs[k])

    seg_spec = lambda sg: pl.BlockSpec((h, rows, LANES), lambda j: (sg, cidx(step_of(j)), 0))
    tail_spec = pl.BlockSpec((1, rows, LANES), lambda j: (0, cidx(step_of(j)), 0))
    vec_spec = pl.BlockSpec((h, 1, LANES), lambda j: (0, 0, 0))
    in_specs = [seg_spec(sg) for sg in segs]
    args = [cbm] * n_in
    if gdn:
        in_specs.append(tail_spec)
        args.append(tail)
    in_specs += [vec_spec, vec_spec, pl.BlockSpec((sc, h, LANES, LANES), lambda j: (step_of(j), 0, 0, 0)),
                 seg_spec(0)]
    args += list(vecs) + [states, d_o]
    if gdn:
        in_specs.append(pl.BlockSpec((sc, h // g, n, n), lambda j: (step_of(j), 0, 0, 0)))
        args.append(x_kept)
    for k, a in enumerate(acc):
        if a is not None:
            in_specs.append(tail_spec if (gdn and k == n_in) else seg_spec(0))
            args.append(a)
    for _, arr in copies:
        in_specs.append(seg_spec(0))
        args.append(arr)
    aliases = {}
    if in_place is not None:
        aliases[len(args)] = len(separate) + (1 if (items and tail_into is not None) else 0)
        in_specs.append(ANY)
        args.append(in_place[0])
    in_specs += c_specs
    args += c_arrs
    out_specs = [seg_spec(0)] * len(separate)
    out_shape = [jax.ShapeDtypeStruct((h, t, LANES), F32)] * len(separate)
    if items:
        pb = len(items) * h
        first = pack_into[1] if pack_into is not None else 0
        assert first % pb == 0
        out_specs.append(pl.BlockSpec((pb, rows, LANES), lambda j: (first // pb, cidx(step_of(j)), 0)))
        out_shape.append(jax.ShapeDtypeStruct(pack_into[0].shape if pack_into is not None else (pb, t, LANES), F32))
    if gdn and tail_into is not None:
        first_t = tail_into[1]
        assert first_t % tail_blocks == 0
        out_specs.append(pl.BlockSpec((tail_blocks, rows, LANES), lambda j: (first_t // tail_blocks, cidx(step_of(j)), 0)))
        out_shape.append(jax.ShapeDtypeStruct(tail_into[0].shape, F32))
    elif gdn:
        out_specs.append(tail_spec)
        out_shape.append(jax.ShapeDtypeStruct((1, t, LANES), F32))
    out_specs += [vec_spec, vec_spec]
    out_shape += [jax.ShapeDtypeStruct((h, 1, LANES), F32)] * 2
    return pl.pallas_call(
        body, name=name, grid=(n_steps,), in_specs=in_specs, out_specs=out_specs, out_shape=out_shape,
        scratch_shapes=[pltpu.VMEM((h, LANES, LANES), F32)], input_output_aliases=aliases,
        compiler_params=_params(("arbitrary",), VMEM_BIG),
    )(*args)


def _gout_tile(oa, ob, za, zb, naw, nbw):
    def hn(o, w):
        return o * lax.rsqrt(jnp.mean(o * o, axis=-1, keepdims=True) + NORM_EPS) * w

    ya = _silu(za) * hn(oa, naw)
    yb = _silu(zb) * hn(ob, nbw)
    nh = oa.shape[0]
    return jnp.concatenate([ya[i] for i in range(nh)] + [yb[i] for i in range(nh)], axis=1)


def _gout_fwd(oa, ob, y_cbm, naw, nbw, l_lat, h):
    tm = _pick(l_lat, (256, 128, 64))
    blk = lambda sg: pl.BlockSpec((h, tm, LANES), lambda i: (sg, i, 0))
    vec = pl.BlockSpec((h, 1, LANES), lambda i: (0, 0, 0))

    def body(oa_ref, ob_ref, za_ref, zb_ref, na_ref, nb_ref, y_ref):
        y_ref[...] = _gout_tile(oa_ref[...], ob_ref[...], za_ref[...], zb_ref[...], na_ref[...], nb_ref[...]).astype(BF16)

    return pl.pallas_call(
        body, name="gout_fwd", grid=(l_lat // tm,),
        in_specs=[blk(0), blk(0), blk(4), blk(8), vec, vec],
        out_specs=pl.BlockSpec((tm, 2 * h * LANES), lambda i: (i, 0)),
        out_shape=jax.ShapeDtypeStruct((l_lat, 2 * h * LANES), BF16),
        compiler_params=_params(("arbitrary",)),
    )(oa, ob, y_cbm, y_cbm, naw, nbw)


def _gout_bwd(oa, ob, y_cbm, naw, nbw, dymix, l_lat, l_ctx, h):
    tm = _pick(l_ctx, (256, 128, 64))
    nl, nc = l_lat // tm, l_ctx // tm
    t = l_lat + l_ctx
    blk_in = lambda sg: pl.BlockSpec((h, tm, LANES), lambda i: (sg, jnp.minimum(i, nl - 1), 0))
    blk_out = pl.BlockSpec((h, tm, LANES), lambda i: (0, i, 0))
    vec = pl.BlockSpec((h, 1, LANES), lambda i: (0, 0, 0))

    def body(oa_ref, ob_ref, za_ref, zb_ref, na_ref, nb_ref, dy_ref, doa_ref, dob_ref, dza_ref, dzb_ref, dna_ref, dnb_ref):
        i = pl.program_id(0)

        @pl.when(i == 0)
        def _():
            dna_ref[...] = jnp.zeros_like(dna_ref)
            dnb_ref[...] = jnp.zeros_like(dnb_ref)

        @pl.when(i < nl)
        def _():
            _, vj = jax.vjp(_gout_tile, oa_ref[...], ob_ref[...], za_ref[...], zb_ref[...], na_ref[...], nb_ref[...])
            doa, dob, dza, dzb, dna, dnb = vj(dy_ref[...])
            doa_ref[...] = doa
            dob_ref[...] = dob
            dza_ref[...] = dza
            dzb_ref[...] = dzb
            dna_ref[...] += dna
            dnb_ref[...] += dnb

        @pl.when(i >= nl)
        def _():
            for r in (doa_ref, dob_ref, dza_ref, dzb_ref):
                r[...] = jnp.zeros_like(r)

    big = jax.ShapeDtypeStruct((h, t, LANES), F32)
    small = jax.ShapeDtypeStruct((h, 1, LANES), F32)
    dza_out = pl.BlockSpec((h, tm, LANES), lambda i: (4, i, 0))
    return pl.pallas_call(
        body, name="gout_bwd", grid=(nl + nc,),
        in_specs=[blk_in(0), blk_in(0), blk_in(4), blk_in(8), vec, vec,
                  pl.BlockSpec((tm, 2 * h * LANES), lambda i: (jnp.minimum(i, nl - 1), 0))],
        out_specs=[blk_out, blk_out, dza_out, blk_out, vec, vec],
        out_shape=[big, big, jax.ShapeDtypeStruct(y_cbm.shape, F32), big, small, small],
        compiler_params=_params(("arbitrary",)),
    )(oa, ob, y_cbm, y_cbm, naw, nbw, dymix)


def _head_tile(xt, mixt, tgt, gt, fw):
    xo = xt + gt * mixt
    y = xo * lax.rsqrt(jnp.mean(xo * xo, axis=-1, keepdims=True) + NORM_EPS) * fw
    err = y - tgt
    return 0.5 * jnp.sum(jnp.mean(err * err, axis=-1, keepdims=True), axis=0, keepdims=True)


def _loss_head(x, mix, target, mods, final_w):
    l_lat, d = x.shape
    tm = _pick(l_lat, (256, 128, 64))
    tok = pl.BlockSpec((tm, d), lambda i: (i, 0))
    row = pl.BlockSpec((1, d), lambda i: (0, 0))

    def body(x_ref, m_ref, t_ref, mod_ref, fw_ref, loss_ref, dmix_ref, dx_ref, dgt_ref, dfw_ref):
        @pl.when(pl.program_id(0) == 0)
        def _():
            loss_ref[...] = jnp.zeros_like(loss_ref)
            dgt_ref[...] = jnp.zeros_like(dgt_ref)
            dfw_ref[...] = jnp.zeros_like(dfw_ref)

        gt = mod_ref[0:1, 2 * d:3 * d]
        fn = lambda xt, mt, g, fw: _head_tile(xt, mt, t_ref[...], g, fw)
        val, vj = jax.vjp(fn, x_ref[...], m_ref[...], gt, fw_ref[...])
        dxt, dmt, dgt, dfw = vj(jnp.ones((1, 1), F32))
        loss_ref[...] += jnp.broadcast_to(val, loss_ref.shape)
        dmix_ref[...] = dmt.astype(BF16)
        dx_ref[...] = dxt
        dgt_ref[...] += dgt
        dfw_ref[...] += dfw

    return pl.pallas_call(
        body, name="loss_head", grid=(l_lat // tm,),
        in_specs=[tok, tok, tok, pl.BlockSpec((8, 3 * d), lambda i: (0, 0)), row],
        out_specs=[pl.BlockSpec((8, LANES), lambda i: (0, 0)), tok, tok, row, row],
        out_shape=[jax.ShapeDtypeStruct((8, LANES), F32), jax.ShapeDtypeStruct((l_lat, d), BF16),
                   jax.ShapeDtypeStruct((l_lat, d), F32), jax.ShapeDtypeStruct((1, d), F32),
                   jax.ShapeDtypeStruct((1, d), F32)],
        compiler_params=_params(("arbitrary",)),
    )(x, mix, target, mods, final_w)


def _adamw(w, g_slots, m, v, name):
    rows, cols = w.shape
    n = g_slots.shape[0]
    tr = _pick(rows, (64, 32, 16, 8))

    def body(w_ref, g_ref, m_ref, v_ref, go_ref, d_ref, nm_ref, nv_ref):
        gg = g_ref[0].astype(F32)
        for s in range(1, n):
            gg = gg + g_ref[s].astype(F32)
        m2 = ADAM_B1 * m_ref[...] + (1.0 - ADAM_B1) * gg
        v2 = ADAM_B2 * v_ref[...] + (1.0 - ADAM_B2) * (gg * gg)
        m_hat = m2 / (1.0 - ADAM_B1 ** ADAM_STEP)
        v_hat = v2 / (1.0 - ADAM_B2 ** ADAM_STEP)
        go_ref[...] = gg
        d_ref[...] = -ADAM_LR * (m_hat / (jnp.sqrt(v_hat) + ADAM_EPS) + ADAM_WD * w_ref[...])
        nm_ref[...] = m2
        nv_ref[...] = v2

    spec = pl.BlockSpec((tr, cols), lambda i: (i, 0))
    shp = jax.ShapeDtypeStruct((rows, cols), F32)
    return pl.pallas_call(
        body, name=name, grid=(rows // tr,),
        in_specs=[spec, pl.BlockSpec((n, tr, cols), lambda i: (0, i, 0)), spec, spec],
        out_specs=[spec] * 4, out_shape=[shp] * 4,
        compiler_params=_params(("arbitrary",)),
    )(w, g_slots, m, v)


def _adamw_nd(w, g_slots, m, v, name):
    shape = w.shape
    two_d = (-1, shape[-1])
    w2 = w.reshape(two_d)
    outs = _adamw(w2, g_slots.reshape((g_slots.shape[0],) + w2.shape), m.reshape(two_d), v.reshape(two_d), name)
    return [o.reshape(shape) for o in outs]


def kernel(x, c, ctx, c_ctx, norm_w, ada_w, ada_b, w_in, conv_w, hg_lb_logits, gdn_a_log, gdn_dt_bias, ha_norm_w, hb_norm_w, w_out, final_norm_w, loss_target, m_c_ctx, m_norm_w, m_ada_w, m_ada_b, m_w_in, m_conv_w, m_hg_lb_logits, m_gdn_a_log, m_gdn_dt_bias, m_ha_norm_w, m_hb_norm_w, m_w_out, m_final_norm_w, v_c_ctx, v_norm_w, v_ada_w, v_ada_b, v_w_in, v_conv_w, v_hg_lb_logits, v_gdn_a_log, v_gdn_dt_bias, v_ha_norm_w, v_hb_norm_w, v_w_out, v_final_norm_w):
    l_lat, d = x.shape[1], x.shape[2]
    l_ctx = ctx.shape[1]
    t = l_lat + l_ctx
    h = d // LANES
    n_in = 9 * d + 4 * h
    nb = -(-(9 * h + 1) // COLS_PER_TILE) * COLS_PER_TILE
    n_pad = nb * LANES
    win_c, ada_c, conv_c, lb_c, wout_r = w_in.shape[2], ada_w.shape[2], conv_w.shape[3], hg_lb_logits.shape[2], w_out.shape[1]
    me = _me()

    gw_in, gw_out, gw_ada, gw_conv, gw_lb = _gather_two_level(
        [w_in[0].astype(BF16), w_out[0].astype(BF16), ada_w[0], conv_w[0], hg_lb_logits], "gather_weights")
    join = lambda g, axis: jnp.concatenate([g[k] for k in range(N_DEV)], axis=axis)
    w_in_full = join(gw_in, 1)
    w_out_full = gw_out.reshape(N_DEV * wout_r, d)
    ada_full = join(gw_ada, 1)
    conv_full = join(gw_conv, 2)
    lb_full = join(gw_lb, 2)
    w_pad = jnp.pad(w_in_full, ((0, 0), (0, n_pad - n_in)))
    w_pad_t = w_pad.T
    w_out_t = w_out_full.T

    lbl = lb_full.reshape(2, 2, h, 1, LANES)
    conv_rows = jnp.pad(conv_full.reshape(9, 3 * h, LANES).transpose(1, 0, 2), ((0, 0), (0, 7), (0, 0)))
    lane_bc = lambda a: jnp.broadcast_to(a.reshape(2, h, 1, 1), (2, h, 1, LANES))
    avec, dtvec = lane_bc(gdn_a_log[0]), lane_bc(gdn_dt_bias[0])
    naw, nbw = ha_norm_w[0].reshape(h, 1, LANES), hb_norm_w[0].reshape(h, 1, LANES)

    cvec = jnp.concatenate([c, c_ctx[None, :], jnp.zeros((6, d), F32)], axis=0)
    mods = _ada_fwd(cvec, ada_full, ada_b)
    x2, ctx2, tgt2 = x[0], ctx[0], loss_target[0]
    h_all = _prenorm_fwd(x2, ctx2, norm_w, mods)
    y = _matmul(h_all, w_pad, "proj", out_cbm=True, tn=COLS_PER_TILE * LANES)
    tail = lax.slice_in_dim(y, 9 * h, 9 * h + 1, axis=0)
    u = _conv_call(y, conv_rows, l_lat, 5 * h, 3 * h, "conv_fwd")

    oa, st_af = _scan_fwd("hg", False, y, (0, 1, 3), (lbl[0, 0], lbl[1, 0]), None, l_lat, l_ctx, h, "hg_fwd_f")
    oa, st_ab = _scan_fwd("hg", True, y, (0, 2, 3), (lbl[0, 1], lbl[1, 1]), oa, l_lat, l_ctx, h, "hg_fwd_b")
    ob, st_bf, xk_f = _scan_fwd("gdn", False, u, (0, 1, 2), (avec[0], dtvec[0]), None, l_lat, l_ctx, h, "gdn_fwd_f", tail, 0)
    ob, st_bb, xk_b = _scan_fwd("gdn", True, u, (0, 1, 2), (avec[1], dtvec[1]), ob, l_lat, l_ctx, h, "gdn_fwd_b", tail, 1)

    ymix = _gout_fwd(oa, ob, y, naw, nbw, l_lat, h)
    mix = _matmul(ymix, w_out_full, "out_proj")
    loss_blk, dmix, dx_res, dgt, dfw = _loss_head(x2, mix, tgt2, mods, final_norm_w.reshape(1, d))

    dymix = _matmul(dmix, w_out_t, "d_ymix")
    dw_out = _matmul(ymix, dmix, "d_w_out", a_t=True)
    doa, dob, dy, dzb, dnaw, dnbw = _gout_bwd(oa, ob, y, naw, nbw, dymix, l_lat, l_ctx, h)

    gq, gff, gi, dl0f, dl1f = _scan_bwd("hg", False, y, (0, 1, 3), (lbl[0, 0], lbl[1, 0]), st_af, doa,
                                        [None, None, None], l_lat, l_ctx, h, "hg_bwd_f")
    dy, dl0b, dl1b = _scan_bwd("hg", True, y, (0, 2, 3), (lbl[0, 1], lbl[1, 1]), st_ab, doa, [gq, None, gi],
                               l_lat, l_ctx, h, "hg_bwd_b",
                               pack=[("grad", 0), ("copy", gff), ("grad", 1), ("grad", 2)], pack_into=(dy, 0))
    guq, guk, guv, gtail, da_f, ddt_f = _scan_bwd("gdn", False, u, (0, 1, 2), (avec[0], dtvec[0]), st_bf, dob,
                                                  [None] * 4, l_lat, l_ctx, h, "gdn_bwd_f", tail, 0, xk_f)
    du, dy, da_b, ddt_b = _scan_bwd("gdn", True, u, (0, 1, 2), (avec[1], dtvec[1]), st_bb, dob,
                                    [guq, guk, guv, gtail], l_lat, l_ctx, h, "gdn_bwd_b", tail, 1, xk_b,
                                    pack=[("grad", 0), ("grad", 1), ("grad", 2)], tail_into=(dy, 9 * h, nb - 9 * h))
    dy, dconv_rows = _conv_call(y, conv_rows, l_lat, 5 * h, 3 * h, "conv_bwd", dout=du, passthru=dzb, into=(dy, 5 * h))
    dh = _matmul(dy, w_pad_t, "d_h", a_cbm=True)
    dw_in = _matmul(h_all, dy, "d_w_in", a_t=True, b_cbm=True)
    grad_x, dnorm_w, dmods_pre = _prenorm_bwd(x2, ctx2, norm_w, mods, dh, dx_res)

    dmods = jnp.concatenate([dmods_pre[:, :2 * d],
                             jnp.concatenate([dgt, jnp.zeros((7, d), F32)], axis=0)], axis=1)
    dcond = _ada_bwd(cvec, ada_full, dmods)

    dw_in_s = jnp.stack([dw_in[:, k * win_c:(k + 1) * win_c] for k in range(N_DEV)])
    dw_out_s = dw_out.reshape(N_DEV, wout_r, d)
    dconv_s = dconv_rows[:, :9, :].transpose(1, 0, 2).reshape(3, 3, N_DEV, conv_c).transpose(2, 0, 1, 3)
    dlb_s = jnp.stack([jnp.stack([dl0f, dl0b]), jnp.stack([dl1f, dl1b])]).reshape(2, 2, N_DEV, lb_c).transpose(2, 0, 1, 3)
    slabs = [dw_in_s, dw_out_s, dconv_s, dlb_s]
    theirs = _swap_with_sibling([a.astype(BF16) for a in slabs], "swap_grads")
    pairs = [_pair_sum(a, b, "pair_sum_%d" % k) for k, (a, b) in enumerate(zip(slabs, theirs))]
    s_w_in, s_w_out, s_conv, s_lb = _scatter_over_chips(pairs, "scatter_grads")

    da = jnp.stack([da_f[:, 0, 0], da_b[:, 0, 0]])
    ddt = jnp.stack([ddt_f[:, 0, 0], ddt_b[:, 0, 0]])
    small_parts = [dcond[1], dnorm_w, dmods[0] + dmods[1], dnaw, dnbw, dfw, da, ddt, c[0], dmods[0], dmods[1]]
    small_blob, soff = _to_blob(small_parts)
    gathered, = _exchange([small_blob], "gather_small", True)
    summed = _sum_slots(gathered, "sum_small").reshape(-1)
    gflat = gathered.reshape(N_DEV, -1)
    take = lambda k, n: summed[soff[k]:soff[k] + n]
    one = lambda k, n, shape: take(k, n).reshape((1,) + shape)
    s_c_ctx = one(0, d, (d,))
    s_norm_w = one(1, d, (1, d))
    s_ada_b = one(2, 3 * d, (1, 3 * d))
    s_ha = one(3, d, (1, h, LANES))
    s_hb = one(4, d, (1, h, LANES))
    s_final = one(5, d, (d,))
    s_a_log = one(6, 2 * h, (1, 2, h))
    s_dt = one(7, 2 * h, (1, 2, h))
    conds = jnp.concatenate([gflat[:, soff[8]:soff[8] + d], c_ctx[None, :], jnp.zeros((7, d), F32)], axis=0)
    col0 = me * ada_c
    dm_lat = lax.dynamic_slice_in_dim(gflat[:, soff[9]:soff[9] + 3 * d], col0, ada_c, axis=1)
    dm_ctx = lax.dynamic_slice_in_dim(gflat[:, soff[10]:soff[10] + 3 * d], col0, ada_c, axis=1)
    s_ada_w = _ada_wgrad(conds, dm_lat, dm_ctx).reshape(1, 1, d, ada_c)

    loss = lax.psum(loss_blk[0, 0], ("x", "y", "c"))

    weights = [("c_ctx", c_ctx, s_c_ctx, m_c_ctx, v_c_ctx), ("norm_w", norm_w, s_norm_w, m_norm_w, v_norm_w),
               ("ada_w", ada_w, s_ada_w, m_ada_w, v_ada_w), ("ada_b", ada_b, s_ada_b, m_ada_b, v_ada_b),
               ("w_in", w_in, s_w_in, m_w_in, v_w_in), ("conv_w", conv_w, s_conv, m_conv_w, v_conv_w),
               ("hg_lb_logits", hg_lb_logits, s_lb, m_hg_lb_logits, v_hg_lb_logits),
               ("gdn_a_log", gdn_a_log, s_a_log, m_gdn_a_log, v_gdn_a_log),
               ("gdn_dt_bias", gdn_dt_bias, s_dt, m_gdn_dt_bias, v_gdn_dt_bias),
               ("ha_norm_w", ha_norm_w, s_ha, m_ha_norm_w, v_ha_norm_w), ("hb_norm_w", hb_norm_w, s_hb, m_hb_norm_w, v_hb_norm_w),
               ("w_out", w_out, s_w_out, m_w_out, v_w_out), ("final_norm_w", final_norm_w, s_final, m_final_norm_w, v_final_norm_w)]
    grads, deltas, new_ms, new_vs = [], [], [], []
    for nm, w, gs, m, v in weights:
        g, dl, m2, v2 = _adamw_nd(w, gs, m, v, "adamw_" + nm)
        grads.append(g)
        deltas.append(dl)
        new_ms.append(m2)
        new_vs.append(v2)
    return (loss, grad_x[None], *grads, *deltas, *new_ms, *new_vs)
```

```python
import functools

import jax
import jax.numpy as jnp
from jax import lax
from jax.experimental import pallas as pl
from jax.experimental.pallas import tpu as pltpu

F32 = jnp.float32
BF16 = jnp.bfloat16
HI = lax.Precision.HIGHEST
HIGH = lax.Precision.HIGH
MESH = pl.DeviceIdType.MESH
ANY = pl.BlockSpec(memory_space=pl.ANY)

N_DEV = 8
LANES = 128
CHUNK = 64
GRID_W = 64
NORM_EPS = 1e-6
NEG = -1e30
COLS_PER_TILE = 8
VMEM_BIG = 56 * 1024 * 1024
HEADS_PER_GROUP = 4
FWD_STEP = 4
BWD_STEP = 2
HALVES = (32, 16, 8, 4, 2, 1)

ADAM_LR, ADAM_B1, ADAM_B2, ADAM_EPS, ADAM_WD, ADAM_STEP = 0.001, 0.9, 0.999, 1e-08, 0.01, 10


def _params(sem=None, vmem=None):
    kw = {}
    if sem is not None:
        kw["dimension_semantics"] = sem
    if vmem is not None:
        kw["vmem_limit_bytes"] = vmem
    return pltpu.CompilerParams(**kw)


def _pick(n, cands):
    for c in cands:
        if n % c == 0:
            return c
    return n


def _silu(x):
    return x * jax.nn.sigmoid(x)


def _mm(a, b, prec=None):
    return lax.dot_general(a, b, (((1,), (0,)), ((), ())), precision=prec, preferred_element_type=F32)


def _mm_nt(a, b, prec=None):
    return lax.dot_general(a, b, (((1,), (1,)), ((), ())), precision=prec, preferred_element_type=F32)


def _mm_tn(a, b, prec=None):
    return lax.dot_general(a, b, (((0,), (0,)), ((), ())), precision=prec, preferred_element_type=F32)


def _me():
    return 4 * lax.axis_index("x") + 2 * lax.axis_index("y") + lax.axis_index("c")


def _peer(r):
    x, y, c = lax.axis_index("x"), lax.axis_index("y"), lax.axis_index("c")
    px = 1 - x if r & 4 else x
    py = 1 - y if r & 2 else y
    pc = 1 - c if r & 1 else c
    return (px, py, pc), 4 * px + 2 * py + pc


def _exchange(arrs, name, gather):
    n_a = len(arrs)

    def body(*refs):
        srcs, outs = refs[:n_a], refs[n_a:2 * n_a]
        send_sems, recv_sems, local_sems = refs[2 * n_a:]
        me = _me()
        copies = []
        for k in range(n_a):
            local = pltpu.make_async_copy(srcs[k] if gather else srcs[k].at[me], outs[k].at[me], local_sems.at[k])
            local.start()
            copies.append(local)
        for r in range(N_DEV - 1, 0, -1):
            dev, pid = _peer(r)
            for k in range(n_a):
                cp = pltpu.make_async_remote_copy(
                    src_ref=srcs[k] if gather else srcs[k].at[pid], dst_ref=outs[k].at[me],
                    send_sem=send_sems.at[(r - 1) * n_a + k], recv_sem=recv_sems.at[(r - 1) * n_a + k],
                    device_id=dev, device_id_type=MESH)
                cp.start()
                copies.append(cp)
        for cp in copies:
            cp.wait()

    n_sem = (N_DEV - 1) * n_a
    return pl.pallas_call(
        body, name=name,
        out_shape=[jax.ShapeDtypeStruct(((N_DEV,) + a.shape) if gather else a.shape, a.dtype) for a in arrs],
        in_specs=[ANY] * n_a, out_specs=[ANY] * n_a,
        scratch_shapes=[pltpu.SemaphoreType.DMA((n_sem,)), pltpu.SemaphoreType.DMA((n_sem,)),
                        pltpu.SemaphoreType.DMA((n_a,))],
    )(*arrs)


def _gather_two_level(arrs, name):
    n_a = len(arrs)

    def body(*refs):
        srcs, outs = refs[:n_a], refs[n_a:2 * n_a]
        send_sems, recv_sems, local_sems = refs[2 * n_a:]
        x, y, c = lax.axis_index("x"), lax.axis_index("y"), lax.axis_index("c")
        me, sibling = (x, y, c), (x, y, 1 - c)
        chips = [(1 - x, y), (x, 1 - y), (1 - x, 1 - y)]

        def copy(k, j, block, to, src=None):
            slot = outs[k].at[4 * block[0] + 2 * block[1] + block[2]]
            return pltpu.make_async_remote_copy(
                src_ref=slot if src is None else src, dst_ref=slot,
                send_sem=send_sems.at[j * n_a + k], recv_sem=recv_sems.at[j * n_a + k],
                device_id=to, device_id_type=MESH)

        mine = [pltpu.make_async_copy(srcs[k], outs[k].at[4 * x + 2 * y + c], local_sems.at[k]) for k in range(n_a)]
        for cp in mine:
            cp.start()
        first = [copy(k, 1 + j, me, (*chip, c), srcs[k]) for j, chip in enumerate(chips) for k in range(n_a)]
        first += [copy(k, 0, me, sibling, srcs[k]) for k in range(n_a)]
        for cp in first:
            cp.start()
        passed = []
        for j, chip in enumerate(chips):
            for k in range(n_a):
                copy(k, 1 + j, (*chip, c), me).wait_recv()
                cp = copy(k, 4 + j, (*chip, c), sibling)
                cp.start()
                passed.append(cp)
        for k in range(n_a):
            copy(k, 0, sibling, me).wait_recv()
        for j, chip in enumerate(chips):
            for k in range(n_a):
                copy(k, 4 + j, (*chip, 1 - c), me).wait_recv()
        for cp in first + passed:
            cp.wait_send()
        for cp in mine:
            cp.wait()

    n_sem = (N_DEV - 1) * n_a
    return pl.pallas_call(
        body, name=name,
        out_shape=[jax.ShapeDtypeStruct((N_DEV,) + a.shape, a.dtype) for a in arrs],
        in_specs=[ANY] * n_a, out_specs=[ANY] * n_a,
        scratch_shapes=[pltpu.SemaphoreType.DMA((n_sem,)), pltpu.SemaphoreType.DMA((n_sem,)),
                        pltpu.SemaphoreType.DMA((n_a,))],
    )(*arrs)


def _swap_with_sibling(arrs, name):
    n_a = len(arrs)

    def body(*refs):
        srcs, outs = refs[:n_a], refs[n_a:2 * n_a]
        send_sems, recv_sems = refs[2 * n_a:]
        sibling = (lax.axis_index("x"), lax.axis_index("y"), 1 - lax.axis_index("c"))
        copies = [pltpu.make_async_remote_copy(src_ref=srcs[k], dst_ref=outs[k], send_sem=send_sems.at[k],
                                               recv_sem=recv_sems.at[k], device_id=sibling, device_id_type=MESH)
                  for k in range(n_a)]
        for cp in copies:
            cp.start()
        for cp in copies:
            cp.wait()

    return pl.pallas_call(
        body, name=name, out_shape=[jax.ShapeDtypeStruct(a.shape, a.dtype) for a in arrs],
        in_specs=[ANY] * n_a, out_specs=[ANY] * n_a,
        scratch_shapes=[pltpu.SemaphoreType.DMA((n_a,)), pltpu.SemaphoreType.DMA((n_a,))],
    )(*arrs)


def _scatter_over_chips(arrs, name):
    n_a = len(arrs)
    n_chip = N_DEV // 2

    def body(*refs):
        srcs, outs = refs[:n_a], refs[n_a:2 * n_a]
        send_sems, recv_sems, local_sems = refs[2 * n_a:]
        x, y, c = lax.axis_index("x"), lax.axis_index("y"), lax.axis_index("c")
        my_chip = 2 * x + y
        copies = []
        for k in range(n_a):
            local = pltpu.make_async_copy(srcs[k].at[4 * x + 2 * y + c], outs[k].at[my_chip], local_sems.at[k])
            local.start()
            copies.append(local)
        for r in range(n_chip - 1, 0, -1):
            px = 1 - x if r & 2 else x
            py = 1 - y if r & 1 else y
            for k in range(n_a):
                cp = pltpu.make_async_remote_copy(
                    src_ref=srcs[k].at[4 * px + 2 * py + c], dst_ref=outs[k].at[my_chip],
                    send_sem=send_sems.at[(r - 1) * n_a + k], recv_sem=recv_sems.at[(r - 1) * n_a + k],
                    device_id=(px, py, c), device_id_type=MESH)
                cp.start()
                copies.append(cp)
        for cp in copies:
            cp.wait()

    n_sem = (n_chip - 1) * n_a
    return pl.pallas_call(
        body, name=name,
        out_shape=[jax.ShapeDtypeStruct((n_chip,) + a.shape[1:], a.dtype) for a in arrs],
        in_specs=[ANY] * n_a, out_specs=[ANY] * n_a,
        scratch_shapes=[pltpu.SemaphoreType.DMA((n_sem,)), pltpu.SemaphoreType.DMA((n_sem,)),
                        pltpu.SemaphoreType.DMA((n_a,))],
    )(*arrs)


def _pair_sum(a, b, name):
    shape = a.shape
    a2, b2 = a.reshape(-1, shape[-1]), b.reshape(-1, shape[-1])
    rows, cols = a2.shape
    tr = _pick(rows, (256, 128, 64, 32, 16))

    def body(a_ref, b_ref, o_ref):
        o_ref[...] = (a_ref[...] + b_ref[...].astype(F32)).astype(BF16)

    spec = pl.BlockSpec((tr, cols), lambda i: (i, 0))
    return pl.pallas_call(
        body, name=name, grid=(rows // tr,), in_specs=[spec, spec], out_specs=spec,
        out_shape=jax.ShapeDtypeStruct((rows, cols), BF16), compiler_params=_params(("arbitrary",)),
    )(a2, b2).reshape(shape)


def _to_blob(parts):
    flat = [p.reshape(-1).astype(F32) for p in parts]
    offs, n = [], 0
    for f in flat:
        offs.append(n)
        n += f.shape[0]
    unit = 8 * LANES
    total = -(-n // unit) * unit
    if total > n:
        flat.append(jnp.zeros((total - n,), F32))
    return jnp.concatenate(flat).reshape(total // LANES, LANES), offs


def _sum_slots(buf, name):
    n, rows, cols = buf.shape
    tr = _pick(rows, (512, 256, 128, 64, 32, 16, 8))

    def body(b_ref, o_ref):
        acc = b_ref[0]
        for s in range(1, n):
            acc = acc + b_ref[s]
        o_ref[...] = acc

    return pl.pallas_call(
        body, name=name, grid=(rows // tr,),
        in_specs=[pl.BlockSpec((n, tr, cols), lambda i: (0, i, 0))],
        out_specs=pl.BlockSpec((tr, cols), lambda i: (i, 0)),
        out_shape=jax.ShapeDtypeStruct((rows, cols), F32),
        compiler_params=_params(("arbitrary",)),
    )(buf)


def _matmul(a, b, name, a_cbm=False, b_cbm=False, out_cbm=False, a_t=False, out_dtype=F32, tm=None, tn=None, tk=None):
    if a_cbm:
        m, k = a.shape[1], a.shape[0] * LANES
    elif a_t:
        k, m = a.shape
    else:
        m, k = a.shape
    n = b.shape[0] * LANES if b_cbm else b.shape[1]
    tm = tm or _pick(m, (1024, 768, 512, 384, 256, 128, 64))
    tn = tn or _pick(n, (1024, 512, 256, 128))
    tk = tk or _pick(k, (2048, 1024, 768, 512, 256, 128))
    nk = k // tk

    def load(ref, cbm):
        if not cbm:
            return ref[...].astype(BF16)
        return jnp.concatenate([ref[j].astype(BF16) for j in range(ref.shape[0])], axis=1)

    def body(a_ref, b_ref, o_ref, acc_ref):
        kk = pl.program_id(2)
        part = (_mm_tn if a_t else _mm)(load(a_ref, a_cbm), load(b_ref, b_cbm))

        @pl.when(kk == 0)
        def _():
            acc_ref[...] = part

        @pl.when(kk > 0)
        def _():
            acc_ref[...] += part

        @pl.when(kk == nk - 1)
        def _():
            r = acc_ref[...]
            if out_cbm:
                for j in range(tn // LANES):
                    o_ref[j] = r[:, j * LANES:(j + 1) * LANES].astype(out_dtype)
            else:
                o_ref[...] = r.astype(out_dtype)

    if a_cbm:
        a_spec = pl.BlockSpec((tk // LANES, tm, LANES), lambda j, i, kk: (kk, i, 0))
    elif a_t:
        a_spec = pl.BlockSpec((tk, tm), lambda j, i, kk: (kk, i))
    else:
        a_spec = pl.BlockSpec((tm, tk), lambda j, i, kk: (i, kk))
    if b_cbm:
        b_spec = pl.BlockSpec((tn // LANES, tk, LANES), lambda j, i, kk: (j, kk, 0))
    else:
        b_spec = pl.BlockSpec((tk, tn), lambda j, i, kk: (kk, j))
    if out_cbm:
        o_spec = pl.BlockSpec((tn // LANES, tm, LANES), lambda j, i, kk: (j, i, 0))
        o_shape = jax.ShapeDtypeStruct((n // LANES, m, LANES), out_dtype)
    else:
        o_spec = pl.BlockSpec((tm, tn), lambda j, i, kk: (i, j))
        o_shape = jax.ShapeDtypeStruct((m, n), out_dtype)
    return pl.pallas_call(
        body, name=name, grid=(n // tn, m // tm, nk),
        in_specs=[a_spec, b_spec], out_specs=o_spec, out_shape=o_shape,
        scratch_shapes=[pltpu.VMEM((tm, tn), F32)],
        compiler_params=_params(("arbitrary", "arbitrary", "arbitrary"), VMEM_BIG),
    )(a, b)


def _ada_fwd(cvec, ada_w, ada_b):
    def body(c_ref, w_ref, b_ref, o_ref):
        o_ref[...] = _mm(_silu(c_ref[...]), w_ref[...], HI) + b_ref[...]

    return pl.pallas_call(
        body, name="ada_fwd", out_shape=jax.ShapeDtypeStruct((8, ada_w.shape[1]), F32),
        compiler_params=_params(None, VMEM_BIG),
    )(cvec, ada_w, ada_b)


def _ada_bwd(cvec, ada_w, dmods):
    def body(c_ref, w_ref, d_ref, o_ref):
        (_, vj) = jax.vjp(_silu, c_ref[...])
        o_ref[...] = vj(_mm_nt(d_ref[...], w_ref[...], HI))[0]

    return pl.pallas_call(
        body, name="ada_bwd", out_shape=jax.ShapeDtypeStruct(cvec.shape, F32),
        compiler_params=_params(None, VMEM_BIG),
    )(cvec, ada_w, dmods)


def _ada_wgrad(conds, dm_lat, dm_ctx):
    d = conds.shape[1]
    cols = dm_lat.shape[1]

    def body(c_ref, dl_ref, dc_ref, o_ref):
        dctx = dc_ref[0:1]
        for s in range(1, N_DEV):
            dctx = dctx + dc_ref[s:s + 1]
        rhs = jnp.concatenate([dl_ref[...], dctx, jnp.zeros((7, cols), F32)], axis=0)
        o_ref[...] = _mm_tn(_silu(c_ref[...]), rhs, HI)

    return pl.pallas_call(
        body, name="ada_wgrad", out_shape=jax.ShapeDtypeStruct((d, cols), F32),
    )(conds, dm_lat, dm_ctx)


def _prenorm_tile(xt, nw, sc, sh):
    r = lax.rsqrt(jnp.mean(xt * xt, axis=-1, keepdims=True) + NORM_EPS)
    return (xt * r * nw) * (1.0 + sc) + sh


def _tok_specs(l_lat, l_ctx, d, tm):
    nl, nc = l_lat // tm, l_ctx // tm
    lat = pl.BlockSpec((tm, d), lambda i: (jnp.minimum(i, nl - 1), 0))
    ctx = pl.BlockSpec((tm, d), lambda i: (jnp.maximum(i - nl, 0), 0))
    return nl, nc, lat, ctx


def _prenorm_fwd(x, ctx, norm_w, mods):
    l_lat, d = x.shape
    l_ctx = ctx.shape[0]
    tm = _pick(l_ctx, (256, 128, 64))
    nl, nc, lat_spec, ctx_spec = _tok_specs(l_lat, l_ctx, d, tm)

    def body(x_ref, c_ref, nw_ref, m_ref, h_ref):
        is_lat = pl.program_id(0) < nl
        xt = jnp.where(is_lat, x_ref[...], c_ref[...])
        row = jnp.where(is_lat, m_ref[0:1, :], m_ref[1:2, :])
        h_ref[...] = _prenorm_tile(xt, nw_ref[...], row[:, d:2 * d], row[:, 0:d]).astype(BF16)

    return pl.pallas_call(
        body, name="prenorm_fwd", grid=(nl + nc,),
        in_specs=[lat_spec, ctx_spec, pl.BlockSpec((1, d), lambda i: (0, 0)), pl.BlockSpec((8, 3 * d), lambda i: (0, 0))],
        out_specs=pl.BlockSpec((tm, d), lambda i: (i, 0)),
        out_shape=jax.ShapeDtypeStruct((l_lat + l_ctx, d), BF16),
        compiler_params=_params(("arbitrary",)),
    )(x, ctx, norm_w, mods)


def _prenorm_bwd(x, ctx, norm_w, mods, dh, dx_res):
    l_lat, d = x.shape
    l_ctx = ctx.shape[0]
    tm = _pick(l_ctx, (256, 128, 64))
    nl, nc, lat_spec, ctx_spec = _tok_specs(l_lat, l_ctx, d, tm)

    def body(x_ref, c_ref, nw_ref, m_ref, dh_ref, dr_ref, gx_ref, dnw_ref, dm_ref):
        i = pl.program_id(0)
        is_lat = i < nl

        @pl.when(i == 0)
        def _():
            dnw_ref[...] = jnp.zeros_like(dnw_ref)
            dm_ref[...] = jnp.zeros_like(dm_ref)

        xt = jnp.where(is_lat, x_ref[...], c_ref[...])
        row = jnp.where(is_lat, m_ref[0:1, :], m_ref[1:2, :])
        _, vj = jax.vjp(_prenorm_tile, xt, nw_ref[...], row[:, d:2 * d], row[:, 0:d])
        dxt, dnw, dsc, dsh = vj(dh_ref[...])
        dnw_ref[...] += dnw
        upd = jnp.concatenate([dsh, dsc, jnp.zeros_like(dsh)], axis=1)

        @pl.when(is_lat)
        def _():
            gx_ref[...] = dr_ref[...] + dxt
            dm_ref[0:1, :] += upd

        @pl.when(jnp.logical_not(is_lat))
        def _():
            dm_ref[1:2, :] += upd

    return pl.pallas_call(
        body, name="prenorm_bwd", grid=(nl + nc,),
        in_specs=[lat_spec, ctx_spec, pl.BlockSpec((1, d), lambda i: (0, 0)), pl.BlockSpec((8, 3 * d), lambda i: (0, 0)),
                  pl.BlockSpec((tm, d), lambda i: (i, 0)), lat_spec],
        out_specs=[lat_spec, pl.BlockSpec((1, d), lambda i: (0, 0)), pl.BlockSpec((8, 3 * d), lambda i: (0, 0))],
        out_shape=[jax.ShapeDtypeStruct((l_lat, d), F32), jax.ShapeDtypeStruct((1, d), F32),
                   jax.ShapeDtypeStruct((8, 3 * d), F32)],
        compiler_params=_params(("arbitrary",)),
    )(x, ctx, norm_w, mods, dh, dx_res)


def _conv_call(src, wts, l_lat, first_block, n_blocks, name, dout=None, passthru=None, into=None):
    t = src.shape[1]
    l_ctx = t - l_lat
    rc = _pick(l_lat, (512, 256, 128, 64))
    n_rc = l_lat // rc
    bwd = dout is not None

    def shifted(xc, n, period):
        pos = lax.broadcasted_iota(jnp.int32, (n, LANES), 0) % period
        xm = jnp.where(pos == 0, 0.0, pltpu.roll(xc, 1, 0))
        xp = jnp.where(pos == period - 1, 0.0, pltpu.roll(xc, n - 1, 0))
        return xm, xc, xp

    def fill(ref, sm, s0, sp):
        zero = jnp.zeros((GRID_W, LANES), F32)
        for s in (sm, s0, sp):
            s[pl.ds(0, GRID_W), :] = zero
            s[pl.ds(l_lat + GRID_W, GRID_W), :] = zero

        def step(i, carry):
            st = pl.multiple_of(i * rc, rc)
            xm, x0, xp = shifted(ref[0, pl.ds(st, rc), :], rc, GRID_W)
            sm[pl.ds(st + GRID_W, rc), :] = xm
            s0[pl.ds(st + GRID_W, rc), :] = x0
            sp[pl.ds(st + GRID_W, rc), :] = xp
            return carry

        lax.fori_loop(0, n_rc, step, 0)

    def apply(w, sm, s0, sp, out_ref, flip):
        def step(i, carry):
            st = pl.multiple_of(i * rc, rc)
            acc = jnp.zeros((rc, LANES), F32)
            for di in range(3):
                for dj, s in enumerate((sm, s0, sp)):
                    kidx = (2 - di) * 3 + (2 - dj) if flip else di * 3 + dj
                    acc = acc + w[kidx:kidx + 1, :] * s[pl.ds(st + di * GRID_W, rc), :]
            out_ref[0, pl.ds(st, rc), :] = acc
            return carry

        lax.fori_loop(0, n_rc, step, 0)

    def ctx_apply(w, xc, flip):
        xm, x0, xp = shifted(xc, l_ctx, l_ctx)
        ks = (5, 4, 3) if flip else (3, 4, 5)
        return w[ks[0]:ks[0] + 1, :] * xm + w[4:5, :] * x0 + w[ks[2]:ks[2] + 1, :] * xp

    def fwd_body(x_ref, w_ref, o_ref, sm, s0, sp):
        w = w_ref[0]
        fill(x_ref, sm, s0, sp)
        apply(w, sm, s0, sp, o_ref, False)
        o_ref[0, pl.ds(l_lat, l_ctx), :] = ctx_apply(w, x_ref[0, pl.ds(l_lat, l_ctx), :], False)

    n_pass = 0 if passthru is None else passthru.shape[0]

    def bwd_body(x_ref, w_ref, d_ref, *rest):
        p_ref = rest[0] if n_pass else None
        dx_ref, dw_ref, sm, s0, sp = rest[-5:]

        @pl.when(pl.program_id(0) < n_blocks)
        def _():
            bwd_block(x_ref, w_ref, d_ref, dx_ref, dw_ref, sm, s0, sp)

        if n_pass:
            @pl.when(pl.program_id(0) >= n_blocks)
            def _():
                dx_ref[0] = p_ref[0]

    def bwd_block(x_ref, w_ref, d_ref, dx_ref, dw_ref, sm, s0, sp):
        w = w_ref[0]
        fill(d_ref, sm, s0, sp)

        def step(i, acc):
            st = pl.multiple_of(i * rc, rc)
            xc = x_ref[0, pl.ds(st, rc), :]
            dx = jnp.zeros((rc, LANES), F32)
            rows = [None] * 9
            for di in range(3):
                for dj, s in enumerate((sm, s0, sp)):
                    kidx = (2 - di) * 3 + (2 - dj)
                    val = s[pl.ds(st + di * GRID_W, rc), :]
                    dx = dx + w[kidx:kidx + 1, :] * val
                    rows[kidx] = jnp.sum(xc * val, axis=0, keepdims=True)
            dx_ref[0, pl.ds(st, rc), :] = dx
            return acc + jnp.concatenate(rows + [jnp.zeros((7, LANES), F32)], axis=0)

        acc = lax.fori_loop(0, n_rc, step, jnp.zeros((16, LANES), F32))
        dctx = d_ref[0, pl.ds(l_lat, l_ctx), :]
        xctx = x_ref[0, pl.ds(l_lat, l_ctx), :]
        dm, d0, dp = shifted(dctx, l_ctx, l_ctx)
        crow = [jnp.sum(xctx * s, axis=0, keepdims=True) for s in (dp, d0, dm)]
        acc = acc + jnp.concatenate([jnp.zeros((3, LANES), F32)] + crow + [jnp.zeros((10, LANES), F32)], axis=0)
        dw_ref[0] = acc
        dx_ref[0, pl.ds(l_lat, l_ctx), :] = ctx_apply(w, dctx, True)

    blk = lambda off: pl.BlockSpec((1, t, LANES), lambda j: (j + off, 0, 0))
    w_spec = pl.BlockSpec((1, 16, LANES), lambda j: (j, 0, 0))
    scratch = [pltpu.VMEM((l_lat + 2 * GRID_W, LANES), F32)] * 3
    out_t = jax.ShapeDtypeStruct((n_blocks, t, LANES), F32)
    if not bwd:
        return pl.pallas_call(
            fwd_body, name=name, grid=(n_blocks,), in_specs=[blk(first_block), w_spec], out_specs=blk(0),
            out_shape=out_t, scratch_shapes=scratch, compiler_params=_params(("arbitrary",), VMEM_BIG),
        )(src, wts)
    last = n_blocks - 1
    clamped = lambda off: pl.BlockSpec((1, t, LANES), lambda j: (jnp.minimum(j, last) + off, 0, 0))
    w_clamped = pl.BlockSpec((1, 16, LANES), lambda j: (jnp.minimum(j, last), 0, 0))
    in_specs, args, aliases = [clamped(first_block), w_clamped, clamped(0)], [src, wts, dout], {}
    if n_pass:
        in_specs.append(pl.BlockSpec((1, t, LANES), lambda j: (jnp.maximum(j - n_blocks, 0), 0, 0)))
        args.append(passthru)
    first_out = 0
    if into is not None:
        aliases[len(args)] = 0
        in_specs.append(ANY)
        args.append(into[0])
        first_out = into[1]
        out_t = jax.ShapeDtypeStruct(into[0].shape, F32)
    else:
        out_t = jax.ShapeDtypeStruct((n_blocks + n_pass, t, LANES), F32)
    return pl.pallas_call(
        bwd_body, name=name, grid=(n_blocks + n_pass,), in_specs=in_specs,
        out_specs=[pl.BlockSpec((1, t, LANES), lambda j: (j + first_out, 0, 0)), w_clamped],
        out_shape=[out_t, jax.ShapeDtypeStruct((n_blocks, 16, LANES), F32)],
        scratch_shapes=scratch, input_output_aliases=aliases, compiler_params=_params(("arbitrary",), VMEM_BIG),
    )(*args)


def _scan_consts(kind, g, h, rev, d_index):
    n = g * CHUNK
    i = jnp.arange(n, dtype=jnp.int32)
    head, pos = i // CHUNK, i % CHUNK
    p = (CHUNK - 1 - pos) if rev else pos
    same_head = head[:, None] == head[None, :]
    pr, pc = p[:, None], p[None, :]
    f = lambda m: m.astype(F32)
    incl = same_head & (pc <= pr)
    out = {"eye": f(i[:, None] == i[None, :]), "incl": f(incl), "incl_t": f(incl).T}
    if kind == "hg":
        mid, same, sec = [], [], []
        for half in HALVES:
            width = 2 * half
            blk = p // width
            second = (p % width) >= half
            ref_pos = blk * width + half - 1
            mid.append(same_head & (pc <= ref_pos[:, None]))
            same.append(same_head & (blk[:, None] == blk[None, :]))
            sec.append(jnp.broadcast_to(second[:, None], (n, LANES)))
        mid = f(jnp.concatenate(mid, axis=0))
        out.update(mid=mid, mid_t=mid.T, same=f(jnp.stack(same)), sec=f(jnp.stack(sec)))
    else:
        low = []
        for half in reversed(HALVES):
            width = 2 * half
            blk = p // width
            second = (p % width) >= half
            low.append(same_head & (blk[:, None] == blk[None, :]) & second[:, None] & jnp.logical_not(second[None, :]))
        lane = jnp.arange(LANES, dtype=jnp.int32)[None, :]
        hh = jnp.arange(h * CHUNK, dtype=jnp.int32) // CHUNK
        out.update(strict=f(same_head & (pc < pr)), low=f(jnp.stack(low)),
                   oh_a=f(lane == (d_index * h + hh)[:, None]), oh_b=f(lane == (2 * h + d_index * h + hh)[:, None]))
    return out


def _mm_01(c, x):
    hi = x.astype(jnp.bfloat16).astype(F32)
    return _mm(c, hi) + _mm(c, x - hi)


def _const_mm(c, c_t, diff, one_pass=False):
    fwd = (lambda x: _mm(c, x)) if one_pass else (lambda x: _mm_01(c, x))
    if not diff:
        return fwd

    f = jax.custom_vjp(fwd)
    f.defvjp(lambda x: (fwd(x), None), lambda _, ct: (_mm_01(c_t, ct),))
    return f


def _kept_inverse(x_kept):
    @jax.custom_vjp
    def f(a):
        return x_kept

    f.defvjp(lambda a: (x_kept, None), lambda _, ct: (-_mm_nt(_mm_tn(x_kept, ct, HIGH), x_kept, HIGH),))
    return f


def _stack_helpers(g):
    n = g * CHUNK
    rows = lambda vec: jnp.broadcast_to(vec, (g, CHUNK, LANES)).reshape(n, LANES)
    per_head = lambda t: [t[i * CHUNK:(i + 1) * CHUNK] for i in range(g)]
    head_sum = lambda t: rows(jnp.sum(t.reshape(g, CHUNK, LANES), axis=1, keepdims=True))
    return rows, per_head, head_sum


def _lockstep(gens):
    results = [None] * len(gens)
    live = list(range(len(gens)))
    while live:
        for i in list(live):
            try:
                next(gens[i])
            except StopIteration as done:
                results[i] = done.value
                live.remove(i)
    return results


def _hg_chunk(raw_q, raw_f, raw_i, l0, l1, state, cst, g, diff=False):
    n = g * CHUNK
    rows, per_head, head_sum = _stack_helpers(g)
    lb = rows(jax.nn.sigmoid(l0 - l1))
    q = _silu(raw_q) * (LANES ** -0.5)
    gl = jnp.log(lb + (1.0 - lb) * jax.nn.sigmoid(raw_f))
    k = (1.0 - lb) * jax.nn.sigmoid(-raw_f)
    v = raw_i
    b = _const_mm(cst["incl"], cst["incl_t"], diff)(gl)
    b_tot = head_sum(gl)
    q_dec = q * jnp.exp(b)
    k_dec = k * jnp.exp(b_tot - b)
    mids = _const_mm(cst["mid"], cst["mid_t"], diff, True)(gl)
    yield
    a = cst["eye"] * jnp.sum(q * k, axis=1, keepdims=True)
    for lv in range(len(HALVES)):
        r = mids[lv * n:(lv + 1) * n]
        sec = cst["sec"][lv]
        fst = 1.0 - sec
        qt = q * jnp.exp((b - r) * sec) * sec
        kt = k * jnp.exp((r - b) * fst) * fst
        a = a + _mm_nt(qt, kt) * cst["same"][lv]
        yield
    o_intra = per_head(_mm(a, v))
    decay = per_head(jnp.exp(b_tot))
    qd, kd, vs = per_head(q_dec), per_head(k_dec), per_head(v)
    sts = state()
    while sts is None:
        yield
        sts = state()
    outs = [o_intra[i] + _mm_nt(qd[i], sts[i]) for i in range(g)]
    new = [sts[i] * decay[i][0:1] + _mm_tn(vs[i], kd[i]) for i in range(g)]
    return jnp.concatenate(outs, axis=0), new, sts


def _gdn_chunk(u_q, u_k, u_v, tail, avec, dtvec, state, cst, g, x_kept=None):
    n = g * CHUNK
    diff = x_kept is not None
    rows, per_head, head_sum = _stack_helpers(g)

    def l2n(t):
        return t * lax.rsqrt(jnp.sum(t * t, axis=-1, keepdims=True) + NORM_EPS)

    q = l2n(_silu(u_q)) * (LANES ** -0.5)
    k = l2n(_silu(u_k))
    v = _silu(u_v)
    tail_n = jnp.concatenate([tail] * g, axis=0)
    za = jnp.sum(tail_n * cst["oh_a"], axis=1, keepdims=True) + rows(dtvec)
    beta = jax.nn.sigmoid(jnp.sum(tail_n * cst["oh_b"], axis=1, keepdims=True))
    gl = -jnp.exp(rows(avec)) * (jnp.maximum(za, 0.0) + jnp.log1p(jnp.exp(-jnp.abs(za))))
    b = _const_mm(cst["incl"], cst["incl_t"], diff)(gl)
    kk = _mm_nt(k, k)
    qk = _mm_nt(q, k)
    yield
    b_tot = head_sum(gl)
    bb = jnp.concatenate([b] * (n // LANES), axis=1)
    bdiff = bb - bb.T
    a_mat = beta * kk * jnp.exp(jnp.where(cst["strict"] > 0.5, bdiff, NEG))
    if diff:
        xinv = _kept_inverse(x_kept)(a_mat)
    else:
        xinv = cst["eye"] - a_mat * cst["low"][0]
        for lv in range(1, len(HALVES)):
            left = _mm(xinv, a_mat * cst["low"][lv])
            yield
            xinv = xinv - _mm(left, xinv)
            yield
        resid = cst["eye"] - _mm(cst["eye"] + a_mat, xinv, HIGH)
        yield
        xinv = xinv + _mm(xinv, resid, HIGH)
        yield
    rhs = jnp.concatenate([beta * v, beta * jnp.exp(b) * k], axis=1)
    sol = _mm(xinv, rhs, HIGH)
    yield
    u0, w = per_head(sol[:, :LANES]), per_head(sol[:, LANES:])
    k_dec = per_head(k * jnp.exp(b_tot - b))
    q_dec = per_head(q * jnp.exp(b))
    decay = per_head(jnp.exp(b_tot))
    p = qk * jnp.exp(jnp.where(cst["incl"] > 0.5, bdiff, NEG))
    ss = state()
    while ss is None:
        yield
        ss = state()
    v_new = [u0[i] - _mm(w[i], ss[i]) for i in range(g)]
    o_state = [_mm(q_dec[i], ss[i]) for i in range(g)]
    yield
    o = _mm(p, jnp.concatenate(v_new, axis=0)) + jnp.concatenate(o_state, axis=0)
    new = [decay[i][0:1] * ss[i] + _mm_tn(k_dec[i], v_new[i]) for i in range(g)]
    return o, new, xinv, ss


def _chunk_index(l_lat, l_ctx, rev, sc):
    rows = sc * CHUNK
    nl, nc = l_lat // rows, l_ctx // rows

    def idx(i):
        if rev:
            return jnp.where(i < nc, nl + nc - 1 - i, nl - 1 - (i - nc))
        return jnp.where(i < nc, nl + i, i - nc)

    subs = list(range(sc))
    return nl + nc, idx, (subs[::-1] if rev else subs)


def _const_args(kind, g, h, rev, d_index):
    consts = _scan_consts(kind, g, h, rev, d_index)
    names = sorted(consts)
    arrs = [consts[k] for k in names]
    specs = [pl.BlockSpec(a.shape, functools.partial(lambda nd, i: (0,) * nd, a.ndim)) for a in arrs]
    return names, arrs, specs


def _load_consts(names, refs, g0, g):
    cst = {}
    for k, r in zip(names, refs):
        if k in ("oh_a", "oh_b"):
            cst[k] = r[pl.ds(pl.multiple_of(g0 * CHUNK, CHUNK), g * CHUNK), :]
        else:
            cst[k] = r[...]
    return cst


def _scan_fwd(kind, rev, cbm, segs, vecs, o_prev, l_lat, l_ctx, h, name, tail=None, d_index=0):
    sc = FWD_STEP
    n_steps, cidx, subs = _chunk_index(l_lat, l_ctx, rev, sc)
    rows = sc * CHUNK
    t = l_lat + l_ctx
    n_in = len(segs)
    gdn = kind == "gdn"
    g = min(HEADS_PER_GROUP, h)
    n = g * CHUNK
    c_names, c_arrs, c_specs = _const_args(kind, g, h, rev, d_index)

    def body(*refs):
        in_refs = refs[:n_in]
        pos = n_in
        tail_ref = None
        if gdn:
            tail_ref = refs[pos]
            pos += 1
        v0_ref, v1_ref = refs[pos], refs[pos + 1]
        pos += 2
        prev_ref = None
        if o_prev is not None:
            prev_ref = refs[pos]
            pos += 1
        c_refs = refs[pos:pos + len(c_names)]
        pos += len(c_names)
        o_ref, st_ref = refs[pos], refs[pos + 1]
        pos += 2
        x_ref = None
        if gdn:
            x_ref = refs[pos]
            pos += 1
        s_scr = refs[pos]

        @pl.when(pl.program_id(0) == 0)
        def _():
            s_scr[...] = jnp.zeros_like(s_scr)

        def instance(k, sub, gi, before, after):
            rs = pl.ds(sub * CHUNK, CHUNK)
            g0 = gi * g
            hs = pl.ds(g0, g)
            cst = _load_consts(c_names, c_refs, g0, g)
            state = (lambda: [s_scr[g0 + i] for i in range(g)]) if before is None else (lambda: before[0])
            ins = [r[hs, rs, :].reshape(n, LANES) for r in in_refs]
            if gdn:
                o, new, xinv, old = yield from _gdn_chunk(*ins, tail_ref[0, rs, :], v0_ref[hs], v1_ref[hs], state, cst, g)
                x_ref[k, gi] = xinv
            else:
                o, new, old = yield from _hg_chunk(*ins, v0_ref[hs], v1_ref[hs], state, cst, g)
            after[0] = new
            for i in range(g):
                st_ref[k, g0 + i] = old[i]
            o = o.reshape(g, CHUNK, LANES)
            if prev_ref is not None:
                o = o + prev_ref[hs, rs, :]
            o_ref[hs, rs, :] = o
            if k == sc - 1:
                for i in range(g):
                    s_scr[g0 + i] = new[i]

        gens = []
        handed = [None] * (h // g)
        for k, sub in enumerate(subs):
            for gi in range(h // g):
                after = [None]
                gens.append(instance(k, sub, gi, handed[gi], after))
                handed[gi] = after
        _lockstep(gens)

    seg_spec = lambda sg: pl.BlockSpec((h, rows, LANES), lambda i: (sg, cidx(i), 0))
    vec_spec = pl.BlockSpec((h, 1, LANES), lambda i: (0, 0, 0))
    in_specs = [seg_spec(sg) for sg in segs]
    args = [cbm] * n_in
    if gdn:
        in_specs.append(pl.BlockSpec((1, rows, LANES), lambda i: (0, cidx(i), 0)))
        args.append(tail)
    in_specs += [vec_spec, vec_spec]
    args += list(vecs)
    if o_prev is not None:
        in_specs.append(seg_spec(0))
        args.append(o_prev)
    in_specs += c_specs
    args += c_arrs
    return pl.pallas_call(
        body, name=name, grid=(n_steps,), in_specs=in_specs,
        out_specs=[seg_spec(0), pl.BlockSpec((sc, h, LANES, LANES), lambda i: (i, 0, 0, 0))]
        + ([pl.BlockSpec((sc, h // g, n, n), lambda i: (i, 0, 0, 0))] if gdn else []),
        out_shape=[jax.ShapeDtypeStruct((h, t, LANES), F32),
                   jax.ShapeDtypeStruct((n_steps * sc, h, LANES, LANES), F32)]
        + ([jax.ShapeDtypeStruct((n_steps * sc, h // g, n, n), F32)] if gdn else []),
        scratch_shapes=[pltpu.VMEM((h, LANES, LANES), F32)],
        compiler_params=_params(("arbitrary",), VMEM_BIG),
    )(*args)


def _scan_bwd(kind, rev, cbm, segs, vecs, states, d_o, acc, l_lat, l_ctx, h, name, tail=None, d_index=0, x_kept=None,
              pack=None, pack_into=None, tail_into=None):
    sc = BWD_STEP
    n_steps, cidx, subs = _chunk_index(l_lat, l_ctx, rev, sc)
    rows = sc * CHUNK
    t = l_lat + l_ctx
    n_in = len(segs)
    gdn = kind == "gdn"
    n_grad = n_in + (1 if gdn else 0)
    assert len(acc) == n_grad
    step_of = lambda j: n_steps - 1 - j
    g = min(HEADS_PER_GROUP, h)
    n = g * CHUNK
    c_names, c_arrs, c_specs = _const_args(kind, g, h, rev, d_index)
    items = pack or []
    packed_at = {it[1]: p for p, it in enumerate(items) if it[0] == "grad"}
    copies = [(p, it[1]) for p, it in enumerate(items) if it[0] == "copy"]
    separate = [m for m in range(n_in) if m not in packed_at]
    in_place = pack_into if pack_into is not None else tail_into
    assert pack_into is None or tail_into is None
    tail_blocks = tail_into[2] if tail_into is not None else 1

    def body(*refs):
        in_refs = refs[:n_in]
        pos = n_in
        tail_ref = None
        if gdn:
            tail_ref = refs[pos]
            pos += 1
        v0_ref, v1_ref, st_ref, do_ref = refs[pos:pos + 4]
        pos += 4
        xk_ref = None
        if gdn:
            xk_ref = refs[pos]
            pos += 1
        acc_refs = []
        for a in acc:
            if a is None:
                acc_refs.append(None)
            else:
                acc_refs.append(refs[pos])
                pos += 1
        copy_refs = refs[pos:pos + len(copies)]
        pos += len(copies) + (1 if in_place is not None else 0)
        c_refs = refs[pos:pos + len(c_names)]
        pos += len(c_names)
        sep_refs = refs[pos:pos + len(separate)]
        pos += len(separate)
        pack_ref = None
        if items:
            pack_ref = refs[pos]
            pos += 1
        tail_out = None
        if gdn:
            tail_out = refs[pos]
            pos += 1
        dv0_ref, dv1_ref, ds_scr = refs[pos:pos + 3]

        @pl.when(pl.program_id(0) == 0)
        def _():
            ds_scr[...] = jnp.zeros_like(ds_scr)
            dv0_ref[...] = jnp.zeros_like(dv0_ref)
            dv1_ref[...] = jnp.zeros_like(dv1_ref)

        if gdn:
            tail_out[0] = jnp.zeros((rows, LANES), F32) if acc_refs[n_in] is None else acc_refs[n_in][0]
            for q in range(1, tail_blocks):
                tail_out[q] = jnp.zeros((rows, LANES), F32)

        n_groups = h // g

        rss = [pl.ds(subs[k] * CHUNK, CHUNK) for k in range(sc)]
        hss = [pl.ds(gi * g, g) for gi in range(n_groups)]
        csts = [_load_consts(c_names, c_refs, gi * g, g) for gi in range(n_groups)]
        xks = [[xk_ref[k, gi] for gi in range(n_groups)] for k in range(sc)] if gdn else None

        def link(k, gi, ins, tl, v0, v1, state, after):
            if gdn:
                res = yield from _gdn_chunk(*ins, tl, v0, v1, state, csts[gi], g, xks[k][gi])
            else:
                res = yield from _hg_chunk(*ins, v0, v1, state, csts[gi], g, True)
            after[0] = res[1]
            return res[0]

        def fn(ins, tls, v0s, v1s, s0):
            gens = []
            handed = [None] * n_groups
            for k in range(sc):
                for gi in range(n_groups):
                    before, after = handed[gi], [None]
                    state = (functools.partial(lambda s: s, s0[gi]) if before is None
                             else functools.partial(lambda cell: cell[0], before))
                    gens.append(link(k, gi, ins[k][gi], tls[k], v0s[gi], v1s[gi], state, after))
                    handed[gi] = after
            outs = _lockstep(gens)
            return ([[outs[k * n_groups + gi] for gi in range(n_groups)] for k in range(sc)],
                    [handed[gi][0] for gi in range(n_groups)])

        ins = [[[r[hs, rs, :].reshape(n, LANES) for r in in_refs] for hs in hss] for rs in rss]
        tls = [tail_ref[0, rs, :] if gdn else jnp.zeros((1, 1), F32) for rs in rss]
        s0 = [[st_ref[0, gi * g + i] for i in range(g)] for gi in range(n_groups)]
        _, vj = jax.vjp(fn, ins, tls, [v0_ref[hs] for hs in hss], [v1_ref[hs] for hs in hss], s0)
        d_o_all = [[do_ref[hs, rs, :].reshape(n, LANES) for hs in hss] for rs in rss]
        d_fin = [[ds_scr[gi * g + i] for i in range(g)] for gi in range(n_groups)]
        d_ins, d_tls, d_v0s, d_v1s, d_s0 = vj((d_o_all, d_fin))
        for gi, hs in enumerate(hss):
            g0 = gi * g
            for k, rs in enumerate(rss):
                for m in range(n_in):
                    gm = d_ins[k][gi][m].reshape(g, CHUNK, LANES)
                    if acc_refs[m] is not None:
                        gm = gm + acc_refs[m][hs, rs, :]
                    if m in packed_at:
                        pack_ref[pl.ds(packed_at[m] * h + g0, g), rs, :] = gm
                    else:
                        sep_refs[separate.index(m)][hs, rs, :] = gm
                for (p, _), c_ref in zip(copies, copy_refs):
                    pack_ref[pl.ds(p * h + g0, g), rs, :] = c_ref[hs, rs, :]
            dv0, dv1 = d_v0s[gi], d_v1s[gi]
            if gdn:
                dv0 = jnp.broadcast_to(jnp.sum(dv0, axis=2, keepdims=True), dv0.shape)
                dv1 = jnp.broadcast_to(jnp.sum(dv1, axis=2, keepdims=True), dv1.shape)
            dv0_ref[hs] += dv0
            dv1_ref[hs] += dv1
            for i in range(g):
                ds_scr[g0 + i] = d_s0[gi][i]
        if gdn:
            for k, rs in enumerate(rss):
                tail_out[0, rs, :] += d_tls[k]

    seg_spec = lambda sg: pl.BlockSpec((h, rows, LANES), lambda j: (sg, cidx(step_of(j)), 0))
    tail_spec = pl.BlockSpec((1, rows, LANES), lambda j: (0, cidx(step_of(j)), 0))
    vec_spec = pl.BlockSpec((h, 1, LANES), lambda j: (0, 0, 0))
    in_specs = [seg_spec(sg) for sg in segs]
    args = [cbm] * n_in
    if gdn:
        in_specs.append(tail_spec)
        args.append(tail)
    in_specs += [vec_spec, vec_spec, pl.BlockSpec((sc, h, LANES, LANES), lambda j: (step_of(j), 0, 0, 0)),
                 seg_spec(0)]
    args += list(vecs) + [states, d_o]
    if gdn:
        in_specs.append(pl.BlockSpec((sc, h // g, n, n), lambda j: (step_of(j), 0, 0, 0)))
        args.append(x_kept)
    for k, a in enumerate(acc):
        if a is not None:
            in_specs.append(tail_spec if (gdn and k == n_in) else seg_spec(0))
            args.append(a)
    for _, arr in copies:
        in_specs.append(seg_spec(0))
        args.append(arr)
    aliases = {}
    if in_place is not None:
        aliases[len(args)] = len(separate) + (1 if (items and tail_into is not None) else 0)
        in_specs.append(ANY)
        args.append(in_place[0])
    in_specs += c_specs
    args += c_arrs
    out_specs = [seg_spec(0)] * len(separate)
    out_shape = [jax.ShapeDtypeStruct((h, t, LANES), F32)] * len(separate)
    if items:
        pb = len(items) * h
        first = pack_into[1] if pack_into is not None else 0
        assert first % pb == 0
        out_specs.append(pl.BlockSpec((pb, rows, LANES), lambda j: (first // pb, cidx(step_of(j)), 0)))
        out_shape.append(jax.ShapeDtypeStruct(pack_into[0].shape if pack_into is not None else (pb, t, LANES), F32))
    if gdn and tail_into is not None:
        first_t = tail_into[1]
        assert first_t % tail_blocks == 0
        out_specs.append(pl.BlockSpec((tail_blocks, rows, LANES), lambda j: (first_t // tail_blocks, cidx(step_of(j)), 0)))
        out_shape.append(jax.ShapeDtypeStruct(tail_into[0].shape, F32))
    elif gdn:
        out_specs.append(tail_spec)
        out_shape.append(jax.ShapeDtypeStruct((1, t, LANES), F32))
    out_specs += [vec_spec, vec_spec]
    out_shape += [jax.ShapeDtypeStruct((h, 1, LANES), F32)] * 2
    return pl.pallas_call(
        body, name=name, grid=(n_steps,), in_specs=in_specs, out_specs=out_specs, out_shape=out_shape,
        scratch_shapes=[pltpu.VMEM((h, LANES, LANES), F32)], input_output_aliases=aliases,
        compiler_params=_params(("arbitrary",), VMEM_BIG),
    )(*args)


def _gout_tile(oa, ob, za, zb, naw, nbw):
    def hn(o, w):
        return o * lax.rsqrt(jnp.mean(o * o, axis=-1, keepdims=True) + NORM_EPS) * w

    ya = _silu(za) * hn(oa, naw)
    yb = _silu(zb) * hn(ob, nbw)
    nh = oa.shape[0]
    return jnp.concatenate([ya[i] for i in range(nh)] + [yb[i] for i in range(nh)], axis=1)


def _gout_fwd(oa, ob, y_cbm, naw, nbw, l_lat, h):
    tm = _pick(l_lat, (256, 128, 64))
    blk = lambda sg: pl.BlockSpec((h, tm, LANES), lambda i: (sg, i, 0))
    vec = pl.BlockSpec((h, 1, LANES), lambda i: (0, 0, 0))

    def body(oa_ref, ob_ref, za_ref, zb_ref, na_ref, nb_ref, y_ref):
        y_ref[...] = _gout_tile(oa_ref[...], ob_ref[...], za_ref[...], zb_ref[...], na_ref[...], nb_ref[...]).astype(BF16)

    return pl.pallas_call(
        body, name="gout_fwd", grid=(l_lat // tm,),
        in_specs=[blk(0), blk(0), blk(4), blk(8), vec, vec],
        out_specs=pl.BlockSpec((tm, 2 * h * LANES), lambda i: (i, 0)),
        out_shape=jax.ShapeDtypeStruct((l_lat, 2 * h * LANES), BF16),
        compiler_params=_params(("arbitrary",)),
    )(oa, ob, y_cbm, y_cbm, naw, nbw)


def _gout_bwd(oa, ob, y_cbm, naw, nbw, dymix, l_lat, l_ctx, h):
    tm = _pick(l_ctx, (256, 128, 64))
    nl, nc = l_lat // tm, l_ctx // tm
    t = l_lat + l_ctx
    blk_in = lambda sg: pl.BlockSpec((h, tm, LANES), lambda i: (sg, jnp.minimum(i, nl - 1), 0))
    blk_out = pl.BlockSpec((h, tm, LANES), lambda i: (0, i, 0))
    vec = pl.BlockSpec((h, 1, LANES), lambda i: (0, 0, 0))

    def body(oa_ref, ob_ref, za_ref, zb_ref, na_ref, nb_ref, dy_ref, doa_ref, dob_ref, dza_ref, dzb_ref, dna_ref, dnb_ref):
        i = pl.program_id(0)

        @pl.when(i == 0)
        def _():
            dna_ref[...] = jnp.zeros_like(dna_ref)
            dnb_ref[...] = jnp.zeros_like(dnb_ref)

        @pl.when(i < nl)
        def _():
            _, vj = jax.vjp(_gout_tile, oa_ref[...], ob_ref[...], za_ref[...], zb_ref[...], na_ref[...], nb_ref[...])
            doa, dob, dza, dzb, dna, dnb = vj(dy_ref[...])
            doa_ref[...] = doa
            dob_ref[...] = dob
            dza_ref[...] = dza
            dzb_ref[...] = dzb
            dna_ref[...] += dna
            dnb_ref[...] += dnb

        @pl.when(i >= nl)
        def _():
            for r in (doa_ref, dob_ref, dza_ref, dzb_ref):
                r[...] = jnp.zeros_like(r)

    big = jax.ShapeDtypeStruct((h, t, LANES), F32)
    small = jax.ShapeDtypeStruct((h, 1, LANES), F32)
    dza_out = pl.BlockSpec((h, tm, LANES), lambda i: (4, i, 0))
    return pl.pallas_call(
        body, name="gout_bwd", grid=(nl + nc,),
        in_specs=[blk_in(0), blk_in(0), blk_in(4), blk_in(8), vec, vec,
                  pl.BlockSpec((tm, 2 * h * LANES), lambda i: (jnp.minimum(i, nl - 1), 0))],
        out_specs=[blk_out, blk_out, dza_out, blk_out, vec, vec],
        out_shape=[big, big, jax.ShapeDtypeStruct(y_cbm.shape, F32), big, small, small],
        compiler_params=_params(("arbitrary",)),
    )(oa, ob, y_cbm, y_cbm, naw, nbw, dymix)


def _head_tile(xt, mixt, tgt, gt, fw):
    xo = xt + gt * mixt
    y = xo * lax.rsqrt(jnp.mean(xo * xo, axis=-1, keepdims=True) + NORM_EPS) * fw
    err = y - tgt
    return 0.5 * jnp.sum(jnp.mean(err * err, axis=-1, keepdims=True), axis=0, keepdims=True)


def _loss_head(x, mix, target, mods, final_w):
    l_lat, d = x.shape
    tm = _pick(l_lat, (256, 128, 64))
    tok = pl.BlockSpec((tm, d), lambda i: (i, 0))
    row = pl.BlockSpec((1, d), lambda i: (0, 0))

    def body(x_ref, m_ref, t_ref, mod_ref, fw_ref, loss_ref, dmix_ref, dx_ref, dgt_ref, dfw_ref):
        @pl.when(pl.program_id(0) == 0)
        def _():
            loss_ref[...] = jnp.zeros_like(loss_ref)
            dgt_ref[...] = jnp.zeros_like(dgt_ref)
            dfw_ref[...] = jnp.zeros_like(dfw_ref)

        gt = mod_ref[0:1, 2 * d:3 * d]
        fn = lambda xt, mt, g, fw: _head_tile(xt, mt, t_ref[...], g, fw)
        val, vj = jax.vjp(fn, x_ref[...], m_ref[...], gt, fw_ref[...])
        dxt, dmt, dgt, dfw = vj(jnp.ones((1, 1), F32))
        loss_ref[...] += jnp.broadcast_to(val, loss_ref.shape)
        dmix_ref[...] = dmt.astype(BF16)
        dx_ref[...] = dxt
        dgt_ref[...] += dgt
        dfw_ref[...] += dfw

    return pl.pallas_call(
        body, name="loss_head", grid=(l_lat // tm,),
        in_specs=[tok, tok, tok, pl.BlockSpec((8, 3 * d), lambda i: (0, 0)), row],
        out_specs=[pl.BlockSpec((8, LANES), lambda i: (0, 0)), tok, tok, row, row],
        out_shape=[jax.ShapeDtypeStruct((8, LANES), F32), jax.ShapeDtypeStruct((l_lat, d), BF16),
                   jax.ShapeDtypeStruct((l_lat, d), F32), jax.ShapeDtypeStruct((1, d), F32),
                   jax.ShapeDtypeStruct((1, d), F32)],
        compiler_params=_params(("arbitrary",)),
    )(x, mix, target, mods, final_w)


def _adamw(w, g_slots, m, v, name):
    rows, cols = w.shape
    n = g_slots.shape[0]
    tr = _pick(rows, (64, 32, 16, 8))

    def body(w_ref, g_ref, m_ref, v_ref, go_ref, d_ref, nm_ref, nv_ref):
        gg = g_ref[0].astype(F32)
        for s in range(1, n):
            gg = gg + g_ref[s].astype(F32)
        m2 = ADAM_B1 * m_ref[...] + (1.0 - ADAM_B1) * gg
        v2 = ADAM_B2 * v_ref[...] + (1.0 - ADAM_B2) * (gg * gg)
        m_hat = m2 / (1.0 - ADAM_B1 ** ADAM_STEP)
        v_hat = v2 / (1.0 - ADAM_B2 ** ADAM_STEP)
        go_ref[...] = gg
        d_ref[...] = -ADAM_LR * (m_hat / (jnp.sqrt(v_hat) + ADAM_EPS) + ADAM_WD * w_ref[...])
        nm_ref[...] = m2
        nv_ref[...] = v2

    spec = pl.BlockSpec((tr, cols), lambda i: (i, 0))
    shp = jax.ShapeDtypeStruct((rows, cols), F32)
    return pl.pallas_call(
        body, name=name, grid=(rows // tr,),
        in_specs=[spec, pl.BlockSpec((n, tr, cols), lambda i: (0, i, 0)), spec, spec],
        out_specs=[spec] * 4, out_shape=[shp] * 4,
        compiler_params=_params(("arbitrary",)),
    )(w, g_slots, m, v)


def _adamw_nd(w, g_slots, m, v, name):
    shape = w.shape
    two_d = (-1, shape[-1])
    w2 = w.reshape(two_d)
    outs = _adamw(w2, g_slots.reshape((g_slots.shape[0],) + w2.shape), m.reshape(two_d), v.reshape(two_d), name)
    return [o.reshape(shape) for o in outs]


def kernel(x, c, ctx, c_ctx, norm_w, ada_w, ada_b, w_in, conv_w, hg_lb_logits, gdn_a_log, gdn_dt_bias, ha_norm_w, hb_norm_w, w_out, final_norm_w, loss_target, m_c_ctx, m_norm_w, m_ada_w, m_ada_b, m_w_in, m_conv_w, m_hg_lb_logits, m_gdn_a_log, m_gdn_dt_bias, m_ha_norm_w, m_hb_norm_w, m_w_out, m_final_norm_w, v_c_ctx, v_norm_w, v_ada_w, v_ada_b, v_w_in, v_conv_w, v_hg_lb_logits, v_gdn_a_log, v_gdn_dt_bias, v_ha_norm_w, v_hb_norm_w, v_w_out, v_final_norm_w):
    l_lat, d = x.shape[1], x.shape[2]
    l_ctx = ctx.shape[1]
    t = l_lat + l_ctx
    h = d // LANES
    n_in = 9 * d + 4 * h
    nb = -(-(9 * h + 1) // COLS_PER_TILE) * COLS_PER_TILE
    n_pad = nb * LANES
    win_c, ada_c, conv_c, lb_c, wout_r = w_in.shape[2], ada_w.shape[2], conv_w.shape[3], hg_lb_logits.shape[2], w_out.shape[1]
    me = _me()

    gw_in, gw_out, gw_ada, gw_conv, gw_lb = _gather_two_level(
        [w_in[0].astype(BF16), w_out[0].astype(BF16), ada_w[0], conv_w[0], hg_lb_logits], "gather_weights")
    join = lambda g, axis: jnp.concatenate([g[k] for k in range(N_DEV)], axis=axis)
    w_in_full = join(gw_in, 1)
    w_out_full = gw_out.reshape(N_DEV * wout_r, d)
    ada_full = join(gw_ada, 1)
    conv_full = join(gw_conv, 2)
    lb_full = join(gw_lb, 2)
    w_pad = jnp.pad(w_in_full, ((0, 0), (0, n_pad - n_in)))
    w_pad_t = w_pad.T
    w_out_t = w_out_full.T

    lbl = lb_full.reshape(2, 2, h, 1, LANES)
    conv_rows = jnp.pad(conv_full.reshape(9, 3 * h, LANES).transpose(1, 0, 2), ((0, 0), (0, 7), (0, 0)))
    lane_bc = lambda a: jnp.broadcast_to(a.reshape(2, h, 1, 1), (2, h, 1, LANES))
    avec, dtvec = lane_bc(gdn_a_log[0]), lane_bc(gdn_dt_bias[0])
    naw, nbw = ha_norm_w[0].reshape(h, 1, LANES), hb_norm_w[0].reshape(h, 1, LANES)

    cvec = jnp.concatenate([c, c_ctx[None, :], jnp.zeros((6, d), F32)], axis=0)
    mods = _ada_fwd(cvec, ada_full, ada_b)
    x2, ctx2, tgt2 = x[0], ctx[0], loss_target[0]
    h_all = _prenorm_fwd(x2, ctx2, norm_w, mods)
    y = _matmul(h_all, w_pad, "proj", out_cbm=True, tn=COLS_PER_TILE * LANES)
    tail = lax.slice_in_dim(y, 9 * h, 9 * h + 1, axis=0)
    u = _conv_call(y, conv_rows, l_lat, 5 * h, 3 * h, "conv_fwd")

    oa, st_af = _scan_fwd("hg", False, y, (0, 1, 3), (lbl[0, 0], lbl[1, 0]), None, l_lat, l_ctx, h, "hg_fwd_f")
    oa, st_ab = _scan_fwd("hg", True, y, (0, 2, 3), (lbl[0, 1], lbl[1, 1]), oa, l_lat, l_ctx, h, "hg_fwd_b")
    ob, st_bf, xk_f = _scan_fwd("gdn", False, u, (0, 1, 2), (avec[0], dtvec[0]), None, l_lat, l_ctx, h, "gdn_fwd_f", tail, 0)
    ob, st_bb, xk_b = _scan_fwd("gdn", True, u, (0, 1, 2), (avec[1], dtvec[1]), ob, l_lat, l_ctx, h, "gdn_fwd_b", tail, 1)

    ymix = _gout_fwd(oa, ob, y, naw, nbw, l_lat, h)
    mix = _matmul(ymix, w_out_full, "out_proj")
    loss_blk, dmix, dx_res, dgt, dfw = _loss_head(x2, mix, tgt2, mods, final_norm_w.reshape(1, d))

    dymix = _matmul(dmix, w_out_t, "d_ymix")
    dw_out = _matmul(ymix, dmix, "d_w_out", a_t=True)
    doa, dob, dy, dzb, dnaw, dnbw = _gout_bwd(oa, ob, y, naw, nbw, dymix, l_lat, l_ctx, h)

    gq, gff, gi, dl0f, dl1f = _scan_bwd("hg", False, y, (0, 1, 3), (lbl[0, 0], lbl[1, 0]), st_af, doa,
                                        [None, None, None], l_lat, l_ctx, h, "hg_bwd_f")
    dy, dl0b, dl1b = _scan_bwd("hg", True, y, (0, 2, 3), (lbl[0, 1], lbl[1, 1]), st_ab, doa, [gq, None, gi],
                               l_lat, l_ctx, h, "hg_bwd_b",
                               pack=[("grad", 0), ("copy", gff), ("grad", 1), ("grad", 2)], pack_into=(dy, 0))
    guq, guk, guv, gtail, da_f, ddt_f = _scan_bwd("gdn", False, u, (0, 1, 2), (avec[0], dtvec[0]), st_bf, dob,
                                                  [None] * 4, l_lat, l_ctx, h, "gdn_bwd_f", tail, 0, xk_f)
    du, dy, da_b, ddt_b = _scan_bwd("gdn", True, u, (0, 1, 2), (avec[1], dtvec[1]), st_bb, dob,
                                    [guq, guk, guv, gtail], l_lat, l_ctx, h, "gdn_bwd_b", tail, 1, xk_b,
                                    pack=[("grad", 0), ("grad", 1), ("grad", 2)], tail_into=(dy, 9 * h, nb - 9 * h))
    dy, dconv_rows = _conv_call(y, conv_rows, l_lat, 5 * h, 3 * h, "conv_bwd", dout=du, passthru=dzb, into=(dy, 5 * h))
    dh = _matmul(dy, w_pad_t, "d_h", a_cbm=True)
    dw_in = _matmul(h_all, dy, "d_w_in", a_t=True, b_cbm=True)
    grad_x, dnorm_w, dmods_pre = _prenorm_bwd(x2, ctx2, norm_w, mods, dh, dx_res)

    dmods = jnp.concatenate([dmods_pre[:, :2 * d],
                             jnp.concatenate([dgt, jnp.zeros((7, d), F32)], axis=0)], axis=1)
    dcond = _ada_bwd(cvec, ada_full, dmods)

    dw_in_s = jnp.stack([dw_in[:, k * win_c:(k + 1) * win_c] for k in range(N_DEV)])
    dw_out_s = dw_out.reshape(N_DEV, wout_r, d)
    dconv_s = dconv_rows[:, :9, :].transpose(1, 0, 2).reshape(3, 3, N_DEV, conv_c).transpose(2, 0, 1, 3)
    dlb_s = jnp.stack([jnp.stack([dl0f, dl0b]), jnp.stack([dl1f, dl1b])]).reshape(2, 2, N_DEV, lb_c).transpose(2, 0, 1, 3)
    slabs = [dw_in_s, dw_out_s, dconv_s, dlb_s]
    theirs = _swap_with_sibling([a.astype(BF16) for a in slabs], "swap_grads")
    pairs = [_pair_sum(a, b, "pair_sum_%d" % k) for k, (a, b) in enumerate(zip(slabs, theirs))]
    s_w_in, s_w_out, s_conv, s_lb = _scatter_over_chips(pairs, "scatter_grads")

    da = jnp.stack([da_f[:, 0, 0], da_b[:, 0, 0]])
    ddt = jnp.stack([ddt_f[:, 0, 0], ddt_b[:, 0, 0]])
    small_parts = [dcond[1], dnorm_w, dmods[0] + dmods[1], dnaw, dnbw, dfw, da, ddt, c[0], dmods[0], dmods[1]]
    small_blob, soff = _to_blob(small_parts)
    gathered, = _exchange([small_blob], "gather_small", True)
    summed = _sum_slots(gathered, "sum_small").reshape(-1)
    gflat = gathered.reshape(N_DEV, -1)
    take = lambda k, n: summed[soff[k]:soff[k] + n]
    one = lambda k, n, shape: take(k, n).reshape((1,) + shape)
    s_c_ctx = one(0, d, (d,))
    s_norm_w = one(1, d, (1, d))
    s_ada_b = one(2, 3 * d, (1, 3 * d))
    s_ha = one(3, d, (1, h, LANES))
    s_hb = one(4, d, (1, h, LANES))
    s_final = one(5, d, (d,))
    s_a_log = one(6, 2 * h, (1, 2, h))
    s_dt = one(7, 2 * h, (1, 2, h))
    conds = jnp.concatenate([gflat[:, soff[8]:soff[8] + d], c_ctx[None, :], jnp.zeros((7, d), F32)], axis=0)
    col0 = me * ada_c
    dm_lat = lax.dynamic_slice_in_dim(gflat[:, soff[9]:soff[9] + 3 * d], col0, ada_c, axis=1)
    dm_ctx = lax.dynamic_slice_in_dim(gflat[:, soff[10]:soff[10] + 3 * d], col0, ada_c, axis=1)
    s_ada_w = _ada_wgrad(conds, dm_lat, dm_ctx).reshape(1, 1, d, ada_c)

    loss = lax.psum(loss_blk[0, 0], ("x", "y", "c"))

    weights = [("c_ctx", c_ctx, s_c_ctx, m_c_ctx, v_c_ctx), ("norm_w", norm_w, s_norm_w, m_norm_w, v_norm_w),
               ("ada_w", ada_w, s_ada_w, m_ada_w, v_ada_w), ("ada_b", ada_b, s_ada_b, m_ada_b, v_ada_b),
               ("w_in", w_in, s_w_in, m_w_in, v_w_in), ("conv_w", conv_w, s_conv, m_conv_w, v_conv_w),
               ("hg_lb_logits", hg_lb_logits, s_lb, m_hg_lb_logits, v_hg_lb_logits),
               ("gdn_a_log", gdn_a_log, s_a_log, m_gdn_a_log, v_gdn_a_log),
               ("gdn_dt_bias", gdn_dt_bias, s_dt, m_gdn_dt_bias, v_gdn_dt_bias),
               ("ha_norm_w", ha_norm_w, s_ha, m_ha_norm_w, v_ha_norm_w), ("hb_norm_w", hb_norm_w, s_hb, m_hb_norm_w, v_hb_norm_w),
               ("w_out", w_out, s_w_out, m_w_out, v_w_out), ("final_norm_w", final_norm_w, s_final, m_final_norm_w, v_final_norm_w)]
    grads, deltas, new_ms, new_vs = [], [], [], []
    for nm, w, gs, m, v in weights:
        g, dl, m2, v2 = _adamw_nd(w, gs, m, v, "adamw_" + nm)
        grads.append(g)
        deltas.append(dl)
        new_ms.append(m2)
        new_vs.append(v2)
    return (loss, grad_x[None], *grads, *deltas, *new_ms, *new_vs)
```

```python
import functools

import jax
import jax.numpy as jnp
import numpy as np
from jax import lax
from jax.experimental import pallas as pl
from jax.experimental.pallas import tpu as pltpu

F32 = jnp.float32
BF16 = jnp.bfloat16
HI = lax.Precision.HIGHEST
HIGH = lax.Precision.HIGH
MESH = pl.DeviceIdType.MESH
ANY = pl.BlockSpec(memory_space=pl.ANY)

N_DEV = 8
LANES = 128
CHUNK = 64
GRID_W = 64
NORM_EPS = 1e-6
NEG = -1e30
COLS_PER_TILE = 8
VMEM_BIG = 56 * 1024 * 1024
HEADS_PER_GROUP = 4
FWD_STEP = 4
BWD_STEP = 2
HALVES = (32, 16, 8, 4, 2, 1)

ADAM_LR, ADAM_B1, ADAM_B2, ADAM_EPS, ADAM_WD, ADAM_STEP = 0.001, 0.9, 0.999, 1e-08, 0.01, 10


def _params(sem=None, vmem=None):
    kw = {}
    if sem is not None:
        kw["dimension_semantics"] = sem
    if vmem is not None:
        kw["vmem_limit_bytes"] = vmem
    return pltpu.CompilerParams(**kw)


def _pick(n, cands):
    for c in cands:
        if n % c == 0:
            return c
    return n


def _silu(x):
    return x * jax.nn.sigmoid(x)


def _mm(a, b, prec=None):
    return lax.dot_general(a, b, (((1,), (0,)), ((), ())), precision=prec, preferred_element_type=F32)


def _mm_nt(a, b, prec=None):
    return lax.dot_general(a, b, (((1,), (1,)), ((), ())), precision=prec, preferred_element_type=F32)


def _mm_tn(a, b, prec=None):
    return lax.dot_general(a, b, (((0,), (0,)), ((), ())), precision=prec, preferred_element_type=F32)


def _me():
    return 4 * lax.axis_index("x") + 2 * lax.axis_index("y") + lax.axis_index("c")


def _peer(r):
    x, y, c = lax.axis_index("x"), lax.axis_index("y"), lax.axis_index("c")
    px = 1 - x if r & 4 else x
    py = 1 - y if r & 2 else y
    pc = 1 - c if r & 1 else c
    return (px, py, pc), 4 * px + 2 * py + pc


def _exchange(arrs, name, gather):
    n_a = len(arrs)

    def body(*refs):
        srcs, outs = refs[:n_a], refs[n_a:2 * n_a]
        send_sems, recv_sems, local_sems = refs[2 * n_a:]
        me = _me()
        copies = []
        for k in range(n_a):
            local = pltpu.make_async_copy(srcs[k] if gather else srcs[k].at[me], outs[k].at[me], local_sems.at[k])
            local.start()
            copies.append(local)
        for r in range(N_DEV - 1, 0, -1):
            dev, pid = _peer(r)
            for k in range(n_a):
                cp = pltpu.make_async_remote_copy(
                    src_ref=srcs[k] if gather else srcs[k].at[pid], dst_ref=outs[k].at[me],
                    send_sem=send_sems.at[(r - 1) * n_a + k], recv_sem=recv_sems.at[(r - 1) * n_a + k],
                    device_id=dev, device_id_type=MESH)
                cp.start()
                copies.append(cp)
        for cp in copies:
            cp.wait()

    n_sem = (N_DEV - 1) * n_a
    return pl.pallas_call(
        body, name=name,
        out_shape=[jax.ShapeDtypeStruct(((N_DEV,) + a.shape) if gather else a.shape, a.dtype) for a in arrs],
        in_specs=[ANY] * n_a, out_specs=[ANY] * n_a,
        scratch_shapes=[pltpu.SemaphoreType.DMA((n_sem,)), pltpu.SemaphoreType.DMA((n_sem,)),
                        pltpu.SemaphoreType.DMA((n_a,))],
    )(*arrs)


def _gather_two_level(arrs, name):
    n_a = len(arrs)

    def body(*refs):
        srcs, outs = refs[:n_a], refs[n_a:2 * n_a]
        send_sems, recv_sems, local_sems = refs[2 * n_a:]
        x, y, c = lax.axis_index("x"), lax.axis_index("y"), lax.axis_index("c")
        me, sibling = (x, y, c), (x, y, 1 - c)
        chips = [(1 - x, y), (x, 1 - y), (1 - x, 1 - y)]

        def copy(k, j, block, to, src=None):
            slot = outs[k].at[4 * block[0] + 2 * block[1] + block[2]]
            return pltpu.make_async_remote_copy(
                src_ref=slot if src is None else src, dst_ref=slot,
                send_sem=send_sems.at[j * n_a + k], recv_sem=recv_sems.at[j * n_a + k],
                device_id=to, device_id_type=MESH)

        mine = [pltpu.make_async_copy(srcs[k], outs[k].at[4 * x + 2 * y + c], local_sems.at[k]) for k in range(n_a)]
        for cp in mine:
            cp.start()
        first = [copy(k, 1 + j, me, (*chip, c), srcs[k]) for j, chip in enumerate(chips) for k in range(n_a)]
        first += [copy(k, 0, me, sibling, srcs[k]) for k in range(n_a)]
        for cp in first:
            cp.start()
        passed = []
        for j, chip in enumerate(chips):
            for k in range(n_a):
                copy(k, 1 + j, (*chip, c), me).wait_recv()
                cp = copy(k, 4 + j, (*chip, c), sibling)
                cp.start()
                passed.append(cp)
        for k in range(n_a):
            copy(k, 0, sibling, me).wait_recv()
        for j, chip in enumerate(chips):
            for k in range(n_a):
                copy(k, 4 + j, (*chip, 1 - c), me).wait_recv()
        for cp in first + passed:
            cp.wait_send()
        for cp in mine:
            cp.wait()

    n_sem = (N_DEV - 1) * n_a
    return pl.pallas_call(
        body, name=name,
        out_shape=[jax.ShapeDtypeStruct((N_DEV,) + a.shape, a.dtype) for a in arrs],
        in_specs=[ANY] * n_a, out_specs=[ANY] * n_a,
        scratch_shapes=[pltpu.SemaphoreType.DMA((n_sem,)), pltpu.SemaphoreType.DMA((n_sem,)),
                        pltpu.SemaphoreType.DMA((n_a,))],
    )(*arrs)


def _swap_with_sibling(arrs, name):
    n_a = len(arrs)

    def body(*refs):
        srcs, outs = refs[:n_a], refs[n_a:2 * n_a]
        send_sems, recv_sems = refs[2 * n_a:]
        sibling = (lax.axis_index("x"), lax.axis_index("y"), 1 - lax.axis_index("c"))
        copies = [pltpu.make_async_remote_copy(src_ref=srcs[k], dst_ref=outs[k], send_sem=send_sems.at[k],
                                               recv_sem=recv_sems.at[k], device_id=sibling, device_id_type=MESH)
                  for k in range(n_a)]
        for cp in copies:
            cp.start()
        for cp in copies:
            cp.wait()

    return pl.pallas_call(
        body, name=name, out_shape=[jax.ShapeDtypeStruct(a.shape, a.dtype) for a in arrs],
        in_specs=[ANY] * n_a, out_specs=[ANY] * n_a,
        scratch_shapes=[pltpu.SemaphoreType.DMA((n_a,)), pltpu.SemaphoreType.DMA((n_a,))],
    )(*arrs)


def _scatter_over_chips(arrs, name):
    n_a = len(arrs)
    n_chip = N_DEV // 2

    def body(*refs):
        srcs, outs = refs[:n_a], refs[n_a:2 * n_a]
        send_sems, recv_sems, local_sems = refs[2 * n_a:]
        x, y, c = lax.axis_index("x"), lax.axis_index("y"), lax.axis_index("c")
        my_chip = 2 * x + y
        copies = []
        for k in range(n_a):
            local = pltpu.make_async_copy(srcs[k].at[4 * x + 2 * y + c], outs[k].at[my_chip], local_sems.at[k])
            local.start()
            copies.append(local)
        for r in range(n_chip - 1, 0, -1):
            px = 1 - x if r & 2 else x
            py = 1 - y if r & 1 else y
            for k in range(n_a):
                cp = pltpu.make_async_remote_copy(
                    src_ref=srcs[k].at[4 * px + 2 * py + c], dst_ref=outs[k].at[my_chip],
                    send_sem=send_sems.at[(r - 1) * n_a + k], recv_sem=recv_sems.at[(r - 1) * n_a + k],
                    device_id=(px, py, c), device_id_type=MESH)
                cp.start()
                copies.append(cp)
        for cp in copies:
            cp.wait()

    n_sem = (n_chip - 1) * n_a
    return pl.pallas_call(
        body, name=name,
        out_shape=[jax.ShapeDtypeStruct((n_chip,) + a.shape[1:], a.dtype) for a in arrs],
        in_specs=[ANY] * n_a, out_specs=[ANY] * n_a,
        scratch_shapes=[pltpu.SemaphoreType.DMA((n_sem,)), pltpu.SemaphoreType.DMA((n_sem,)),
                        pltpu.SemaphoreType.DMA((n_a,))],
    )(*arrs)


def _pair_sum(a, b, name):
    shape = a.shape
    a2, b2 = a.reshape(-1, shape[-1]), b.reshape(-1, shape[-1])
    rows, cols = a2.shape
    tr = _pick(rows, (256, 128, 64, 32, 16))

    def body(a_ref, b_ref, o_ref):
        o_ref[...] = (a_ref[...] + b_ref[...].astype(F32)).astype(BF16)

    spec = pl.BlockSpec((tr, cols), lambda i: (i, 0))
    return pl.pallas_call(
        body, name=name, grid=(rows // tr,), in_specs=[spec, spec], out_specs=spec,
        out_shape=jax.ShapeDtypeStruct((rows, cols), BF16), compiler_params=_params(("arbitrary",)),
    )(a2, b2).reshape(shape)


def _to_blob(parts):
    flat = [p.reshape(-1).astype(F32) for p in parts]
    offs, n = [], 0
    for f in flat:
        offs.append(n)
        n += f.shape[0]
    unit = 8 * LANES
    total = -(-n // unit) * unit
    if total > n:
        flat.append(jnp.zeros((total - n,), F32))
    return jnp.concatenate(flat).reshape(total // LANES, LANES), offs


def _sum_slots(buf, name):
    n, rows, cols = buf.shape
    tr = _pick(rows, (512, 256, 128, 64, 32, 16, 8))

    def body(b_ref, o_ref):
        acc = b_ref[0]
        for s in range(1, n):
            acc = acc + b_ref[s]
        o_ref[...] = acc

    return pl.pallas_call(
        body, name=name, grid=(rows // tr,),
        in_specs=[pl.BlockSpec((n, tr, cols), lambda i: (0, i, 0))],
        out_specs=pl.BlockSpec((tr, cols), lambda i: (i, 0)),
        out_shape=jax.ShapeDtypeStruct((rows, cols), F32),
        compiler_params=_params(("arbitrary",)),
    )(buf)


def _matmul(a, b, name, a_cbm=False, b_cbm=False, out_cbm=False, a_t=False, out_dtype=F32, tm=None, tn=None, tk=None):
    if a_cbm:
        m, k = a.shape[1], a.shape[0] * LANES
    elif a_t:
        k, m = a.shape
    else:
        m, k = a.shape
    n = b.shape[0] * LANES if b_cbm else b.shape[1]
    tm = tm or _pick(m, (1024, 768, 512, 384, 256, 128, 64))
    tn = tn or _pick(n, (1024, 512, 256, 128))
    tk = tk or _pick(k, (2048, 1024, 768, 512, 256, 128))
    nk = k // tk

    def load(ref, cbm):
        if not cbm:
            return ref[...].astype(BF16)
        return jnp.concatenate([ref[j].astype(BF16) for j in range(ref.shape[0])], axis=1)

    def body(a_ref, b_ref, o_ref, acc_ref):
        kk = pl.program_id(2)
        part = (_mm_tn if a_t else _mm)(load(a_ref, a_cbm), load(b_ref, b_cbm))

        @pl.when(kk == 0)
        def _():
            acc_ref[...] = part

        @pl.when(kk > 0)
        def _():
            acc_ref[...] += part

        @pl.when(kk == nk - 1)
        def _():
            r = acc_ref[...]
            if out_cbm:
                for j in range(tn // LANES):
                    o_ref[j] = r[:, j * LANES:(j + 1) * LANES].astype(out_dtype)
            else:
                o_ref[...] = r.astype(out_dtype)

    if a_cbm:
        a_spec = pl.BlockSpec((tk // LANES, tm, LANES), lambda j, i, kk: (kk, i, 0))
    elif a_t:
        a_spec = pl.BlockSpec((tk, tm), lambda j, i, kk: (kk, i))
    else:
        a_spec = pl.BlockSpec((tm, tk), lambda j, i, kk: (i, kk))
    if b_cbm:
        b_spec = pl.BlockSpec((tn // LANES, tk, LANES), lambda j, i, kk: (j, kk, 0))
    else:
        b_spec = pl.BlockSpec((tk, tn), lambda j, i, kk: (kk, j))
    if out_cbm:
        o_spec = pl.BlockSpec((tn // LANES, tm, LANES), lambda j, i, kk: (j, i, 0))
        o_shape = jax.ShapeDtypeStruct((n // LANES, m, LANES), out_dtype)
    else:
        o_spec = pl.BlockSpec((tm, tn), lambda j, i, kk: (i, j))
        o_shape = jax.ShapeDtypeStruct((m, n), out_dtype)
    return pl.pallas_call(
        body, name=name, grid=(n // tn, m // tm, nk),
        in_specs=[a_spec, b_spec], out_specs=o_spec, out_shape=o_shape,
        scratch_shapes=[pltpu.VMEM((tm, tn), F32)],
        compiler_params=_params(("arbitrary", "arbitrary", "arbitrary"), VMEM_BIG),
    )(a, b)


def _ada_fwd(cvec, ada_w, ada_b):
    def body(c_ref, w_ref, b_ref, o_ref):
        o_ref[...] = _mm(_silu(c_ref[...]), w_ref[...], HI) + b_ref[...]

    return pl.pallas_call(
        body, name="ada_fwd", out_shape=jax.ShapeDtypeStruct((8, ada_w.shape[1]), F32),
        compiler_params=_params(None, VMEM_BIG),
    )(cvec, ada_w, ada_b)


def _ada_bwd(cvec, ada_w, dmods):
    def body(c_ref, w_ref, d_ref, o_ref):
        (_, vj) = jax.vjp(_silu, c_ref[...])
        o_ref[...] = vj(_mm_nt(d_ref[...], w_ref[...], HI))[0]

    return pl.pallas_call(
        body, name="ada_bwd", out_shape=jax.ShapeDtypeStruct(cvec.shape, F32),
        compiler_params=_params(None, VMEM_BIG),
    )(cvec, ada_w, dmods)


def _ada_wgrad(conds, dm_lat, dm_ctx):
    d = conds.shape[1]
    cols = dm_lat.shape[1]

    def body(c_ref, dl_ref, dc_ref, o_ref):
        dctx = dc_ref[0:1]
        for s in range(1, N_DEV):
            dctx = dctx + dc_ref[s:s + 1]
        rhs = jnp.concatenate([dl_ref[...], dctx, jnp.zeros((7, cols), F32)], axis=0)
        o_ref[...] = _mm_tn(_silu(c_ref[...]), rhs, HI)

    return pl.pallas_call(
        body, name="ada_wgrad", out_shape=jax.ShapeDtypeStruct((d, cols), F32),
    )(conds, dm_lat, dm_ctx)


def _prenorm_tile(xt, nw, sc, sh):
    r = lax.rsqrt(jnp.mean(xt * xt, axis=-1, keepdims=True) + NORM_EPS)
    return (xt * r * nw) * (1.0 + sc) + sh


def _tok_specs(l_lat, l_ctx, d, tm):
    nl, nc = l_lat // tm, l_ctx // tm
    lat = pl.BlockSpec((tm, d), lambda i: (jnp.minimum(i, nl - 1), 0))
    ctx = pl.BlockSpec((tm, d), lambda i: (jnp.maximum(i - nl, 0), 0))
    return nl, nc, lat, ctx


def _prenorm_fwd(x, ctx, norm_w, mods):
    l_lat, d = x.shape
    l_ctx = ctx.shape[0]
    tm = _pick(l_ctx, (256, 128, 64))
    nl, nc, lat_spec, ctx_spec = _tok_specs(l_lat, l_ctx, d, tm)

    def body(x_ref, c_ref, nw_ref, m_ref, h_ref):
        is_lat = pl.program_id(0) < nl
        xt = jnp.where(is_lat, x_ref[...], c_ref[...])
        row = jnp.where(is_lat, m_ref[0:1, :], m_ref[1:2, :])
        h_ref[...] = _prenorm_tile(xt, nw_ref[...], row[:, d:2 * d], row[:, 0:d]).astype(BF16)

    return pl.pallas_call(
        body, name="prenorm_fwd", grid=(nl + nc,),
        in_specs=[lat_spec, ctx_spec, pl.BlockSpec((1, d), lambda i: (0, 0)), pl.BlockSpec((8, 3 * d), lambda i: (0, 0))],
        out_specs=pl.BlockSpec((tm, d), lambda i: (i, 0)),
        out_shape=jax.ShapeDtypeStruct((l_lat + l_ctx, d), BF16),
        compiler_params=_params(("arbitrary",)),
    )(x, ctx, norm_w, mods)


def _prenorm_bwd(x, ctx, norm_w, mods, dh, dx_res):
    l_lat, d = x.shape
    l_ctx = ctx.shape[0]
    tm = _pick(l_ctx, (256, 128, 64))
    nl, nc, lat_spec, ctx_spec = _tok_specs(l_lat, l_ctx, d, tm)

    def body(x_ref, c_ref, nw_ref, m_ref, dh_ref, dr_ref, gx_ref, dnw_ref, dm_ref):
        i = pl.program_id(0)
        is_lat = i < nl

        @pl.when(i == 0)
        def _():
            dnw_ref[...] = jnp.zeros_like(dnw_ref)
            dm_ref[...] = jnp.zeros_like(dm_ref)

        xt = jnp.where(is_lat, x_ref[...], c_ref[...])
        row = jnp.where(is_lat, m_ref[0:1, :], m_ref[1:2, :])
        _, vj = jax.vjp(_prenorm_tile, xt, nw_ref[...], row[:, d:2 * d], row[:, 0:d])
        dxt, dnw, dsc, dsh = vj(dh_ref[...])
        dnw_ref[...] += dnw
        upd = jnp.concatenate([dsh, dsc, jnp.zeros_like(dsh)], axis=1)

        @pl.when(is_lat)
        def _():
            gx_ref[...] = dr_ref[...] + dxt
            dm_ref[0:1, :] += upd

        @pl.when(jnp.logical_not(is_lat))
        def _():
            dm_ref[1:2, :] += upd

    return pl.pallas_call(
        body, name="prenorm_bwd", grid=(nl + nc,),
        in_specs=[lat_spec, ctx_spec, pl.BlockSpec((1, d), lambda i: (0, 0)), pl.BlockSpec((8, 3 * d), lambda i: (0, 0)),
                  pl.BlockSpec((tm, d), lambda i: (i, 0)), lat_spec],
        out_specs=[lat_spec, pl.BlockSpec((1, d), lambda i: (0, 0)), pl.BlockSpec((8, 3 * d), lambda i: (0, 0))],
        out_shape=[jax.ShapeDtypeStruct((l_lat, d), F32), jax.ShapeDtypeStruct((1, d), F32),
                   jax.ShapeDtypeStruct((8, 3 * d), F32)],
        compiler_params=_params(("arbitrary",)),
    )(x, ctx, norm_w, mods, dh, dx_res)


def _conv_call(src, wts, l_lat, first_block, n_blocks, name, dout=None, passthru=None, into=None):
    t = src.shape[1]
    l_ctx = t - l_lat
    rc = _pick(l_lat, (512, 256, 128, 64))
    n_rc = l_lat // rc
    bwd = dout is not None

    def shifted(xc, n, period):
        pos = lax.broadcasted_iota(jnp.int32, (n, LANES), 0) % period
        xm = jnp.where(pos == 0, 0.0, pltpu.roll(xc, 1, 0))
        xp = jnp.where(pos == period - 1, 0.0, pltpu.roll(xc, n - 1, 0))
        return xm, xc, xp

    def fill(ref, sm, s0, sp):
        zero = jnp.zeros((GRID_W, LANES), F32)
        for s in (sm, s0, sp):
            s[pl.ds(0, GRID_W), :] = zero
            s[pl.ds(l_lat + GRID_W, GRID_W), :] = zero

        def step(i, carry):
            st = pl.multiple_of(i * rc, rc)
            xm, x0, xp = shifted(ref[0, pl.ds(st, rc), :], rc, GRID_W)
            sm[pl.ds(st + GRID_W, rc), :] = xm
            s0[pl.ds(st + GRID_W, rc), :] = x0
            sp[pl.ds(st + GRID_W, rc), :] = xp
            return carry

        lax.fori_loop(0, n_rc, step, 0)

    def apply(w, sm, s0, sp, out_ref, flip):
        def step(i, carry):
            st = pl.multiple_of(i * rc, rc)
            acc = jnp.zeros((rc, LANES), F32)
            for di in range(3):
                for dj, s in enumerate((sm, s0, sp)):
                    kidx = (2 - di) * 3 + (2 - dj) if flip else di * 3 + dj
                    acc = acc + w[kidx:kidx + 1, :] * s[pl.ds(st + di * GRID_W, rc), :]
            out_ref[0, pl.ds(st, rc), :] = acc
            return carry

        lax.fori_loop(0, n_rc, step, 0)

    def ctx_apply(w, xc, flip):
        xm, x0, xp = shifted(xc, l_ctx, l_ctx)
        ks = (5, 4, 3) if flip else (3, 4, 5)
        return w[ks[0]:ks[0] + 1, :] * xm + w[4:5, :] * x0 + w[ks[2]:ks[2] + 1, :] * xp

    def fwd_body(x_ref, w_ref, o_ref, sm, s0, sp):
        w = w_ref[0]
        fill(x_ref, sm, s0, sp)
        apply(w, sm, s0, sp, o_ref, False)
        o_ref[0, pl.ds(l_lat, l_ctx), :] = ctx_apply(w, x_ref[0, pl.ds(l_lat, l_ctx), :], False)

    n_pass = 0 if passthru is None else passthru.shape[0]

    def bwd_body(x_ref, w_ref, d_ref, *rest):
        p_ref = rest[0] if n_pass else None
        dx_ref, dw_ref, sm, s0, sp = rest[-5:]

        @pl.when(pl.program_id(0) < n_blocks)
        def _():
            bwd_block(x_ref, w_ref, d_ref, dx_ref, dw_ref, sm, s0, sp)

        if n_pass:
            @pl.when(pl.program_id(0) >= n_blocks)
            def _():
                dx_ref[0] = p_ref[0]

    def bwd_block(x_ref, w_ref, d_ref, dx_ref, dw_ref, sm, s0, sp):
        w = w_ref[0]
        fill(d_ref, sm, s0, sp)

        def step(i, acc):
            st = pl.multiple_of(i * rc, rc)
            xc = x_ref[0, pl.ds(st, rc), :]
            dx = jnp.zeros((rc, LANES), F32)
            rows = [None] * 9
            for di in range(3):
                for dj, s in enumerate((sm, s0, sp)):
                    kidx = (2 - di) * 3 + (2 - dj)
                    val = s[pl.ds(st + di * GRID_W, rc), :]
                    dx = dx + w[kidx:kidx + 1, :] * val
                    rows[kidx] = jnp.sum(xc * val, axis=0, keepdims=True)
            dx_ref[0, pl.ds(st, rc), :] = dx
            return acc + jnp.concatenate(rows + [jnp.zeros((7, LANES), F32)], axis=0)

        acc = lax.fori_loop(0, n_rc, step, jnp.zeros((16, LANES), F32))
        dctx = d_ref[0, pl.ds(l_lat, l_ctx), :]
        xctx = x_ref[0, pl.ds(l_lat, l_ctx), :]
        dm, d0, dp = shifted(dctx, l_ctx, l_ctx)
        crow = [jnp.sum(xctx * s, axis=0, keepdims=True) for s in (dp, d0, dm)]
        acc = acc + jnp.concatenate([jnp.zeros((3, LANES), F32)] + crow + [jnp.zeros((10, LANES), F32)], axis=0)
        dw_ref[0] = acc
        dx_ref[0, pl.ds(l_lat, l_ctx), :] = ctx_apply(w, dctx, True)

    blk = lambda off: pl.BlockSpec((1, t, LANES), lambda j: (j + off, 0, 0))
    w_spec = pl.BlockSpec((1, 16, LANES), lambda j: (j, 0, 0))
    scratch = [pltpu.VMEM((l_lat + 2 * GRID_W, LANES), F32)] * 3
    out_t = jax.ShapeDtypeStruct((n_blocks, t, LANES), F32)
    if not bwd:
        return pl.pallas_call(
            fwd_body, name=name, grid=(n_blocks,), in_specs=[blk(first_block), w_spec], out_specs=blk(0),
            out_shape=out_t, scratch_shapes=scratch, compiler_params=_params(("arbitrary",), VMEM_BIG),
        )(src, wts)
    last = n_blocks - 1
    clamped = lambda off: pl.BlockSpec((1, t, LANES), lambda j: (jnp.minimum(j, last) + off, 0, 0))
    w_clamped = pl.BlockSpec((1, 16, LANES), lambda j: (jnp.minimum(j, last), 0, 0))
    in_specs, args, aliases = [clamped(first_block), w_clamped, clamped(0)], [src, wts, dout], {}
    if n_pass:
        in_specs.append(pl.BlockSpec((1, t, LANES), lambda j: (jnp.maximum(j - n_blocks, 0), 0, 0)))
        args.append(passthru)
    first_out = 0
    if into is not None:
        aliases[len(args)] = 0
        in_specs.append(ANY)
        args.append(into[0])
        first_out = into[1]
        out_t = jax.ShapeDtypeStruct(into[0].shape, F32)
    else:
        out_t = jax.ShapeDtypeStruct((n_blocks + n_pass, t, LANES), F32)
    return pl.pallas_call(
        bwd_body, name=name, grid=(n_blocks + n_pass,), in_specs=in_specs,
        out_specs=[pl.BlockSpec((1, t, LANES), lambda j: (j + first_out, 0, 0)), w_clamped],
        out_shape=[out_t, jax.ShapeDtypeStruct((n_blocks, 16, LANES), F32)],
        scratch_shapes=scratch, input_output_aliases=aliases, compiler_params=_params(("arbitrary",), VMEM_BIG),
    )(*args)


def _scan_consts(kind, g, h, rev, d_index):
    n = g * CHUNK
    i = np.arange(n, dtype=np.int32)
    head, pos = i // CHUNK, i % CHUNK
    p = (CHUNK - 1 - pos) if rev else pos
    same_head = head[:, None] == head[None, :]
    pr, pc = p[:, None], p[None, :]
    f = lambda m: np.ascontiguousarray(m, dtype=np.float32)
    incl = same_head & (pc <= pr)
    out = {"eye": f(i[:, None] == i[None, :]), "incl": f(incl), "incl_t": f(incl.T)}
    if kind == "hg":
        mid, same, sec = [], [], []
        for half in HALVES:
            width = 2 * half
            blk = p // width
            second = (p % width) >= half
            ref_pos = blk * width + half - 1
            mid.append(same_head & (pc <= ref_pos[:, None]))
            same.append(same_head & (blk[:, None] == blk[None, :]))
            sec.append(np.broadcast_to(second[:, None], (n, LANES)))
        mid = np.concatenate(mid, axis=0)
        out.update(mid=f(mid), mid_t=f(mid.T), same=f(np.stack(same)), sec=f(np.stack(sec)))
    else:
        low = []
        for half in reversed(HALVES):
            width = 2 * half
            blk = p // width
            second = (p % width) >= half
            low.append(same_head & (blk[:, None] == blk[None, :]) & second[:, None] & np.logical_not(second[None, :]))
        lane = np.arange(LANES, dtype=np.int32)[None, :]
        hh = np.arange(h * CHUNK, dtype=np.int32) // CHUNK
        out.update(strict=f(same_head & (pc < pr)), low=f(np.stack(low)),
                   oh_a=f(lane == (d_index * h + hh)[:, None]), oh_b=f(lane == (2 * h + d_index * h + hh)[:, None]))
    return out


def _mm_01(c, x):
    hi = x.astype(jnp.bfloat16).astype(F32)
    return _mm(c, hi) + _mm(c, x - hi)


def _const_mm(c, c_t, diff, one_pass=False):
    fwd = (lambda x: _mm(c, x)) if one_pass else (lambda x: _mm_01(c, x))
    if not diff:
        return fwd

    f = jax.custom_vjp(fwd)
    f.defvjp(lambda x: (fwd(x), None), lambda _, ct: (_mm_01(c_t, ct),))
    return f


def _kept_inverse(x_kept):
    @jax.custom_vjp
    def f(a):
        return x_kept

    f.defvjp(lambda a: (x_kept, None), lambda _, ct: (-_mm_nt(_mm_tn(x_kept, ct, HIGH), x_kept, HIGH),))
    return f


def _stack_helpers(g):
    n = g * CHUNK
    rows = lambda vec: jnp.broadcast_to(vec, (g, CHUNK, LANES)).reshape(n, LANES)
    per_head = lambda t: [t[i * CHUNK:(i + 1) * CHUNK] for i in range(g)]
    head_sum = lambda t: rows(jnp.sum(t.reshape(g, CHUNK, LANES), axis=1, keepdims=True))
    return rows, per_head, head_sum


def _lockstep(gens):
    results = [None] * len(gens)
    live = list(range(len(gens)))
    while live:
        for i in list(live):
            try:
                next(gens[i])
            except StopIteration as done:
                results[i] = done.value
                live.remove(i)
    return results


def _hg_chunk(raw_q, raw_f, raw_i, l0, l1, state, cst, g, diff=False):
    n = g * CHUNK
    rows, per_head, head_sum = _stack_helpers(g)
    lb = rows(jax.nn.sigmoid(l0 - l1))
    q = _silu(raw_q) * (LANES ** -0.5)
    gl = jnp.log(lb + (1.0 - lb) * jax.nn.sigmoid(raw_f))
    k = (1.0 - lb) * jax.nn.sigmoid(-raw_f)
    v = raw_i
    b = _const_mm(cst["incl"], cst["incl_t"], diff)(gl)
    b_tot = head_sum(gl)
    q_dec = q * jnp.exp(b)
    k_dec = k * jnp.exp(b_tot - b)
    mids = _const_mm(cst["mid"], cst["mid_t"], diff, True)(gl)
    yield
    a = cst["eye"] * jnp.sum(q * k, axis=1, keepdims=True)
    for lv in range(len(HALVES)):
        r = mids[lv * n:(lv + 1) * n]
        sec = cst["sec"][lv]
        fst = 1.0 - sec
        qt = q * jnp.exp((b - r) * sec) * sec
        kt = k * jnp.exp((r - b) * fst) * fst
        a = a + _mm_nt(qt, kt) * cst["same"][lv]
        yield
    o_intra = per_head(_mm(a, v))
    decay = per_head(jnp.exp(b_tot))
    qd, kd, vs = per_head(q_dec), per_head(k_dec), per_head(v)
    sts = state()
    while sts is None:
        yield
        sts = state()
    outs = [o_intra[i] + _mm_nt(qd[i], sts[i]) for i in range(g)]
    new = [sts[i] * decay[i][0:1] + _mm_tn(vs[i], kd[i]) for i in range(g)]
    return jnp.concatenate(outs, axis=0), new, sts


def _gdn_chunk(u_q, u_k, u_v, tail, avec, dtvec, state, cst, g, x_kept=None):
    n = g * CHUNK
    diff = x_kept is not None
    rows, per_head, head_sum = _stack_helpers(g)

    def l2n(t):
        return t * lax.rsqrt(jnp.sum(t * t, axis=-1, keepdims=True) + NORM_EPS)

    q = l2n(_silu(u_q)) * (LANES ** -0.5)
    k = l2n(_silu(u_k))
    v = _silu(u_v)
    tail_n = jnp.concatenate([tail] * g, axis=0)
    za = jnp.sum(tail_n * cst["oh_a"], axis=1, keepdims=True) + rows(dtvec)
    beta = jax.nn.sigmoid(jnp.sum(tail_n * cst["oh_b"], axis=1, keepdims=True))
    gl = -jnp.exp(rows(avec)) * (jnp.maximum(za, 0.0) + jnp.log1p(jnp.exp(-jnp.abs(za))))
    b = _const_mm(cst["incl"], cst["incl_t"], diff)(gl)
    kk = _mm_nt(k, k)
    qk = _mm_nt(q, k)
    yield
    b_tot = head_sum(gl)
    bb = jnp.concatenate([b] * (n // LANES), axis=1)
    bdiff = bb - bb.T
    a_mat = beta * kk * jnp.exp(jnp.where(cst["strict"] > 0.5, bdiff, NEG))
    if diff:
        xinv = _kept_inverse(x_kept)(a_mat)
    else:
        xinv = cst["eye"] - a_mat * cst["low"][0]
        for lv in range(1, len(HALVES)):
            left = _mm(xinv, a_mat * cst["low"][lv])
            yield
            xinv = xinv - _mm(left, xinv)
            yield
        resid = cst["eye"] - _mm(cst["eye"] + a_mat, xinv, HIGH)
        yield
        xinv = xinv + _mm(xinv, resid, HIGH)
        yield
    rhs = jnp.concatenate([beta * v, beta * jnp.exp(b) * k], axis=1)
    sol = _mm(xinv, rhs, HIGH)
    yield
    u0, w = per_head(sol[:, :LANES]), per_head(sol[:, LANES:])
    k_dec = per_head(k * jnp.exp(b_tot - b))
    q_dec = per_head(q * jnp.exp(b))
    decay = per_head(jnp.exp(b_tot))
    p = qk * jnp.exp(jnp.where(cst["incl"] > 0.5, bdiff, NEG))
    ss = state()
    while ss is None:
        yield
        ss = state()
    v_new = [u0[i] - _mm(w[i], ss[i]) for i in range(g)]
    o_state = [_mm(q_dec[i], ss[i]) for i in range(g)]
    yield
    o = _mm(p, jnp.concatenate(v_new, axis=0)) + jnp.concatenate(o_state, axis=0)
    new = [decay[i][0:1] * ss[i] + _mm_tn(k_dec[i], v_new[i]) for i in range(g)]
    return o, new, xinv, ss


def _chunk_index(l_lat, l_ctx, rev, sc):
    rows = sc * CHUNK
    nl, nc = l_lat // rows, l_ctx // rows

    def idx(i):
        if rev:
            return jnp.where(i < nc, nl + nc - 1 - i, nl - 1 - (i - nc))
        return jnp.where(i < nc, nl + i, i - nc)

    subs = list(range(sc))
    return nl + nc, idx, (subs[::-1] if rev else subs)


def _const_args(kind, g, h, rev, d_index):
    consts = _scan_consts(kind, g, h, rev, d_index)
    names = sorted(consts)
    arrs = [jnp.asarray(consts[k]) for k in names]
    specs = [pl.BlockSpec(a.shape, functools.partial(lambda nd, i: (0,) * nd, a.ndim)) for a in arrs]
    return names, arrs, specs


def _load_consts(names, refs, g0, g):
    cst = {}
    for k, r in zip(names, refs):
        if k in ("oh_a", "oh_b"):
            cst[k] = r[pl.ds(pl.multiple_of(g0 * CHUNK, CHUNK), g * CHUNK), :]
        else:
            cst[k] = r[...]
    return cst


def _scan_fwd(kind, rev, cbm, segs, vecs, o_prev, l_lat, l_ctx, h, name, tail=None, d_index=0):
    sc = FWD_STEP
    n_steps, cidx, subs = _chunk_index(l_lat, l_ctx, rev, sc)
    rows = sc * CHUNK
    t = l_lat + l_ctx
    n_in = len(segs)
    gdn = kind == "gdn"
    g = min(HEADS_PER_GROUP, h)
    n = g * CHUNK
    c_names, c_arrs, c_specs = _const_args(kind, g, h, rev, d_index)

    def body(*refs):
        in_refs = refs[:n_in]
        pos = n_in
        tail_ref = None
        if gdn:
            tail_ref = refs[pos]
            pos += 1
        v0_ref, v1_ref = refs[pos], refs[pos + 1]
        pos += 2
        prev_ref = None
        if o_prev is not None:
            prev_ref = refs[pos]
            pos += 1
        c_refs = refs[pos:pos + len(c_names)]
        pos += len(c_names)
        o_ref, st_ref = refs[pos], refs[pos + 1]
        pos += 2
        x_ref = None
        if gdn:
            x_ref = refs[pos]
            pos += 1
        s_scr = refs[pos]

        @pl.when(pl.program_id(0) == 0)
        def _():
            s_scr[...] = jnp.zeros_like(s_scr)

        def instance(k, sub, gi, before, after):
            rs = pl.ds(sub * CHUNK, CHUNK)
            g0 = gi * g
            hs = pl.ds(g0, g)
            cst = _load_consts(c_names, c_refs, g0, g)
            state = (lambda: [s_scr[g0 + i] for i in range(g)]) if before is None else (lambda: before[0])
            ins = [r[hs, rs, :].reshape(n, LANES) for r in in_refs]
            if gdn:
                o, new, xinv, old = yield from _gdn_chunk(*ins, tail_ref[0, rs, :], v0_ref[hs], v1_ref[hs], state, cst, g)
                x_ref[k, gi] = xinv
            else:
                o, new, old = yield from _hg_chunk(*ins, v0_ref[hs], v1_ref[hs], state, cst, g)
            after[0] = new
            for i in range(g):
                st_ref[k, g0 + i] = old[i]
            o = o.reshape(g, CHUNK, LANES)
            if prev_ref is not None:
                o = o + prev_ref[hs, rs, :]
            o_ref[hs, rs, :] = o
            if k == sc - 1:
                for i in range(g):
                    s_scr[g0 + i] = new[i]

        gens = []
        handed = [None] * (h // g)
        for k, sub in enumerate(subs):
            for gi in range(h // g):
                after = [None]
                gens.append(instance(k, sub, gi, handed[gi], after))
                handed[gi] = after
        _lockstep(gens)

    seg_spec = lambda sg: pl.BlockSpec((h, rows, LANES), lambda i: (sg, cidx(i), 0))
    vec_spec = pl.BlockSpec((h, 1, LANES), lambda i: (0, 0, 0))
    in_specs = [seg_spec(sg) for sg in segs]
    args = [cbm] * n_in
    if gdn:
        in_specs.append(pl.BlockSpec((1, rows, LANES), lambda i: (0, cidx(i), 0)))
        args.append(tail)
    in_specs += [vec_spec, vec_spec]
    args += list(vecs)
    if o_prev is not None:
        in_specs.append(seg_spec(0))
        args.append(o_prev)
    in_specs += c_specs
    args += c_arrs
    return pl.pallas_call(
        body, name=name, grid=(n_steps,), in_specs=in_specs,
        out_specs=[seg_spec(0), pl.BlockSpec((sc, h, LANES, LANES), lambda i: (i, 0, 0, 0))]
        + ([pl.BlockSpec((sc, h // g, n, n), lambda i: (i, 0, 0, 0))] if gdn else []),
        out_shape=[jax.ShapeDtypeStruct((h, t, LANES), F32),
                   jax.ShapeDtypeStruct((n_steps * sc, h, LANES, LANES), F32)]
        + ([jax.ShapeDtypeStruct((n_steps * sc, h // g, n, n), F32)] if gdn else []),
        scratch_shapes=[pltpu.VMEM((h, LANES, LANES), F32)],
        compiler_params=_params(("arbitrary",), VMEM_BIG),
    )(*args)


def _scan_bwd(kind, rev, cbm, segs, vecs, states, d_o, acc, l_lat, l_ctx, h, name, tail=None, d_index=0, x_kept=None,
              pack=None, pack_into=None, tail_into=None):
    sc = BWD_STEP
    n_steps, cidx, subs = _chunk_index(l_lat, l_ctx, rev, sc)
    rows = sc * CHUNK
    t = l_lat + l_ctx
    n_in = len(segs)
    gdn = kind == "gdn"
    n_grad = n_in + (1 if gdn else 0)
    assert len(acc) == n_grad
    step_of = lambda j: n_steps - 1 - j
    g = min(HEADS_PER_GROUP, h)
    n = g * CHUNK
    c_names, c_arrs, c_specs = _const_args(kind, g, h, rev, d_index)
    items = pack or []
    packed_at = {it[1]: p for p, it in enumerate(items) if it[0] == "grad"}
    copies = [(p, it[1]) for p, it in enumerate(items) if it[0] == "copy"]
    separate = [m for m in range(n_in) if m not in packed_at]
    in_place = pack_into if pack_into is not None else tail_into
    assert pack_into is None or tail_into is None
    tail_blocks = tail_into[2] if tail_into is not None else 1

    def body(*refs):
        in_refs = refs[:n_in]
        pos = n_in
        tail_ref = None
        if gdn:
            tail_ref = refs[pos]
            pos += 1
        v0_ref, v1_ref, st_ref, do_ref = refs[pos:pos + 4]
        pos += 4
        xk_ref = None
        if gdn:
            xk_ref = refs[pos]
            pos += 1
        acc_refs = []
        for a in acc:
            if a is None:
                acc_refs.append(None)
            else:
                acc_refs.append(refs[pos])
                pos += 1
        copy_refs = refs[pos:pos + len(copies)]
        pos += len(copies) + (1 if in_place is not None else 0)
        c_refs = refs[pos:pos + len(c_names)]
        pos += len(c_names)
        sep_refs = refs[pos:pos + len(separate)]
        pos += len(separate)
        pack_ref = None
        if items:
            pack_ref = refs[pos]
            pos += 1
        tail_out = None
        if gdn:
            tail_out = refs[pos]
            pos += 1
        dv0_ref, dv1_ref, ds_scr = refs[pos:pos + 3]

        @pl.when(pl.program_id(0) == 0)
        def _():
            ds_scr[...] = jnp.zeros_like(ds_scr)
            dv0_ref[...] = jnp.zeros_like(dv0_ref)
            dv1_ref[...] = jnp.zeros_like(dv1_ref)

        if gdn:
            tail_out[0] = jnp.zeros((rows, LANES), F32) if acc_refs[n_in] is None else acc_refs[n_in][0]
            for q in range(1, tail_blocks):
                tail_out[q] = jnp.zeros((rows, LANES), F32)

        n_groups = h // g

        rss = [pl.ds(subs[k] * CHUNK, CHUNK) for k in range(sc)]
        hss = [pl.ds(gi * g, g) for gi in range(n_groups)]
        csts = [_load_consts(c_names, c_refs, gi * g, g) for gi in range(n_groups)]
        xks = [[xk_ref[k, gi] for gi in range(n_groups)] for k in range(sc)] if gdn else None

        def link(k, gi, ins, tl, v0, v1, state, after):
            if gdn:
                res = yield from _gdn_chunk(*ins, tl, v0, v1, state, csts[gi], g, xks[k][gi])
            else:
                res = yield from _hg_chunk(*ins, v0, v1, state, csts[gi], g, True)
            after[0] = res[1]
            return res[0]

        def fn(ins, tls, v0s, v1s, s0):
            gens = []
            handed = [None] * n_groups
            for k in range(sc):
                for gi in range(n_groups):
                    before, after = handed[gi], [None]
                    state = (functools.partial(lambda s: s, s0[gi]) if before is None
                             else functools.partial(lambda cell: cell[0], before))
                    gens.append(link(k, gi, ins[k][gi], tls[k], v0s[gi], v1s[gi], state, after))
                    handed[gi] = after
            outs = _lockstep(gens)
            return ([[outs[k * n_groups + gi] for gi in range(n_groups)] for k in range(sc)],
                    [handed[gi][0] for gi in range(n_groups)])

        ins = [[[r[hs, rs, :].reshape(n, LANES) for r in in_refs] for hs in hss] for rs in rss]
        tls = [tail_ref[0, rs, :] if gdn else jnp.zeros((1, 1), F32) for rs in rss]
        s0 = [[st_ref[0, gi * g + i] for i in range(g)] for gi in range(n_groups)]
        _, vj = jax.vjp(fn, ins, tls, [v0_ref[hs] for hs in hss], [v1_ref[hs] for hs in hss], s0)
        d_o_all = [[do_ref[hs, rs, :].reshape(n, LANES) for hs in hss] for rs in rss]
        d_fin = [[ds_scr[gi * g + i] for i in range(g)] for gi in range(n_groups)]
        d_ins, d_tls, d_v0s, d_v1s, d_s0 = vj((d_o_all, d_fin))
        for gi, hs in enumerate(hss):
            g0 = gi * g
            for k, rs in enumerate(rss):
                for m in range(n_in):
                    gm = d_ins[k][gi][m].reshape(g, CHUNK, LANES)
                    if acc_refs[m] is not None:
                        gm = gm + acc_refs[m][hs, rs, :]
                    if m in packed_at:
                        pack_ref[pl.ds(packed_at[m] * h + g0, g), rs, :] = gm
                    else:
                        sep_refs[separate.index(m)][hs, rs, :] = gm
                for (p, _), c_ref in zip(copies, copy_refs):
                    pack_ref[pl.ds(p * h + g0, g), rs, :] = c_ref[hs, rs, :]
            dv0, dv1 = d_v0s[gi], d_v1s[gi]
            if gdn:
                dv0 = jnp.broadcast_to(jnp.sum(dv0, axis=2, keepdims=True), dv0.shape)
                dv1 = jnp.broadcast_to(jnp.sum(dv1, axis=2, keepdims=True), dv1.shape)
            dv0_ref[hs] += dv0
            dv1_ref[hs] += dv1
            for i in range(g):
                ds_scr[g0 + i] = d_s0[gi][i]
        if gdn:
            for k, rs in enumerate(rss):
                tail_out[0, rs, :] += d_tls[k]

    seg_spec = lambda sg: pl.BlockSpec((h, rows, LANES), lambda j: (sg, cidx(step_of(j)), 0))
    tail_spec = pl.BlockSpec((1, rows, LANES), lambda j: (0, cidx(step_of(j)), 0))
    vec_spec = pl.BlockSpec((h, 1, LANES), lambda j: (0, 0, 0))
    in_specs = [seg_spec(sg) for sg in segs]
    args = [cbm] * n_in
    if gdn:
        in_specs.append(tail_spec)
        args.append(tail)
    in_specs += [vec_spec, vec_spec, pl.BlockSpec((sc, h, LANES, LANES), lambda j: (step_of(j), 0, 0, 0)),
                 seg_spec(0)]
    args += list(vecs) + [states, d_o]
    if gdn:
        in_specs.append(pl.BlockSpec((sc, h // g, n, n), lambda j: (step_of(j), 0, 0, 0)))
        args.append(x_kept)
    for k, a in enumerate(acc):
        if a is not None:
            in_specs.append(tail_spec if (gdn and k == n_in) else seg_spec(0))
            args.append(a)
    for _, arr in copies:
        in_specs.append(seg_spec(0))
        args.append(arr)
    aliases = {}
    if in_place is not None:
        aliases[len(args)] = len(separate) + (1 if (items and tail_into is not None) else 0)
        in_specs.append(ANY)
        args.append(in_place[0])
    in_specs += c_specs
    args += c_arrs
    out_specs = [seg_spec(0)] * len(separate)
    out_shape = [jax.ShapeDtypeStruct((h, t, LANES), F32)] * len(separate)
    if items:
        pb = len(items) * h
        first = pack_into[1] if pack_into is not None else 0
        assert first % pb == 0
        out_specs.append(pl.BlockSpec((pb, rows, LANES), lambda j: (first // pb, cidx(step_of(j)), 0)))
        out_shape.append(jax.ShapeDtypeStruct(pack_into[0].shape if pack_into is not None else (pb, t, LANES), F32))
    if gdn and tail_into is not None:
        first_t = tail_into[1]
        assert first_t % tail_blocks == 0
        out_specs.append(pl.BlockSpec((tail_blocks, rows, LANES), lambda j: (first_t // tail_blocks, cidx(step_of(j)), 0)))
        out_shape.append(jax.ShapeDtypeStruct(tail_into[0].shape, F32))
    elif gdn:
        out_specs.append(tail_spec)
        out_shape.append(jax.ShapeDtypeStruct((1, t, LANES), F32))
    out_specs += [vec_spec, vec_spec]
    out_shape += [jax.ShapeDtypeStruct((h, 1, LANES), F32)] * 2
    return pl.pallas_call(
        body, name=name, grid=(n_steps,), in_specs=in_specs, out_specs=out_specs, out_shape=out_shape,
        scratch_shapes=[pltpu.VMEM((h, LANES, LANES), F32)], input_output_aliases=aliases,
        compiler_params=_params(("arbitrary",), VMEM_BIG),
    )(*args)


def _gout_tile(oa, ob, za, zb, naw, nbw):
    def hn(o, w):
        return o * lax.rsqrt(jnp.mean(o * o, axis=-1, keepdims=True) + NORM_EPS) * w

    ya = _silu(za) * hn(oa, naw)
    yb = _silu(zb) * hn(ob, nbw)
    nh = oa.shape[0]
    return jnp.concatenate([ya[i] for i in range(nh)] + [yb[i] for i in range(nh)], axis=1)


def _gout_fwd(oa, ob, y_cbm, naw, nbw, l_lat, h):
    tm = _pick(l_lat, (256, 128, 64))
    blk = lambda sg: pl.BlockSpec((h, tm, LANES), lambda i: (sg, i, 0))
    vec = pl.BlockSpec((h, 1, LANES), lambda i: (0, 0, 0))

    def body(oa_ref, ob_ref, za_ref, zb_ref, na_ref, nb_ref, y_ref):
        y_ref[...] = _gout_tile(oa_ref[...], ob_ref[...], za_ref[...], zb_ref[...], na_ref[...], nb_ref[...]).astype(BF16)

    return pl.pallas_call(
        body, name="gout_fwd", grid=(l_lat // tm,),
        in_specs=[blk(0), blk(0), blk(4), blk(8), vec, vec],
        out_specs=pl.BlockSpec((tm, 2 * h * LANES), lambda i: (i, 0)),
        out_shape=jax.ShapeDtypeStruct((l_lat, 2 * h * LANES), BF16),
        compiler_params=_params(("arbitrary",)),
    )(oa, ob, y_cbm, y_cbm, naw, nbw)


def _gout_bwd(oa, ob, y_cbm, naw, nbw, dymix, l_lat, l_ctx, h):
    tm = _pick(l_ctx, (256, 128, 64))
    nl, nc = l_lat // tm, l_ctx // tm
    t = l_lat + l_ctx
    blk_in = lambda sg: pl.BlockSpec((h, tm, LANES), lambda i: (sg, jnp.minimum(i, nl - 1), 0))
    blk_out = pl.BlockSpec((h, tm, LANES), lambda i: (0, i, 0))
    vec = pl.BlockSpec((h, 1, LANES), lambda i: (0, 0, 0))

    def body(oa_ref, ob_ref, za_ref, zb_ref, na_ref, nb_ref, dy_ref, doa_ref, dob_ref, dza_ref, dzb_ref, dna_ref, dnb_ref):
        i = pl.program_id(0)

        @pl.when(i == 0)
        def _():
            dna_ref[...] = jnp.zeros_like(dna_ref)
            dnb_ref[...] = jnp.zeros_like(dnb_ref)

        @pl.when(i < nl)
        def _():
            _, vj = jax.vjp(_gout_tile, oa_ref[...], ob_ref[...], za_ref[...], zb_ref[...], na_ref[...], nb_ref[...])
            doa, dob, dza, dzb, dna, dnb = vj(dy_ref[...])
            doa_ref[...] = doa
            dob_ref[...] = dob
            dza_ref[...] = dza
            dzb_ref[...] = dzb
            dna_ref[...] += dna
            dnb_ref[...] += dnb

        @pl.when(i >= nl)
        def _():
            for r in (doa_ref, dob_ref, dza_ref, dzb_ref):
                r[...] = jnp.zeros_like(r)

    big = jax.ShapeDtypeStruct((h, t, LANES), F32)
    small = jax.ShapeDtypeStruct((h, 1, LANES), F32)
    dza_out = pl.BlockSpec((h, tm, LANES), lambda i: (4, i, 0))
    return pl.pallas_call(
        body, name="gout_bwd", grid=(nl + nc,),
        in_specs=[blk_in(0), blk_in(0), blk_in(4), blk_in(8), vec, vec,
                  pl.BlockSpec((tm, 2 * h * LANES), lambda i: (jnp.minimum(i, nl - 1), 0))],
        out_specs=[blk_out, blk_out, dza_out, blk_out, vec, vec],
        out_shape=[big, big, jax.ShapeDtypeStruct(y_cbm.shape, F32), big, small, small],
        compiler_params=_params(("arbitrary",)),
    )(oa, ob, y_cbm, y_cbm, naw, nbw, dymix)


def _head_tile(xt, mixt, tgt, gt, fw):
    xo = xt + gt * mixt
    y = xo * lax.rsqrt(jnp.mean(xo * xo, axis=-1, keepdims=True) + NORM_EPS) * fw
    err = y - tgt
    return 0.5 * jnp.sum(jnp.mean(err * err, axis=-1, keepdims=True), axis=0, keepdims=True)


def _loss_head(x, mix, target, mods, final_w):
    l_lat, d = x.shape
    tm = _pick(l_lat, (256, 128, 64))
    tok = pl.BlockSpec((tm, d), lambda i: (i, 0))
    row = pl.BlockSpec((1, d), lambda i: (0, 0))

    def body(x_ref, m_ref, t_ref, mod_ref, fw_ref, loss_ref, dmix_ref, dx_ref, dgt_ref, dfw_ref):
        @pl.when(pl.program_id(0) == 0)
        def _():
            loss_ref[...] = jnp.zeros_like(loss_ref)
            dgt_ref[...] = jnp.zeros_like(dgt_ref)
            dfw_ref[...] = jnp.zeros_like(dfw_ref)

        gt = mod_ref[0:1, 2 * d:3 * d]
        fn = lambda xt, mt, g, fw: _head_tile(xt, mt, t_ref[...], g, fw)
        val, vj = jax.vjp(fn, x_ref[...], m_ref[...], gt, fw_ref[...])
        dxt, dmt, dgt, dfw = vj(jnp.ones((1, 1), F32))
        loss_ref[...] += jnp.broadcast_to(val, loss_ref.shape)
        dmix_ref[...] = dmt.astype(BF16)
        dx_ref[...] = dxt
        dgt_ref[...] += dgt
        dfw_ref[...] += dfw

    return pl.pallas_call(
        body, name="loss_head", grid=(l_lat // tm,),
        in_specs=[tok, tok, tok, pl.BlockSpec((8, 3 * d), lambda i: (0, 0)), row],
        out_specs=[pl.BlockSpec((8, LANES), lambda i: (0, 0)), tok, tok, row, row],
        out_shape=[jax.ShapeDtypeStruct((8, LANES), F32), jax.ShapeDtypeStruct((l_lat, d), BF16),
                   jax.ShapeDtypeStruct((l_lat, d), F32), jax.ShapeDtypeStruct((1, d), F32),
                   jax.ShapeDtypeStruct((1, d), F32)],
        compiler_params=_params(("arbitrary",)),
    )(x, mix, target, mods, final_w)


def _adamw(w, g_slots, m, v, name):
    rows, cols = w.shape
    n = g_slots.shape[0]
    tr = _pick(rows, (64, 32, 16, 8))

    def body(w_ref, g_ref, m_ref, v_ref, go_ref, d_ref, nm_ref, nv_ref):
        gg = g_ref[0].astype(F32)
        for s in range(1, n):
            gg = gg + g_ref[s].astype(F32)
        m2 = ADAM_B1 * m_ref[...] + (1.0 - ADAM_B1) * gg
        v2 = ADAM_B2 * v_ref[...] + (1.0 - ADAM_B2) * (gg * gg)
        m_hat = m2 / (1.0 - ADAM_B1 ** ADAM_STEP)
        v_hat = v2 / (1.0 - ADAM_B2 ** ADAM_STEP)
        go_ref[...] = gg
        d_ref[...] = -ADAM_LR * (m_hat / (jnp.sqrt(v_hat) + ADAM_EPS) + ADAM_WD * w_ref[...])
        nm_ref[...] = m2
        nv_ref[...] = v2

    spec = pl.BlockSpec((tr, cols), lambda i: (i, 0))
    shp = jax.ShapeDtypeStruct((rows, cols), F32)
    return pl.pallas_call(
        body, name=name, grid=(rows // tr,),
        in_specs=[spec, pl.BlockSpec((n, tr, cols), lambda i: (0, i, 0)), spec, spec],
        out_specs=[spec] * 4, out_shape=[shp] * 4,
        compiler_params=_params(("arbitrary",)),
    )(w, g_slots, m, v)


def _adamw_nd(w, g_slots, m, v, name):
    shape = w.shape
    two_d = (-1, shape[-1])
    w2 = w.reshape(two_d)
    outs = _adamw(w2, g_slots.reshape((g_slots.shape[0],) + w2.shape), m.reshape(two_d), v.reshape(two_d), name)
    return [o.reshape(shape) for o in outs]


def kernel(x, c, ctx, c_ctx, norm_w, ada_w, ada_b, w_in, conv_w, hg_lb_logits, gdn_a_log, gdn_dt_bias, ha_norm_w, hb_norm_w, w_out, final_norm_w, loss_target, m_c_ctx, m_norm_w, m_ada_w, m_ada_b, m_w_in, m_conv_w, m_hg_lb_logits, m_gdn_a_log, m_gdn_dt_bias, m_ha_norm_w, m_hb_norm_w, m_w_out, m_final_norm_w, v_c_ctx, v_norm_w, v_ada_w, v_ada_b, v_w_in, v_conv_w, v_hg_lb_logits, v_gdn_a_log, v_gdn_dt_bias, v_ha_norm_w, v_hb_norm_w, v_w_out, v_final_norm_w):
    l_lat, d = x.shape[1], x.shape[2]
    l_ctx = ctx.shape[1]
    t = l_lat + l_ctx
    h = d // LANES
    n_in = 9 * d + 4 * h
    nb = -(-(9 * h + 1) // COLS_PER_TILE) * COLS_PER_TILE
    n_pad = nb * LANES
    win_c, ada_c, conv_c, lb_c, wout_r = w_in.shape[2], ada_w.shape[2], conv_w.shape[3], hg_lb_logits.shape[2], w_out.shape[1]
    me = _me()

    gw_in, gw_out, gw_ada, gw_conv, gw_lb = _gather_two_level(
        [w_in[0].astype(BF16), w_out[0].astype(BF16), ada_w[0], conv_w[0], hg_lb_logits], "gather_weights")
    join = lambda g, axis: jnp.concatenate([g[k] for k in range(N_DEV)], axis=axis)
    w_in_full = join(gw_in, 1)
    w_out_full = gw_out.reshape(N_DEV * wout_r, d)
    ada_full = join(gw_ada, 1)
    conv_full = join(gw_conv, 2)
    lb_full = join(gw_lb, 2)
    w_pad = jnp.pad(w_in_full, ((0, 0), (0, n_pad - n_in)))
    w_pad_t = w_pad.T
    w_out_t = w_out_full.T

    lbl = lb_full.reshape(2, 2, h, 1, LANES)
    conv_rows = jnp.pad(conv_full.reshape(9, 3 * h, LANES).transpose(1, 0, 2), ((0, 0), (0, 7), (0, 0)))
    lane_bc = lambda a: jnp.broadcast_to(a.reshape(2, h, 1, 1), (2, h, 1, LANES))
    avec, dtvec = lane_bc(gdn_a_log[0]), lane_bc(gdn_dt_bias[0])
    naw, nbw = ha_norm_w[0].reshape(h, 1, LANES), hb_norm_w[0].reshape(h, 1, LANES)

    cvec = jnp.concatenate([c, c_ctx[None, :], jnp.zeros((6, d), F32)], axis=0)
    mods = _ada_fwd(cvec, ada_full, ada_b)
    x2, ctx2, tgt2 = x[0], ctx[0], loss_target[0]
    h_all = _prenorm_fwd(x2, ctx2, norm_w, mods)
    y = _matmul(h_all, w_pad, "proj", out_cbm=True, tn=COLS_PER_TILE * LANES)
    tail = lax.slice_in_dim(y, 9 * h, 9 * h + 1, axis=0)
    u = _conv_call(y, conv_rows, l_lat, 5 * h, 3 * h, "conv_fwd")

    oa, st_af = _scan_fwd("hg", False, y, (0, 1, 3), (lbl[0, 0], lbl[1, 0]), None, l_lat, l_ctx, h, "hg_fwd_f")
    oa, st_ab = _scan_fwd("hg", True, y, (0, 2, 3), (lbl[0, 1], lbl[1, 1]), oa, l_lat, l_ctx, h, "hg_fwd_b")
    ob, st_bf, xk_f = _scan_fwd("gdn", False, u, (0, 1, 2), (avec[0], dtvec[0]), None, l_lat, l_ctx, h, "gdn_fwd_f", tail, 0)
    ob, st_bb, xk_b = _scan_fwd("gdn", True, u, (0, 1, 2), (avec[1], dtvec[1]), ob, l_lat, l_ctx, h, "gdn_fwd_b", tail, 1)

    ymix = _gout_fwd(oa, ob, y, naw, nbw, l_lat, h)
    mix = _matmul(ymix, w_out_full, "out_proj")
    loss_blk, dmix, dx_res, dgt, dfw = _loss_head(x2, mix, tgt2, mods, final_norm_w.reshape(1, d))

    dymix = _matmul(dmix, w_out_t, "d_ymix")
    dw_out = _matmul(ymix, dmix, "d_w_out", a_t=True)
    doa, dob, dy, dzb, dnaw, dnbw = _gout_bwd(oa, ob, y, naw, nbw, dymix, l_lat, l_ctx, h)

    gq, gff, gi, dl0f, dl1f = _scan_bwd("hg", False, y, (0, 1, 3), (lbl[0, 0], lbl[1, 0]), st_af, doa,
                                        [None, None, None], l_lat, l_ctx, h, "hg_bwd_f")
    dy, dl0b, dl1b = _scan_bwd("hg", True, y, (0, 2, 3), (lbl[0, 1], lbl[1, 1]), st_ab, doa, [gq, None, gi],
                               l_lat, l_ctx, h, "hg_bwd_b",
                               pack=[("grad", 0), ("copy", gff), ("grad", 1), ("grad", 2)], pack_into=(dy, 0))
    guq, guk, guv, gtail, da_f, ddt_f = _scan_bwd("gdn", False, u, (0, 1, 2), (avec[0], dtvec[0]), st_bf, dob,
                                                  [None] * 4, l_lat, l_ctx, h, "gdn_bwd_f", tail, 0, xk_f)
    du, dy, da_b, ddt_b = _scan_bwd("gdn", True, u, (0, 1, 2), (avec[1], dtvec[1]), st_bb, dob,
                                    [guq, guk, guv, gtail], l_lat, l_ctx, h, "gdn_bwd_b", tail, 1, xk_b,
                                    pack=[("grad", 0), ("grad", 1), ("grad", 2)], tail_into=(dy, 9 * h, nb - 9 * h))
    dy, dconv_rows = _conv_call(y, conv_rows, l_lat, 5 * h, 3 * h, "conv_bwd", dout=du, passthru=dzb, into=(dy, 5 * h))
    dh = _matmul(dy, w_pad_t, "d_h", a_cbm=True)
    dw_in = _matmul(h_all, dy, "d_w_in", a_t=True, b_cbm=True)
    grad_x, dnorm_w, dmods_pre = _prenorm_bwd(x2, ctx2, norm_w, mods, dh, dx_res)

    dmods = jnp.concatenate([dmods_pre[:, :2 * d],
                             jnp.concatenate([dgt, jnp.zeros((7, d), F32)], axis=0)], axis=1)
    dcond = _ada_bwd(cvec, ada_full, dmods)

    dw_in_s = jnp.stack([dw_in[:, k * win_c:(k + 1) * win_c] for k in range(N_DEV)])
    dw_out_s = dw_out.reshape(N_DEV, wout_r, d)
    dconv_s = dconv_rows[:, :9, :].transpose(1, 0, 2).reshape(3, 3, N_DEV, conv_c).transpose(2, 0, 1, 3)
    dlb_s = jnp.stack([jnp.stack([dl0f, dl0b]), jnp.stack([dl1f, dl1b])]).reshape(2, 2, N_DEV, lb_c).transpose(2, 0, 1, 3)
    slabs = [dw_in_s, dw_out_s, dconv_s, dlb_s]
    theirs = _swap_with_sibling([a.astype(BF16) for a in slabs], "swap_grads")
    pairs = [_pair_sum(a, b, "pair_sum_%d" % k) for k, (a, b) in enumerate(zip(slabs, theirs))]
    s_w_in, s_w_out, s_conv, s_lb = _scatter_over_chips(pairs, "scatter_grads")

    da = jnp.stack([da_f[:, 0, 0], da_b[:, 0, 0]])
    ddt = jnp.stack([ddt_f[:, 0, 0], ddt_b[:, 0, 0]])
    small_parts = [dcond[1], dnorm_w, dmods[0] + dmods[1], dnaw, dnbw, dfw, da, ddt, c[0], dmods[0], dmods[1]]
    small_blob, soff = _to_blob(small_parts)
    gathered, = _exchange([small_blob], "gather_small", True)
    summed = _sum_slots(gathered, "sum_small").reshape(-1)
    gflat = gathered.reshape(N_DEV, -1)
    take = lambda k, n: summed[soff[k]:soff[k] + n]
    one = lambda k, n, shape: take(k, n).reshape((1,) + shape)
    s_c_ctx = one(0, d, (d,))
    s_norm_w = one(1, d, (1, d))
    s_ada_b = one(2, 3 * d, (1, 3 * d))
    s_ha = one(3, d, (1, h, LANES))
    s_hb = one(4, d, (1, h, LANES))
    s_final = one(5, d, (d,))
    s_a_log = one(6, 2 * h, (1, 2, h))
    s_dt = one(7, 2 * h, (1, 2, h))
    conds = jnp.concatenate([gflat[:, soff[8]:soff[8] + d], c_ctx[None, :], jnp.zeros((7, d), F32)], axis=0)
    col0 = me * ada_c
    dm_lat = lax.dynamic_slice_in_dim(gflat[:, soff[9]:soff[9] + 3 * d], col0, ada_c, axis=1)
    dm_ctx = lax.dynamic_slice_in_dim(gflat[:, soff[10]:soff[10] + 3 * d], col0, ada_c, axis=1)
    s_ada_w = _ada_wgrad(conds, dm_lat, dm_ctx).reshape(1, 1, d, ada_c)

    loss = lax.psum(loss_blk[0, 0], ("x", "y", "c"))

    weights = [("c_ctx", c_ctx, s_c_ctx, m_c_ctx, v_c_ctx), ("norm_w", norm_w, s_norm_w, m_norm_w, v_norm_w),
               ("ada_w", ada_w, s_ada_w, m_ada_w, v_ada_w), ("ada_b", ada_b, s_ada_b, m_ada_b, v_ada_b),
               ("w_in", w_in, s_w_in, m_w_in, v_w_in), ("conv_w", conv_w, s_conv, m_conv_w, v_conv_w),
               ("hg_lb_logits", hg_lb_logits, s_lb, m_hg_lb_logits, v_hg_lb_logits),
               ("gdn_a_log", gdn_a_log, s_a_log, m_gdn_a_log, v_gdn_a_log),
               ("gdn_dt_bias", gdn_dt_bias, s_dt, m_gdn_dt_bias, v_gdn_dt_bias),
               ("ha_norm_w", ha_norm_w, s_ha, m_ha_norm_w, v_ha_norm_w), ("hb_norm_w", hb_norm_w, s_hb, m_hb_norm_w, v_hb_norm_w),
               ("w_out", w_out, s_w_out, m_w_out, v_w_out), ("final_norm_w", final_norm_w, s_final, m_final_norm_w, v_final_norm_w)]
    grads, deltas, new_ms, new_vs = [], [], [], []
    for nm, w, gs, m, v in weights:
        g, dl, m2, v2 = _adamw_nd(w, gs, m, v, "adamw_" + nm)
        grads.append(g)
        deltas.append(dl)
        new_ms.append(m2)
        new_vs.append(v2)
    return (loss, grad_x[None], *grads, *deltas, *new_ms, *new_vs)
```

```python
import functools

import jax
import jax.numpy as jnp
import numpy as np
from jax import lax
from jax.experimental import pallas as pl
from jax.experimental.pallas import tpu as pltpu

F32 = jnp.float32
BF16 = jnp.bfloat16
HI = lax.Precision.HIGHEST
HIGH = lax.Precision.HIGH
MESH = pl.DeviceIdType.MESH
ANY = pl.BlockSpec(memory_space=pl.ANY)

N_DEV = 8
LANES = 128
CHUNK = 64
GRID_W = 64
NORM_EPS = 1e-6
NEG = -1e30
COLS_PER_TILE = 8
VMEM_BIG = 56 * 1024 * 1024
HEADS_PER_GROUP = 4
FWD_STEP = 4
BWD_STEP = 2
HALVES = (32, 16, 8, 4, 2, 1)

ADAM_LR, ADAM_B1, ADAM_B2, ADAM_EPS, ADAM_WD, ADAM_STEP = 0.001, 0.9, 0.999, 1e-08, 0.01, 10


def _params(sem=None, vmem=None):
    kw = {}
    if sem is not None:
        kw["dimension_semantics"] = sem
    if vmem is not None:
        kw["vmem_limit_bytes"] = vmem
    return pltpu.CompilerParams(**kw)


def _pick(n, cands):
    for c in cands:
        if n % c == 0:
            return c
    return n


def _silu(x):
    return x * jax.nn.sigmoid(x)


def _mm(a, b, prec=None):
    return lax.dot_general(a, b, (((1,), (0,)), ((), ())), precision=prec, preferred_element_type=F32)


def _mm_nt(a, b, prec=None):
    return lax.dot_general(a, b, (((1,), (1,)), ((), ())), precision=prec, preferred_element_type=F32)


def _mm_tn(a, b, prec=None):
    return lax.dot_general(a, b, (((0,), (0,)), ((), ())), precision=prec, preferred_element_type=F32)


def _me():
    return 4 * lax.axis_index("x") + 2 * lax.axis_index("y") + lax.axis_index("c")


def _peer(r):
    x, y, c = lax.axis_index("x"), lax.axis_index("y"), lax.axis_index("c")
    px = 1 - x if r & 4 else x
    py = 1 - y if r & 2 else y
    pc = 1 - c if r & 1 else c
    return (px, py, pc), 4 * px + 2 * py + pc


def _exchange(arrs, name, gather):
    n_a = len(arrs)

    def body(*refs):
        srcs, outs = refs[:n_a], refs[n_a:2 * n_a]
        send_sems, recv_sems, local_sems = refs[2 * n_a:]
        me = _me()
        copies = []
        for k in range(n_a):
            local = pltpu.make_async_copy(srcs[k] if gather else srcs[k].at[me], outs[k].at[me], local_sems.at[k])
            local.start()
            copies.append(local)
        for r in range(N_DEV - 1, 0, -1):
            dev, pid = _peer(r)
            for k in range(n_a):
                cp = pltpu.make_async_remote_copy(
                    src_ref=srcs[k] if gather else srcs[k].at[pid], dst_ref=outs[k].at[me],
                    send_sem=send_sems.at[(r - 1) * n_a + k], recv_sem=recv_sems.at[(r - 1) * n_a + k],
                    device_id=dev, device_id_type=MESH)
                cp.start()
                copies.append(cp)
        for cp in copies:
            cp.wait()

    n_sem = (N_DEV - 1) * n_a
    return pl.pallas_call(
        body, name=name,
        out_shape=[jax.ShapeDtypeStruct(((N_DEV,) + a.shape) if gather else a.shape, a.dtype) for a in arrs],
        in_specs=[ANY] * n_a, out_specs=[ANY] * n_a,
        scratch_shapes=[pltpu.SemaphoreType.DMA((n_sem,)), pltpu.SemaphoreType.DMA((n_sem,)),
                        pltpu.SemaphoreType.DMA((n_a,))],
    )(*arrs)


def _gather_two_level(arrs, name):
    n_a = len(arrs)

    def body(*refs):
        srcs, outs = refs[:n_a], refs[n_a:2 * n_a]
        send_sems, recv_sems, local_sems = refs[2 * n_a:]
        x, y, c = lax.axis_index("x"), lax.axis_index("y"), lax.axis_index("c")
        me, sibling = (x, y, c), (x, y, 1 - c)
        chips = [(1 - x, y), (x, 1 - y), (1 - x, 1 - y)]

        def copy(k, j, block, to, src=None):
            slot = outs[k].at[4 * block[0] + 2 * block[1] + block[2]]
            return pltpu.make_async_remote_copy(
                src_ref=slot if src is None else src, dst_ref=slot,
                send_sem=send_sems.at[j * n_a + k], recv_sem=recv_sems.at[j * n_a + k],
                device_id=to, device_id_type=MESH)

        mine = [pltpu.make_async_copy(srcs[k], outs[k].at[4 * x + 2 * y + c], local_sems.at[k]) for k in range(n_a)]
        for cp in mine:
            cp.start()
        first = [copy(k, 1 + j, me, (*chip, c), srcs[k]) for j, chip in enumerate(chips) for k in range(n_a)]
        first += [copy(k, 0, me, sibling, srcs[k]) for k in range(n_a)]
        for cp in first:
            cp.start()
        passed = []
        for j, chip in enumerate(chips):
            for k in range(n_a):
                copy(k, 1 + j, (*chip, c), me).wait_recv()
                cp = copy(k, 4 + j, (*chip, c), sibling)
                cp.start()
                passed.append(cp)
        for k in range(n_a):
            copy(k, 0, sibling, me).wait_recv()
        for j, chip in enumerate(chips):
            for k in range(n_a):
                copy(k, 4 + j, (*chip, 1 - c), me).wait_recv()
        for cp in first + passed:
            cp.wait_send()
        for cp in mine:
            cp.wait()

    n_sem = (N_DEV - 1) * n_a
    return pl.pallas_call(
        body, name=name,
        out_shape=[jax.ShapeDtypeStruct((N_DEV,) + a.shape, a.dtype) for a in arrs],
        in_specs=[ANY] * n_a, out_specs=[ANY] * n_a,
        scratch_shapes=[pltpu.SemaphoreType.DMA((n_sem,)), pltpu.SemaphoreType.DMA((n_sem,)),
                        pltpu.SemaphoreType.DMA((n_a,))],
    )(*arrs)


def _swap_with_sibling(arrs, name):
    n_a = len(arrs)

    def body(*refs):
        srcs, outs = refs[:n_a], refs[n_a:2 * n_a]
        send_sems, recv_sems = refs[2 * n_a:]
        sibling = (lax.axis_index("x"), lax.axis_index("y"), 1 - lax.axis_index("c"))
        copies = [pltpu.make_async_remote_copy(src_ref=srcs[k], dst_ref=outs[k], send_sem=send_sems.at[k],
                                               recv_sem=recv_sems.at[k], device_id=sibling, device_id_type=MESH)
                  for k in range(n_a)]
        for cp in copies:
            cp.start()
        for cp in copies:
            cp.wait()

    return pl.pallas_call(
        body, name=name, out_shape=[jax.ShapeDtypeStruct(a.shape, a.dtype) for a in arrs],
        in_specs=[ANY] * n_a, out_specs=[ANY] * n_a,
        scratch_shapes=[pltpu.SemaphoreType.DMA((n_a,)), pltpu.SemaphoreType.DMA((n_a,))],
    )(*arrs)


def _scatter_over_chips(arrs, name):
    n_a = len(arrs)
    n_chip = N_DEV // 2

    def body(*refs):
        srcs, outs = refs[:n_a], refs[n_a:2 * n_a]
        send_sems, recv_sems, local_sems = refs[2 * n_a:]
        x, y, c = lax.axis_index("x"), lax.axis_index("y"), lax.axis_index("c")
        my_chip = 2 * x + y
        copies = []
        for k in range(n_a):
            local = pltpu.make_async_copy(srcs[k].at[4 * x + 2 * y + c], outs[k].at[my_chip], local_sems.at[k])
            local.start()
            copies.append(local)
        for r in range(n_chip - 1, 0, -1):
            px = 1 - x if r & 2 else x
            py = 1 - y if r & 1 else y
            for k in range(n_a):
                cp = pltpu.make_async_remote_copy(
                    src_ref=srcs[k].at[4 * px + 2 * py + c], dst_ref=outs[k].at[my_chip],
                    send_sem=send_sems.at[(r - 1) * n_a + k], recv_sem=recv_sems.at[(r - 1) * n_a + k],
                    device_id=(px, py, c), device_id_type=MESH)
                cp.start()
                copies.append(cp)
        for cp in copies:
            cp.wait()

    n_sem = (n_chip - 1) * n_a
    return pl.pallas_call(
        body, name=name,
        out_shape=[jax.ShapeDtypeStruct((n_chip,) + a.shape[1:], a.dtype) for a in arrs],
        in_specs=[ANY] * n_a, out_specs=[ANY] * n_a,
        scratch_shapes=[pltpu.SemaphoreType.DMA((n_sem,)), pltpu.SemaphoreType.DMA((n_sem,)),
                        pltpu.SemaphoreType.DMA((n_a,))],
    )(*arrs)


def _pair_sum(a, b, name):
    shape = a.shape
    a2, b2 = a.reshape(-1, shape[-1]), b.reshape(-1, shape[-1])
    rows, cols = a2.shape
    tr = _pick(rows, (256, 128, 64, 32, 16))

    def body(a_ref, b_ref, o_ref):
        o_ref[...] = (a_ref[...] + b_ref[...].astype(F32)).astype(BF16)

    spec = pl.BlockSpec((tr, cols), lambda i: (i, 0))
    return pl.pallas_call(
        body, name=name, grid=(rows // tr,), in_specs=[spec, spec], out_specs=spec,
        out_shape=jax.ShapeDtypeStruct((rows, cols), BF16), compiler_params=_params(("arbitrary",)),
    )(a2, b2).reshape(shape)


def _to_blob(parts):
    flat = [p.reshape(-1).astype(F32) for p in parts]
    offs, n = [], 0
    for f in flat:
        offs.append(n)
        n += f.shape[0]
    unit = 8 * LANES
    total = -(-n // unit) * unit
    if total > n:
        flat.append(jnp.zeros((total - n,), F32))
    return jnp.concatenate(flat).reshape(total // LANES, LANES), offs


def _sum_slots(buf, name):
    n, rows, cols = buf.shape
    tr = _pick(rows, (512, 256, 128, 64, 32, 16, 8))

    def body(b_ref, o_ref):
        acc = b_ref[0]
        for s in range(1, n):
            acc = acc + b_ref[s]
        o_ref[...] = acc

    return pl.pallas_call(
        body, name=name, grid=(rows // tr,),
        in_specs=[pl.BlockSpec((n, tr, cols), lambda i: (0, i, 0))],
        out_specs=pl.BlockSpec((tr, cols), lambda i: (i, 0)),
        out_shape=jax.ShapeDtypeStruct((rows, cols), F32),
        compiler_params=_params(("arbitrary",)),
    )(buf)


def _matmul(a, b, name, a_cbm=False, b_cbm=False, out_cbm=False, a_t=False, out_dtype=F32, tm=None, tn=None, tk=None):
    if a_cbm:
        m, k = a.shape[1], a.shape[0] * LANES
    elif a_t:
        k, m = a.shape
    else:
        m, k = a.shape
    n = b.shape[0] * LANES if b_cbm else b.shape[1]
    tm = tm or _pick(m, (1024, 768, 512, 384, 256, 128, 64))
    tn = tn or _pick(n, (1024, 512, 256, 128))
    tk = tk or _pick(k, (2048, 1024, 768, 512, 256, 128))
    nk = k // tk

    def load(ref, cbm):
        if not cbm:
            return ref[...].astype(BF16)
        return jnp.concatenate([ref[j].astype(BF16) for j in range(ref.shape[0])], axis=1)

    def body(a_ref, b_ref, o_ref, acc_ref):
        kk = pl.program_id(2)
        part = (_mm_tn if a_t else _mm)(load(a_ref, a_cbm), load(b_ref, b_cbm))

        @pl.when(kk == 0)
        def _():
            acc_ref[...] = part

        @pl.when(kk > 0)
        def _():
            acc_ref[...] += part

        @pl.when(kk == nk - 1)
        def _():
            r = acc_ref[...]
            if out_cbm:
                for j in range(tn // LANES):
                    o_ref[j] = r[:, j * LANES:(j + 1) * LANES].astype(out_dtype)
            else:
                o_ref[...] = r.astype(out_dtype)

    if a_cbm:
        a_spec = pl.BlockSpec((tk // LANES, tm, LANES), lambda j, i, kk: (kk, i, 0))
    elif a_t:
        a_spec = pl.BlockSpec((tk, tm), lambda j, i, kk: (kk, i))
    else:
        a_spec = pl.BlockSpec((tm, tk), lambda j, i, kk: (i, kk))
    if b_cbm:
        b_spec = pl.BlockSpec((tn // LANES, tk, LANES), lambda j, i, kk: (j, kk, 0))
    else:
        b_spec = pl.BlockSpec((tk, tn), lambda j, i, kk: (kk, j))
    if out_cbm:
        o_spec = pl.BlockSpec((tn // LANES, tm, LANES), lambda j, i, kk: (j, i, 0))
        o_shape = jax.ShapeDtypeStruct((n // LANES, m, LANES), out_dtype)
    else:
        o_spec = pl.BlockSpec((tm, tn), lambda j, i, kk: (i, j))
        o_shape = jax.ShapeDtypeStruct((m, n), out_dtype)
    return pl.pallas_call(
        body, name=name, grid=(n // tn, m // tm, nk),
        in_specs=[a_spec, b_spec], out_specs=o_spec, out_shape=o_shape,
        scratch_shapes=[pltpu.VMEM((tm, tn), F32)],
        compiler_params=_params(("arbitrary", "arbitrary", "arbitrary"), VMEM_BIG),
    )(a, b)


def _ada_fwd(cvec, ada_w, ada_b):
    def body(c_ref, w_ref, b_ref, o_ref):
        o_ref[...] = _mm(_silu(c_ref[...]), w_ref[...], HI) + b_ref[...]

    return pl.pallas_call(
        body, name="ada_fwd", out_shape=jax.ShapeDtypeStruct((8, ada_w.shape[1]), F32),
        compiler_params=_params(None, VMEM_BIG),
    )(cvec, ada_w, ada_b)


def _ada_bwd(cvec, ada_w, dmods):
    def body(c_ref, w_ref, d_ref, o_ref):
        (_, vj) = jax.vjp(_silu, c_ref[...])
        o_ref[...] = vj(_mm_nt(d_ref[...], w_ref[...], HI))[0]

    return pl.pallas_call(
        body, name="ada_bwd", out_shape=jax.ShapeDtypeStruct(cvec.shape, F32),
        compiler_params=_params(None, VMEM_BIG),
    )(cvec, ada_w, dmods)


def _ada_wgrad(conds, dm_lat, dm_ctx):
    d = conds.shape[1]
    cols = dm_lat.shape[1]

    def body(c_ref, dl_ref, dc_ref, o_ref):
        dctx = dc_ref[0:1]
        for s in range(1, N_DEV):
            dctx = dctx + dc_ref[s:s + 1]
        rhs = jnp.concatenate([dl_ref[...], dctx, jnp.zeros((7, cols), F32)], axis=0)
        o_ref[...] = _mm_tn(_silu(c_ref[...]), rhs, HI)

    return pl.pallas_call(
        body, name="ada_wgrad", out_shape=jax.ShapeDtypeStruct((d, cols), F32),
    )(conds, dm_lat, dm_ctx)


def _prenorm_tile(xt, nw, sc, sh):
    r = lax.rsqrt(jnp.mean(xt * xt, axis=-1, keepdims=True) + NORM_EPS)
    return (xt * r * nw) * (1.0 + sc) + sh


def _tok_specs(l_lat, l_ctx, d, tm):
    nl, nc = l_lat // tm, l_ctx // tm
    lat = pl.BlockSpec((tm, d), lambda i: (jnp.minimum(i, nl - 1), 0))
    ctx = pl.BlockSpec((tm, d), lambda i: (jnp.maximum(i - nl, 0), 0))
    return nl, nc, lat, ctx


def _prenorm_fwd(x, ctx, norm_w, mods):
    l_lat, d = x.shape
    l_ctx = ctx.shape[0]
    tm = _pick(l_ctx, (256, 128, 64))
    nl, nc, lat_spec, ctx_spec = _tok_specs(l_lat, l_ctx, d, tm)

    def body(x_ref, c_ref, nw_ref, m_ref, h_ref):
        is_lat = pl.program_id(0) < nl
        xt = jnp.where(is_lat, x_ref[...], c_ref[...])
        row = jnp.where(is_lat, m_ref[0:1, :], m_ref[1:2, :])
        h_ref[...] = _prenorm_tile(xt, nw_ref[...], row[:, d:2 * d], row[:, 0:d]).astype(BF16)

    return pl.pallas_call(
        body, name="prenorm_fwd", grid=(nl + nc,),
        in_specs=[lat_spec, ctx_spec, pl.BlockSpec((1, d), lambda i: (0, 0)), pl.BlockSpec((8, 3 * d), lambda i: (0, 0))],
        out_specs=pl.BlockSpec((tm, d), lambda i: (i, 0)),
        out_shape=jax.ShapeDtypeStruct((l_lat + l_ctx, d), BF16),
        compiler_params=_params(("arbitrary",)),
    )(x, ctx, norm_w, mods)


def _prenorm_bwd(x, ctx, norm_w, mods, dh, dx_res):
    l_lat, d = x.shape
    l_ctx = ctx.shape[0]
    tm = _pick(l_ctx, (256, 128, 64))
    nl, nc, lat_spec, ctx_spec = _tok_specs(l_lat, l_ctx, d, tm)

    def body(x_ref, c_ref, nw_ref, m_ref, dh_ref, dr_ref, gx_ref, dnw_ref, dm_ref):
        i = pl.program_id(0)
        is_lat = i < nl

        @pl.when(i == 0)
        def _():
            dnw_ref[...] = jnp.zeros_like(dnw_ref)
            dm_ref[...] = jnp.zeros_like(dm_ref)

        xt = jnp.where(is_lat, x_ref[...], c_ref[...])
        row = jnp.where(is_lat, m_ref[0:1, :], m_ref[1:2, :])
        _, vj = jax.vjp(_prenorm_tile, xt, nw_ref[...], row[:, d:2 * d], row[:, 0:d])
        dxt, dnw, dsc, dsh = vj(dh_ref[...])
        dnw_ref[...] += dnw
        upd = jnp.concatenate([dsh, dsc, jnp.zeros_like(dsh)], axis=1)

        @pl.when(is_lat)
        def _():
            gx_ref[...] = dr_ref[...] + dxt
            dm_ref[0:1, :] += upd

        @pl.when(jnp.logical_not(is_lat))
        def _():
            dm_ref[1:2, :] += upd

    return pl.pallas_call(
        body, name="prenorm_bwd", grid=(nl + nc,),
        in_specs=[lat_spec, ctx_spec, pl.BlockSpec((1, d), lambda i: (0, 0)), pl.BlockSpec((8, 3 * d), lambda i: (0, 0)),
                  pl.BlockSpec((tm, d), lambda i: (i, 0)), lat_spec],
        out_specs=[lat_spec, pl.BlockSpec((1, d), lambda i: (0, 0)), pl.BlockSpec((8, 3 * d), lambda i: (0, 0))],
        out_shape=[jax.ShapeDtypeStruct((l_lat, d), F32), jax.ShapeDtypeStruct((1, d), F32),
                   jax.ShapeDtypeStruct((8, 3 * d), F32)],
        compiler_params=_params(("arbitrary",)),
    )(x, ctx, norm_w, mods, dh, dx_res)


def _conv_call(src, wts, l_lat, first_block, n_blocks, name, dout=None, passthru=None, into=None):
    t = src.shape[1]
    l_ctx = t - l_lat
    rc = _pick(l_lat, (512, 256, 128, 64))
    n_rc = l_lat // rc
    bwd = dout is not None

    def shifted(xc, n, period):
        pos = lax.broadcasted_iota(jnp.int32, (n, LANES), 0) % period
        xm = jnp.where(pos == 0, 0.0, pltpu.roll(xc, 1, 0))
        xp = jnp.where(pos == period - 1, 0.0, pltpu.roll(xc, n - 1, 0))
        return xm, xc, xp

    def fill(ref, sm, s0, sp):
        zero = jnp.zeros((GRID_W, LANES), F32)
        for s in (sm, s0, sp):
            s[pl.ds(0, GRID_W), :] = zero
            s[pl.ds(l_lat + GRID_W, GRID_W), :] = zero

        def step(i, carry):
            st = pl.multiple_of(i * rc, rc)
            xm, x0, xp = shifted(ref[0, pl.ds(st, rc), :], rc, GRID_W)
            sm[pl.ds(st + GRID_W, rc), :] = xm
            s0[pl.ds(st + GRID_W, rc), :] = x0
            sp[pl.ds(st + GRID_W, rc), :] = xp
            return carry

        lax.fori_loop(0, n_rc, step, 0)

    def apply(w, sm, s0, sp, out_ref, flip):
        def step(i, carry):
            st = pl.multiple_of(i * rc, rc)
            acc = jnp.zeros((rc, LANES), F32)
            for di in range(3):
                for dj, s in enumerate((sm, s0, sp)):
                    kidx = (2 - di) * 3 + (2 - dj) if flip else di * 3 + dj
                    acc = acc + w[kidx:kidx + 1, :] * s[pl.ds(st + di * GRID_W, rc), :]
            out_ref[0, pl.ds(st, rc), :] = acc
            return carry

        lax.fori_loop(0, n_rc, step, 0)

    def ctx_apply(w, xc, flip):
        xm, x0, xp = shifted(xc, l_ctx, l_ctx)
        ks = (5, 4, 3) if flip else (3, 4, 5)
        return w[ks[0]:ks[0] + 1, :] * xm + w[4:5, :] * x0 + w[ks[2]:ks[2] + 1, :] * xp

    def fwd_body(x_ref, w_ref, o_ref, sm, s0, sp):
        w = w_ref[0]
        fill(x_ref, sm, s0, sp)
        apply(w, sm, s0, sp, o_ref, False)
        o_ref[0, pl.ds(l_lat, l_ctx), :] = ctx_apply(w, x_ref[0, pl.ds(l_lat, l_ctx), :], False)

    n_pass = 0 if passthru is None else passthru.shape[0]

    def bwd_body(x_ref, w_ref, d_ref, *rest):
        p_ref = rest[0] if n_pass else None
        dx_ref, dw_ref, sm, s0, sp = rest[-5:]

        @pl.when(pl.program_id(0) < n_blocks)
        def _():
            bwd_block(x_ref, w_ref, d_ref, dx_ref, dw_ref, sm, s0, sp)

        if n_pass:
            @pl.when(pl.program_id(0) >= n_blocks)
            def _():
                dx_ref[0] = p_ref[0]

    def bwd_block(x_ref, w_ref, d_ref, dx_ref, dw_ref, sm, s0, sp):
        w = w_ref[0]
        fill(d_ref, sm, s0, sp)

        def step(i, acc):
            st = pl.multiple_of(i * rc, rc)
            xc = x_ref[0, pl.ds(st, rc), :]
            dx = jnp.zeros((rc, LANES), F32)
            rows = [None] * 9
            for di in range(3):
                for dj, s in enumerate((sm, s0, sp)):
                    kidx = (2 - di) * 3 + (2 - dj)
                    val = s[pl.ds(st + di * GRID_W, rc), :]
                    dx = dx + w[kidx:kidx + 1, :] * val
                    rows[kidx] = jnp.sum(xc * val, axis=0, keepdims=True)
            dx_ref[0, pl.ds(st, rc), :] = dx
            return acc + jnp.concatenate(rows + [jnp.zeros((7, LANES), F32)], axis=0)

        acc = lax.fori_loop(0, n_rc, step, jnp.zeros((16, LANES), F32))
        dctx = d_ref[0, pl.ds(l_lat, l_ctx), :]
        xctx = x_ref[0, pl.ds(l_lat, l_ctx), :]
        dm, d0, dp = shifted(dctx, l_ctx, l_ctx)
        crow = [jnp.sum(xctx * s, axis=0, keepdims=True) for s in (dp, d0, dm)]
        acc = acc + jnp.concatenate([jnp.zeros((3, LANES), F32)] + crow + [jnp.zeros((10, LANES), F32)], axis=0)
        dw_ref[0] = acc
        dx_ref[0, pl.ds(l_lat, l_ctx), :] = ctx_apply(w, dctx, True)

    blk = lambda off: pl.BlockSpec((1, t, LANES), lambda j: (j + off, 0, 0))
    w_spec = pl.BlockSpec((1, 16, LANES), lambda j: (j, 0, 0))
    scratch = [pltpu.VMEM((l_lat + 2 * GRID_W, LANES), F32)] * 3
    out_t = jax.ShapeDtypeStruct((n_blocks, t, LANES), F32)
    if not bwd:
        return pl.pallas_call(
            fwd_body, name=name, grid=(n_blocks,), in_specs=[blk(first_block), w_spec], out_specs=blk(0),
            out_shape=out_t, scratch_shapes=scratch, compiler_params=_params(("arbitrary",), VMEM_BIG),
        )(src, wts)
    last = n_blocks - 1
    clamped = lambda off: pl.BlockSpec((1, t, LANES), lambda j: (jnp.minimum(j, last) + off, 0, 0))
    w_clamped = pl.BlockSpec((1, 16, LANES), lambda j: (jnp.minimum(j, last), 0, 0))
    in_specs, args, aliases = [clamped(first_block), w_clamped, clamped(0)], [src, wts, dout], {}
    if n_pass:
        in_specs.append(pl.BlockSpec((1, t, LANES), lambda j: (jnp.maximum(j - n_blocks, 0), 0, 0)))
        args.append(passthru)
    first_out = 0
    if into is not None:
        aliases[len(args)] = 0
        in_specs.append(ANY)
        args.append(into[0])
        first_out = into[1]
        out_t = jax.ShapeDtypeStruct(into[0].shape, F32)
    else:
        out_t = jax.ShapeDtypeStruct((n_blocks + n_pass, t, LANES), F32)
    return pl.pallas_call(
        bwd_body, name=name, grid=(n_blocks + n_pass,), in_specs=in_specs,
        out_specs=[pl.BlockSpec((1, t, LANES), lambda j: (j + first_out, 0, 0)), w_clamped],
        out_shape=[out_t, jax.ShapeDtypeStruct((n_blocks, 16, LANES), F32)],
        scratch_shapes=scratch, input_output_aliases=aliases, compiler_params=_params(("arbitrary",), VMEM_BIG),
    )(*args)


def _scan_consts(kind, g, h, rev, d_index):
    n = g * CHUNK
    i = np.arange(n, dtype=np.int32)
    head, pos = i // CHUNK, i % CHUNK
    p = (CHUNK - 1 - pos) if rev else pos
    same_head = head[:, None] == head[None, :]
    pr, pc = p[:, None], p[None, :]
    f = lambda m: np.ascontiguousarray(m, dtype=np.float32)
    incl = same_head & (pc <= pr)
    out = {"eye": f(i[:, None] == i[None, :]), "incl": f(incl), "incl_t": f(incl.T)}
    if kind == "hg":
        mid, same, sec = [], [], []
        for half in HALVES:
            width = 2 * half
            blk = p // width
            second = (p % width) >= half
            ref_pos = blk * width + half - 1
            mid.append(same_head & (pc <= ref_pos[:, None]))
            same.append(same_head & (blk[:, None] == blk[None, :]))
            sec.append(np.broadcast_to(second[:, None], (n, LANES)))
        mid = np.concatenate(mid, axis=0)
        out.update(mid=f(mid), mid_t=f(mid.T), same=f(np.stack(same)), sec=f(np.stack(sec)))
    else:
        low = []
        for half in reversed(HALVES):
            width = 2 * half
            blk = p // width
            second = (p % width) >= half
            low.append(same_head & (blk[:, None] == blk[None, :]) & second[:, None] & np.logical_not(second[None, :]))
        lane = np.arange(LANES, dtype=np.int32)[None, :]
        hh = np.arange(h * CHUNK, dtype=np.int32) // CHUNK
        out.update(strict=f(same_head & (pc < pr)), low=f(np.stack(low)),
                   oh_a=f(lane == (d_index * h + hh)[:, None]), oh_b=f(lane == (2 * h + d_index * h + hh)[:, None]))
    return out


def _mm_01(c, x):
    hi = x.astype(jnp.bfloat16).astype(F32)
    return _mm(c, hi) + _mm(c, x - hi)


def _const_mm(c, c_t, diff, one_pass=False):
    fwd = (lambda x: _mm(c, x)) if one_pass else (lambda x: _mm_01(c, x))
    if not diff:
        return fwd

    f = jax.custom_vjp(fwd)
    f.defvjp(lambda x: (fwd(x), None), lambda _, ct: (_mm_01(c_t, ct),))
    return f


def _kept_inverse(x_kept):
    @jax.custom_vjp
    def f(a):
        return x_kept

    f.defvjp(lambda a: (x_kept, None), lambda _, ct: (-_mm_nt(_mm_tn(x_kept, ct, HIGH), x_kept, HIGH),))
    return f


def _stack_helpers(g):
    n = g * CHUNK
    rows = lambda vec: jnp.broadcast_to(vec, (g, CHUNK, LANES)).reshape(n, LANES)
    per_head = lambda t: [t[i * CHUNK:(i + 1) * CHUNK] for i in range(g)]
    head_sum = lambda t: rows(jnp.sum(t.reshape(g, CHUNK, LANES), axis=1, keepdims=True))
    return rows, per_head, head_sum


def _lockstep(gens):
    results = [None] * len(gens)
    live = list(range(len(gens)))
    while live:
        for i in list(live):
            try:
                next(gens[i])
            except StopIteration as done:
                results[i] = done.value
                live.remove(i)
    return results


def _hg_chunk(raw_q, raw_f, raw_i, l0, l1, state, cst, g, diff=False):
    n = g * CHUNK
    rows, per_head, head_sum = _stack_helpers(g)
    lb = rows(jax.nn.sigmoid(l0 - l1))
    q = _silu(raw_q) * (LANES ** -0.5)
    gl = jnp.log(lb + (1.0 - lb) * jax.nn.sigmoid(raw_f))
    k = (1.0 - lb) * jax.nn.sigmoid(-raw_f)
    v = raw_i
    b = _const_mm(cst["incl"], cst["incl_t"], diff)(gl)
    b_tot = head_sum(gl)
    q_dec = q * jnp.exp(b)
    k_dec = k * jnp.exp(b_tot - b)
    mids = _const_mm(cst["mid"], cst["mid_t"], diff, True)(gl)
    yield
    a = (cst["eye"] * jnp.sum(q * k, axis=1, keepdims=True)).astype(jnp.bfloat16)
    for lv in range(len(HALVES)):
        r = mids[lv * n:(lv + 1) * n]
        sec = cst["sec"][lv]
        fst = 1.0 - sec
        qt = q * jnp.exp((b - r) * sec) * sec
        kt = k * jnp.exp((r - b) * fst) * fst
        a = a + _mm_nt(qt, kt).astype(jnp.bfloat16) * cst["same"][lv]
        yield
    o_intra = per_head(_mm(a.astype(F32), v))
    decay = per_head(jnp.exp(b_tot))
    qd, kd, vs = per_head(q_dec), per_head(k_dec), per_head(v)
    sts = state()
    while sts is None:
        yield
        sts = state()
    outs = [o_intra[i] + _mm_nt(qd[i], sts[i]) for i in range(g)]
    new = [sts[i] * decay[i][0:1] + _mm_tn(vs[i], kd[i]) for i in range(g)]
    return jnp.concatenate(outs, axis=0), new, sts


def _gdn_chunk(u_q, u_k, u_v, tail, avec, dtvec, state, cst, g, x_kept=None):
    n = g * CHUNK
    diff = x_kept is not None
    rows, per_head, head_sum = _stack_helpers(g)

    def l2n(t):
        return t * lax.rsqrt(jnp.sum(t * t, axis=-1, keepdims=True) + NORM_EPS)

    q = l2n(_silu(u_q)) * (LANES ** -0.5)
    k = l2n(_silu(u_k))
    v = _silu(u_v)
    tail_n = jnp.concatenate([tail] * g, axis=0)
    za = jnp.sum(tail_n * cst["oh_a"], axis=1, keepdims=True) + rows(dtvec)
    beta = jax.nn.sigmoid(jnp.sum(tail_n * cst["oh_b"], axis=1, keepdims=True))
    gl = -jnp.exp(rows(avec)) * (jnp.maximum(za, 0.0) + jnp.log1p(jnp.exp(-jnp.abs(za))))
    b = _const_mm(cst["incl"], cst["incl_t"], diff)(gl)
    kk = _mm_nt(k, k)
    qk = _mm_nt(q, k)
    yield
    b_tot = head_sum(gl)
    bb = jnp.concatenate([b] * (n // LANES), axis=1)
    bdiff = bb - bb.T
    a_mat = beta * kk * jnp.exp(jnp.where(cst["strict"] > 0.5, bdiff, NEG))
    if diff:
        xinv = _kept_inverse(x_kept)(a_mat)
    else:
        xinv = cst["eye"] - a_mat * cst["low"][0]
        for lv in range(1, len(HALVES)):
            left = _mm(xinv, a_mat * cst["low"][lv])
            yield
            xinv = xinv - _mm(left, xinv)
            yield
        resid = cst["eye"] - _mm(cst["eye"] + a_mat, xinv, HIGH)
        yield
        xinv = xinv + _mm(xinv, resid, HIGH)
        yield
    rhs = jnp.concatenate([beta * v, beta * jnp.exp(b) * k], axis=1)
    sol = _mm(xinv, rhs, HIGH)
    yield
    u0, w = per_head(sol[:, :LANES]), per_head(sol[:, LANES:])
    k_dec = per_head(k * jnp.exp(b_tot - b))
    q_dec = per_head(q * jnp.exp(b))
    decay = per_head(jnp.exp(b_tot))
    p = qk * jnp.exp(jnp.where(cst["incl"] > 0.5, bdiff, NEG))
    ss = state()
    while ss is None:
        yield
        ss = state()
    v_new = [u0[i] - _mm(w[i], ss[i]) for i in range(g)]
    o_state = [_mm(q_dec[i], ss[i]) for i in range(g)]
    yield
    o = _mm(p, jnp.concatenate(v_new, axis=0)) + jnp.concatenate(o_state, axis=0)
    new = [decay[i][0:1] * ss[i] + _mm_tn(k_dec[i], v_new[i]) for i in range(g)]
    return o, new, xinv, ss


def _chunk_index(l_lat, l_ctx, rev, sc):
    rows = sc * CHUNK
    nl, nc = l_lat // rows, l_ctx // rows

    def idx(i):
        if rev:
            return jnp.where(i < nc, nl + nc - 1 - i, nl - 1 - (i - nc))
        return jnp.where(i < nc, nl + i, i - nc)

    subs = list(range(sc))
    return nl + nc, idx, (subs[::-1] if rev else subs)


def _const_args(kind, g, h, rev, d_index):
    consts = _scan_consts(kind, g, h, rev, d_index)
    names = sorted(consts)
    arrs = [jnp.asarray(consts[k]).astype(jnp.bfloat16 if k == "same" else F32) for k in names]
    specs = [pl.BlockSpec(a.shape, functools.partial(lambda nd, i: (0,) * nd, a.ndim)) for a in arrs]
    return names, arrs, specs


def _load_consts(names, refs, g0, g):
    cst = {}
    for k, r in zip(names, refs):
        if k in ("oh_a", "oh_b"):
            cst[k] = r[pl.ds(pl.multiple_of(g0 * CHUNK, CHUNK), g * CHUNK), :]
        else:
            cst[k] = r[...]
    return cst


def _scan_fwd(kind, rev, cbm, segs, vecs, o_prev, l_lat, l_ctx, h, name, tail=None, d_index=0):
    sc = FWD_STEP
    n_steps, cidx, subs = _chunk_index(l_lat, l_ctx, rev, sc)
    rows = sc * CHUNK
    t = l_lat + l_ctx
    n_in = len(segs)
    gdn = kind == "gdn"
    g = min(HEADS_PER_GROUP, h)
    n = g * CHUNK
    c_names, c_arrs, c_specs = _const_args(kind, g, h, rev, d_index)

    def body(*refs):
        in_refs = refs[:n_in]
        pos = n_in
        tail_ref = None
        if gdn:
            tail_ref = refs[pos]
            pos += 1
        v0_ref, v1_ref = refs[pos], refs[pos + 1]
        pos += 2
        prev_ref = None
        if o_prev is not None:
            prev_ref = refs[pos]
            pos += 1
        c_refs = refs[pos:pos + len(c_names)]
        pos += len(c_names)
        o_ref, st_ref = refs[pos], refs[pos + 1]
        pos += 2
        x_ref = None
        if gdn:
            x_ref = refs[pos]
            pos += 1
        s_scr = refs[pos]

        @pl.when(pl.program_id(0) == 0)
        def _():
            s_scr[...] = jnp.zeros_like(s_scr)

        def instance(k, sub, gi, before, after):
            rs = pl.ds(sub * CHUNK, CHUNK)
            g0 = gi * g
            hs = pl.ds(g0, g)
            cst = _load_consts(c_names, c_refs, g0, g)
            state = (lambda: [s_scr[g0 + i] for i in range(g)]) if before is None else (lambda: before[0])
            ins = [r[hs, rs, :].reshape(n, LANES) for r in in_refs]
            if gdn:
                o, new, xinv, old = yield from _gdn_chunk(*ins, tail_ref[0, rs, :], v0_ref[hs], v1_ref[hs], state, cst, g)
                x_ref[k, gi] = xinv
            else:
                o, new, old = yield from _hg_chunk(*ins, v0_ref[hs], v1_ref[hs], state, cst, g)
            after[0] = new
            for i in range(g):
                st_ref[k, g0 + i] = old[i]
            o = o.reshape(g, CHUNK, LANES)
            if prev_ref is not None:
                o = o + prev_ref[hs, rs, :]
            o_ref[hs, rs, :] = o
            if k == sc - 1:
                for i in range(g):
                    s_scr[g0 + i] = new[i]

        gens = []
        handed = [None] * (h // g)
        for k, sub in enumerate(subs):
            for gi in range(h // g):
                after = [None]
                gens.append(instance(k, sub, gi, handed[gi], after))
                handed[gi] = after
        _lockstep(gens)

    seg_spec = lambda sg: pl.BlockSpec((h, rows, LANES), lambda i: (sg, cidx(i), 0))
    vec_spec = pl.BlockSpec((h, 1, LANES), lambda i: (0, 0, 0))
    in_specs = [seg_spec(sg) for sg in segs]
    args = [cbm] * n_in
    if gdn:
        in_specs.append(pl.BlockSpec((1, rows, LANES), lambda i: (0, cidx(i), 0)))
        args.append(tail)
    in_specs += [vec_spec, vec_spec]
    args += list(vecs)
    if o_prev is not None:
        in_specs.append(seg_spec(0))
        args.append(o_prev)
    in_specs += c_specs
    args += c_arrs
    return pl.pallas_call(
        body, name=name, grid=(n_steps,), in_specs=in_specs,
        out_specs=[seg_spec(0), pl.BlockSpec((sc, h, LANES, LANES), lambda i: (i, 0, 0, 0))]
        + ([pl.BlockSpec((sc, h // g, n, n), lambda i: (i, 0, 0, 0))] if gdn else []),
        out_shape=[jax.ShapeDtypeStruct((h, t, LANES), F32),
                   jax.ShapeDtypeStruct((n_steps * sc, h, LANES, LANES), F32)]
        + ([jax.ShapeDtypeStruct((n_steps * sc, h // g, n, n), F32)] if gdn else []),
        scratch_shapes=[pltpu.VMEM((h, LANES, LANES), F32)],
        compiler_params=_params(("arbitrary",), VMEM_BIG),
    )(*args)


def _scan_bwd(kind, rev, cbm, segs, vecs, states, d_o, acc, l_lat, l_ctx, h, name, tail=None, d_index=0, x_kept=None,
              pack=None, pack_into=None, tail_into=None):
    sc = BWD_STEP
    n_steps, cidx, subs = _chunk_index(l_lat, l_ctx, rev, sc)
    rows = sc * CHUNK
    t = l_lat + l_ctx
    n_in = len(segs)
    gdn = kind == "gdn"
    n_grad = n_in + (1 if gdn else 0)
    assert len(acc) == n_grad
    step_of = lambda j: n_steps - 1 - j
    g = min(HEADS_PER_GROUP, h)
    n = g * CHUNK
    c_names, c_arrs, c_specs = _const_args(kind, g, h, rev, d_index)
    items = pack or []
    packed_at = {it[1]: p for p, it in enumerate(items) if it[0] == "grad"}
    copies = [(p, it[1]) for p, it in enumerate(items) if it[0] == "copy"]
    separate = [m for m in range(n_in) if m not in packed_at]
    in_place = pack_into if pack_into is not None else tail_into
    assert pack_into is None or tail_into is None
    tail_blocks = tail_into[2] if tail_into is not None else 1

    def body(*refs):
        in_refs = refs[:n_in]
        pos = n_in
        tail_ref = None
        if gdn:
            tail_ref = refs[pos]
            pos += 1
        v0_ref, v1_ref, st_ref, do_ref = refs[pos:pos + 4]
        pos += 4
        xk_ref = None
        if gdn:
            xk_ref = refs[pos]
            pos += 1
        acc_refs = []
        for a in acc:
            if a is None:
                acc_refs.append(None)
            else:
                acc_refs.append(refs[pos])
                pos += 1
        copy_refs = refs[pos:pos + len(copies)]
        pos += len(copies) + (1 if in_place is not None else 0)
        c_refs = refs[pos:pos + len(c_names)]
        pos += len(c_names)
        sep_refs = refs[pos:pos + len(separate)]
        pos += len(separate)
        pack_ref = None
        if items:
            pack_ref = refs[pos]
            pos += 1
        tail_out = None
        if gdn:
            tail_out = refs[pos]
            pos += 1
        dv0_ref, dv1_ref, ds_scr = refs[pos:pos + 3]

        @pl.when(pl.program_id(0) == 0)
        def _():
            ds_scr[...] = jnp.zeros_like(ds_scr)
            dv0_ref[...] = jnp.zeros_like(dv0_ref)
            dv1_ref[...] = jnp.zeros_like(dv1_ref)

        if gdn:
            tail_out[0] = jnp.zeros((rows, LANES), F32) if acc_refs[n_in] is None else acc_refs[n_in][0]
            for q in range(1, tail_blocks):
                tail_out[q] = jnp.zeros((rows, LANES), F32)

        n_groups = h // g

        rss = [pl.ds(subs[k] * CHUNK, CHUNK) for k in range(sc)]
        hss = [pl.ds(gi * g, g) for gi in range(n_groups)]
        csts = [_load_consts(c_names, c_refs, gi * g, g) for gi in range(n_groups)]
        xks = [[xk_ref[k, gi] for gi in range(n_groups)] for k in range(sc)] if gdn else None

        def link(k, gi, ins, tl, v0, v1, state, after):
            if gdn:
                res = yield from _gdn_chunk(*ins, tl, v0, v1, state, csts[gi], g, xks[k][gi])
            else:
                res = yield from _hg_chunk(*ins, v0, v1, state, csts[gi], g, True)
            after[0] = res[1]
            return res[0]

        def fn(ins, tls, v0s, v1s, s0):
            gens = []
            handed = [None] * n_groups
            for k in range(sc):
                for gi in range(n_groups):
                    before, after = handed[gi], [None]
                    state = (functools.partial(lambda s: s, s0[gi]) if before is None
                             else functools.partial(lambda cell: cell[0], before))
                    gens.append(link(k, gi, ins[k][gi], tls[k], v0s[gi], v1s[gi], state, after))
                    handed[gi] = after
            outs = _lockstep(gens)
            return ([[outs[k * n_groups + gi] for gi in range(n_groups)] for k in range(sc)],
                    [handed[gi][0] for gi in range(n_groups)])

        ins = [[[r[hs, rs, :].reshape(n, LANES) for r in in_refs] for hs in hss] for rs in rss]
        tls = [tail_ref[0, rs, :] if gdn else jnp.zeros((1, 1), F32) for rs in rss]
        s0 = [[st_ref[0, gi * g + i] for i in range(g)] for gi in range(n_groups)]
        _, vj = jax.vjp(fn, ins, tls, [v0_ref[hs] for hs in hss], [v1_ref[hs] for hs in hss], s0)
        d_o_all = [[do_ref[hs, rs, :].reshape(n, LANES) for hs in hss] for rs in rss]
        d_fin = [[ds_scr[gi * g + i] for i in range(g)] for gi in range(n_groups)]
        d_ins, d_tls, d_v0s, d_v1s, d_s0 = vj((d_o_all, d_fin))
        for gi, hs in enumerate(hss):
            g0 = gi * g
            for k, rs in enumerate(rss):
                for m in range(n_in):
                    gm = d_ins[k][gi][m].reshape(g, CHUNK, LANES)
                    if acc_refs[m] is not None:
                        gm = gm + acc_refs[m][hs, rs, :]
                    if m in packed_at:
                        pack_ref[pl.ds(packed_at[m] * h + g0, g), rs, :] = gm
                    else:
                        sep_refs[separate.index(m)][hs, rs, :] = gm
                for (p, _), c_ref in zip(copies, copy_refs):
                    pack_ref[pl.ds(p * h + g0, g), rs, :] = c_ref[hs, rs, :]
            dv0, dv1 = d_v0s[gi], d_v1s[gi]
            if gdn:
                dv0 = jnp.broadcast_to(jnp.sum(dv0, axis=2, keepdims=True), dv0.shape)
                dv1 = jnp.broadcast_to(jnp.sum(dv1, axis=2, keepdims=True), dv1.shape)
            dv0_ref[hs] += dv0
            dv1_ref[hs] += dv1
            for i in range(g):
                ds_scr[g0 + i] = d_s0[gi][i]
        if gdn:
            for k, rs in enumerate(rss):
                tail_out[0, rs, :] += d_tls[k]

    seg_spec = lambda sg: pl.BlockSpec((h, rows, LANES), lambda j: (sg, cidx(step_of(j)), 0))
    tail_spec = pl.BlockSpec((1, rows, LANES), lambda j: (0, cidx(step_of(j)), 0))
    vec_spec = pl.BlockSpec((h, 1, LANES), lambda j: (0, 0, 0))
    in_specs = [seg_spec(sg) for sg in segs]
    args = [cbm] * n_in
    if gdn:
        in_specs.append(tail_spec)
        args.append(tail)
    in_specs += [vec_spec, vec_spec, pl.BlockSpec((sc, h, LANES, LANES), lambda j: (step_of(j), 0, 0, 0)),
                 seg_spec(0)]
    args += list(vecs) + [states, d_o]
    if gdn:
        in_specs.append(pl.BlockSpec((sc, h // g, n, n), lambda j: (step_of(j), 0, 0, 0)))
        args.append(x_kept)
    for k, a in enumerate(acc):
        if a is not None:
            in_specs.append(tail_spec if (gdn and k == n_in) else seg_spec(0))
            args.append(a)
    for _, arr in copies:
        in_specs.append(seg_spec(0))
        args.append(arr)
    aliases = {}
    if in_place is not None:
        aliases[len(args)] = len(separate) + (1 if (items and tail_into is not None) else 0)
        in_specs.append(ANY)
        args.append(in_place[0])
    in_specs += c_specs
    args += c_arrs
    out_specs = [seg_spec(0)] * len(separate)
    out_shape = [jax.ShapeDtypeStruct((h, t, LANES), F32)] * len(separate)
    if items:
        pb = len(items) * h
        first = pack_into[1] if pack_into is not None else 0
        assert first % pb == 0
        out_specs.append(pl.BlockSpec((pb, rows, LANES), lambda j: (first // pb, cidx(step_of(j)), 0)))
        out_shape.append(jax.ShapeDtypeStruct(pack_into[0].shape if pack_into is not None else (pb, t, LANES), F32))
    if gdn and tail_into is not None:
        first_t = tail_into[1]
        assert first_t % tail_blocks == 0
        out_specs.append(pl.BlockSpec((tail_blocks, rows, LANES), lambda j: (first_t // tail_blocks, cidx(step_of(j)), 0)))
        out_shape.append(jax.ShapeDtypeStruct(tail_into[0].shape, F32))
    elif gdn:
        out_specs.append(tail_spec)
        out_shape.append(jax.ShapeDtypeStruct((1, t, LANES), F32))
    out_specs += [vec_spec, vec_spec]
    out_shape += [jax.ShapeDtypeStruct((h, 1, LANES), F32)] * 2
    return pl.pallas_call(
        body, name=name, grid=(n_steps,), in_specs=in_specs, out_specs=out_specs, out_shape=out_shape,
        scratch_shapes=[pltpu.VMEM((h, LANES, LANES), F32)], input_output_aliases=aliases,
        compiler_params=_params(("arbitrary",), VMEM_BIG),
    )(*args)


def _gout_tile(oa, ob, za, zb, naw, nbw):
    def hn(o, w):
        return o * lax.rsqrt(jnp.mean(o * o, axis=-1, keepdims=True) + NORM_EPS) * w

    ya = _silu(za) * hn(oa, naw)
    yb = _silu(zb) * hn(ob, nbw)
    nh = oa.shape[0]
    return jnp.concatenate([ya[i] for i in range(nh)] + [yb[i] for i in range(nh)], axis=1)


def _gout_fwd(oa, ob, y_cbm, naw, nbw, l_lat, h):
    tm = _pick(l_lat, (256, 128, 64))
    blk = lambda sg: pl.BlockSpec((h, tm, LANES), lambda i: (sg, i, 0))
    vec = pl.BlockSpec((h, 1, LANES), lambda i: (0, 0, 0))

    def body(oa_ref, ob_ref, za_ref, zb_ref, na_ref, nb_ref, y_ref):
        y_ref[...] = _gout_tile(oa_ref[...], ob_ref[...], za_ref[...], zb_ref[...], na_ref[...], nb_ref[...]).astype(BF16)

    return pl.pallas_call(
        body, name="gout_fwd", grid=(l_lat // tm,),
        in_specs=[blk(0), blk(0), blk(4), blk(8), vec, vec],
        out_specs=pl.BlockSpec((tm, 2 * h * LANES), lambda i: (i, 0)),
        out_shape=jax.ShapeDtypeStruct((l_lat, 2 * h * LANES), BF16),
        compiler_params=_params(("arbitrary",)),
    )(oa, ob, y_cbm, y_cbm, naw, nbw)


def _gout_bwd(oa, ob, y_cbm, naw, nbw, dymix, l_lat, l_ctx, h):
    tm = _pick(l_ctx, (256, 128, 64))
    nl, nc = l_lat // tm, l_ctx // tm
    t = l_lat + l_ctx
    blk_in = lambda sg: pl.BlockSpec((h, tm, LANES), lambda i: (sg, jnp.minimum(i, nl - 1), 0))
    blk_out = pl.BlockSpec((h, tm, LANES), lambda i: (0, i, 0))
    vec = pl.BlockSpec((h, 1, LANES), lambda i: (0, 0, 0))

    def body(oa_ref, ob_ref, za_ref, zb_ref, na_ref, nb_ref, dy_ref, doa_ref, dob_ref, dza_ref, dzb_ref, dna_ref, dnb_ref):
        i = pl.program_id(0)

        @pl.when(i == 0)
        def _():
            dna_ref[...] = jnp.zeros_like(dna_ref)
            dnb_ref[...] = jnp.zeros_like(dnb_ref)

        @pl.when(i < nl)
        def _():
            _, vj = jax.vjp(_gout_tile, oa_ref[...], ob_ref[...], za_ref[...], zb_ref[...], na_ref[...], nb_ref[...])
            doa, dob, dza, dzb, dna, dnb = vj(dy_ref[...])
            doa_ref[...] = doa
            dob_ref[...] = dob
            dza_ref[...] = dza
            dzb_ref[...] = dzb
            dna_ref[...] += dna
            dnb_ref[...] += dnb

        @pl.when(i >= nl)
        def _():
            for r in (doa_ref, dob_ref, dza_ref, dzb_ref):
                r[...] = jnp.zeros_like(r)

    big = jax.ShapeDtypeStruct((h, t, LANES), F32)
    small = jax.ShapeDtypeStruct((h, 1, LANES), F32)
    dza_out = pl.BlockSpec((h, tm, LANES), lambda i: (4, i, 0))
    return pl.pallas_call(
        body, name="gout_bwd", grid=(nl + nc,),
        in_specs=[blk_in(0), blk_in(0), blk_in(4), blk_in(8), vec, vec,
                  pl.BlockSpec((tm, 2 * h * LANES), lambda i: (jnp.minimum(i, nl - 1), 0))],
        out_specs=[blk_out, blk_out, dza_out, blk_out, vec, vec],
        out_shape=[big, big, jax.ShapeDtypeStruct(y_cbm.shape, F32), big, small, small],
        compiler_params=_params(("arbitrary",)),
    )(oa, ob, y_cbm, y_cbm, naw, nbw, dymix)


def _head_tile(xt, mixt, tgt, gt, fw):
    xo = xt + gt * mixt
    y = xo * lax.rsqrt(jnp.mean(xo * xo, axis=-1, keepdims=True) + NORM_EPS) * fw
    err = y - tgt
    return 0.5 * jnp.sum(jnp.mean(err * err, axis=-1, keepdims=True), axis=0, keepdims=True)


def _loss_head(x, mix, target, mods, final_w):
    l_lat, d = x.shape
    tm = _pick(l_lat, (256, 128, 64))
    tok = pl.BlockSpec((tm, d), lambda i: (i, 0))
    row = pl.BlockSpec((1, d), lambda i: (0, 0))

    def body(x_ref, m_ref, t_ref, mod_ref, fw_ref, loss_ref, dmix_ref, dx_ref, dgt_ref, dfw_ref):
        @pl.when(pl.program_id(0) == 0)
        def _():
            loss_ref[...] = jnp.zeros_like(loss_ref)
            dgt_ref[...] = jnp.zeros_like(dgt_ref)
            dfw_ref[...] = jnp.zeros_like(dfw_ref)

        gt = mod_ref[0:1, 2 * d:3 * d]
        fn = lambda xt, mt, g, fw: _head_tile(xt, mt, t_ref[...], g, fw)
        val, vj = jax.vjp(fn, x_ref[...], m_ref[...], gt, fw_ref[...])
        dxt, dmt, dgt, dfw = vj(jnp.ones((1, 1), F32))
        loss_ref[...] += jnp.broadcast_to(val, loss_ref.shape)
        dmix_ref[...] = dmt.astype(BF16)
        dx_ref[...] = dxt
        dgt_ref[...] += dgt
        dfw_ref[...] += dfw

    return pl.pallas_call(
        body, name="loss_head", grid=(l_lat // tm,),
        in_specs=[tok, tok, tok, pl.BlockSpec((8, 3 * d), lambda i: (0, 0)), row],
        out_specs=[pl.BlockSpec((8, LANES), lambda i: (0, 0)), tok, tok, row, row],
        out_shape=[jax.ShapeDtypeStruct((8, LANES), F32), jax.ShapeDtypeStruct((l_lat, d), BF16),
                   jax.ShapeDtypeStruct((l_lat, d), F32), jax.ShapeDtypeStruct((1, d), F32),
                   jax.ShapeDtypeStruct((1, d), F32)],
        compiler_params=_params(("arbitrary",)),
    )(x, mix, target, mods, final_w)


def _adamw(w, g_slots, m, v, name):
    rows, cols = w.shape
    n = g_slots.shape[0]
    tr = _pick(rows, (64, 32, 16, 8))

    def body(w_ref, g_ref, m_ref, v_ref, go_ref, d_ref, nm_ref, nv_ref):
        gg = g_ref[0].astype(F32)
        for s in range(1, n):
            gg = gg + g_ref[s].astype(F32)
        m2 = ADAM_B1 * m_ref[...] + (1.0 - ADAM_B1) * gg
        v2 = ADAM_B2 * v_ref[...] + (1.0 - ADAM_B2) * (gg * gg)
        m_hat = m2 / (1.0 - ADAM_B1 ** ADAM_STEP)
        v_hat = v2 / (1.0 - ADAM_B2 ** ADAM_STEP)
        go_ref[...] = gg
        d_ref[...] = -ADAM_LR * (m_hat / (jnp.sqrt(v_hat) + ADAM_EPS) + ADAM_WD * w_ref[...])
        nm_ref[...] = m2
        nv_ref[...] = v2

    spec = pl.BlockSpec((tr, cols), lambda i: (i, 0))
    shp = jax.ShapeDtypeStruct((rows, cols), F32)
    return pl.pallas_call(
        body, name=name, grid=(rows // tr,),
        in_specs=[spec, pl.BlockSpec((n, tr, cols), lambda i: (0, i, 0)), spec, spec],
        out_specs=[spec] * 4, out_shape=[shp] * 4,
        compiler_params=_params(("arbitrary",)),
    )(w, g_slots, m, v)


def _adamw_nd(w, g_slots, m, v, name):
    shape = w.shape
    two_d = (-1, shape[-1])
    w2 = w.reshape(two_d)
    outs = _adamw(w2, g_slots.reshape((g_slots.shape[0],) + w2.shape), m.reshape(two_d), v.reshape(two_d), name)
    return [o.reshape(shape) for o in outs]


def kernel(x, c, ctx, c_ctx, norm_w, ada_w, ada_b, w_in, conv_w, hg_lb_logits, gdn_a_log, gdn_dt_bias, ha_norm_w, hb_norm_w, w_out, final_norm_w, loss_target, m_c_ctx, m_norm_w, m_ada_w, m_ada_b, m_w_in, m_conv_w, m_hg_lb_logits, m_gdn_a_log, m_gdn_dt_bias, m_ha_norm_w, m_hb_norm_w, m_w_out, m_final_norm_w, v_c_ctx, v_norm_w, v_ada_w, v_ada_b, v_w_in, v_conv_w, v_hg_lb_logits, v_gdn_a_log, v_gdn_dt_bias, v_ha_norm_w, v_hb_norm_w, v_w_out, v_final_norm_w):
    l_lat, d = x.shape[1], x.shape[2]
    l_ctx = ctx.shape[1]
    t = l_lat + l_ctx
    h = d // LANES
    n_in = 9 * d + 4 * h
    nb = -(-(9 * h + 1) // COLS_PER_TILE) * COLS_PER_TILE
    n_pad = nb * LANES
    win_c, ada_c, conv_c, lb_c, wout_r = w_in.shape[2], ada_w.shape[2], conv_w.shape[3], hg_lb_logits.shape[2], w_out.shape[1]
    me = _me()

    gw_in, gw_out, gw_ada, gw_conv, gw_lb = _gather_two_level(
        [w_in[0].astype(BF16), w_out[0].astype(BF16), ada_w[0], conv_w[0], hg_lb_logits], "gather_weights")
    join = lambda g, axis: jnp.concatenate([g[k] for k in range(N_DEV)], axis=axis)
    w_in_full = join(gw_in, 1)
    w_out_full = gw_out.reshape(N_DEV * wout_r, d)
    ada_full = join(gw_ada, 1)
    conv_full = join(gw_conv, 2)
    lb_full = join(gw_lb, 2)
    w_pad = jnp.pad(w_in_full, ((0, 0), (0, n_pad - n_in)))
    w_pad_t = w_pad.T
    w_out_t = w_out_full.T

    lbl = lb_full.reshape(2, 2, h, 1, LANES)
    conv_rows = jnp.pad(conv_full.reshape(9, 3 * h, LANES).transpose(1, 0, 2), ((0, 0), (0, 7), (0, 0)))
    lane_bc = lambda a: jnp.broadcast_to(a.reshape(2, h, 1, 1), (2, h, 1, LANES))
    avec, dtvec = lane_bc(gdn_a_log[0]), lane_bc(gdn_dt_bias[0])
    naw, nbw = ha_norm_w[0].reshape(h, 1, LANES), hb_norm_w[0].reshape(h, 1, LANES)

    cvec = jnp.concatenate([c, c_ctx[None, :], jnp.zeros((6, d), F32)], axis=0)
    mods = _ada_fwd(cvec, ada_full, ada_b)
    x2, ctx2, tgt2 = x[0], ctx[0], loss_target[0]
    h_all = _prenorm_fwd(x2, ctx2, norm_w, mods)
    y = _matmul(h_all, w_pad, "proj", out_cbm=True, tn=COLS_PER_TILE * LANES)
    tail = lax.slice_in_dim(y, 9 * h, 9 * h + 1, axis=0)
    u = _conv_call(y, conv_rows, l_lat, 5 * h, 3 * h, "conv_fwd")

    oa, st_af = _scan_fwd("hg", False, y, (0, 1, 3), (lbl[0, 0], lbl[1, 0]), None, l_lat, l_ctx, h, "hg_fwd_f")
    oa, st_ab = _scan_fwd("hg", True, y, (0, 2, 3), (lbl[0, 1], lbl[1, 1]), oa, l_lat, l_ctx, h, "hg_fwd_b")
    ob, st_bf, xk_f = _scan_fwd("gdn", False, u, (0, 1, 2), (avec[0], dtvec[0]), None, l_lat, l_ctx, h, "gdn_fwd_f", tail, 0)
    ob, st_bb, xk_b = _scan_fwd("gdn", True, u, (0, 1, 2), (avec[1], dtvec[1]), ob, l_lat, l_ctx, h, "gdn_fwd_b", tail, 1)

    ymix = _gout_fwd(oa, ob, y, naw, nbw, l_lat, h)
    mix = _matmul(ymix, w_out_full, "out_proj")
    loss_blk, dmix, dx_res, dgt, dfw = _loss_head(x2, mix, tgt2, mods, final_norm_w.reshape(1, d))

    dymix = _matmul(dmix, w_out_t, "d_ymix")
    dw_out = _matmul(ymix, dmix, "d_w_out", a_t=True)
    doa, dob, dy, dzb, dnaw, dnbw = _gout_bwd(oa, ob, y, naw, nbw, dymix, l_lat, l_ctx, h)

    gq, gff, gi, dl0f, dl1f = _scan_bwd("hg", False, y, (0, 1, 3), (lbl[0, 0], lbl[1, 0]), st_af, doa,
                                        [None, None, None], l_lat, l_ctx, h, "hg_bwd_f")
    dy, dl0b, dl1b = _scan_bwd("hg", True, y, (0, 2, 3), (lbl[0, 1], lbl[1, 1]), st_ab, doa, [gq, None, gi],
                               l_lat, l_ctx, h, "hg_bwd_b",
                               pack=[("grad", 0), ("copy", gff), ("grad", 1), ("grad", 2)], pack_into=(dy, 0))
    guq, guk, guv, gtail, da_f, ddt_f = _scan_bwd("gdn", False, u, (0, 1, 2), (avec[0], dtvec[0]), st_bf, dob,
                                                  [None] * 4, l_lat, l_ctx, h, "gdn_bwd_f", tail, 0, xk_f)
    du, dy, da_b, ddt_b = _scan_bwd("gdn", True, u, (0, 1, 2), (avec[1], dtvec[1]), st_bb, dob,
                                    [guq, guk, guv, gtail], l_lat, l_ctx, h, "gdn_bwd_b", tail, 1, xk_b,
                                    pack=[("grad", 0), ("grad", 1), ("grad", 2)], tail_into=(dy, 9 * h, nb - 9 * h))
    dy, dconv_rows = _conv_call(y, conv_rows, l_lat, 5 * h, 3 * h, "conv_bwd", dout=du, passthru=dzb, into=(dy, 5 * h))
    dh = _matmul(dy, w_pad_t, "d_h", a_cbm=True)
    dw_in = _matmul(h_all, dy, "d_w_in", a_t=True, b_cbm=True)
    grad_x, dnorm_w, dmods_pre = _prenorm_bwd(x2, ctx2, norm_w, mods, dh, dx_res)

    dmods = jnp.concatenate([dmods_pre[:, :2 * d],
                             jnp.concatenate([dgt, jnp.zeros((7, d), F32)], axis=0)], axis=1)
    dcond = _ada_bwd(cvec, ada_full, dmods)

    dw_in_s = jnp.stack([dw_in[:, k * win_c:(k + 1) * win_c] for k in range(N_DEV)])
    dw_out_s = dw_out.reshape(N_DEV, wout_r, d)
    dconv_s = dconv_rows[:, :9, :].transpose(1, 0, 2).reshape(3, 3, N_DEV, conv_c).transpose(2, 0, 1, 3)
    dlb_s = jnp.stack([jnp.stack([dl0f, dl0b]), jnp.stack([dl1f, dl1b])]).reshape(2, 2, N_DEV, lb_c).transpose(2, 0, 1, 3)
    slabs = [dw_in_s, dw_out_s, dconv_s, dlb_s]
    theirs = _swap_with_sibling([a.astype(BF16) for a in slabs], "swap_grads")
    pairs = [_pair_sum(a, b, "pair_sum_%d" % k) for k, (a, b) in enumerate(zip(slabs, theirs))]
    s_w_in, s_w_out, s_conv, s_lb = _scatter_over_chips(pairs, "scatter_grads")

    da = jnp.stack([da_f[:, 0, 0], da_b[:, 0, 0]])
    ddt = jnp.stack([ddt_f[:, 0, 0], ddt_b[:, 0, 0]])
    small_parts = [dcond[1], dnorm_w, dmods[0] + dmods[1], dnaw, dnbw, dfw, da, ddt, c[0], dmods[0], dmods[1]]
    small_blob, soff = _to_blob(small_parts)
    gathered, = _exchange([small_blob], "gather_small", True)
    summed = _sum_slots(gathered, "sum_small").reshape(-1)
    gflat = gathered.reshape(N_DEV, -1)
    take = lambda k, n: summed[soff[k]:soff[k] + n]
    one = lambda k, n, shape: take(k, n).reshape((1,) + shape)
    s_c_ctx = one(0, d, (d,))
    s_norm_w = one(1, d, (1, d))
    s_ada_b = one(2, 3 * d, (1, 3 * d))
    s_ha = one(3, d, (1, h, LANES))
    s_hb = one(4, d, (1, h, LANES))
    s_final = one(5, d, (d,))
    s_a_log = one(6, 2 * h, (1, 2, h))
    s_dt = one(7, 2 * h, (1, 2, h))
    conds = jnp.concatenate([gflat[:, soff[8]:soff[8] + d], c_ctx[None, :], jnp.zeros((7, d), F32)], axis=0)
    col0 = me * ada_c
    dm_lat = lax.dynamic_slice_in_dim(gflat[:, soff[9]:soff[9] + 3 * d], col0, ada_c, axis=1)
    dm_ctx = lax.dynamic_slice_in_dim(gflat[:, soff[10]:soff[10] + 3 * d], col0, ada_c, axis=1)
    s_ada_w = _ada_wgrad(conds, dm_lat, dm_ctx).reshape(1, 1, d, ada_c)

    loss = lax.psum(loss_blk[0, 0], ("x", "y", "c"))

    weights = [("c_ctx", c_ctx, s_c_ctx, m_c_ctx, v_c_ctx), ("norm_w", norm_w, s_norm_w, m_norm_w, v_norm_w),
               ("ada_w", ada_w, s_ada_w, m_ada_w, v_ada_w), ("ada_b", ada_b, s_ada_b, m_ada_b, v_ada_b),
               ("w_in", w_in, s_w_in, m_w_in, v_w_in), ("conv_w", conv_w, s_conv, m_conv_w, v_conv_w),
               ("hg_lb_logits", hg_lb_logits, s_lb, m_hg_lb_logits, v_hg_lb_logits),
               ("gdn_a_log", gdn_a_log, s_a_log, m_gdn_a_log, v_gdn_a_log),
               ("gdn_dt_bias", gdn_dt_bias, s_dt, m_gdn_dt_bias, v_gdn_dt_bias),
               ("ha_norm_w", ha_norm_w, s_ha, m_ha_norm_w, v_ha_norm_w), ("hb_norm_w", hb_norm_w, s_hb, m_hb_norm_w, v_hb_norm_w),
               ("w_out", w_out, s_w_out, m_w_out, v_w_out), ("final_norm_w", final_norm_w, s_final, m_final_norm_w, v_final_norm_w)]
    grads, deltas, new_ms, new_vs = [], [], [], []
    for nm, w, gs, m, v in weights:
        g, dl, m2, v2 = _adamw_nd(w, gs, m, v, "adamw_" + nm)
        grads.append(g)
        deltas.append(dl)
        new_ms.append(m2)
        new_vs.append(v2)
    return (loss, grad_x[None], *grads, *deltas, *new_ms, *new_vs)
```

```python
import functools

import jax
import jax.numpy as jnp
import numpy as np
from jax import lax
from jax.experimental import pallas as pl
from jax.experimental.pallas import tpu as pltpu

F32 = jnp.float32
BF16 = jnp.bfloat16
HI = lax.Precision.HIGHEST
HIGH = lax.Precision.HIGH
MESH = pl.DeviceIdType.MESH
ANY = pl.BlockSpec(memory_space=pl.ANY)

N_DEV = 8
LANES = 128
CHUNK = 64
GRID_W = 64
NORM_EPS = 1e-6
NEG = -1e30
COLS_PER_TILE = 8
VMEM_BIG = 56 * 1024 * 1024
HEADS_PER_GROUP = 4
FWD_STEP = 4
BWD_STEP = 2
HALVES = (32, 16, 8, 4, 2, 1)

ADAM_LR, ADAM_B1, ADAM_B2, ADAM_EPS, ADAM_WD, ADAM_STEP = 0.001, 0.9, 0.999, 1e-08, 0.01, 10


def _params(sem=None, vmem=None):
    kw = {}
    if sem is not None:
        kw["dimension_semantics"] = sem
    if vmem is not None:
        kw["vmem_limit_bytes"] = vmem
    return pltpu.CompilerParams(**kw)


def _pick(n, cands):
    for c in cands:
        if n % c == 0:
            return c
    return n


def _silu(x):
    return x * jax.nn.sigmoid(x)


def _mm(a, b, prec=None):
    return lax.dot_general(a, b, (((1,), (0,)), ((), ())), precision=prec, preferred_element_type=F32)


def _mm_nt(a, b, prec=None):
    return lax.dot_general(a, b, (((1,), (1,)), ((), ())), precision=prec, preferred_element_type=F32)


def _mm_tn(a, b, prec=None):
    return lax.dot_general(a, b, (((0,), (0,)), ((), ())), precision=prec, preferred_element_type=F32)


def _me():
    return 4 * lax.axis_index("x") + 2 * lax.axis_index("y") + lax.axis_index("c")


def _peer(r):
    x, y, c = lax.axis_index("x"), lax.axis_index("y"), lax.axis_index("c")
    px = 1 - x if r & 4 else x
    py = 1 - y if r & 2 else y
    pc = 1 - c if r & 1 else c
    return (px, py, pc), 4 * px + 2 * py + pc


def _exchange(arrs, name, gather):
    n_a = len(arrs)

    def body(*refs):
        srcs, outs = refs[:n_a], refs[n_a:2 * n_a]
        send_sems, recv_sems, local_sems = refs[2 * n_a:]
        me = _me()
        copies = []
        for k in range(n_a):
            local = pltpu.make_async_copy(srcs[k] if gather else srcs[k].at[me], outs[k].at[me], local_sems.at[k])
            local.start()
            copies.append(local)
        for r in range(N_DEV - 1, 0, -1):
            dev, pid = _peer(r)
            for k in range(n_a):
                cp = pltpu.make_async_remote_copy(
                    src_ref=srcs[k] if gather else srcs[k].at[pid], dst_ref=outs[k].at[me],
                    send_sem=send_sems.at[(r - 1) * n_a + k], recv_sem=recv_sems.at[(r - 1) * n_a + k],
                    device_id=dev, device_id_type=MESH)
                cp.start()
                copies.append(cp)
        for cp in copies:
            cp.wait()

    n_sem = (N_DEV - 1) * n_a
    return pl.pallas_call(
        body, name=name,
        out_shape=[jax.ShapeDtypeStruct(((N_DEV,) + a.shape) if gather else a.shape, a.dtype) for a in arrs],
        in_specs=[ANY] * n_a, out_specs=[ANY] * n_a,
        scratch_shapes=[pltpu.SemaphoreType.DMA((n_sem,)), pltpu.SemaphoreType.DMA((n_sem,)),
                        pltpu.SemaphoreType.DMA((n_a,))],
    )(*arrs)


def _gather_two_level(arrs, name):
    n_a = len(arrs)

    def body(*refs):
        srcs, outs = refs[:n_a], refs[n_a:2 * n_a]
        send_sems, recv_sems, local_sems = refs[2 * n_a:]
        x, y, c = lax.axis_index("x"), lax.axis_index("y"), lax.axis_index("c")
        me, sibling = (x, y, c), (x, y, 1 - c)
        chips = [(1 - x, y), (x, 1 - y), (1 - x, 1 - y)]

        def copy(k, j, block, to, src=None):
            slot = outs[k].at[4 * block[0] + 2 * block[1] + block[2]]
            return pltpu.make_async_remote_copy(
                src_ref=slot if src is None else src, dst_ref=slot,
                send_sem=send_sems.at[j * n_a + k], recv_sem=recv_sems.at[j * n_a + k],
                device_id=to, device_id_type=MESH)

        mine = [pltpu.make_async_copy(srcs[k], outs[k].at[4 * x + 2 * y + c], local_sems.at[k]) for k in range(n_a)]
        for cp in mine:
            cp.start()
        first = [copy(k, 1 + j, me, (*chip, c), srcs[k]) for j, chip in enumerate(chips) for k in range(n_a)]
        first += [copy(k, 0, me, sibling, srcs[k]) for k in range(n_a)]
        for cp in first:
            cp.start()
        passed = []
        for j, chip in enumerate(chips):
            for k in range(n_a):
                copy(k, 1 + j, (*chip, c), me).wait_recv()
                cp = copy(k, 4 + j, (*chip, c), sibling)
                cp.start()
                passed.append(cp)
        for k in range(n_a):
            copy(k, 0, sibling, me).wait_recv()
        for j, chip in enumerate(chips):
            for k in range(n_a):
                copy(k, 4 + j, (*chip, 1 - c), me).wait_recv()
        for cp in first + passed:
            cp.wait_send()
        for cp in mine:
            cp.wait()

    n_sem = (N_DEV - 1) * n_a
    return pl.pallas_call(
        body, name=name,
        out_shape=[jax.ShapeDtypeStruct((N_DEV,) + a.shape, a.dtype) for a in arrs],
        in_specs=[ANY] * n_a, out_specs=[ANY] * n_a,
        scratch_shapes=[pltpu.SemaphoreType.DMA((n_sem,)), pltpu.SemaphoreType.DMA((n_sem,)),
                        pltpu.SemaphoreType.DMA((n_a,))],
    )(*arrs)


def _swap_with_sibling(arrs, name):
    n_a = len(arrs)

    def body(*refs):
        srcs, outs = refs[:n_a], refs[n_a:2 * n_a]
        send_sems, recv_sems = refs[2 * n_a:]
        sibling = (lax.axis_index("x"), lax.axis_index("y"), 1 - lax.axis_index("c"))
        copies = [pltpu.make_async_remote_copy(src_ref=srcs[k], dst_ref=outs[k], send_sem=send_sems.at[k],
                                               recv_sem=recv_sems.at[k], device_id=sibling, device_id_type=MESH)
                  for k in range(n_a)]
        for cp in copies:
            cp.start()
        for cp in copies:
            cp.wait()

    return pl.pallas_call(
        body, name=name, out_shape=[jax.ShapeDtypeStruct(a.shape, a.dtype) for a in arrs],
        in_specs=[ANY] * n_a, out_specs=[ANY] * n_a,
        scratch_shapes=[pltpu.SemaphoreType.DMA((n_a,)), pltpu.SemaphoreType.DMA((n_a,))],
    )(*arrs)


def _scatter_over_chips(arrs, name):
    n_a = len(arrs)
    n_chip = N_DEV // 2

    def body(*refs):
        srcs, outs = refs[:n_a], refs[n_a:2 * n_a]
        send_sems, recv_sems, local_sems = refs[2 * n_a:]
        x, y, c = lax.axis_index("x"), lax.axis_index("y"), lax.axis_index("c")
        my_chip = 2 * x + y
        copies = []
        for k in range(n_a):
            local = pltpu.make_async_copy(srcs[k].at[4 * x + 2 * y + c], outs[k].at[my_chip], local_sems.at[k])
            local.start()
            copies.append(local)
        for r in range(n_chip - 1, 0, -1):
            px = 1 - x if r & 2 else x
            py = 1 - y if r & 1 else y
            for k in range(n_a):
                cp = pltpu.make_async_remote_copy(
                    src_ref=srcs[k].at[4 * px + 2 * py + c], dst_ref=outs[k].at[my_chip],
                    send_sem=send_sems.at[(r - 1) * n_a + k], recv_sem=recv_sems.at[(r - 1) * n_a + k],
                    device_id=(px, py, c), device_id_type=MESH)
                cp.start()
                copies.append(cp)
        for cp in copies:
            cp.wait()

    n_sem = (n_chip - 1) * n_a
    return pl.pallas_call(
        body, name=name,
        out_shape=[jax.ShapeDtypeStruct((n_chip,) + a.shape[1:], a.dtype) for a in arrs],
        in_specs=[ANY] * n_a, out_specs=[ANY] * n_a,
        scratch_shapes=[pltpu.SemaphoreType.DMA((n_sem,)), pltpu.SemaphoreType.DMA((n_sem,)),
                        pltpu.SemaphoreType.DMA((n_a,))],
    )(*arrs)


def _pair_sum(a, b, name):
    shape = a.shape
    a2, b2 = a.reshape(-1, shape[-1]), b.reshape(-1, shape[-1])
    rows, cols = a2.shape
    tr = _pick(rows, (256, 128, 64, 32, 16))

    def body(a_ref, b_ref, o_ref):
        o_ref[...] = (a_ref[...] + b_ref[...].astype(F32)).astype(BF16)

    spec = pl.BlockSpec((tr, cols), lambda i: (i, 0))
    return pl.pallas_call(
        body, name=name, grid=(rows // tr,), in_specs=[spec, spec], out_specs=spec,
        out_shape=jax.ShapeDtypeStruct((rows, cols), BF16), compiler_params=_params(("arbitrary",)),
    )(a2, b2).reshape(shape)


def _to_blob(parts):
    flat = [p.reshape(-1).astype(F32) for p in parts]
    offs, n = [], 0
    for f in flat:
        offs.append(n)
        n += f.shape[0]
    unit = 8 * LANES
    total = -(-n // unit) * unit
    if total > n:
        flat.append(jnp.zeros((total - n,), F32))
    return jnp.concatenate(flat).reshape(total // LANES, LANES), offs


def _sum_slots(buf, name):
    n, rows, cols = buf.shape
    tr = _pick(rows, (512, 256, 128, 64, 32, 16, 8))

    def body(b_ref, o_ref):
        acc = b_ref[0]
        for s in range(1, n):
            acc = acc + b_ref[s]
        o_ref[...] = acc

    return pl.pallas_call(
        body, name=name, grid=(rows // tr,),
        in_specs=[pl.BlockSpec((n, tr, cols), lambda i: (0, i, 0))],
        out_specs=pl.BlockSpec((tr, cols), lambda i: (i, 0)),
        out_shape=jax.ShapeDtypeStruct((rows, cols), F32),
        compiler_params=_params(("arbitrary",)),
    )(buf)


def _matmul(a, b, name, a_cbm=False, b_cbm=False, out_cbm=False, a_t=False, out_dtype=F32, tm=None, tn=None, tk=None):
    if a_cbm:
        m, k = a.shape[1], a.shape[0] * LANES
    elif a_t:
        k, m = a.shape
    else:
        m, k = a.shape
    n = b.shape[0] * LANES if b_cbm else b.shape[1]
    tm = tm or _pick(m, (1024, 768, 512, 384, 256, 128, 64))
    tn = tn or _pick(n, (1024, 512, 256, 128))
    tk = tk or _pick(k, (2048, 1024, 768, 512, 256, 128))
    nk = k // tk

    def load(ref, cbm):
        if not cbm:
            return ref[...].astype(BF16)
        return jnp.concatenate([ref[j].astype(BF16) for j in range(ref.shape[0])], axis=1)

    def body(a_ref, b_ref, o_ref, acc_ref):
        kk = pl.program_id(2)
        part = (_mm_tn if a_t else _mm)(load(a_ref, a_cbm), load(b_ref, b_cbm))

        @pl.when(kk == 0)
        def _():
            acc_ref[...] = part

        @pl.when(kk > 0)
        def _():
            acc_ref[...] += part

        @pl.when(kk == nk - 1)
        def _():
            r = acc_ref[...]
            if out_cbm:
                for j in range(tn // LANES):
                    o_ref[j] = r[:, j * LANES:(j + 1) * LANES].astype(out_dtype)
            else:
                o_ref[...] = r.astype(out_dtype)

    if a_cbm:
        a_spec = pl.BlockSpec((tk // LANES, tm, LANES), lambda j, i, kk: (kk, i, 0))
    elif a_t:
        a_spec = pl.BlockSpec((tk, tm), lambda j, i, kk: (kk, i))
    else:
        a_spec = pl.BlockSpec((tm, tk), lambda j, i, kk: (i, kk))
    if b_cbm:
        b_spec = pl.BlockSpec((tn // LANES, tk, LANES), lambda j, i, kk: (j, kk, 0))
    else:
        b_spec = pl.BlockSpec((tk, tn), lambda j, i, kk: (kk, j))
    if out_cbm:
        o_spec = pl.BlockSpec((tn // LANES, tm, LANES), lambda j, i, kk: (j, i, 0))
        o_shape = jax.ShapeDtypeStruct((n // LANES, m, LANES), out_dtype)
    else:
        o_spec = pl.BlockSpec((tm, tn), lambda j, i, kk: (i, j))
        o_shape = jax.ShapeDtypeStruct((m, n), out_dtype)
    return pl.pallas_call(
        body, name=name, grid=(n // tn, m // tm, nk),
        in_specs=[a_spec, b_spec], out_specs=o_spec, out_shape=o_shape,
        scratch_shapes=[pltpu.VMEM((tm, tn), F32)],
        compiler_params=_params(("arbitrary", "arbitrary", "arbitrary"), VMEM_BIG),
    )(a, b)


def _ada_fwd(cvec, ada_w, ada_b):
    def body(c_ref, w_ref, b_ref, o_ref):
        o_ref[...] = _mm(_silu(c_ref[...]), w_ref[...], HI) + b_ref[...]

    return pl.pallas_call(
        body, name="ada_fwd", out_shape=jax.ShapeDtypeStruct((8, ada_w.shape[1]), F32),
        compiler_params=_params(None, VMEM_BIG),
    )(cvec, ada_w, ada_b)


def _ada_bwd(cvec, ada_w, dmods):
    def body(c_ref, w_ref, d_ref, o_ref):
        (_, vj) = jax.vjp(_silu, c_ref[...])
        o_ref[...] = vj(_mm_nt(d_ref[...], w_ref[...], HI))[0]

    return pl.pallas_call(
        body, name="ada_bwd", out_shape=jax.ShapeDtypeStruct(cvec.shape, F32),
        compiler_params=_params(None, VMEM_BIG),
    )(cvec, ada_w, dmods)


def _ada_wgrad(conds, dm_lat, dm_ctx):
    d = conds.shape[1]
    cols = dm_lat.shape[1]

    def body(c_ref, dl_ref, dc_ref, o_ref):
        dctx = dc_ref[0:1]
        for s in range(1, N_DEV):
            dctx = dctx + dc_ref[s:s + 1]
        rhs = jnp.concatenate([dl_ref[...], dctx, jnp.zeros((7, cols), F32)], axis=0)
        o_ref[...] = _mm_tn(_silu(c_ref[...]), rhs, HI)

    return pl.pallas_call(
        body, name="ada_wgrad", out_shape=jax.ShapeDtypeStruct((d, cols), F32),
    )(conds, dm_lat, dm_ctx)


def _prenorm_tile(xt, nw, sc, sh):
    r = lax.rsqrt(jnp.mean(xt * xt, axis=-1, keepdims=True) + NORM_EPS)
    return (xt * r * nw) * (1.0 + sc) + sh


def _tok_specs(l_lat, l_ctx, d, tm):
    nl, nc = l_lat // tm, l_ctx // tm
    lat = pl.BlockSpec((tm, d), lambda i: (jnp.minimum(i, nl - 1), 0))
    ctx = pl.BlockSpec((tm, d), lambda i: (jnp.maximum(i - nl, 0), 0))
    return nl, nc, lat, ctx


def _prenorm_fwd(x, ctx, norm_w, mods):
    l_lat, d = x.shape
    l_ctx = ctx.shape[0]
    tm = _pick(l_ctx, (256, 128, 64))
    nl, nc, lat_spec, ctx_spec = _tok_specs(l_lat, l_ctx, d, tm)

    def body(x_ref, c_ref, nw_ref, m_ref, h_ref):
        is_lat = pl.program_id(0) < nl
        xt = jnp.where(is_lat, x_ref[...], c_ref[...])
        row = jnp.where(is_lat, m_ref[0:1, :], m_ref[1:2, :])
        h_ref[...] = _prenorm_tile(xt, nw_ref[...], row[:, d:2 * d], row[:, 0:d]).astype(BF16)

    return pl.pallas_call(
        body, name="prenorm_fwd", grid=(nl + nc,),
        in_specs=[lat_spec, ctx_spec, pl.BlockSpec((1, d), lambda i: (0, 0)), pl.BlockSpec((8, 3 * d), lambda i: (0, 0))],
        out_specs=pl.BlockSpec((tm, d), lambda i: (i, 0)),
        out_shape=jax.ShapeDtypeStruct((l_lat + l_ctx, d), BF16),
        compiler_params=_params(("arbitrary",)),
    )(x, ctx, norm_w, mods)


def _prenorm_bwd(x, ctx, norm_w, mods, dh, dx_res):
    l_lat, d = x.shape
    l_ctx = ctx.shape[0]
    tm = _pick(l_ctx, (256, 128, 64))
    nl, nc, lat_spec, ctx_spec = _tok_specs(l_lat, l_ctx, d, tm)

    def body(x_ref, c_ref, nw_ref, m_ref, dh_ref, dr_ref, gx_ref, dnw_ref, dm_ref):
        i = pl.program_id(0)
        is_lat = i < nl

        @pl.when(i == 0)
        def _():
            dnw_ref[...] = jnp.zeros_like(dnw_ref)
            dm_ref[...] = jnp.zeros_like(dm_ref)

        xt = jnp.where(is_lat, x_ref[...], c_ref[...])
        row = jnp.where(is_lat, m_ref[0:1, :], m_ref[1:2, :])
        _, vj = jax.vjp(_prenorm_tile, xt, nw_ref[...], row[:, d:2 * d], row[:, 0:d])
        dxt, dnw, dsc, dsh = vj(dh_ref[...])
        dnw_ref[...] += dnw
        upd = jnp.concatenate([dsh, dsc, jnp.zeros_like(dsh)], axis=1)

        @pl.when(is_lat)
        def _():
            gx_ref[...] = dr_ref[...] + dxt
            dm_ref[0:1, :] += upd

        @pl.when(jnp.logical_not(is_lat))
        def _():
            dm_ref[1:2, :] += upd

    return pl.pallas_call(
        body, name="prenorm_bwd", grid=(nl + nc,),
        in_specs=[lat_spec, ctx_spec, pl.BlockSpec((1, d), lambda i: (0, 0)), pl.BlockSpec((8, 3 * d), lambda i: (0, 0)),
                  pl.BlockSpec((tm, d), lambda i: (i, 0)), lat_spec],
        out_specs=[lat_spec, pl.BlockSpec((1, d), lambda i: (0, 0)), pl.BlockSpec((8, 3 * d), lambda i: (0, 0))],
        out_shape=[jax.ShapeDtypeStruct((l_lat, d), F32), jax.ShapeDtypeStruct((1, d), F32),
                   jax.ShapeDtypeStruct((8, 3 * d), F32)],
        compiler_params=_params(("arbitrary",)),
    )(x, ctx, norm_w, mods, dh, dx_res)


def _conv_call(src, wts, l_lat, first_block, n_blocks, name, dout=None, passthru=None, into=None):
    t = src.shape[1]
    l_ctx = t - l_lat
    rc = _pick(l_lat, (512, 256, 128, 64))
    n_rc = l_lat // rc
    bwd = dout is not None

    def shifted(xc, n, period):
        pos = lax.broadcasted_iota(jnp.int32, (n, LANES), 0) % period
        xm = jnp.where(pos == 0, 0.0, pltpu.roll(xc, 1, 0))
        xp = jnp.where(pos == period - 1, 0.0, pltpu.roll(xc, n - 1, 0))
        return xm, xc, xp

    def fill(ref, sm, s0, sp):
        zero = jnp.zeros((GRID_W, LANES), F32)
        for s in (sm, s0, sp):
            s[pl.ds(0, GRID_W), :] = zero
            s[pl.ds(l_lat + GRID_W, GRID_W), :] = zero

        def step(i, carry):
            st = pl.multiple_of(i * rc, rc)
            xm, x0, xp = shifted(ref[0, pl.ds(st, rc), :], rc, GRID_W)
            sm[pl.ds(st + GRID_W, rc), :] = xm
            s0[pl.ds(st + GRID_W, rc), :] = x0
            sp[pl.ds(st + GRID_W, rc), :] = xp
            return carry

        lax.fori_loop(0, n_rc, step, 0)

    def apply(w, sm, s0, sp, out_ref, flip):
        def step(i, carry):
            st = pl.multiple_of(i * rc, rc)
            acc = jnp.zeros((rc, LANES), F32)
            for di in range(3):
                for dj, s in enumerate((sm, s0, sp)):
                    kidx = (2 - di) * 3 + (2 - dj) if flip else di * 3 + dj
                    acc = acc + w[kidx:kidx + 1, :] * s[pl.ds(st + di * GRID_W, rc), :]
            out_ref[0, pl.ds(st, rc), :] = acc
            return carry

        lax.fori_loop(0, n_rc, step, 0)

    def ctx_apply(w, xc, flip):
        xm, x0, xp = shifted(xc, l_ctx, l_ctx)
        ks = (5, 4, 3) if flip else (3, 4, 5)
        return w[ks[0]:ks[0] + 1, :] * xm + w[4:5, :] * x0 + w[ks[2]:ks[2] + 1, :] * xp

    def fwd_body(x_ref, w_ref, o_ref, sm, s0, sp):
        w = w_ref[0]
        fill(x_ref, sm, s0, sp)
        apply(w, sm, s0, sp, o_ref, False)
        o_ref[0, pl.ds(l_lat, l_ctx), :] = ctx_apply(w, x_ref[0, pl.ds(l_lat, l_ctx), :], False)

    n_pass = 0 if passthru is None else passthru.shape[0]

    def bwd_body(x_ref, w_ref, d_ref, *rest):
        p_ref = rest[0] if n_pass else None
        dx_ref, dw_ref, sm, s0, sp = rest[-5:]

        @pl.when(pl.program_id(0) < n_blocks)
        def _():
            bwd_block(x_ref, w_ref, d_ref, dx_ref, dw_ref, sm, s0, sp)

        if n_pass:
            @pl.when(pl.program_id(0) >= n_blocks)
            def _():
                dx_ref[0] = p_ref[0]

    def bwd_block(x_ref, w_ref, d_ref, dx_ref, dw_ref, sm, s0, sp):
        w = w_ref[0]
        fill(d_ref, sm, s0, sp)

        def step(i, acc):
            st = pl.multiple_of(i * rc, rc)
            xc = x_ref[0, pl.ds(st, rc), :]
            dx = jnp.zeros((rc, LANES), F32)
            rows = [None] * 9
            for di in range(3):
                for dj, s in enumerate((sm, s0, sp)):
                    kidx = (2 - di) * 3 + (2 - dj)
                    val = s[pl.ds(st + di * GRID_W, rc), :]
                    dx = dx + w[kidx:kidx + 1, :] * val
                    rows[kidx] = jnp.sum(xc * val, axis=0, keepdims=True)
            dx_ref[0, pl.ds(st, rc), :] = dx
            return acc + jnp.concatenate(rows + [jnp.zeros((7, LANES), F32)], axis=0)

        acc = lax.fori_loop(0, n_rc, step, jnp.zeros((16, LANES), F32))
        dctx = d_ref[0, pl.ds(l_lat, l_ctx), :]
        xctx = x_ref[0, pl.ds(l_lat, l_ctx), :]
        dm, d0, dp = shifted(dctx, l_ctx, l_ctx)
        crow = [jnp.sum(xctx * s, axis=0, keepdims=True) for s in (dp, d0, dm)]
        acc = acc + jnp.concatenate([jnp.zeros((3, LANES), F32)] + crow + [jnp.zeros((10, LANES), F32)], axis=0)
        dw_ref[0] = acc
        dx_ref[0, pl.ds(l_lat, l_ctx), :] = ctx_apply(w, dctx, True)

    blk = lambda off: pl.BlockSpec((1, t, LANES), lambda j: (j + off, 0, 0))
    w_spec = pl.BlockSpec((1, 16, LANES), lambda j: (j, 0, 0))
    scratch = [pltpu.VMEM((l_lat + 2 * GRID_W, LANES), F32)] * 3
    out_t = jax.ShapeDtypeStruct((n_blocks, t, LANES), F32)
    if not bwd:
        return pl.pallas_call(
            fwd_body, name=name, grid=(n_blocks,), in_specs=[blk(first_block), w_spec], out_specs=blk(0),
            out_shape=out_t, scratch_shapes=scratch, compiler_params=_params(("arbitrary",), VMEM_BIG),
        )(src, wts)
    last = n_blocks - 1
    clamped = lambda off: pl.BlockSpec((1, t, LANES), lambda j: (jnp.minimum(j, last) + off, 0, 0))
    w_clamped = pl.BlockSpec((1, 16, LANES), lambda j: (jnp.minimum(j, last), 0, 0))
    in_specs, args, aliases = [clamped(first_block), w_clamped, clamped(0)], [src, wts, dout], {}
    if n_pass:
        in_specs.append(pl.BlockSpec((1, t, LANES), lambda j: (jnp.maximum(j - n_blocks, 0), 0, 0)))
        args.append(passthru)
    first_out = 0
    if into is not None:
        aliases[len(args)] = 0
        in_specs.append(ANY)
        args.append(into[0])
        first_out = into[1]
        out_t = jax.ShapeDtypeStruct(into[0].shape, F32)
    else:
        out_t = jax.ShapeDtypeStruct((n_blocks + n_pass, t, LANES), F32)
    return pl.pallas_call(
        bwd_body, name=name, grid=(n_blocks + n_pass,), in_specs=in_specs,
        out_specs=[pl.BlockSpec((1, t, LANES), lambda j: (j + first_out, 0, 0)), w_clamped],
        out_shape=[out_t, jax.ShapeDtypeStruct((n_blocks, 16, LANES), F32)],
        scratch_shapes=scratch, input_output_aliases=aliases, compiler_params=_params(("arbitrary",), VMEM_BIG),
    )(*args)


def _scan_consts(kind, g, h, rev, d_index):
    n = g * CHUNK
    i = np.arange(n, dtype=np.int32)
    head, pos = i // CHUNK, i % CHUNK
    p = (CHUNK - 1 - pos) if rev else pos
    same_head = head[:, None] == head[None, :]
    pr, pc = p[:, None], p[None, :]
    f = lambda m: np.ascontiguousarray(m, dtype=np.float32)
    incl = same_head & (pc <= pr)
    out = {"eye": f(i[:, None] == i[None, :]), "incl": f(incl), "incl_t": f(incl.T)}
    if kind == "hg":
        mid, same, sec = [], [], []
        for half in HALVES:
            width = 2 * half
            blk = p // width
            second = (p % width) >= half
            ref_pos = blk * width + half - 1
            mid.append(same_head & (pc <= ref_pos[:, None]))
            same.append(same_head & (blk[:, None] == blk[None, :]))
            sec.append(np.broadcast_to(second[:, None], (n, LANES)))
        mid = np.concatenate(mid, axis=0)
        out.update(mid=f(mid), mid_t=f(mid.T), same=f(np.stack(same)), sec=f(np.stack(sec)))
    else:
        low = []
        for half in reversed(HALVES):
            width = 2 * half
            blk = p // width
            second = (p % width) >= half
            low.append(same_head & (blk[:, None] == blk[None, :]) & second[:, None] & np.logical_not(second[None, :]))
        lane = np.arange(LANES, dtype=np.int32)[None, :]
        hh = np.arange(h * CHUNK, dtype=np.int32) // CHUNK
        out.update(strict=f(same_head & (pc < pr)), low=f(np.stack(low)),
                   oh_a=f(lane == (d_index * h + hh)[:, None]), oh_b=f(lane == (2 * h + d_index * h + hh)[:, None]))
    return out


def _mm_01(c, x):
    hi = x.astype(jnp.bfloat16).astype(F32)
    return _mm(c, hi) + _mm(c, x - hi)


def _const_mm(c, c_t, diff, one_pass=False):
    fwd = (lambda x: _mm(c, x)) if one_pass else (lambda x: _mm_01(c, x))
    if not diff:
        return fwd

    f = jax.custom_vjp(fwd)
    f.defvjp(lambda x: (fwd(x), None), lambda _, ct: (_mm_01(c_t, ct),))
    return f


def _kept_inverse(x_kept):
    @jax.custom_vjp
    def f(a):
        return x_kept

    f.defvjp(lambda a: (x_kept, None), lambda _, ct: (-_mm_nt(_mm_tn(x_kept, ct, HIGH), x_kept, HIGH),))
    return f


def _narrow(diff):
    return (lambda x: x) if diff else (lambda x: x.astype(jnp.bfloat16))


def _stack_helpers(g):
    n = g * CHUNK
    rows = lambda vec: jnp.broadcast_to(vec, (g, CHUNK, LANES)).reshape(n, LANES)
    per_head = lambda t: [t[i * CHUNK:(i + 1) * CHUNK] for i in range(g)]
    head_sum = lambda t: rows(jnp.sum(t.reshape(g, CHUNK, LANES), axis=1, keepdims=True))
    return rows, per_head, head_sum


def _lockstep(gens):
    results = [None] * len(gens)
    live = list(range(len(gens)))
    while live:
        for i in list(live):
            try:
                next(gens[i])
            except StopIteration as done:
                results[i] = done.value
                live.remove(i)
    return results


def _hg_chunk(raw_q, raw_f, raw_i, l0, l1, state, cst, g, diff=False):
    n = g * CHUNK
    rows, per_head, head_sum = _stack_helpers(g)
    lb = rows(jax.nn.sigmoid(l0 - l1))
    q = _silu(raw_q) * (LANES ** -0.5)
    gl = jnp.log(lb + (1.0 - lb) * jax.nn.sigmoid(raw_f))
    k = (1.0 - lb) * jax.nn.sigmoid(-raw_f)
    v = raw_i
    b = _const_mm(cst["incl"], cst["incl_t"], diff)(gl)
    b_tot = head_sum(gl)
    q_dec = q * jnp.exp(b)
    k_dec = k * jnp.exp(b_tot - b)
    mids = _const_mm(cst["mid"], cst["mid_t"], diff, True)(gl)
    lo = _narrow(diff)
    yield
    a = (cst["eye"] * jnp.sum(q * k, axis=1, keepdims=True)).astype(jnp.bfloat16)
    for lv in range(len(HALVES)):
        r = mids[lv * n:(lv + 1) * n]
        sec = cst["sec"][lv]
        fst = 1.0 - sec
        qt = q * jnp.exp((b - r) * sec) * sec
        kt = k * jnp.exp((r - b) * fst) * fst
        a = a + _mm_nt(lo(qt), lo(kt)).astype(jnp.bfloat16) * cst["same"][lv]
        yield
    o_intra = per_head(_mm(a.astype(F32), v))
    decay = per_head(jnp.exp(b_tot))
    qd, kd, vs = per_head(q_dec), per_head(k_dec), per_head(v)
    sts = state()
    while sts is None:
        yield
        sts = state()
    outs = [o_intra[i] + _mm_nt(lo(qd[i]), lo(sts[i])) for i in range(g)]
    new = [sts[i] * decay[i][0:1] + _mm_tn(lo(vs[i]), lo(kd[i])) for i in range(g)]
    return jnp.concatenate(outs, axis=0), new, sts


def _gdn_chunk(u_q, u_k, u_v, tail, avec, dtvec, state, cst, g, x_kept=None):
    n = g * CHUNK
    diff = x_kept is not None
    rows, per_head, head_sum = _stack_helpers(g)

    def l2n(t):
        return t * lax.rsqrt(jnp.sum(t * t, axis=-1, keepdims=True) + NORM_EPS)

    q = l2n(_silu(u_q)) * (LANES ** -0.5)
    k = l2n(_silu(u_k))
    v = _silu(u_v)
    tail_n = jnp.concatenate([tail] * g, axis=0)
    za = jnp.sum(tail_n * cst["oh_a"], axis=1, keepdims=True) + rows(dtvec)
    beta = jax.nn.sigmoid(jnp.sum(tail_n * cst["oh_b"], axis=1, keepdims=True))
    gl = -jnp.exp(rows(avec)) * (jnp.maximum(za, 0.0) + jnp.log1p(jnp.exp(-jnp.abs(za))))
    b = _const_mm(cst["incl"], cst["incl_t"], diff)(gl)
    kk = _mm_nt(k, k)
    qk = _mm_nt(q, k)
    yield
    b_tot = head_sum(gl)
    bb = jnp.concatenate([b] * (n // LANES), axis=1)
    bdiff = bb - bb.T
    a_mat = beta * kk * jnp.exp(jnp.where(cst["strict"] > 0.5, bdiff, NEG))
    if diff:
        xinv = _kept_inverse(x_kept)(a_mat)
    else:
        xinv = cst["eye"] - a_mat * cst["low"][0]
        for lv in range(1, len(HALVES)):
            left = _mm(xinv, a_mat * cst["low"][lv])
            yield
            xinv = xinv - _mm(left, xinv)
            yield
        resid = cst["eye"] - _mm(cst["eye"] + a_mat, xinv, HIGH)
        yield
        xinv = xinv + _mm(xinv, resid, HIGH)
        yield
    rhs = jnp.concatenate([beta * v, beta * jnp.exp(b) * k], axis=1)
    sol = _mm(xinv, rhs, HIGH)
    yield
    u0, w = per_head(sol[:, :LANES]), per_head(sol[:, LANES:])
    k_dec = per_head(k * jnp.exp(b_tot - b))
    q_dec = per_head(q * jnp.exp(b))
    decay = per_head(jnp.exp(b_tot))
    p = qk * jnp.exp(jnp.where(cst["incl"] > 0.5, bdiff, NEG))
    ss = state()
    while ss is None:
        yield
        ss = state()
    v_new = [u0[i] - _mm(w[i], ss[i]) for i in range(g)]
    o_state = [_mm(q_dec[i], ss[i]) for i in range(g)]
    yield
    o = _mm(p, jnp.concatenate(v_new, axis=0)) + jnp.concatenate(o_state, axis=0)
    new = [decay[i][0:1] * ss[i] + _mm_tn(k_dec[i], v_new[i]) for i in range(g)]
    return o, new, xinv, ss


def _chunk_index(l_lat, l_ctx, rev, sc):
    rows = sc * CHUNK
    nl, nc = l_lat // rows, l_ctx // rows

    def idx(i):
        if rev:
            return jnp.where(i < nc, nl + nc - 1 - i, nl - 1 - (i - nc))
        return jnp.where(i < nc, nl + i, i - nc)

    subs = list(range(sc))
    return nl + nc, idx, (subs[::-1] if rev else subs)


def _const_args(kind, g, h, rev, d_index):
    consts = _scan_consts(kind, g, h, rev, d_index)
    names = sorted(consts)
    arrs = [jnp.asarray(consts[k]).astype(jnp.bfloat16 if k == "same" else F32) for k in names]
    specs = [pl.BlockSpec(a.shape, functools.partial(lambda nd, i: (0,) * nd, a.ndim)) for a in arrs]
    return names, arrs, specs


def _load_consts(names, refs, g0, g):
    cst = {}
    for k, r in zip(names, refs):
        if k in ("oh_a", "oh_b"):
            cst[k] = r[pl.ds(pl.multiple_of(g0 * CHUNK, CHUNK), g * CHUNK), :]
        else:
            cst[k] = r[...]
    return cst


def _scan_fwd(kind, rev, cbm, segs, vecs, o_prev, l_lat, l_ctx, h, name, tail=None, d_index=0):
    sc = FWD_STEP
    n_steps, cidx, subs = _chunk_index(l_lat, l_ctx, rev, sc)
    rows = sc * CHUNK
    t = l_lat + l_ctx
    n_in = len(segs)
    gdn = kind == "gdn"
    g = min(HEADS_PER_GROUP, h)
    n = g * CHUNK
    c_names, c_arrs, c_specs = _const_args(kind, g, h, rev, d_index)

    def body(*refs):
        in_refs = refs[:n_in]
        pos = n_in
        tail_ref = None
        if gdn:
            tail_ref = refs[pos]
            pos += 1
        v0_ref, v1_ref = refs[pos], refs[pos + 1]
        pos += 2
        prev_ref = None
        if o_prev is not None:
            prev_ref = refs[pos]
            pos += 1
        c_refs = refs[pos:pos + len(c_names)]
        pos += len(c_names)
        o_ref, st_ref = refs[pos], refs[pos + 1]
        pos += 2
        x_ref = None
        if gdn:
            x_ref = refs[pos]
            pos += 1
        s_scr = refs[pos]

        @pl.when(pl.program_id(0) == 0)
        def _():
            s_scr[...] = jnp.zeros_like(s_scr)

        def instance(k, sub, gi, before, after):
            rs = pl.ds(sub * CHUNK, CHUNK)
            g0 = gi * g
            hs = pl.ds(g0, g)
            cst = _load_consts(c_names, c_refs, g0, g)
            state = (lambda: [s_scr[g0 + i] for i in range(g)]) if before is None else (lambda: before[0])
            ins = [r[hs, rs, :].reshape(n, LANES) for r in in_refs]
            if gdn:
                o, new, xinv, old = yield from _gdn_chunk(*ins, tail_ref[0, rs, :], v0_ref[hs], v1_ref[hs], state, cst, g)
                x_ref[k, gi] = xinv
            else:
                o, new, old = yield from _hg_chunk(*ins, v0_ref[hs], v1_ref[hs], state, cst, g)
            after[0] = new
            for i in range(g):
                st_ref[k, g0 + i] = old[i]
            o = o.reshape(g, CHUNK, LANES)
            if prev_ref is not None:
                o = o + prev_ref[hs, rs, :]
            o_ref[hs, rs, :] = o
            if k == sc - 1:
                for i in range(g):
                    s_scr[g0 + i] = new[i]

        gens = []
        handed = [None] * (h // g)
        for k, sub in enumerate(subs):
            for gi in range(h // g):
                after = [None]
                gens.append(instance(k, sub, gi, handed[gi], after))
                handed[gi] = after
        _lockstep(gens)

    seg_spec = lambda sg: pl.BlockSpec((h, rows, LANES), lambda i: (sg, cidx(i), 0))
    vec_spec = pl.BlockSpec((h, 1, LANES), lambda i: (0, 0, 0))
    in_specs = [seg_spec(sg) for sg in segs]
    args = [cbm] * n_in
    if gdn:
        in_specs.append(pl.BlockSpec((1, rows, LANES), lambda i: (0, cidx(i), 0)))
        args.append(tail)
    in_specs += [vec_spec, vec_spec]
    args += list(vecs)
    if o_prev is not None:
        in_specs.append(seg_spec(0))
        args.append(o_prev)
    in_specs += c_specs
    args += c_arrs
    return pl.pallas_call(
        body, name=name, grid=(n_steps,), in_specs=in_specs,
        out_specs=[seg_spec(0), pl.BlockSpec((sc, h, LANES, LANES), lambda i: (i, 0, 0, 0))]
        + ([pl.BlockSpec((sc, h // g, n, n), lambda i: (i, 0, 0, 0))] if gdn else []),
        out_shape=[jax.ShapeDtypeStruct((h, t, LANES), F32),
                   jax.ShapeDtypeStruct((n_steps * sc, h, LANES, LANES), F32)]
        + ([jax.ShapeDtypeStruct((n_steps * sc, h // g, n, n), F32)] if gdn else []),
        scratch_shapes=[pltpu.VMEM((h, LANES, LANES), F32)],
        compiler_params=_params(("arbitrary",), VMEM_BIG),
    )(*args)


def _scan_bwd(kind, rev, cbm, segs, vecs, states, d_o, acc, l_lat, l_ctx, h, name, tail=None, d_index=0, x_kept=None,
              pack=None, pack_into=None, tail_into=None):
    sc = BWD_STEP
    n_steps, cidx, subs = _chunk_index(l_lat, l_ctx, rev, sc)
    rows = sc * CHUNK
    t = l_lat + l_ctx
    n_in = len(segs)
    gdn = kind == "gdn"
    n_grad = n_in + (1 if gdn else 0)
    assert len(acc) == n_grad
    step_of = lambda j: n_steps - 1 - j
    g = min(HEADS_PER_GROUP, h)
    n = g * CHUNK
    c_names, c_arrs, c_specs = _const_args(kind, g, h, rev, d_index)
    items = pack or []
    packed_at = {it[1]: p for p, it in enumerate(items) if it[0] == "grad"}
    copies = [(p, it[1]) for p, it in enumerate(items) if it[0] == "copy"]
    separate = [m for m in range(n_in) if m not in packed_at]
    in_place = pack_into if pack_into is not None else tail_into
    assert pack_into is None or tail_into is None
    tail_blocks = tail_into[2] if tail_into is not None else 1

    def body(*refs):
        in_refs = refs[:n_in]
        pos = n_in
        tail_ref = None
        if gdn:
            tail_ref = refs[pos]
            pos += 1
        v0_ref, v1_ref, st_ref, do_ref = refs[pos:pos + 4]
        pos += 4
        xk_ref = None
        if gdn:
            xk_ref = refs[pos]
            pos += 1
        acc_refs = []
        for a in acc:
            if a is None:
                acc_refs.append(None)
            else:
                acc_refs.append(refs[pos])
                pos += 1
        copy_refs = refs[pos:pos + len(copies)]
        pos += len(copies) + (1 if in_place is not None else 0)
        c_refs = refs[pos:pos + len(c_names)]
        pos += len(c_names)
        sep_refs = refs[pos:pos + len(separate)]
        pos += len(separate)
        pack_ref = None
        if items:
            pack_ref = refs[pos]
            pos += 1
        tail_out = None
        if gdn:
            tail_out = refs[pos]
            pos += 1
        dv0_ref, dv1_ref, ds_scr = refs[pos:pos + 3]

        @pl.when(pl.program_id(0) == 0)
        def _():
            ds_scr[...] = jnp.zeros_like(ds_scr)
            dv0_ref[...] = jnp.zeros_like(dv0_ref)
            dv1_ref[...] = jnp.zeros_like(dv1_ref)

        if gdn:
            tail_out[0] = jnp.zeros((rows, LANES), F32) if acc_refs[n_in] is None else acc_refs[n_in][0]
            for q in range(1, tail_blocks):
                tail_out[q] = jnp.zeros((rows, LANES), F32)

        n_groups = h // g

        rss = [pl.ds(subs[k] * CHUNK, CHUNK) for k in range(sc)]
        hss = [pl.ds(gi * g, g) for gi in range(n_groups)]
        csts = [_load_consts(c_names, c_refs, gi * g, g) for gi in range(n_groups)]
        xks = [[xk_ref[k, gi] for gi in range(n_groups)] for k in range(sc)] if gdn else None

        def link(k, gi, ins, tl, v0, v1, state, after):
            if gdn:
                res = yield from _gdn_chunk(*ins, tl, v0, v1, state, csts[gi], g, xks[k][gi])
            else:
                res = yield from _hg_chunk(*ins, v0, v1, state, csts[gi], g, True)
            after[0] = res[1]
            return res[0]

        def fn(ins, tls, v0s, v1s, s0):
            gens = []
            handed = [None] * n_groups
            for k in range(sc):
                for gi in range(n_groups):
                    before, after = handed[gi], [None]
                    state = (functools.partial(lambda s: s, s0[gi]) if before is None
                             else functools.partial(lambda cell: cell[0], before))
                    gens.append(link(k, gi, ins[k][gi], tls[k], v0s[gi], v1s[gi], state, after))
                    handed[gi] = after
            outs = _lockstep(gens)
            return ([[outs[k * n_groups + gi] for gi in range(n_groups)] for k in range(sc)],
                    [handed[gi][0] for gi in range(n_groups)])

        ins = [[[r[hs, rs, :].reshape(n, LANES) for r in in_refs] for hs in hss] for rs in rss]
        tls = [tail_ref[0, rs, :] if gdn else jnp.zeros((1, 1), F32) for rs in rss]
        s0 = [[st_ref[0, gi * g + i] for i in range(g)] for gi in range(n_groups)]
        _, vj = jax.vjp(fn, ins, tls, [v0_ref[hs] for hs in hss], [v1_ref[hs] for hs in hss], s0)
        d_o_all = [[do_ref[hs, rs, :].reshape(n, LANES) for hs in hss] for rs in rss]
        d_fin = [[ds_scr[gi * g + i] for i in range(g)] for gi in range(n_groups)]
        d_ins, d_tls, d_v0s, d_v1s, d_s0 = vj((d_o_all, d_fin))
        for gi, hs in enumerate(hss):
            g0 = gi * g
            for k, rs in enumerate(rss):
                for m in range(n_in):
                    gm = d_ins[k][gi][m].reshape(g, CHUNK, LANES)
                    if acc_refs[m] is not None:
                        gm = gm + acc_refs[m][hs, rs, :]
                    if m in packed_at:
                        pack_ref[pl.ds(packed_at[m] * h + g0, g), rs, :] = gm
                    else:
                        sep_refs[separate.index(m)][hs, rs, :] = gm
                for (p, _), c_ref in zip(copies, copy_refs):
                    pack_ref[pl.ds(p * h + g0, g), rs, :] = c_ref[hs, rs, :]
            dv0, dv1 = d_v0s[gi], d_v1s[gi]
            if gdn:
                dv0 = jnp.broadcast_to(jnp.sum(dv0, axis=2, keepdims=True), dv0.shape)
                dv1 = jnp.broadcast_to(jnp.sum(dv1, axis=2, keepdims=True), dv1.shape)
            dv0_ref[hs] += dv0
            dv1_ref[hs] += dv1
            for i in range(g):
                ds_scr[g0 + i] = d_s0[gi][i]
        if gdn:
            for k, rs in enumerate(rss):
                tail_out[0, rs, :] += d_tls[k]

    seg_spec = lambda sg: pl.BlockSpec((h, rows, LANES), lambda j: (sg, cidx(step_of(j)), 0))
    tail_spec = pl.BlockSpec((1, rows, LANES), lambda j: (0, cidx(step_of(j)), 0))
    vec_spec = pl.BlockSpec((h, 1, LANES), lambda j: (0, 0, 0))
    in_specs = [seg_spec(sg) for sg in segs]
    args = [cbm] * n_in
    if gdn:
        in_specs.append(tail_spec)
        args.append(tail)
    in_specs += [vec_spec, vec_spec, pl.BlockSpec((sc, h, LANES, LANES), lambda j: (step_of(j), 0, 0, 0)),
                 seg_spec(0)]
    args += list(vecs) + [states, d_o]
    if gdn:
        in_specs.append(pl.BlockSpec((sc, h // g, n, n), lambda j: (step_of(j), 0, 0, 0)))
        args.append(x_kept)
    for k, a in enumerate(acc):
        if a is not None:
            in_specs.append(tail_spec if (gdn and k == n_in) else seg_spec(0))
            args.append(a)
    for _, arr in copies:
        in_specs.append(seg_spec(0))
        args.append(arr)
    aliases = {}
    if in_place is not None:
        aliases[len(args)] = len(separate) + (1 if (items and tail_into is not None) else 0)
        in_specs.append(ANY)
        args.append(in_place[0])
    in_specs += c_specs
    args += c_arrs
    out_specs = [seg_spec(0)] * len(separate)
    out_shape = [jax.ShapeDtypeStruct((h, t, LANES), F32)] * len(separate)
    if items:
        pb = len(items) * h
        first = pack_into[1] if pack_into is not None else 0
        assert first % pb == 0
        out_specs.append(pl.BlockSpec((pb, rows, LANES), lambda j: (first // pb, cidx(step_of(j)), 0)))
        out_shape.append(jax.ShapeDtypeStruct(pack_into[0].shape if pack_into is not None else (pb, t, LANES), F32))
    if gdn and tail_into is not None:
        first_t = tail_into[1]
        assert first_t % tail_blocks == 0
        out_specs.append(pl.BlockSpec((tail_blocks, rows, LANES), lambda j: (first_t // tail_blocks, cidx(step_of(j)), 0)))
        out_shape.append(jax.ShapeDtypeStruct(tail_into[0].shape, F32))
    elif gdn:
        out_specs.append(tail_spec)
        out_shape.append(jax.ShapeDtypeStruct((1, t, LANES), F32))
    out_specs += [vec_spec, vec_spec]
    out_shape += [jax.ShapeDtypeStruct((h, 1, LANES), F32)] * 2
    return pl.pallas_call(
        body, name=name, grid=(n_steps,), in_specs=in_specs, out_specs=out_specs, out_shape=out_shape,
        scratch_shapes=[pltpu.VMEM((h, LANES, LANES), F32)], input_output_aliases=aliases,
        compiler_params=_params(("arbitrary",), VMEM_BIG),
    )(*args)


def _gout_tile(oa, ob, za, zb, naw, nbw):
    def hn(o, w):
        return o * lax.rsqrt(jnp.mean(o * o, axis=-1, keepdims=True) + NORM_EPS) * w

    ya = _silu(za) * hn(oa, naw)
    yb = _silu(zb) * hn(ob, nbw)
    nh = oa.shape[0]
    return jnp.concatenate([ya[i] for i in range(nh)] + [yb[i] for i in range(nh)], axis=1)


def _gout_fwd(oa, ob, y_cbm, naw, nbw, l_lat, h):
    tm = _pick(l_lat, (256, 128, 64))
    blk = lambda sg: pl.BlockSpec((h, tm, LANES), lambda i: (sg, i, 0))
    vec = pl.BlockSpec((h, 1, LANES), lambda i: (0, 0, 0))

    def body(oa_ref, ob_ref, za_ref, zb_ref, na_ref, nb_ref, y_ref):
        y_ref[...] = _gout_tile(oa_ref[...], ob_ref[...], za_ref[...], zb_ref[...], na_ref[...], nb_ref[...]).astype(BF16)

    return pl.pallas_call(
        body, name="gout_fwd", grid=(l_lat // tm,),
        in_specs=[blk(0), blk(0), blk(4), blk(8), vec, vec],
        out_specs=pl.BlockSpec((tm, 2 * h * LANES), lambda i: (i, 0)),
        out_shape=jax.ShapeDtypeStruct((l_lat, 2 * h * LANES), BF16),
        compiler_params=_params(("arbitrary",)),
    )(oa, ob, y_cbm, y_cbm, naw, nbw)


def _gout_bwd(oa, ob, y_cbm, naw, nbw, dymix, l_lat, l_ctx, h):
    tm = _pick(l_ctx, (256, 128, 64))
    nl, nc = l_lat // tm, l_ctx // tm
    t = l_lat + l_ctx
    blk_in = lambda sg: pl.BlockSpec((h, tm, LANES), lambda i: (sg, jnp.minimum(i, nl - 1), 0))
    blk_out = pl.BlockSpec((h, tm, LANES), lambda i: (0, i, 0))
    vec = pl.BlockSpec((h, 1, LANES), lambda i: (0, 0, 0))

    def body(oa_ref, ob_ref, za_ref, zb_ref, na_ref, nb_ref, dy_ref, doa_ref, dob_ref, dza_ref, dzb_ref, dna_ref, dnb_ref):
        i = pl.program_id(0)

        @pl.when(i == 0)
        def _():
            dna_ref[...] = jnp.zeros_like(dna_ref)
            dnb_ref[...] = jnp.zeros_like(dnb_ref)

        @pl.when(i < nl)
        def _():
            _, vj = jax.vjp(_gout_tile, oa_ref[...], ob_ref[...], za_ref[...], zb_ref[...], na_ref[...], nb_ref[...])
            doa, dob, dza, dzb, dna, dnb = vj(dy_ref[...])
            doa_ref[...] = doa
            dob_ref[...] = dob
            dza_ref[...] = dza
            dzb_ref[...] = dzb
            dna_ref[...] += dna
            dnb_ref[...] += dnb

        @pl.when(i >= nl)
        def _():
            for r in (doa_ref, dob_ref, dza_ref, dzb_ref):
                r[...] = jnp.zeros_like(r)

    big = jax.ShapeDtypeStruct((h, t, LANES), F32)
    small = jax.ShapeDtypeStruct((h, 1, LANES), F32)
    dza_out = pl.BlockSpec((h, tm, LANES), lambda i: (4, i, 0))
    return pl.pallas_call(
        body, name="gout_bwd", grid=(nl + nc,),
        in_specs=[blk_in(0), blk_in(0), blk_in(4), blk_in(8), vec, vec,
                  pl.BlockSpec((tm, 2 * h * LANES), lambda i: (jnp.minimum(i, nl - 1), 0))],
        out_specs=[blk_out, blk_out, dza_out, blk_out, vec, vec],
        out_shape=[big, big, jax.ShapeDtypeStruct(y_cbm.shape, F32), big, small, small],
        compiler_params=_params(("arbitrary",)),
    )(oa, ob, y_cbm, y_cbm, naw, nbw, dymix)


def _head_tile(xt, mixt, tgt, gt, fw):
    xo = xt + gt * mixt
    y = xo * lax.rsqrt(jnp.mean(xo * xo, axis=-1, keepdims=True) + NORM_EPS) * fw
    err = y - tgt
    return 0.5 * jnp.sum(jnp.mean(err * err, axis=-1, keepdims=True), axis=0, keepdims=True)


def _loss_head(x, mix, target, mods, final_w):
    l_lat, d = x.shape
    tm = _pick(l_lat, (256, 128, 64))
    tok = pl.BlockSpec((tm, d), lambda i: (i, 0))
    row = pl.BlockSpec((1, d), lambda i: (0, 0))

    def body(x_ref, m_ref, t_ref, mod_ref, fw_ref, loss_ref, dmix_ref, dx_ref, dgt_ref, dfw_ref):
        @pl.when(pl.program_id(0) == 0)
        def _():
            loss_ref[...] = jnp.zeros_like(loss_ref)
            dgt_ref[...] = jnp.zeros_like(dgt_ref)
            dfw_ref[...] = jnp.zeros_like(dfw_ref)

        gt = mod_ref[0:1, 2 * d:3 * d]
        fn = lambda xt, mt, g, fw: _head_tile(xt, mt, t_ref[...], g, fw)
        val, vj = jax.vjp(fn, x_ref[...], m_ref[...], gt, fw_ref[...])
        dxt, dmt, dgt, dfw = vj(jnp.ones((1, 1), F32))
        loss_ref[...] += jnp.broadcast_to(val, loss_ref.shape)
        dmix_ref[...] = dmt.astype(BF16)
        dx_ref[...] = dxt
        dgt_ref[...] += dgt
        dfw_ref[...] += dfw

    return pl.pallas_call(
        body, name="loss_head", grid=(l_lat // tm,),
        in_specs=[tok, tok, tok, pl.BlockSpec((8, 3 * d), lambda i: (0, 0)), row],
        out_specs=[pl.BlockSpec((8, LANES), lambda i: (0, 0)), tok, tok, row, row],
        out_shape=[jax.ShapeDtypeStruct((8, LANES), F32), jax.ShapeDtypeStruct((l_lat, d), BF16),
                   jax.ShapeDtypeStruct((l_lat, d), F32), jax.ShapeDtypeStruct((1, d), F32),
                   jax.ShapeDtypeStruct((1, d), F32)],
        compiler_params=_params(("arbitrary",)),
    )(x, mix, target, mods, final_w)


def _adamw(w, g_slots, m, v, name):
    rows, cols = w.shape
    n = g_slots.shape[0]
    tr = _pick(rows, (64, 32, 16, 8))

    def body(w_ref, g_ref, m_ref, v_ref, go_ref, d_ref, nm_ref, nv_ref):
        gg = g_ref[0].astype(F32)
        for s in range(1, n):
            gg = gg + g_ref[s].astype(F32)
        m2 = ADAM_B1 * m_ref[...] + (1.0 - ADAM_B1) * gg
        v2 = ADAM_B2 * v_ref[...] + (1.0 - ADAM_B2) * (gg * gg)
        m_hat = m2 / (1.0 - ADAM_B1 ** ADAM_STEP)
        v_hat = v2 / (1.0 - ADAM_B2 ** ADAM_STEP)
        go_ref[...] = gg
        d_ref[...] = -ADAM_LR * (m_hat / (jnp.sqrt(v_hat) + ADAM_EPS) + ADAM_WD * w_ref[...])
        nm_ref[...] = m2
        nv_ref[...] = v2

    spec = pl.BlockSpec((tr, cols), lambda i: (i, 0))
    shp = jax.ShapeDtypeStruct((rows, cols), F32)
    return pl.pallas_call(
        body, name=name, grid=(rows // tr,),
        in_specs=[spec, pl.BlockSpec((n, tr, cols), lambda i: (0, i, 0)), spec, spec],
        out_specs=[spec] * 4, out_shape=[shp] * 4,
        compiler_params=_params(("arbitrary",)),
    )(w, g_slots, m, v)


def _adamw_nd(w, g_slots, m, v, name):
    shape = w.shape
    two_d = (-1, shape[-1])
    w2 = w.reshape(two_d)
    outs = _adamw(w2, g_slots.reshape((g_slots.shape[0],) + w2.shape), m.reshape(two_d), v.reshape(two_d), name)
    return [o.reshape(shape) for o in outs]


def kernel(x, c, ctx, c_ctx, norm_w, ada_w, ada_b, w_in, conv_w, hg_lb_logits, gdn_a_log, gdn_dt_bias, ha_norm_w, hb_norm_w, w_out, final_norm_w, loss_target, m_c_ctx, m_norm_w, m_ada_w, m_ada_b, m_w_in, m_conv_w, m_hg_lb_logits, m_gdn_a_log, m_gdn_dt_bias, m_ha_norm_w, m_hb_norm_w, m_w_out, m_final_norm_w, v_c_ctx, v_norm_w, v_ada_w, v_ada_b, v_w_in, v_conv_w, v_hg_lb_logits, v_gdn_a_log, v_gdn_dt_bias, v_ha_norm_w, v_hb_norm_w, v_w_out, v_final_norm_w):
    l_lat, d = x.shape[1], x.shape[2]
    l_ctx = ctx.shape[1]
    t = l_lat + l_ctx
    h = d // LANES
    n_in = 9 * d + 4 * h
    nb = -(-(9 * h + 1) // COLS_PER_TILE) * COLS_PER_TILE
    n_pad = nb * LANES
    win_c, ada_c, conv_c, lb_c, wout_r = w_in.shape[2], ada_w.shape[2], conv_w.shape[3], hg_lb_logits.shape[2], w_out.shape[1]
    me = _me()

    gw_in, gw_out, gw_ada, gw_conv, gw_lb = _gather_two_level(
        [w_in[0].astype(BF16), w_out[0].astype(BF16), ada_w[0], conv_w[0], hg_lb_logits], "gather_weights")
    join = lambda g, axis: jnp.concatenate([g[k] for k in range(N_DEV)], axis=axis)
    w_in_full = join(gw_in, 1)
    w_out_full = gw_out.reshape(N_DEV * wout_r, d)
    ada_full = join(gw_ada, 1)
    conv_full = join(gw_conv, 2)
    lb_full = join(gw_lb, 2)
    w_pad = jnp.pad(w_in_full, ((0, 0), (0, n_pad - n_in)))
    w_pad_t = w_pad.T
    w_out_t = w_out_full.T

    lbl = lb_full.reshape(2, 2, h, 1, LANES)
    conv_rows = jnp.pad(conv_full.reshape(9, 3 * h, LANES).transpose(1, 0, 2), ((0, 0), (0, 7), (0, 0)))
    lane_bc = lambda a: jnp.broadcast_to(a.reshape(2, h, 1, 1), (2, h, 1, LANES))
    avec, dtvec = lane_bc(gdn_a_log[0]), lane_bc(gdn_dt_bias[0])
    naw, nbw = ha_norm_w[0].reshape(h, 1, LANES), hb_norm_w[0].reshape(h, 1, LANES)

    cvec = jnp.concatenate([c, c_ctx[None, :], jnp.zeros((6, d), F32)], axis=0)
    mods = _ada_fwd(cvec, ada_full, ada_b)
    x2, ctx2, tgt2 = x[0], ctx[0], loss_target[0]
    h_all = _prenorm_fwd(x2, ctx2, norm_w, mods)
    y = _matmul(h_all, w_pad, "proj", out_cbm=True, tn=COLS_PER_TILE * LANES)
    tail = lax.slice_in_dim(y, 9 * h, 9 * h + 1, axis=0)
    u = _conv_call(y, conv_rows, l_lat, 5 * h, 3 * h, "conv_fwd")

    oa, st_af = _scan_fwd("hg", False, y, (0, 1, 3), (lbl[0, 0], lbl[1, 0]), None, l_lat, l_ctx, h, "hg_fwd_f")
    oa, st_ab = _scan_fwd("hg", True, y, (0, 2, 3), (lbl[0, 1], lbl[1, 1]), oa, l_lat, l_ctx, h, "hg_fwd_b")
    ob, st_bf, xk_f = _scan_fwd("gdn", False, u, (0, 1, 2), (avec[0], dtvec[0]), None, l_lat, l_ctx, h, "gdn_fwd_f", tail, 0)
    ob, st_bb, xk_b = _scan_fwd("gdn", True, u, (0, 1, 2), (avec[1], dtvec[1]), ob, l_lat, l_ctx, h, "gdn_fwd_b", tail, 1)

    ymix = _gout_fwd(oa, ob, y, naw, nbw, l_lat, h)
    mix = _matmul(ymix, w_out_full, "out_proj")
    loss_blk, dmix, dx_res, dgt, dfw = _loss_head(x2, mix, tgt2, mods, final_norm_w.reshape(1, d))

    dymix = _matmul(dmix, w_out_t, "d_ymix")
    dw_out = _matmul(ymix, dmix, "d_w_out", a_t=True)
    doa, dob, dy, dzb, dnaw, dnbw = _gout_bwd(oa, ob, y, naw, nbw, dymix, l_lat, l_ctx, h)

    gq, gff, gi, dl0f, dl1f = _scan_bwd("hg", False, y, (0, 1, 3), (lbl[0, 0], lbl[1, 0]), st_af, doa,
                                        [None, None, None], l_lat, l_ctx, h, "hg_bwd_f")
    dy, dl0b, dl1b = _scan_bwd("hg", True, y, (0, 2, 3), (lbl[0, 1], lbl[1, 1]), st_ab, doa, [gq, None, gi],
                               l_lat, l_ctx, h, "hg_bwd_b",
                               pack=[("grad", 0), ("copy", gff), ("grad", 1), ("grad", 2)], pack_into=(dy, 0))
    guq, guk, guv, gtail, da_f, ddt_f = _scan_bwd("gdn", False, u, (0, 1, 2), (avec[0], dtvec[0]), st_bf, dob,
                                                  [None] * 4, l_lat, l_ctx, h, "gdn_bwd_f", tail, 0, xk_f)
    du, dy, da_b, ddt_b = _scan_bwd("gdn", True, u, (0, 1, 2), (avec[1], dtvec[1]), st_bb, dob,
                                    [guq, guk, guv, gtail], l_lat, l_ctx, h, "gdn_bwd_b", tail, 1, xk_b,
                                    pack=[("grad", 0), ("grad", 1), ("grad", 2)], tail_into=(dy, 9 * h, nb - 9 * h))
    dy, dconv_rows = _conv_call(y, conv_rows, l_lat, 5 * h, 3 * h, "conv_bwd", dout=du, passthru=dzb, into=(dy, 5 * h))
    dh = _matmul(dy, w_pad_t, "d_h", a_cbm=True)
    dw_in = _matmul(h_all, dy, "d_w_in", a_t=True, b_cbm=True)
    grad_x, dnorm_w, dmods_pre = _prenorm_bwd(x2, ctx2, norm_w, mods, dh, dx_res)

    dmods = jnp.concatenate([dmods_pre[:, :2 * d],
                             jnp.concatenate([dgt, jnp.zeros((7, d), F32)], axis=0)], axis=1)
    dcond = _ada_bwd(cvec, ada_full, dmods)

    dw_in_s = jnp.stack([dw_in[:, k * win_c:(k + 1) * win_c] for k in range(N_DEV)])
    dw_out_s = dw_out.reshape(N_DEV, wout_r, d)
    dconv_s = dconv_rows[:, :9, :].transpose(1, 0, 2).reshape(3, 3, N_DEV, conv_c).transpose(2, 0, 1, 3)
    dlb_s = jnp.stack([jnp.stack([dl0f, dl0b]), jnp.stack([dl1f, dl1b])]).reshape(2, 2, N_DEV, lb_c).transpose(2, 0, 1, 3)
    slabs = [dw_in_s, dw_out_s, dconv_s, dlb_s]
    theirs = _swap_with_sibling([a.astype(BF16) for a in slabs], "swap_grads")
    pairs = [_pair_sum(a, b, "pair_sum_%d" % k) for k, (a, b) in enumerate(zip(slabs, theirs))]
    s_w_in, s_w_out, s_conv, s_lb = _scatter_over_chips(pairs, "scatter_grads")

    da = jnp.stack([da_f[:, 0, 0], da_b[:, 0, 0]])
    ddt = jnp.stack([ddt_f[:, 0, 0], ddt_b[:, 0, 0]])
    small_parts = [dcond[1], dnorm_w, dmods[0] + dmods[1], dnaw, dnbw, dfw, da, ddt, c[0], dmods[0], dmods[1]]
    small_blob, soff = _to_blob(small_parts)
    gathered, = _exchange([small_blob], "gather_small", True)
    summed = _sum_slots(gathered, "sum_small").reshape(-1)
    gflat = gathered.reshape(N_DEV, -1)
    take = lambda k, n: summed[soff[k]:soff[k] + n]
    one = lambda k, n, shape: take(k, n).reshape((1,) + shape)
    s_c_ctx = one(0, d, (d,))
    s_norm_w = one(1, d, (1, d))
    s_ada_b = one(2, 3 * d, (1, 3 * d))
    s_ha = one(3, d, (1, h, LANES))
    s_hb = one(4, d, (1, h, LANES))
    s_final = one(5, d, (d,))
    s_a_log = one(6, 2 * h, (1, 2, h))
    s_dt = one(7, 2 * h, (1, 2, h))
    conds = jnp.concatenate([gflat[:, soff[8]:soff[8] + d], c_ctx[None, :], jnp.zeros((7, d), F32)], axis=0)
    col0 = me * ada_c
    dm_lat = lax.dynamic_slice_in_dim(gflat[:, soff[9]:soff[9] + 3 * d], col0, ada_c, axis=1)
    dm_ctx = lax.dynamic_slice_in_dim(gflat[:, soff[10]:soff[10] + 3 * d], col0, ada_c, axis=1)
    s_ada_w = _ada_wgrad(conds, dm_lat, dm_ctx).reshape(1, 1, d, ada_c)

    loss = lax.psum(loss_blk[0, 0], ("x", "y", "c"))

    weights = [("c_ctx", c_ctx, s_c_ctx, m_c_ctx, v_c_ctx), ("norm_w", norm_w, s_norm_w, m_norm_w, v_norm_w),
               ("ada_w", ada_w, s_ada_w, m_ada_w, v_ada_w), ("ada_b", ada_b, s_ada_b, m_ada_b, v_ada_b),
               ("w_in", w_in, s_w_in, m_w_in, v_w_in), ("conv_w", conv_w, s_conv, m_conv_w, v_conv_w),
               ("hg_lb_logits", hg_lb_logits, s_lb, m_hg_lb_logits, v_hg_lb_logits),
               ("gdn_a_log", gdn_a_log, s_a_log, m_gdn_a_log, v_gdn_a_log),
               ("gdn_dt_bias", gdn_dt_bias, s_dt, m_gdn_dt_bias, v_gdn_dt_bias),
               ("ha_norm_w", ha_norm_w, s_ha, m_ha_norm_w, v_ha_norm_w), ("hb_norm_w", hb_norm_w, s_hb, m_hb_norm_w, v_hb_norm_w),
               ("w_out", w_out, s_w_out, m_w_out, v_w_out), ("final_norm_w", final_norm_w, s_final, m_final_norm_w, v_final_norm_w)]
    grads, deltas, new_ms, new_vs = [], [], [], []
    for nm, w, gs, m, v in weights:
        g, dl, m2, v2 = _adamw_nd(w, gs, m, v, "adamw_" + nm)
        grads.append(g)
        deltas.append(dl)
        new_ms.append(m2)
        new_vs.append(v2)
    return (loss, grad_x[None], *grads, *deltas, *new_ms, *new_vs)
```
